```python
import jax, jax.numpy as jnp
from jax import lax
import numpy as np

D_MODEL = 2048
BATCH = 2
SEQ = 4096
DEPTH = 1
DEC_BATCH = 32
DEC_SEQ = 1
PAST_LEN = 16384
PAGE_SIZE = 128

N_HEADS = 16
N_KV_HEADS = 2
HEAD_DIM = 64
GROUP = N_HEADS // N_KV_HEADS
ATTN_WIDTH = N_HEADS * HEAD_DIM
KV_WIDTH = N_KV_HEADS * HEAD_DIM
WINDOW = 128
BLOCK = WINDOW
ROPE_THETA = 10000.0
CONV_WIDTH = D_MODEL // 2
CONV_K = 3
D_FF = 5632
PLE_DIM = 256
RMS_EPS = 1e-6
IN_COLS = ATTN_WIDTH + 2 * KV_WIDTH + 3 * CONV_WIDTH + 2 * D_MODEL

kernel_name = "hybrid_swa_sink_shortconv_convffn_step"


def rmsnorm(x, g):
    xf = x.astype(jnp.float32)
    y = xf * lax.rsqrt(jnp.mean(xf * xf, axis=-1, keepdims=True) + RMS_EPS)
    return (y * g.astype(jnp.float32)).astype(x.dtype)


def rope(x, pos):
    half = HEAD_DIM // 2
    inv = ROPE_THETA ** (-jnp.arange(half, dtype=jnp.float32) / half)
    ang = pos.astype(jnp.float32)[:, None] * inv[None, :]
    cos = jnp.cos(ang)[None, :, None, :]
    sin = jnp.sin(ang)[None, :, None, :]
    xf = x.astype(jnp.float32)
    x1, x2 = xf[..., :half], xf[..., half:]
    return jnp.concatenate([x1 * cos - x2 * sin, x2 * cos + x1 * sin], axis=-1).astype(x.dtype)


def causal_dwconv(u, prev, w):
    full = jnp.concatenate([prev.astype(u.dtype), u], axis=1)
    T = u.shape[1]
    out = w[0] * full[:, 0:T]
    for k in range(1, CONV_K):
        out = out + w[k] * full[:, k:k + T]
    return out, full[:, -(CONV_K - 1):]


def sink_attention(q, k, v, mask, sinks):
    scale = HEAD_DIM ** -0.5
    s = jnp.einsum('nbqkgd,nbckd->nbkgqc', q, k).astype(jnp.float32) * scale
    s = jnp.where(mask[None, :, None, None], s, -jnp.inf)
    sink = sinks.astype(jnp.float32).reshape(N_KV_HEADS, GROUP)[None, None, :, :, None, None]
    sink = jnp.broadcast_to(sink, s.shape[:-1] + (1,))
    pr = jax.nn.softmax(jnp.concatenate([s, sink], axis=-1), axis=-1)[..., :-1]
    return jnp.einsum('nbkgqc,nbckd->nbqkgd', pr.astype(v.dtype), v)


def window_attention_prompt(q, k, v, sinks):
    N, T = q.shape[:2]
    nb = T // BLOCK
    qb = q.reshape(N, nb, BLOCK, N_KV_HEADS, GROUP, HEAD_DIM)
    kb = k.reshape(N, nb, BLOCK, N_KV_HEADS, HEAD_DIM)
    vb = v.reshape(N, nb, BLOCK, N_KV_HEADS, HEAD_DIM)

    def with_prev(xb):
        prev = jnp.pad(xb[:, :-1], ((0, 0), (1, 0), (0, 0), (0, 0), (0, 0)))
        return jnp.concatenate([prev, xb], axis=2)

    qi = jnp.arange(BLOCK)[:, None]
    ci = jnp.arange(2 * BLOCK)[None, :]
    rel = BLOCK + qi - ci
    band = (rel >= 0) & (rel < WINDOW)
    blk = jnp.arange(nb)[:, None, None]
    mask = band[None] & ((blk > 0) | (ci >= BLOCK)[None])
    o = sink_attention(qb, with_prev(kb), with_prev(vb), mask, sinks)
    return o.reshape(N, T, ATTN_WIDTH)


def window_attention_sample(q, k, v, buf_k, buf_v, sinks):
    N, T = q.shape[:2]
    kf = jnp.concatenate([buf_k.astype(k.dtype), k], axis=1)
    vf = jnp.concatenate([buf_v.astype(v.dtype), v], axis=1)
    qpos = PAST_LEN + jnp.arange(T)
    kpos = PAST_LEN - WINDOW + jnp.arange(WINDOW + T)
    rel = qpos[:, None] - kpos[None, :]
    mask = ((rel >= 0) & (rel < WINDOW))[None]
    o = sink_attention(q.reshape(N, 1, T, N_KV_HEADS, GROUP, HEAD_DIM), kf[:, None], vf[:, None], mask, sinks)
    return o.reshape(N, T, ATTN_WIDTH), kf[:, -WINDOW:], vf[:, -WINDOW:]


def layer(x, p, pos, prev_k, prev_v, prev_conv, prev_ffn, w, prompt):
    (g_mix_pre, g_mix_post, g_ffn_pre, g_ffn_post, w_in, sinks, conv_w, w_attn_branch,
     w_conv_branch, b_gate, w_o, w_ffn_up, ffn_conv_w, ffn_conv_b, w_ffn_down, w_ple, w_ple_gate) = w
    N, T = x.shape[:2]
    h = rmsnorm(x, g_mix_pre)
    proj = h @ w_in
    splits = [ATTN_WIDTH, ATTN_WIDTH + KV_WIDTH, ATTN_WIDTH + 2 * KV_WIDTH,
              ATTN_WIDTH + 2 * KV_WIDTH + CONV_WIDTH, ATTN_WIDTH + 2 * KV_WIDTH + 2 * CONV_WIDTH,
              ATTN_WIDTH + 2 * KV_WIDTH + 3 * CONV_WIDTH, ATTN_WIDTH + 2 * KV_WIDTH + 3 * CONV_WIDTH + D_MODEL]
    q, k, v, cb, cc, cu, ga, gb = jnp.split(proj, splits, axis=-1)
    q = rope(q.reshape(N, T, N_HEADS, HEAD_DIM), pos)
    k = rope(k.reshape(N, T, N_KV_HEADS, HEAD_DIM), pos)
    v = v.reshape(N, T, N_KV_HEADS, HEAD_DIM)
    if prompt:
        a = window_attention_prompt(q, k, v, sinks)
        new_k, new_v = k[:, -WINDOW:], v[:, -WINDOW:]
        prev_conv = jnp.zeros((N, CONV_K - 1, CONV_WIDTH), x.dtype)
        prev_ffn = jnp.zeros((N, CONV_K - 1, D_FF), x.dtype)
    else:
        a, new_k, new_v = window_attention_sample(q, k, v, prev_k, prev_v, sinks)
    conv_out, new_conv = causal_dwconv(cc * cu, prev_conv, conv_w)
    b_out = cb * conv_out
    branch_a = a @ w_attn_branch
    branch_b = b_out @ w_conv_branch
    merged = jax.nn.sigmoid(ga + b_gate[0]) * branch_a + jax.nn.sigmoid(gb + b_gate[1]) * branch_b
    x = x + rmsnorm(merged @ w_o, g_mix_post)
    h2 = rmsnorm(x, g_ffn_pre)
    up, gt = jnp.split(h2 @ w_ffn_up, 2, axis=-1)
    uc, new_ffn = causal_dwconv(up, prev_ffn, ffn_conv_w)
    f = (jax.nn.gelu(uc + ffn_conv_b, approximate=True) * gt) @ w_ffn_down
    x = x + rmsnorm(f, g_ffn_post)
    x = x + jax.nn.sigmoid(x @ w_ple_gate) * (p @ w_ple)
    return x, new_k, new_v, new_conv, new_ffn


def setup_inputs(seed: int = 0) -> dict:
    key = jax.random.key(seed)
    ks = jax.random.split(key, 26)
    f32 = jnp.float32

    def nrm(k, shape, scale=1.0):
        return jax.random.normal(k, shape, f32) * scale

    def gain(k):
        return 1.0 + 0.05 * jax.random.normal(k, (DEPTH, D_MODEL), f32)

    return {
        "x_prompt": nrm(ks[0], (BATCH, SEQ, D_MODEL)),
        "x_sample": nrm(ks[1], (DEC_BATCH, DEC_SEQ, D_MODEL)),
        "state_attn_k": nrm(ks[2], (DEPTH, DEC_BATCH, WINDOW, N_KV_HEADS, HEAD_DIM)),
        "state_attn_v": nrm(ks[3], (DEPTH, DEC_BATCH, WINDOW, N_KV_HEADS, HEAD_DIM)),
        "state_conv": nrm(ks[4], (DEPTH, DEC_BATCH, CONV_K - 1, CONV_WIDTH)),
        "state_ffn_conv": nrm(ks[5], (DEPTH, DEC_BATCH, CONV_K - 1, D_FF)),
        "p_prompt": nrm(ks[6], (DEPTH, BATCH, SEQ, PLE_DIM)),
        "p_sample": nrm(ks[7], (DEPTH, DEC_BATCH, DEC_SEQ, PLE_DIM)),
        "norm_mix_pre": gain(ks[8]),
        "norm_mix_post": gain(ks[9]),
        "norm_ffn_pre": gain(ks[10]),
        "norm_ffn_post": gain(ks[11]),
        "w_in": nrm(ks[12], (DEPTH, D_MODEL, IN_COLS), D_MODEL ** -0.5),
        "attn_sinks": nrm(ks[13], (DEPTH, N_HEADS), 0.5),
        "conv_w": nrm(ks[14], (DEPTH, CONV_K, CONV_WIDTH), CONV_K ** -0.5),
        "w_attn_branch": nrm(ks[15], (DEPTH, ATTN_WIDTH, D_MODEL), ATTN_WIDTH ** -0.5),
        "w_conv_branch": nrm(ks[16], (DEPTH, CONV_WIDTH, D_MODEL), CONV_WIDTH ** -0.5),
        "b_gate": nrm(ks[17], (DEPTH, 2, D_MODEL), 0.02),
        "w_o": nrm(ks[18], (DEPTH, D_MODEL, D_MODEL), D_MODEL ** -0.5),
        "w_ffn_up": nrm(ks[19], (DEPTH, D_MODEL, 2 * D_FF), D_MODEL ** -0.5),
        "ffn_conv_w": nrm(ks[20], (DEPTH, CONV_K, D_FF), CONV_K ** -0.5),
        "ffn_conv_b": nrm(ks[21], (DEPTH, D_FF), 0.02),
        "w_ffn_down": nrm(ks[22], (DEPTH, D_FF, D_MODEL), D_FF ** -0.5),
        "w_ple": nrm(ks[23], (DEPTH, PLE_DIM, D_MODEL), PLE_DIM ** -0.5),
        "w_ple_gate": nrm(ks[24], (DEPTH, D_MODEL, D_MODEL), D_MODEL ** -0.5),
    }


def reference(x_prompt, x_sample, state_attn_k, state_attn_v, state_conv, state_ffn_conv,
              p_prompt, p_sample, norm_mix_pre, norm_mix_post, norm_ffn_pre, norm_ffn_post,
              w_in, attn_sinks, conv_w, w_attn_branch, w_conv_branch, b_gate, w_o,
              w_ffn_up, ffn_conv_w, ffn_conv_b, w_ffn_down, w_ple, w_ple_gate):
    pos_prompt = jnp.arange(SEQ, dtype=jnp.int32)
    pos_sample = PAST_LEN + jnp.arange(DEC_SEQ, dtype=jnp.int32)
    xp, xs = x_prompt, x_sample
    kp_l, vp_l, cp_l, fp_l = [], [], [], []
    ks_l, vs_l, cs_l, fs_l = [], [], [], []
    for i in range(DEPTH):
        w = (norm_mix_pre[i], norm_mix_post[i], norm_ffn_pre[i], norm_ffn_post[i], w_in[i],
             attn_sinks[i], conv_w[i], w_attn_branch[i], w_conv_branch[i], b_gate[i], w_o[i],
             w_ffn_up[i], ffn_conv_w[i], ffn_conv_b[i], w_ffn_down[i], w_ple[i], w_ple_gate[i])
        xp, kp, vp, cp, fp = layer(xp, p_prompt[i], pos_prompt, None, None, None, None, w, True)
        xs, kss, vss, css, fss = layer(xs, p_sample[i], pos_sample, state_attn_k[i], state_attn_v[i],
                                       state_conv[i], state_ffn_conv[i], w, False)
        kp_l.append(kp); vp_l.append(vp); cp_l.append(cp); fp_l.append(fp)
        ks_l.append(kss); vs_l.append(vss); cs_l.append(css); fs_l.append(fss)
    new_k_p = jnp.stack(kp_l, 0)
    new_v_p = jnp.stack(vp_l, 0)
    new_c_p = jnp.stack(cp_l, 0)
    new_f_p = jnp.stack(fp_l, 0)
    new_k_s = jnp.stack(ks_l, 0)
    new_v_s = jnp.stack(vs_l, 0)
    new_c_s = jnp.stack(cs_l, 0)
    new_f_s = jnp.stack(fs_l, 0)
    return (xp, xs, new_k_p, new_v_p, new_c_p, new_f_p, new_k_s, new_v_s, new_c_s, new_f_s)
```

```python
import functools

import jax
import jax.numpy as jnp
from jax import lax
from jax.experimental import pallas as pl
from jax.experimental.pallas import tpu as pltpu

D_MODEL = 2048
N_HEADS = 16
N_KV_HEADS = 2
HEAD_DIM = 64
GROUP = N_HEADS // N_KV_HEADS
ATTN_WIDTH = N_HEADS * HEAD_DIM
KV_WIDTH = N_KV_HEADS * HEAD_DIM
QKV_WIDTH = ATTN_WIDTH + 2 * KV_WIDTH
WINDOW = 128
PAST_LEN = 16384
ROPE_THETA = 10000.0
CONV_WIDTH = D_MODEL // 2
D_FF = 5632
PLE_DIM = 256
RMS_EPS = 1e-6
OFF_CB = QKV_WIDTH
OFF_CC = OFF_CB + CONV_WIDTH
OFF_CU = OFF_CC + CONV_WIDTH
OFF_GA = OFF_CU + CONV_WIDTH
OFF_GB = OFF_GA + D_MODEL

LANES = 128
MASKED = -1e30
VMEM_LIMIT = 56 * 1024 * 1024

BF16 = jnp.bfloat16
F32 = jnp.float32


def _params(n_axes):
    return pltpu.CompilerParams(dimension_semantics=("arbitrary",) * n_axes,
                                vmem_limit_bytes=VMEM_LIMIT)


def _rms(x, g):
    return x * lax.rsqrt(jnp.mean(x * x, axis=-1, keepdims=True) + RMS_EPS) * g


def _dot(a, b):
    return jnp.dot(a, b, preferred_element_type=F32)


def _wcol(block_cols, col_fn):
    def spec(k_rows):
        return pl.BlockSpec((pl.Squeezed(), pl.Element(k_rows), pl.Element(block_cols)),
                            lambda *g: (0, 0, pl.multiple_of(col_fn(*g), LANES)))
    return spec


def _qkv_kernel(x_ref, g_ref, w_ref, cos_ref, sin_ref, h_ref, q_ref, k_ref, v_ref, wbf_ref):
    @pl.when(pl.program_id(0) == 0)
    def _():
        wbf_ref[...] = w_ref[...].astype(BF16)

    h = _rms(x_ref[...], g_ref[...]).astype(BF16)
    h_ref[...] = h
    proj = _dot(h, wbf_ref[...])
    cos = cos_ref[...]
    sin = sin_ref[...]
    lane = lax.broadcasted_iota(jnp.int32, cos.shape, 1)
    first_half = (lane % HEAD_DIM) < (HEAD_DIM // 2)
    n_q = ATTN_WIDTH // LANES
    for c in range(n_q + 1):
        xg = proj[:, c * LANES:(c + 1) * LANES]
        partner = jnp.where(first_half, pltpu.roll(xg, LANES - HEAD_DIM // 2, 1),
                            pltpu.roll(xg, HEAD_DIM // 2, 1))
        yg = xg * cos + partner * sin
        if c < n_q:
            q_ref[:, c * LANES:(c + 1) * LANES] = (yg * (HEAD_DIM ** -0.5)).astype(BF16)
        else:
            k_ref[...] = yg
    v_ref[...] = proj[:, ATTN_WIDTH + KV_WIDTH:QKV_WIDTH]


def _qkv(x, g, w_in, cos, sin, *, bm):
    m = x.shape[0]
    t_tiles = cos.shape[0] // bm
    return pl.pallas_call(
        _qkv_kernel,
        grid=(m // bm,),
        in_specs=[
            pl.BlockSpec((bm, D_MODEL), lambda i: (i, 0)),
            pl.BlockSpec((1, D_MODEL), lambda i: (0, 0)),
            pl.BlockSpec((pl.Squeezed(), D_MODEL, QKV_WIDTH), lambda i: (0, 0, 0),
                         pipeline_mode=pl.Buffered(1)),
            pl.BlockSpec((bm, LANES), lambda i: (i % t_tiles, 0)),
            pl.BlockSpec((bm, LANES), lambda i: (i % t_tiles, 0)),
        ],
        out_specs=[
            pl.BlockSpec((bm, D_MODEL), lambda i: (i, 0)),
            pl.BlockSpec((bm, ATTN_WIDTH), lambda i: (i, 0)),
            pl.BlockSpec((bm, KV_WIDTH), lambda i: (i, 0)),
            pl.BlockSpec((bm, KV_WIDTH), lambda i: (i, 0)),
        ],
        out_shape=[
            jax.ShapeDtypeStruct((m, D_MODEL), BF16),
            jax.ShapeDtypeStruct((m, ATTN_WIDTH), BF16),
            jax.ShapeDtypeStruct((m, KV_WIDTH), F32),
            jax.ShapeDtypeStruct((m, KV_WIDTH), F32),
        ],
        scratch_shapes=[pltpu.VMEM((D_MODEL, QKV_WIDTH), BF16)],
        compiler_params=_params(1),
        name="qkv_rope",
    )(x, g, w_in, cos, sin)


def _softmax_pv(s, sink, v):
    m = jnp.maximum(jnp.max(s, axis=-1, keepdims=True), sink)
    p = jnp.exp(s - m)
    den = jnp.sum(p, axis=-1, keepdims=True) + jnp.exp(sink - m)
    return _dot(p.astype(BF16), v) / den


def _attn_prompt_kernel(sinks_ref, q_ref, kc_ref, kp_ref, vc_ref, vp_ref, o_ref):
    blk = pl.program_id(1)
    qi = lax.broadcasted_iota(jnp.int32, (WINDOW, 2 * WINDOW), 0)
    ci = lax.broadcasted_iota(jnp.int32, (WINDOW, 2 * WINDOW), 1)
    rel = WINDOW + qi - ci
    mask = (rel >= 0) & (rel < WINDOW) & ((blk > 0) | (ci >= WINDOW))
    for kh in range(N_KV_HEADS):
        cols = slice(kh * HEAD_DIM, (kh + 1) * HEAD_DIM)
        k = jnp.concatenate([kp_ref[:, cols], kc_ref[:, cols]], axis=0).astype(BF16)
        v = jnp.concatenate([vp_ref[:, cols], vc_ref[:, cols]], axis=0).astype(BF16)
        for g in range(GROUP):
            head = kh * GROUP + g
            hcols = slice(head * HEAD_DIM, (head + 1) * HEAD_DIM)
            s = lax.dot_general(q_ref[:, hcols], k, (((1,), (1,)), ((), ())),
                                preferred_element_type=F32)
            s = jnp.where(mask, s, MASKED)
            o_ref[:, hcols] = _softmax_pv(s, sinks_ref[head], v).astype(BF16)


def _attn_prompt(sinks, q, k, v, *, n_seq, seq_len):
    nb = seq_len // WINDOW
    cur = lambda n, b: (n * nb + b, 0)
    prev = lambda n, b: (jnp.maximum(n * nb + b - 1, 0), 0)
    return pl.pallas_call(
        _attn_prompt_kernel,
        grid=(n_seq, nb),
        in_specs=[
            pl.BlockSpec(memory_space=pltpu.SMEM),
            pl.BlockSpec((WINDOW, ATTN_WIDTH), cur),
            pl.BlockSpec((WINDOW, KV_WIDTH), cur),
            pl.BlockSpec((WINDOW, KV_WIDTH), prev),
            pl.BlockSpec((WINDOW, KV_WIDTH), cur),
            pl.BlockSpec((WINDOW, KV_WIDTH), prev),
        ],
        out_specs=pl.BlockSpec((WINDOW, ATTN_WIDTH), cur),
        out_shape=jax.ShapeDtypeStruct((n_seq * seq_len, ATTN_WIDTH), BF16),
        compiler_params=_params(2),
        name="attn_prompt",
    )(sinks, q, k, k, v, v)


def _attn_sample_kernel(sinks_ref, q_ref, kn_ref, vn_ref, sk_ref, sv_ref, a_ref, nk_ref, nv_ref):
    nk_ref[:, 0:WINDOW - 1, :] = sk_ref[:, 1:WINDOW, :]
    nk_ref[:, WINDOW - 1:WINDOW, :] = kn_ref[...]
    nv_ref[:, 0:WINDOW - 1, :] = sv_ref[:, 1:WINDOW, :]
    nv_ref[:, WINDOW - 1:WINDOW, :] = vn_ref[...]
    for kh in range(N_KV_HEADS):
        cols = slice(kh * HEAD_DIM, (kh + 1) * HEAD_DIM)
        heads = slice(kh * GROUP, (kh + 1) * GROUP)
        k = nk_ref[:, :, cols].astype(BF16)
        v = nv_ref[:, :, cols].astype(BF16)
        s = jnp.einsum("ngd,ncd->ngc", q_ref[:, heads, :], k, preferred_element_type=F32)
        sink = sinks_ref[heads, :][None]
        m = jnp.maximum(jnp.max(s, axis=-1, keepdims=True), sink)
        p = jnp.exp(s - m)
        den = jnp.sum(p, axis=-1, keepdims=True) + jnp.exp(sink - m)
        o = jnp.einsum("ngc,ncd->ngd", p.astype(BF16), v, preferred_element_type=F32) / den
        a_ref[:, heads, :] = o.astype(BF16)


def _attn_sample(sinks_col, q3, k_new3, v_new3, state_k, state_v):
    n = q3.shape[0]
    return pl.pallas_call(
        _attn_sample_kernel,
        out_shape=[
            jax.ShapeDtypeStruct((n, N_HEADS, HEAD_DIM), BF16),
            jax.ShapeDtypeStruct((n, WINDOW, KV_WIDTH), F32),
            jax.ShapeDtypeStruct((n, WINDOW, KV_WIDTH), F32),
        ],
        compiler_params=pltpu.CompilerParams(vmem_limit_bytes=VMEM_LIMIT),
        name="attn_sample",
    )(sinks_col, q3, k_new3, v_new3, state_k, state_v)


def _shift_rows(cur, halo):
    rid = lax.broadcasted_iota(jnp.int32, cur.shape, 0)
    h0 = halo[0:1, :]
    h1 = halo[1:2, :]
    s1 = jnp.where(rid == 0, h1, pltpu.roll(cur, 1, 0))
    s2 = jnp.where(rid == 0, h0, jnp.where(rid == 1, h1, pltpu.roll(cur, 2, 0)))
    return s1, s2


def _conv3(cw_ref, cur, s1, s2):
    return cw_ref[0:1, :] * s2 + cw_ref[1:2, :] * s1 + cw_ref[2:3, :] * cur


def _convmix_kernel(tiles_per_seq, sample, h_ref, wb_ref, wc_ref, wu_ref, cw_ref, *rest):
    if sample:
        p2_ref, p1_ref, o_ref, pre_ref, wbf_ref = rest
    else:
        o_ref, st_ref, wbf_ref, halo_ref = rest
    i = pl.program_id(1)
    bn = wb_ref.shape[1]

    @pl.when(i == 0)
    def _():
        wbf_ref[:, 0:bn] = wb_ref[...].astype(BF16)
        wbf_ref[:, bn:2 * bn] = wc_ref[...].astype(BF16)
        wbf_ref[:, 2 * bn:3 * bn] = wu_ref[...].astype(BF16)

    proj = _dot(h_ref[...], wbf_ref[...])
    cur = proj[:, bn:2 * bn] * proj[:, 2 * bn:3 * bn]
    if sample:
        s1, s2 = p1_ref[...], p2_ref[...]
        pre_ref[...] = cur
    else:
        @pl.when(i % tiles_per_seq == 0)
        def _():
            halo_ref[...] = jnp.zeros_like(halo_ref)

        s1, s2 = _shift_rows(cur, halo_ref[...])
        last2 = cur[cur.shape[0] - 2:, :]
        halo_ref[...] = last2

        @pl.when(i % tiles_per_seq == tiles_per_seq - 1)
        def _():
            st_ref[...] = last2

    o_ref[...] = (proj[:, 0:bn] * _conv3(cw_ref, cur, s1, s2)).astype(BF16)


def _convmix(h, w_in, conv_w, *, bm, bn, seq_len, prev=None):
    m = h.shape[0]
    sample = prev is not None
    tiles_per_seq = max(seq_len // bm, 1)
    n_seq = m // seq_len
    wspec = lambda off: _wcol(bn, lambda j, i: off + j * bn)(D_MODEL)
    tile = pl.BlockSpec((bm, bn), lambda j, i: (i, j))
    in_specs = [
        pl.BlockSpec((bm, D_MODEL), lambda j, i: (i, 0)),
        wspec(OFF_CB), wspec(OFF_CC), wspec(OFF_CU),
        pl.BlockSpec((pl.Squeezed(), 3, bn), lambda j, i: (0, 0, j)),
    ]
    args = [h, w_in, w_in, w_in, conv_w]
    scratch = [pltpu.VMEM((D_MODEL, 3 * bn), BF16)]
    if sample:
        in_specs += [tile, tile]
        args += list(prev)
        out_specs = [tile, tile]
        out_shape = [jax.ShapeDtypeStruct((m, CONV_WIDTH), BF16), jax.ShapeDtypeStruct((m, CONV_WIDTH), F32)]
    else:
        out_specs = [tile, pl.BlockSpec((pl.Squeezed(), 2, bn), lambda j, i: (i // tiles_per_seq, 0, j))]
        out_shape = [jax.ShapeDtypeStruct((m, CONV_WIDTH), BF16),
                     jax.ShapeDtypeStruct((n_seq, 2, CONV_WIDTH), F32)]
        scratch.append(pltpu.VMEM((2, bn), F32))
    return pl.pallas_call(
        functools.partial(_convmix_kernel, tiles_per_seq, sample),
        grid=(CONV_WIDTH // bn, m // bm),
        in_specs=in_specs, out_specs=out_specs, out_shape=out_shape,
        scratch_shapes=scratch,
        compiler_params=_params(2),
        name="convmix",
    )(*args)


def _merge_kernel(h_ref, a_ref, b_ref, wga_ref, wgb_ref, wa_ref, wb_ref, bg_ref, o_ref,
                  wg_bf_ref, wa_bf_ref, wb_bf_ref):
    bn = wa_ref.shape[1]

    @pl.when(pl.program_id(1) == 0)
    def _():
        wg_bf_ref[:, 0:bn] = wga_ref[...].astype(BF16)
        wg_bf_ref[:, bn:2 * bn] = wgb_ref[...].astype(BF16)
        wa_bf_ref[...] = wa_ref[...].astype(BF16)
        wb_bf_ref[...] = wb_ref[...].astype(BF16)

    gates = _dot(h_ref[...], wg_bf_ref[...])
    ga = jax.nn.sigmoid(gates[:, 0:bn] + bg_ref[0:1, :])
    gb = jax.nn.sigmoid(gates[:, bn:2 * bn] + bg_ref[1:2, :])
    merged = ga * _dot(a_ref[...], wa_bf_ref[...]) + gb * _dot(b_ref[...], wb_bf_ref[...])
    o_ref[...] = merged.astype(BF16)


def _merge(h, a, b, w_in, w_attn_branch, w_conv_branch, b_gate, *, bm, bn):
    m = h.shape[0]
    return pl.pallas_call(
        _merge_kernel,
        grid=(D_MODEL // bn, m // bm),
        in_specs=[
            pl.BlockSpec((bm, D_MODEL), lambda j, i: (i, 0)),
            pl.BlockSpec((bm, ATTN_WIDTH), lambda j, i: (i, 0)),
            pl.BlockSpec((bm, CONV_WIDTH), lambda j, i: (i, 0)),
            _wcol(bn, lambda j, i: OFF_GA + j * bn)(D_MODEL),
            _wcol(bn, lambda j, i: OFF_GB + j * bn)(D_MODEL),
            pl.BlockSpec((pl.Squeezed(), ATTN_WIDTH, bn), lambda j, i: (0, 0, j)),
            pl.BlockSpec((pl.Squeezed(), CONV_WIDTH, bn), lambda j, i: (0, 0, j)),
            pl.BlockSpec((pl.Squeezed(), 2, bn), lambda j, i: (0, 0, j)),
        ],
        out_specs=pl.BlockSpec((bm, bn), lambda j, i: (i, j)),
        out_shape=jax.ShapeDtypeStruct((m, D_MODEL), BF16),
        scratch_shapes=[pltpu.VMEM((D_MODEL, 2 * bn), BF16), pltpu.VMEM((ATTN_WIDTH, bn), BF16),
                        pltpu.VMEM((CONV_WIDTH, bn), BF16)],
        compiler_params=_params(2),
        name="merge",
    )(h, a, b, w_in, w_in, w_attn_branch, w_conv_branch, b_gate)


def _oproj_kernel(mg_ref, x_ref, wo_ref, gpost_ref, gpre_ref, x1_ref, h2_ref, wbf_ref):
    @pl.when(pl.program_id(0) == 0)
    def _():
        wbf_ref[...] = wo_ref[...].astype(BF16)

    x1 = x_ref[...] + _rms(_dot(mg_ref[...], wbf_ref[...]), gpost_ref[...])
    x1_ref[...] = x1
    h2_ref[...] = _rms(x1, gpre_ref[...]).astype(BF16)


def _oproj(merged, x, w_o, g_post, g_pre, *, bm):
    m = x.shape[0]
    row = lambda i: (i, 0)
    vec = pl.BlockSpec((1, D_MODEL), lambda i: (0, 0))
    return pl.pallas_call(
        _oproj_kernel,
        grid=(m // bm,),
        in_specs=[
            pl.BlockSpec((bm, D_MODEL), row),
            pl.BlockSpec((bm, D_MODEL), row),
            pl.BlockSpec((pl.Squeezed(), D_MODEL, D_MODEL), lambda i: (0, 0, 0),
                         pipeline_mode=pl.Buffered(1)),
            vec, vec,
        ],
        out_specs=[pl.BlockSpec((bm, D_MODEL), row), pl.BlockSpec((bm, D_MODEL), row)],
        out_shape=[jax.ShapeDtypeStruct((m, D_MODEL), F32), jax.ShapeDtypeStruct((m, D_MODEL), BF16)],
        scratch_shapes=[pltpu.VMEM((D_MODEL, D_MODEL), BF16)],
        compiler_params=_params(1),
        name="oproj_norm",
    )(merged, x, w_o, g_post, g_pre)


def _ffn_up_kernel(tiles_per_seq, sample, h_ref, wa_ref, wg_ref, cw_ref, cb_ref, *rest):
    if sample:
        p2_ref, p1_ref, o_ref, up_ref, wbf_ref = rest
    else:
        o_ref, st_ref, wbf_ref, halo_ref = rest
    i = pl.program_id(1)
    bn = wa_ref.shape[1]

    @pl.when(i == 0)
    def _():
        wbf_ref[:, 0:bn] = wa_ref[...].astype(BF16)
        wbf_ref[:, bn:2 * bn] = wg_ref[...].astype(BF16)

    proj = _dot(h_ref[...], wbf_ref[...])
    cur = proj[:, 0:bn]
    if sample:
        s1, s2 = p1_ref[...], p2_ref[...]
        up_ref[...] = cur
    else:
        @pl.when(i % tiles_per_seq == 0)
        def _():
            halo_ref[...] = jnp.zeros_like(halo_ref)

        s1, s2 = _shift_rows(cur, halo_ref[...])
        last2 = cur[cur.shape[0] - 2:, :]
        halo_ref[...] = last2

        @pl.when(i % tiles_per_seq == tiles_per_seq - 1)
        def _():
            st_ref[...] = last2

    act = jax.nn.gelu(_conv3(cw_ref, cur, s1, s2) + cb_ref[...], approximate=True)
    o_ref[...] = (act * proj[:, bn:2 * bn]).astype(BF16)


def _ffn_up(h2, w_ffn_up, ffn_conv_w, ffn_conv_b, *, bm, bn, seq_len, prev=None):
    m = h2.shape[0]
    sample = prev is not None
    tiles_per_seq = max(seq_len // bm, 1)
    n_seq = m // seq_len
    n_tiles = D_FF // bn
    tile = pl.BlockSpec((bm, bn), lambda j, i: (i, j))
    in_specs = [
        pl.BlockSpec((bm, D_MODEL), lambda j, i: (i, 0)),
        pl.BlockSpec((pl.Squeezed(), D_MODEL, bn), lambda j, i: (0, 0, j)),
        pl.BlockSpec((pl.Squeezed(), D_MODEL, bn), lambda j, i: (0, 0, n_tiles + j)),
        pl.BlockSpec((pl.Squeezed(), 3, bn), lambda j, i: (0, 0, j)),
        pl.BlockSpec((1, bn), lambda j, i: (0, j)),
    ]
    args = [h2, w_ffn_up, w_ffn_up, ffn_conv_w, ffn_conv_b]
    scratch = [pltpu.VMEM((D_MODEL, 2 * bn), BF16)]
    if sample:
        in_specs += [tile, tile]
        args += list(prev)
        out_specs = [tile, tile]
        out_shape = [jax.ShapeDtypeStruct((m, D_FF), BF16), jax.ShapeDtypeStruct((m, D_FF), F32)]
    else:
        out_specs = [tile, pl.BlockSpec((pl.Squeezed(), 2, bn), lambda j, i: (i // tiles_per_seq, 0, j))]
        out_shape = [jax.ShapeDtypeStruct((m, D_FF), BF16), jax.ShapeDtypeStruct((n_seq, 2, D_FF), F32)]
        scratch.append(pltpu.VMEM((2, bn), F32))
    return pl.pallas_call(
        functools.partial(_ffn_up_kernel, tiles_per_seq, sample),
        grid=(n_tiles, m // bm),
        in_specs=in_specs, out_specs=out_specs, out_shape=out_shape,
        scratch_shapes=scratch,
        compiler_params=_params(2),
        name="ffn_up",
    )(*args)


def _ffn_down_kernel(a_ref, w_ref, o_ref, wbf_ref):
    @pl.when(pl.program_id(1) == 0)
    def _():
        wbf_ref[...] = w_ref[...].astype(BF16)

    o_ref[...] = _dot(a_ref[...], wbf_ref[...]).astype(BF16)


def _ffn_down(act, w_ffn_down, *, bm, bn):
    m = act.shape[0]
    return pl.pallas_call(
        _ffn_down_kernel,
        grid=(D_MODEL // bn, m // bm),
        in_specs=[
            pl.BlockSpec((bm, D_FF), lambda j, i: (i, 0)),
            pl.BlockSpec((pl.Squeezed(), D_FF, bn), lambda j, i: (0, 0, j)),
        ],
        out_specs=pl.BlockSpec((bm, bn), lambda j, i: (i, j)),
        out_shape=jax.ShapeDtypeStruct((m, D_MODEL), BF16),
        scratch_shapes=[pltpu.VMEM((D_FF, bn), BF16)],
        compiler_params=_params(2),
        name="ffn_down",
    )(act, w_ffn_down)


def _final_kernel(f_ref, x1_ref, p_ref, g_ref, wg_ref, wp_ref, y_ref, wg_bf_ref, wp_bf_ref):
    @pl.when(pl.program_id(0) == 0)
    def _():
        wg_bf_ref[...] = wg_ref[...].astype(BF16)
        wp_bf_ref[...] = wp_ref[...].astype(BF16)

    x2 = x1_ref[...] + _rms(f_ref[...].astype(F32), g_ref[...])
    gate = jax.nn.sigmoid(_dot(x2.astype(BF16), wg_bf_ref[...]))
    y_ref[...] = x2 + gate * _dot(p_ref[...].astype(BF16), wp_bf_ref[...])


def _final(f, x1, p, g_post, w_ple_gate, w_ple, *, bm):
    m = x1.shape[0]
    row = lambda i: (i, 0)
    return pl.pallas_call(
        _final_kernel,
        grid=(m // bm,),
        in_specs=[
            pl.BlockSpec((bm, D_MODEL), row),
            pl.BlockSpec((bm, D_MODEL), row),
            pl.BlockSpec((bm, PLE_DIM), row),
            pl.BlockSpec((1, D_MODEL), lambda i: (0, 0)),
            pl.BlockSpec((pl.Squeezed(), D_MODEL, D_MODEL), lambda i: (0, 0, 0),
                         pipeline_mode=pl.Buffered(1)),
            pl.BlockSpec((pl.Squeezed(), PLE_DIM, D_MODEL), lambda i: (0, 0, 0),
                         pipeline_mode=pl.Buffered(1)),
        ],
        out_specs=pl.BlockSpec((bm, D_MODEL), row),
        out_shape=jax.ShapeDtypeStruct((m, D_MODEL), F32),
        scratch_shapes=[pltpu.VMEM((D_MODEL, D_MODEL), BF16), pltpu.VMEM((PLE_DIM, D_MODEL), BF16)],
        compiler_params=_params(1),
        name="ffn_norm_ple",
    )(f, x1, p, g_post, w_ple_gate, w_ple)


def _rope_tables(pos):
    half = HEAD_DIM // 2
    inv = ROPE_THETA ** (-jnp.arange(half, dtype=F32) / half)
    ang = pos.astype(F32)[:, None] * inv[None, :]
    cos = jnp.cos(ang)
    sin = jnp.sin(ang)
    reps = LANES // HEAD_DIM
    return (jnp.tile(jnp.concatenate([cos, cos], axis=1), (1, reps)),
            jnp.tile(jnp.concatenate([-sin, sin], axis=1), (1, reps)))


def _layer(x, p, pos, seq_len, state, w, tiles):
    (g_mix_pre, g_mix_post, g_ffn_pre, g_ffn_post, w_in, sinks, conv_w, w_attn_branch, w_conv_branch,
     b_gate, w_o, w_ffn_up, ffn_conv_w, ffn_conv_b, w_ffn_down, w_ple, w_ple_gate) = w
    m = x.shape[0]
    cos, sin = _rope_tables(pos)
    h, q, k, v = _qkv(x, g_mix_pre, w_in, cos, sin, bm=tiles["qkv"])
    if state is None:
        n_seq = m // seq_len
        a = _attn_prompt(sinks.reshape(N_HEADS), q, k, v, n_seq=n_seq, seq_len=seq_len)
        new_k = k.reshape(n_seq, seq_len, N_KV_HEADS, HEAD_DIM)[:, -WINDOW:]
        new_v = v.reshape(n_seq, seq_len, N_KV_HEADS, HEAD_DIM)[:, -WINDOW:]
        b, new_conv = _convmix(h, w_in, conv_w, seq_len=seq_len, **tiles["convmix"])
    else:
        state_k, state_v, state_conv, state_ffn = state
        a3, new_k, new_v = _attn_sample(
            sinks.reshape(N_HEADS, 1), q.reshape(m, N_HEADS, HEAD_DIM), k.reshape(m, 1, KV_WIDTH),
            v.reshape(m, 1, KV_WIDTH), state_k.reshape(m, WINDOW, KV_WIDTH), state_v.reshape(m, WINDOW, KV_WIDTH))
        a = a3.reshape(m, ATTN_WIDTH)
        new_k = new_k.reshape(m, WINDOW, N_KV_HEADS, HEAD_DIM)
        new_v = new_v.reshape(m, WINDOW, N_KV_HEADS, HEAD_DIM)
        b, pre = _convmix(h, w_in, conv_w, seq_len=seq_len, prev=(state_conv[:, 0], state_conv[:, 1]),
                          **tiles["convmix"])
        new_conv = jnp.stack([state_conv[:, 1], pre], axis=1)
    merged = _merge(h, a, b, w_in, w_attn_branch, w_conv_branch, b_gate, **tiles["merge"])
    x1, h2 = _oproj(merged, x, w_o, g_mix_post, g_ffn_pre, bm=tiles["oproj"])
    if state is None:
        act, new_ffn = _ffn_up(h2, w_ffn_up, ffn_conv_w, ffn_conv_b, seq_len=seq_len, **tiles["ffn_up"])
    else:
        act, up = _ffn_up(h2, w_ffn_up, ffn_conv_w, ffn_conv_b, seq_len=seq_len,
                          prev=(state_ffn[:, 0], state_ffn[:, 1]), **tiles["ffn_up"])
        new_ffn = jnp.stack([state_ffn[:, 1], up], axis=1)
    f = _ffn_down(act, w_ffn_down, **tiles["ffn_down"])
    y = _final(f, x1, p, g_ffn_post, w_ple_gate, w_ple, bm=tiles["final"])
    return y, new_k, new_v, new_conv, new_ffn


PROMPT_TILES = dict(qkv=512, convmix=dict(bm=1024, bn=256), merge=dict(bm=512, bn=512), oproj=256,
                    ffn_up=dict(bm=1024, bn=512), ffn_down=dict(bm=512, bn=512), final=256)


def _sample_tiles(m):
    return dict(qkv=m, convmix=dict(bm=m, bn=256), merge=dict(bm=m, bn=512), oproj=m,
                ffn_up=dict(bm=m, bn=512), ffn_down=dict(bm=m, bn=512), final=m)


def kernel(x_prompt, x_sample, state_attn_k, state_attn_v, state_conv, state_ffn_conv, p_prompt, p_sample,
           norm_mix_pre, norm_mix_post, norm_ffn_pre, norm_ffn_post, w_in, attn_sinks, conv_w, w_attn_branch,
           w_conv_branch, b_gate, w_o, w_ffn_up, ffn_conv_w, ffn_conv_b, w_ffn_down, w_ple, w_ple_gate):
    depth = w_in.shape[0]
    assert depth == 1, "weights are indexed with a squeezed depth axis of size 1"
    batch, seq, _ = x_prompt.shape
    dec_batch, dec_seq, _ = x_sample.shape
    assert dec_seq == 1
    w = (norm_mix_pre, norm_mix_post, norm_ffn_pre, norm_ffn_post, w_in, attn_sinks, conv_w, w_attn_branch,
         w_conv_branch, b_gate, w_o, w_ffn_up, ffn_conv_w, ffn_conv_b, w_ffn_down, w_ple, w_ple_gate)

    yp, kp, vp, cp, fp = _layer(
        x_prompt.reshape(batch * seq, D_MODEL), p_prompt[0].reshape(batch * seq, PLE_DIM),
        jnp.arange(seq, dtype=jnp.int32), seq, None, w, PROMPT_TILES)
    ys, ks, vs, cs, fs = _layer(
        x_sample.reshape(dec_batch, D_MODEL), p_sample[0].reshape(dec_batch, PLE_DIM),
        jnp.full((dec_batch,), PAST_LEN, dtype=jnp.int32), 1,
        (state_attn_k[0], state_attn_v[0], state_conv[0], state_ffn_conv[0]), w, _sample_tiles(dec_batch))
    return (yp.reshape(batch, seq, D_MODEL), ys.reshape(dec_batch, dec_seq, D_MODEL),
            kp[None], vp[None], cp[None], fp[None], ks[None], vs[None], cs[None], fs[None])
```

```python
import functools

import jax
import jax.numpy as jnp
from jax import lax
from jax.experimental import pallas as pl
from jax.experimental.pallas import tpu as pltpu

D_MODEL = 2048
N_HEADS = 16
N_KV_HEADS = 2
HEAD_DIM = 64
GROUP = N_HEADS // N_KV_HEADS
ATTN_WIDTH = N_HEADS * HEAD_DIM
KV_WIDTH = N_KV_HEADS * HEAD_DIM
QKV_WIDTH = ATTN_WIDTH + 2 * KV_WIDTH
WINDOW = 128
PAST_LEN = 16384
ROPE_THETA = 10000.0
CONV_WIDTH = D_MODEL // 2
D_FF = 5632
PLE_DIM = 256
RMS_EPS = 1e-6
OFF_CB = QKV_WIDTH
OFF_CC = OFF_CB + CONV_WIDTH
OFF_CU = OFF_CC + CONV_WIDTH
OFF_GA = OFF_CU + CONV_WIDTH
OFF_GB = OFF_GA + D_MODEL

LANES = 128
MASKED = -1e30
VMEM_LIMIT = 56 * 1024 * 1024

BF16 = jnp.bfloat16
F32 = jnp.float32


def _params(n_axes):
    return pltpu.CompilerParams(dimension_semantics=("arbitrary",) * n_axes,
                                vmem_limit_bytes=VMEM_LIMIT)


def _rms(x, g):
    return x * lax.rsqrt(jnp.mean(x * x, axis=-1, keepdims=True) + RMS_EPS) * g


def _dot(a, b):
    return jnp.dot(a, b, preferred_element_type=F32)


ROW_CHUNK = 256


def _row_chunks(bm):
    rc = min(bm, ROW_CHUNK)
    return [pl.ds(c * rc, rc) for c in range(bm // rc)]


def _wcol(block_cols, col_fn):
    def spec(k_rows):
        return pl.BlockSpec((pl.Squeezed(), pl.Element(k_rows), pl.Element(block_cols)),
                            lambda *g: (0, 0, pl.multiple_of(col_fn(*g), LANES)))
    return spec


def _qkv_kernel(x_ref, g_ref, w_ref, cos_ref, sin_ref, h_ref, q_ref, k_ref, v_ref, wbf_ref):
    @pl.when(pl.program_id(0) == 0)
    def _():
        wbf_ref[...] = w_ref[...].astype(BF16)

    n_q = ATTN_WIDTH // LANES
    for rows in _row_chunks(x_ref.shape[0]):
        h = _rms(x_ref[rows, :], g_ref[...]).astype(BF16)
        h_ref[rows, :] = h
        proj = _dot(h, wbf_ref[...])
        cos = cos_ref[rows, :]
        sin = sin_ref[rows, :]
        lane = lax.broadcasted_iota(jnp.int32, cos.shape, 1)
        first_half = (lane % HEAD_DIM) < (HEAD_DIM // 2)
        for c in range(n_q + 1):
            xg = proj[:, c * LANES:(c + 1) * LANES]
            partner = jnp.where(first_half, pltpu.roll(xg, LANES - HEAD_DIM // 2, 1),
                                pltpu.roll(xg, HEAD_DIM // 2, 1))
            yg = xg * cos + partner * sin
            if c < n_q:
                q_ref[rows, c * LANES:(c + 1) * LANES] = (yg * (HEAD_DIM ** -0.5)).astype(BF16)
            else:
                k_ref[rows, :] = yg
        v_ref[rows, :] = proj[:, ATTN_WIDTH + KV_WIDTH:QKV_WIDTH]


def _qkv(x, g, w_in, cos, sin, *, bm):
    m = x.shape[0]
    t_tiles = cos.shape[0] // bm
    return pl.pallas_call(
        _qkv_kernel,
        grid=(m // bm,),
        in_specs=[
            pl.BlockSpec((bm, D_MODEL), lambda i: (i, 0)),
            pl.BlockSpec((1, D_MODEL), lambda i: (0, 0)),
            pl.BlockSpec((pl.Squeezed(), D_MODEL, QKV_WIDTH), lambda i: (0, 0, 0),
                         pipeline_mode=pl.Buffered(1)),
            pl.BlockSpec((bm, LANES), lambda i: (i % t_tiles, 0)),
            pl.BlockSpec((bm, LANES), lambda i: (i % t_tiles, 0)),
        ],
        out_specs=[
            pl.BlockSpec((bm, D_MODEL), lambda i: (i, 0)),
            pl.BlockSpec((bm, ATTN_WIDTH), lambda i: (i, 0)),
            pl.BlockSpec((bm, KV_WIDTH), lambda i: (i, 0)),
            pl.BlockSpec((bm, KV_WIDTH), lambda i: (i, 0)),
        ],
        out_shape=[
            jax.ShapeDtypeStruct((m, D_MODEL), BF16),
            jax.ShapeDtypeStruct((m, ATTN_WIDTH), BF16),
            jax.ShapeDtypeStruct((m, KV_WIDTH), F32),
            jax.ShapeDtypeStruct((m, KV_WIDTH), F32),
        ],
        scratch_shapes=[pltpu.VMEM((D_MODEL, QKV_WIDTH), BF16)],
        compiler_params=_params(1),
        name="qkv_rope",
    )(x, g, w_in, cos, sin)


def _softmax_pv(s, sink, v):
    m = jnp.maximum(jnp.max(s, axis=-1, keepdims=True), sink)
    p = jnp.exp(s - m)
    den = jnp.sum(p, axis=-1, keepdims=True) + jnp.exp(sink - m)
    return _dot(p.astype(BF16), v) / den


def _attn_prompt_kernel(sinks_ref, q_ref, kc_ref, kp_ref, vc_ref, vp_ref, o_ref):
    blk = pl.program_id(1)
    qi = lax.broadcasted_iota(jnp.int32, (WINDOW, 2 * WINDOW), 0)
    ci = lax.broadcasted_iota(jnp.int32, (WINDOW, 2 * WINDOW), 1)
    rel = WINDOW + qi - ci
    mask = (rel >= 0) & (rel < WINDOW) & ((blk > 0) | (ci >= WINDOW))
    for kh in range(N_KV_HEADS):
        cols = slice(kh * HEAD_DIM, (kh + 1) * HEAD_DIM)
        k = jnp.concatenate([kp_ref[:, cols], kc_ref[:, cols]], axis=0).astype(BF16)
        v = jnp.concatenate([vp_ref[:, cols], vc_ref[:, cols]], axis=0).astype(BF16)
        for g in range(GROUP):
            head = kh * GROUP + g
            hcols = slice(head * HEAD_DIM, (head + 1) * HEAD_DIM)
            s = lax.dot_general(q_ref[:, hcols], k, (((1,), (1,)), ((), ())),
                                preferred_element_type=F32)
            s = jnp.where(mask, s, MASKED)
            o_ref[:, hcols] = _softmax_pv(s, sinks_ref[head], v).astype(BF16)


def _attn_prompt(sinks, q, k, v, *, n_seq, seq_len):
    nb = seq_len // WINDOW
    cur = lambda n, b: (n * nb + b, 0)
    prev = lambda n, b: (jnp.maximum(n * nb + b - 1, 0), 0)
    return pl.pallas_call(
        _attn_prompt_kernel,
        grid=(n_seq, nb),
        in_specs=[
            pl.BlockSpec(memory_space=pltpu.SMEM),
            pl.BlockSpec((WINDOW, ATTN_WIDTH), cur),
            pl.BlockSpec((WINDOW, KV_WIDTH), cur),
            pl.BlockSpec((WINDOW, KV_WIDTH), prev),
            pl.BlockSpec((WINDOW, KV_WIDTH), cur),
            pl.BlockSpec((WINDOW, KV_WIDTH), prev),
        ],
        out_specs=pl.BlockSpec((WINDOW, ATTN_WIDTH), cur),
        out_shape=jax.ShapeDtypeStruct((n_seq * seq_len, ATTN_WIDTH), BF16),
        compiler_params=_params(2),
        name="attn_prompt",
    )(sinks, q, k, k, v, v)


def _attn_sample_kernel(sinks_ref, q_ref, kn_ref, vn_ref, sk_ref, sv_ref, a_ref, nk_ref, nv_ref):
    nk_ref[:, 0:WINDOW - 1, :] = sk_ref[:, 1:WINDOW, :]
    nk_ref[:, WINDOW - 1:WINDOW, :] = kn_ref[...]
    nv_ref[:, 0:WINDOW - 1, :] = sv_ref[:, 1:WINDOW, :]
    nv_ref[:, WINDOW - 1:WINDOW, :] = vn_ref[...]
    for kh in range(N_KV_HEADS):
        cols = slice(kh * HEAD_DIM, (kh + 1) * HEAD_DIM)
        heads = slice(kh * GROUP, (kh + 1) * GROUP)
        k = nk_ref[:, :, cols].astype(BF16)
        v = nv_ref[:, :, cols].astype(BF16)
        s = jnp.einsum("ngd,ncd->ngc", q_ref[:, heads, :], k, preferred_element_type=F32)
        sink = sinks_ref[heads, :][None]
        m = jnp.maximum(jnp.max(s, axis=-1, keepdims=True), sink)
        p = jnp.exp(s - m)
        den = jnp.sum(p, axis=-1, keepdims=True) + jnp.exp(sink - m)
        o = jnp.einsum("ngc,ncd->ngd", p.astype(BF16), v, preferred_element_type=F32) / den
        a_ref[:, heads, :] = o.astype(BF16)


def _attn_sample(sinks_col, q3, k_new3, v_new3, state_k, state_v):
    n = q3.shape[0]
    return pl.pallas_call(
        _attn_sample_kernel,
        out_shape=[
            jax.ShapeDtypeStruct((n, N_HEADS, HEAD_DIM), BF16),
            jax.ShapeDtypeStruct((n, WINDOW, KV_WIDTH), F32),
            jax.ShapeDtypeStruct((n, WINDOW, KV_WIDTH), F32),
        ],
        compiler_params=pltpu.CompilerParams(vmem_limit_bytes=VMEM_LIMIT),
        name="attn_sample",
    )(sinks_col, q3, k_new3, v_new3, state_k, state_v)


def _shift_rows(cur, halo):
    rid = lax.broadcasted_iota(jnp.int32, cur.shape, 0)
    h0 = halo[0:1, :]
    h1 = halo[1:2, :]
    s1 = jnp.where(rid == 0, h1, pltpu.roll(cur, 1, 0))
    s2 = jnp.where(rid == 0, h0, jnp.where(rid == 1, h1, pltpu.roll(cur, 2, 0)))
    return s1, s2


def _conv3(cw_ref, cur, s1, s2):
    return cw_ref[0:1, :] * s2 + cw_ref[1:2, :] * s1 + cw_ref[2:3, :] * cur


def _convmix_kernel(tiles_per_seq, sample, h_ref, wb_ref, wc_ref, wu_ref, cw_ref, *rest):
    if sample:
        p2_ref, p1_ref, o_ref, pre_ref, wbf_ref = rest
    else:
        o_ref, st_ref, wbf_ref, halo_ref = rest
    i = pl.program_id(1)
    bn = wb_ref.shape[1]

    @pl.when(i == 0)
    def _():
        wbf_ref[:, 0:bn] = wb_ref[...].astype(BF16)
        wbf_ref[:, bn:2 * bn] = wc_ref[...].astype(BF16)
        wbf_ref[:, 2 * bn:3 * bn] = wu_ref[...].astype(BF16)

    if not sample:
        @pl.when(i % tiles_per_seq == 0)
        def _():
            halo_ref[...] = jnp.zeros_like(halo_ref)

        halo = halo_ref[...]

    for rows in _row_chunks(h_ref.shape[0]):
        proj = _dot(h_ref[rows, :], wbf_ref[...])
        cur = proj[:, bn:2 * bn] * proj[:, 2 * bn:3 * bn]
        if sample:
            s1, s2 = p1_ref[rows, :], p2_ref[rows, :]
            pre_ref[rows, :] = cur
        else:
            s1, s2 = _shift_rows(cur, halo)
            halo = cur[cur.shape[0] - 2:, :]
        o_ref[rows, :] = (proj[:, 0:bn] * _conv3(cw_ref, cur, s1, s2)).astype(BF16)

    if not sample:
        halo_ref[...] = halo

        @pl.when(i % tiles_per_seq == tiles_per_seq - 1)
        def _():
            st_ref[...] = halo


def _convmix(h, w_in, conv_w, *, bm, bn, seq_len, prev=None):
    m = h.shape[0]
    sample = prev is not None
    tiles_per_seq = max(seq_len // bm, 1)
    n_seq = m // seq_len
    wspec = lambda off: _wcol(bn, lambda j, i: off + j * bn)(D_MODEL)
    tile = pl.BlockSpec((bm, bn), lambda j, i: (i, j))
    in_specs = [
        pl.BlockSpec((bm, D_MODEL), lambda j, i: (i, 0)),
        wspec(OFF_CB), wspec(OFF_CC), wspec(OFF_CU),
        pl.BlockSpec((pl.Squeezed(), 3, bn), lambda j, i: (0, 0, j)),
    ]
    args = [h, w_in, w_in, w_in, conv_w]
    scratch = [pltpu.VMEM((D_MODEL, 3 * bn), BF16)]
    if sample:
        in_specs += [tile, tile]
        args += list(prev)
        out_specs = [tile, tile]
        out_shape = [jax.ShapeDtypeStruct((m, CONV_WIDTH), BF16), jax.ShapeDtypeStruct((m, CONV_WIDTH), F32)]
    else:
        out_specs = [tile, pl.BlockSpec((pl.Squeezed(), 2, bn), lambda j, i: (i // tiles_per_seq, 0, j))]
        out_shape = [jax.ShapeDtypeStruct((m, CONV_WIDTH), BF16),
                     jax.ShapeDtypeStruct((n_seq, 2, CONV_WIDTH), F32)]
        scratch.append(pltpu.VMEM((2, bn), F32))
    return pl.pallas_call(
        functools.partial(_convmix_kernel, tiles_per_seq, sample),
        grid=(CONV_WIDTH // bn, m // bm),
        in_specs=in_specs, out_specs=out_specs, out_shape=out_shape,
        scratch_shapes=scratch,
        compiler_params=_params(2),
        name="convmix",
    )(*args)


def _merge_kernel(h_ref, a_ref, b_ref, wga_ref, wgb_ref, wa_ref, wb_ref, bg_ref, o_ref,
                  wg_bf_ref, wa_bf_ref, wb_bf_ref):
    bn = wa_ref.shape[1]

    @pl.when(pl.program_id(1) == 0)
    def _():
        wg_bf_ref[:, 0:bn] = wga_ref[...].astype(BF16)
        wg_bf_ref[:, bn:2 * bn] = wgb_ref[...].astype(BF16)
        wa_bf_ref[...] = wa_ref[...].astype(BF16)
        wb_bf_ref[...] = wb_ref[...].astype(BF16)

    for rows in _row_chunks(h_ref.shape[0]):
        gates = _dot(h_ref[rows, :], wg_bf_ref[...])
        ga = jax.nn.sigmoid(gates[:, 0:bn] + bg_ref[0:1, :])
        gb = jax.nn.sigmoid(gates[:, bn:2 * bn] + bg_ref[1:2, :])
        merged = ga * _dot(a_ref[rows, :], wa_bf_ref[...]) + gb * _dot(b_ref[rows, :], wb_bf_ref[...])
        o_ref[rows, :] = merged.astype(BF16)


def _merge(h, a, b, w_in, w_attn_branch, w_conv_branch, b_gate, *, bm, bn):
    m = h.shape[0]
    return pl.pallas_call(
        _merge_kernel,
        grid=(D_MODEL // bn, m // bm),
        in_specs=[
            pl.BlockSpec((bm, D_MODEL), lambda j, i: (i, 0)),
            pl.BlockSpec((bm, ATTN_WIDTH), lambda j, i: (i, 0)),
            pl.BlockSpec((bm, CONV_WIDTH), lambda j, i: (i, 0)),
            _wcol(bn, lambda j, i: OFF_GA + j * bn)(D_MODEL),
            _wcol(bn, lambda j, i: OFF_GB + j * bn)(D_MODEL),
            pl.BlockSpec((pl.Squeezed(), ATTN_WIDTH, bn), lambda j, i: (0, 0, j)),
            pl.BlockSpec((pl.Squeezed(), CONV_WIDTH, bn), lambda j, i: (0, 0, j)),
            pl.BlockSpec((pl.Squeezed(), 2, bn), lambda j, i: (0, 0, j)),
        ],
        out_specs=pl.BlockSpec((bm, bn), lambda j, i: (i, j)),
        out_shape=jax.ShapeDtypeStruct((m, D_MODEL), BF16),
        scratch_shapes=[pltpu.VMEM((D_MODEL, 2 * bn), BF16), pltpu.VMEM((ATTN_WIDTH, bn), BF16),
                        pltpu.VMEM((CONV_WIDTH, bn), BF16)],
        compiler_params=_params(2),
        name="merge",
    )(h, a, b, w_in, w_in, w_attn_branch, w_conv_branch, b_gate)


def _oproj_kernel(mg_ref, x_ref, wo_ref, gpost_ref, gpre_ref, x1_ref, h2_ref):
    for rows in _row_chunks(x_ref.shape[0]):
        x1 = x_ref[rows, :] + _rms(_dot(mg_ref[rows, :], wo_ref[...]), gpost_ref[...])
        x1_ref[rows, :] = x1
        h2_ref[rows, :] = _rms(x1, gpre_ref[...]).astype(BF16)


def _oproj(merged, x, w_o, g_post, g_pre, *, bm):
    m = x.shape[0]
    row = lambda i: (i, 0)
    vec = pl.BlockSpec((1, D_MODEL), lambda i: (0, 0))
    return pl.pallas_call(
        _oproj_kernel,
        grid=(m // bm,),
        in_specs=[
            pl.BlockSpec((bm, D_MODEL), row),
            pl.BlockSpec((bm, D_MODEL), row),
            pl.BlockSpec((pl.Squeezed(), D_MODEL, D_MODEL), lambda i: (0, 0, 0),
                         pipeline_mode=pl.Buffered(1)),
            vec, vec,
        ],
        out_specs=[pl.BlockSpec((bm, D_MODEL), row), pl.BlockSpec((bm, D_MODEL), row)],
        out_shape=[jax.ShapeDtypeStruct((m, D_MODEL), F32), jax.ShapeDtypeStruct((m, D_MODEL), BF16)],
        compiler_params=_params(1),
        name="oproj_norm",
    )(merged, x, w_o, g_post, g_pre)


def _ffn_up_kernel(tiles_per_seq, sample, h_ref, wa_ref, wg_ref, cw_ref, cb_ref, *rest):
    if sample:
        p2_ref, p1_ref, o_ref, up_ref, wbf_ref = rest
    else:
        o_ref, st_ref, wbf_ref, halo_ref = rest
    i = pl.program_id(1)
    bn = wa_ref.shape[1]

    @pl.when(i == 0)
    def _():
        wbf_ref[:, 0:bn] = wa_ref[...].astype(BF16)
        wbf_ref[:, bn:2 * bn] = wg_ref[...].astype(BF16)

    if not sample:
        @pl.when(i % tiles_per_seq == 0)
        def _():
            halo_ref[...] = jnp.zeros_like(halo_ref)

        halo = halo_ref[...]

    for rows in _row_chunks(h_ref.shape[0]):
        proj = _dot(h_ref[rows, :], wbf_ref[...])
        cur = proj[:, 0:bn]
        if sample:
            s1, s2 = p1_ref[rows, :], p2_ref[rows, :]
            up_ref[rows, :] = cur
        else:
            s1, s2 = _shift_rows(cur, halo)
            halo = cur[cur.shape[0] - 2:, :]
        act = jax.nn.gelu(_conv3(cw_ref, cur, s1, s2) + cb_ref[...], approximate=True)
        o_ref[rows, :] = (act * proj[:, bn:2 * bn]).astype(BF16)

    if not sample:
        halo_ref[...] = halo

        @pl.when(i % tiles_per_seq == tiles_per_seq - 1)
        def _():
            st_ref[...] = halo


def _ffn_up(h2, w_ffn_up, ffn_conv_w, ffn_conv_b, *, bm, bn, seq_len, prev=None):
    m = h2.shape[0]
    sample = prev is not None
    tiles_per_seq = max(seq_len // bm, 1)
    n_seq = m // seq_len
    n_tiles = D_FF // bn
    tile = pl.BlockSpec((bm, bn), lambda j, i: (i, j))
    in_specs = [
        pl.BlockSpec((bm, D_MODEL), lambda j, i: (i, 0)),
        pl.BlockSpec((pl.Squeezed(), D_MODEL, bn), lambda j, i: (0, 0, j)),
        pl.BlockSpec((pl.Squeezed(), D_MODEL, bn), lambda j, i: (0, 0, n_tiles + j)),
        pl.BlockSpec((pl.Squeezed(), 3, bn), lambda j, i: (0, 0, j)),
        pl.BlockSpec((1, bn), lambda j, i: (0, j)),
    ]
    args = [h2, w_ffn_up, w_ffn_up, ffn_conv_w, ffn_conv_b]
    scratch = [pltpu.VMEM((D_MODEL, 2 * bn), BF16)]
    if sample:
        in_specs += [tile, tile]
        args += list(prev)
        out_specs = [tile, tile]
        out_shape = [jax.ShapeDtypeStruct((m, D_FF), BF16), jax.ShapeDtypeStruct((m, D_FF), F32)]
    else:
        out_specs = [tile, pl.BlockSpec((pl.Squeezed(), 2, bn), lambda j, i: (i // tiles_per_seq, 0, j))]
        out_shape = [jax.ShapeDtypeStruct((m, D_FF), BF16), jax.ShapeDtypeStruct((n_seq, 2, D_FF), F32)]
        scratch.append(pltpu.VMEM((2, bn), F32))
    return pl.pallas_call(
        functools.partial(_ffn_up_kernel, tiles_per_seq, sample),
        grid=(n_tiles, m // bm),
        in_specs=in_specs, out_specs=out_specs, out_shape=out_shape,
        scratch_shapes=scratch,
        compiler_params=_params(2),
        name="ffn_up",
    )(*args)


def _ffn_down_kernel(a_ref, w_ref, o_ref, wbf_ref):
    @pl.when(pl.program_id(1) == 0)
    def _():
        wbf_ref[...] = w_ref[...].astype(BF16)

    o_ref[...] = _dot(a_ref[...], wbf_ref[...]).astype(BF16)


def _ffn_down(act, w_ffn_down, *, bm, bn):
    m = act.shape[0]
    return pl.pallas_call(
        _ffn_down_kernel,
        grid=(D_MODEL // bn, m // bm),
        in_specs=[
            pl.BlockSpec((bm, D_FF), lambda j, i: (i, 0)),
            pl.BlockSpec((pl.Squeezed(), D_FF, bn), lambda j, i: (0, 0, j)),
        ],
        out_specs=pl.BlockSpec((bm, bn), lambda j, i: (i, j)),
        out_shape=jax.ShapeDtypeStruct((m, D_MODEL), BF16),
        scratch_shapes=[pltpu.VMEM((D_FF, bn), BF16)],
        compiler_params=_params(2),
        name="ffn_down",
    )(act, w_ffn_down)


def _final_kernel(f_ref, x1_ref, p_ref, g_ref, wg_ref, wp_ref, y_ref):
    for rows in _row_chunks(x1_ref.shape[0]):
        x2 = x1_ref[rows, :] + _rms(f_ref[rows, :].astype(F32), g_ref[...])
        gate = jax.nn.sigmoid(_dot(x2.astype(BF16), wg_ref[...]))
        y_ref[rows, :] = x2 + gate * _dot(p_ref[rows, :].astype(BF16), wp_ref[...])


def _final(f, x1, p, g_post, w_ple_gate, w_ple, *, bm):
    m = x1.shape[0]
    row = lambda i: (i, 0)
    return pl.pallas_call(
        _final_kernel,
        grid=(m // bm,),
        in_specs=[
            pl.BlockSpec((bm, D_MODEL), row),
            pl.BlockSpec((bm, D_MODEL), row),
            pl.BlockSpec((bm, PLE_DIM), row),
            pl.BlockSpec((1, D_MODEL), lambda i: (0, 0)),
            pl.BlockSpec((pl.Squeezed(), D_MODEL, D_MODEL), lambda i: (0, 0, 0),
                         pipeline_mode=pl.Buffered(1)),
            pl.BlockSpec((pl.Squeezed(), PLE_DIM, D_MODEL), lambda i: (0, 0, 0),
                         pipeline_mode=pl.Buffered(1)),
        ],
        out_specs=pl.BlockSpec((bm, D_MODEL), row),
        out_shape=jax.ShapeDtypeStruct((m, D_MODEL), F32),
        compiler_params=_params(1),
        name="ffn_norm_ple",
    )(f, x1, p, g_post, w_ple_gate, w_ple)


def _rope_tables(pos):
    half = HEAD_DIM // 2
    inv = ROPE_THETA ** (-jnp.arange(half, dtype=F32) / half)
    ang = pos.astype(F32)[:, None] * inv[None, :]
    cos = jnp.cos(ang)
    sin = jnp.sin(ang)
    reps = LANES // HEAD_DIM
    return (jnp.tile(jnp.concatenate([cos, cos], axis=1), (1, reps)),
            jnp.tile(jnp.concatenate([-sin, sin], axis=1), (1, reps)))


def _layer(x, p, pos, seq_len, state, w, tiles):
    (g_mix_pre, g_mix_post, g_ffn_pre, g_ffn_post, w_in, sinks, conv_w, w_attn_branch, w_conv_branch,
     b_gate, w_o, w_ffn_up, ffn_conv_w, ffn_conv_b, w_ffn_down, w_ple, w_ple_gate) = w
    m = x.shape[0]
    cos, sin = _rope_tables(pos)
    h, q, k, v = _qkv(x, g_mix_pre, w_in, cos, sin, bm=tiles["qkv"])
    if state is None:
        n_seq = m // seq_len
        a = _attn_prompt(sinks.reshape(N_HEADS), q, k, v, n_seq=n_seq, seq_len=seq_len)
        new_k = k.reshape(n_seq, seq_len, N_KV_HEADS, HEAD_DIM)[:, -WINDOW:]
        new_v = v.reshape(n_seq, seq_len, N_KV_HEADS, HEAD_DIM)[:, -WINDOW:]
        b, new_conv = _convmix(h, w_in, conv_w, seq_len=seq_len, **tiles["convmix"])
    else:
        state_k, state_v, state_conv, state_ffn = state
        a3, new_k, new_v = _attn_sample(
            sinks.reshape(N_HEADS, 1), q.reshape(m, N_HEADS, HEAD_DIM), k.reshape(m, 1, KV_WIDTH),
            v.reshape(m, 1, KV_WIDTH), state_k.reshape(m, WINDOW, KV_WIDTH), state_v.reshape(m, WINDOW, KV_WIDTH))
        a = a3.reshape(m, ATTN_WIDTH)
        new_k = new_k.reshape(m, WINDOW, N_KV_HEADS, HEAD_DIM)
        new_v = new_v.reshape(m, WINDOW, N_KV_HEADS, HEAD_DIM)
        b, pre = _convmix(h, w_in, conv_w, seq_len=seq_len, prev=(state_conv[:, 0], state_conv[:, 1]),
                          **tiles["convmix"])
        new_conv = jnp.stack([state_conv[:, 1], pre], axis=1)
    merged = _merge(h, a, b, w_in, w_attn_branch, w_conv_branch, b_gate, **tiles["merge"])
    x1, h2 = _oproj(merged, x, w_o, g_mix_post, g_ffn_pre, bm=tiles["oproj"])
    if state is None:
        act, new_ffn = _ffn_up(h2, w_ffn_up, ffn_conv_w, ffn_conv_b, seq_len=seq_len, **tiles["ffn_up"])
    else:
        act, up = _ffn_up(h2, w_ffn_up, ffn_conv_w, ffn_conv_b, seq_len=seq_len,
                          prev=(state_ffn[:, 0], state_ffn[:, 1]), **tiles["ffn_up"])
        new_ffn = jnp.stack([state_ffn[:, 1], up], axis=1)
    f = _ffn_down(act, w_ffn_down, **tiles["ffn_down"])
    y = _final(f, x1, p, g_ffn_post, w_ple_gate, w_ple, bm=tiles["final"])
    return y, new_k, new_v, new_conv, new_ffn


PROMPT_TILES = dict(qkv=512, convmix=dict(bm=1024, bn=256), merge=dict(bm=512, bn=512), oproj=512,
                    ffn_up=dict(bm=1024, bn=512), ffn_down=dict(bm=512, bn=512), final=512)


def _sample_tiles(m):
    return dict(qkv=m, convmix=dict(bm=m, bn=256), merge=dict(bm=m, bn=512), oproj=m,
                ffn_up=dict(bm=m, bn=512), ffn_down=dict(bm=m, bn=512), final=m)


def kernel(x_prompt, x_sample, state_attn_k, state_attn_v, state_conv, state_ffn_conv, p_prompt, p_sample,
           norm_mix_pre, norm_mix_post, norm_ffn_pre, norm_ffn_post, w_in, attn_sinks, conv_w, w_attn_branch,
           w_conv_branch, b_gate, w_o, w_ffn_up, ffn_conv_w, ffn_conv_b, w_ffn_down, w_ple, w_ple_gate):
    depth = w_in.shape[0]
    assert depth == 1, "weights are indexed with a squeezed depth axis of size 1"
    batch, seq, _ = x_prompt.shape
    dec_batch, dec_seq, _ = x_sample.shape
    assert dec_seq == 1
    w = (norm_mix_pre, norm_mix_post, norm_ffn_pre, norm_ffn_post, w_in, attn_sinks, conv_w, w_attn_branch,
         w_conv_branch, b_gate, w_o.astype(BF16), w_ffn_up, ffn_conv_w, ffn_conv_b, w_ffn_down,
         w_ple.astype(BF16), w_ple_gate.astype(BF16))

    yp, kp, vp, cp, fp = _layer(
        x_prompt.reshape(batch * seq, D_MODEL), p_prompt[0].reshape(batch * seq, PLE_DIM),
        jnp.arange(seq, dtype=jnp.int32), seq, None, w, PROMPT_TILES)
    ys, ks, vs, cs, fs = _layer(
        x_sample.reshape(dec_batch, D_MODEL), p_sample[0].reshape(dec_batch, PLE_DIM),
        jnp.full((dec_batch,), PAST_LEN, dtype=jnp.int32), 1,
        (state_attn_k[0], state_attn_v[0], state_conv[0], state_ffn_conv[0]), w, _sample_tiles(dec_batch))
    return (yp.reshape(batch, seq, D_MODEL), ys.reshape(dec_batch, dec_seq, D_MODEL),
            kp[None], vp[None], cp[None], fp[None], ks[None], vs[None], cs[None], fs[None])
```

```python
import functools

import jax
import jax.numpy as jnp
from jax import lax
from jax.experimental import pallas as pl
from jax.experimental.pallas import tpu as pltpu

D_MODEL = 2048
N_HEADS = 16
N_KV_HEADS = 2
HEAD_DIM = 64
GROUP = N_HEADS // N_KV_HEADS
ATTN_WIDTH = N_HEADS * HEAD_DIM
KV_WIDTH = N_KV_HEADS * HEAD_DIM
QKV_WIDTH = ATTN_WIDTH + 2 * KV_WIDTH
WINDOW = 128
PAST_LEN = 16384
ROPE_THETA = 10000.0
CONV_WIDTH = D_MODEL // 2
D_FF = 5632
PLE_DIM = 256
RMS_EPS = 1e-6
OFF_CB = QKV_WIDTH
OFF_CC = OFF_CB + CONV_WIDTH
OFF_CU = OFF_CC + CONV_WIDTH
OFF_GA = OFF_CU + CONV_WIDTH
OFF_GB = OFF_GA + D_MODEL

LANES = 128
MASKED = -1e30
VMEM_LIMIT = 56 * 1024 * 1024
ROW_CHUNK = 256

BF16 = jnp.bfloat16
F32 = jnp.float32


def _params(n_axes):
    return pltpu.CompilerParams(dimension_semantics=("arbitrary",) * n_axes,
                                vmem_limit_bytes=VMEM_LIMIT)


def _rms(x, g):
    return x * lax.rsqrt(jnp.mean(x * x, axis=-1, keepdims=True) + RMS_EPS) * g


def _dot(a, b):
    return jnp.dot(a, b, preferred_element_type=F32)


def _row_chunks(bm):
    rc = min(bm, ROW_CHUNK)
    return [pl.ds(c * rc, rc) for c in range(bm // rc)]


def _wcol(k_rows, block_cols, col_fn):
    return pl.BlockSpec((pl.Squeezed(), pl.Element(k_rows), pl.Element(block_cols)),
                        lambda *g: (0, 0, pl.multiple_of(col_fn(*g), LANES)))


def _resident(arr):
    zeros = (0,) * arr.ndim
    return pl.BlockSpec(arr.shape, lambda *g: zeros)


def _sds(shape, dtype):
    return jax.ShapeDtypeStruct(shape, dtype)


def _qkv_rows(x_ref, g_ref, wbf_ref, cos_ref, sin_ref, h_ref, q_ref, k_ref, v_ref):
    n_q = ATTN_WIDTH // LANES
    for rows in _row_chunks(x_ref.shape[0]):
        h = _rms(x_ref[rows, :], g_ref[...]).astype(BF16)
        h_ref[rows, :] = h
        proj = _dot(h, wbf_ref[...])
        cos = cos_ref[rows, :]
        sin = sin_ref[rows, :]
        lane = lax.broadcasted_iota(jnp.int32, cos.shape, 1)
        first_half = (lane % HEAD_DIM) < (HEAD_DIM // 2)
        for c in range(n_q + 1):
            xg = proj[:, c * LANES:(c + 1) * LANES]
            partner = jnp.where(first_half, pltpu.roll(xg, LANES - HEAD_DIM // 2, 1),
                                pltpu.roll(xg, HEAD_DIM // 2, 1))
            yg = xg * cos + partner * sin
            if c < n_q:
                q_ref[rows, c * LANES:(c + 1) * LANES] = (yg * (HEAD_DIM ** -0.5)).astype(BF16)
            else:
                k_ref[rows, :] = yg
        v_ref[rows, :] = proj[:, ATTN_WIDTH + KV_WIDTH:QKV_WIDTH]


def _qkv_kernel(n_m, x_ref, xs_ref, g_ref, w_ref, cos_ref, sin_ref, coss_ref, sins_ref,
                h_ref, q_ref, k_ref, v_ref, hs_ref, qs_ref, ks_ref, vs_ref, wbf_ref):
    i = pl.program_id(0)

    @pl.when(i == 0)
    def _():
        wbf_ref[...] = w_ref[...].astype(BF16)

    @pl.when(i < n_m)
    def _():
        _qkv_rows(x_ref, g_ref, wbf_ref, cos_ref, sin_ref, h_ref, q_ref, k_ref, v_ref)

    @pl.when(i == n_m)
    def _():
        _qkv_rows(xs_ref, g_ref, wbf_ref, coss_ref, sins_ref, hs_ref, qs_ref, ks_ref, vs_ref)


def _qkv(x, xs, g, w_in, rope_p, rope_s, *, bm):
    m, ms = x.shape[0], xs.shape[0]
    n_m = m // bm
    t_tiles = rope_p[0].shape[0] // bm
    row = lambda i: (jnp.minimum(i, n_m - 1), 0)
    pos = lambda i: (jnp.minimum(i, n_m - 1) % t_tiles, 0)
    outs = lambda rows: [_sds((rows, D_MODEL), BF16), _sds((rows, ATTN_WIDTH), BF16),
                         _sds((rows, KV_WIDTH), F32), _sds((rows, KV_WIDTH), F32)]
    out_p, out_s = outs(m), outs(ms)
    return pl.pallas_call(
        functools.partial(_qkv_kernel, n_m),
        grid=(n_m + 1,),
        in_specs=[
            pl.BlockSpec((bm, D_MODEL), row),
            _resident(xs),
            _resident(g),
            pl.BlockSpec((pl.Squeezed(), D_MODEL, QKV_WIDTH), lambda i: (0, 0, 0),
                         pipeline_mode=pl.Buffered(1)),
            pl.BlockSpec((bm, LANES), pos),
            pl.BlockSpec((bm, LANES), pos),
            _resident(rope_s[0]),
            _resident(rope_s[1]),
        ],
        out_specs=[pl.BlockSpec((bm, s.shape[1]), row) for s in out_p] + [_resident(s) for s in out_s],
        out_shape=out_p + out_s,
        scratch_shapes=[pltpu.VMEM((D_MODEL, QKV_WIDTH), BF16)],
        compiler_params=_params(1),
        name="qkv_rope",
    )(x, xs, g, w_in, rope_p[0], rope_p[1], rope_s[0], rope_s[1])


def _attn_prompt_kernel(sinks_ref, q_ref, kc_ref, kp_ref, vc_ref, vp_ref, o_ref):
    blk = pl.program_id(1)
    qi = lax.broadcasted_iota(jnp.int32, (WINDOW, 2 * WINDOW), 0)
    ci = lax.broadcasted_iota(jnp.int32, (WINDOW, 2 * WINDOW), 1)
    rel = WINDOW + qi - ci
    mask = (rel >= 0) & (rel < WINDOW) & ((blk > 0) | (ci >= WINDOW))
    for kh in range(N_KV_HEADS):
        cols = slice(kh * HEAD_DIM, (kh + 1) * HEAD_DIM)
        k = jnp.concatenate([kp_ref[:, cols], kc_ref[:, cols]], axis=0).astype(BF16)
        v = jnp.concatenate([vp_ref[:, cols], vc_ref[:, cols]], axis=0).astype(BF16)
        heads = range(kh * GROUP, (kh + 1) * GROUP)
        qs = jnp.concatenate([q_ref[:, h * HEAD_DIM:(h + 1) * HEAD_DIM] for h in heads], axis=0)
        s_all = lax.dot_general(qs, k, (((1,), (1,)), ((), ())), preferred_element_type=F32)
        ps, dens = [], []
        for g, h in enumerate(heads):
            s = jnp.where(mask, s_all[g * WINDOW:(g + 1) * WINDOW, :], MASKED)
            sink = sinks_ref[h]
            m = jnp.maximum(jnp.max(s, axis=-1, keepdims=True), sink)
            p = jnp.exp(s - m)
            dens.append(jnp.sum(p, axis=-1, keepdims=True) + jnp.exp(sink - m))
            ps.append(p.astype(BF16))
        o_all = _dot(jnp.concatenate(ps, axis=0), v)
        for g, h in enumerate(heads):
            o = o_all[g * WINDOW:(g + 1) * WINDOW, :] / dens[g]
            o_ref[:, h * HEAD_DIM:(h + 1) * HEAD_DIM] = o.astype(BF16)


def _attn_prompt(sinks, q, k, v, *, n_seq, seq_len):
    nb = seq_len // WINDOW
    cur = lambda n, b: (n * nb + b, 0)
    prev = lambda n, b: (jnp.maximum(n * nb + b - 1, 0), 0)
    return pl.pallas_call(
        _attn_prompt_kernel,
        grid=(n_seq, nb),
        in_specs=[
            pl.BlockSpec(memory_space=pltpu.SMEM),
            pl.BlockSpec((WINDOW, ATTN_WIDTH), cur),
            pl.BlockSpec((WINDOW, KV_WIDTH), cur),
            pl.BlockSpec((WINDOW, KV_WIDTH), prev),
            pl.BlockSpec((WINDOW, KV_WIDTH), cur),
            pl.BlockSpec((WINDOW, KV_WIDTH), prev),
        ],
        out_specs=pl.BlockSpec((WINDOW, ATTN_WIDTH), cur),
        out_shape=_sds((n_seq * seq_len, ATTN_WIDTH), BF16),
        compiler_params=_params(2),
        name="attn_prompt",
    )(sinks, q, k, k, v, v)


def _attn_sample_kernel(sinks_ref, q_ref, kn_ref, vn_ref, sk_ref, sv_ref, a_ref, nk_ref, nv_ref):
    nk_ref[:, 0:WINDOW - 1, :] = sk_ref[:, 1:WINDOW, :]
    nk_ref[:, WINDOW - 1:WINDOW, :] = kn_ref[...]
    nv_ref[:, 0:WINDOW - 1, :] = sv_ref[:, 1:WINDOW, :]
    nv_ref[:, WINDOW - 1:WINDOW, :] = vn_ref[...]
    for kh in range(N_KV_HEADS):
        cols = slice(kh * HEAD_DIM, (kh + 1) * HEAD_DIM)
        heads = slice(kh * GROUP, (kh + 1) * GROUP)
        k = nk_ref[:, :, cols].astype(BF16)
        v = nv_ref[:, :, cols].astype(BF16)
        s = jnp.einsum("ngd,ncd->ngc", q_ref[:, heads, :], k, preferred_element_type=F32)
        sink = sinks_ref[heads, :][None]
        m = jnp.maximum(jnp.max(s, axis=-1, keepdims=True), sink)
        p = jnp.exp(s - m)
        den = jnp.sum(p, axis=-1, keepdims=True) + jnp.exp(sink - m)
        o = jnp.einsum("ngc,ncd->ngd", p.astype(BF16), v, preferred_element_type=F32) / den
        a_ref[:, heads, :] = o.astype(BF16)


def _attn_sample(sinks_col, q3, k_new3, v_new3, state_k, state_v):
    n = q3.shape[0]
    return pl.pallas_call(
        _attn_sample_kernel,
        out_shape=[_sds((n, N_HEADS, HEAD_DIM), BF16), _sds((n, WINDOW, KV_WIDTH), F32),
                   _sds((n, WINDOW, KV_WIDTH), F32)],
        compiler_params=pltpu.CompilerParams(vmem_limit_bytes=VMEM_LIMIT),
        name="attn_sample",
    )(sinks_col, q3, k_new3, v_new3, state_k, state_v)


def _shift_rows(cur, halo):
    rid = lax.broadcasted_iota(jnp.int32, cur.shape, 0)
    h0 = halo[0:1, :]
    h1 = halo[1:2, :]
    s1 = jnp.where(rid == 0, h1, pltpu.roll(cur, 1, 0))
    s2 = jnp.where(rid == 0, h0, jnp.where(rid == 1, h1, pltpu.roll(cur, 2, 0)))
    return s1, s2


def _conv3(cw_ref, cur, s1, s2):
    return cw_ref[0:1, :] * s2 + cw_ref[1:2, :] * s1 + cw_ref[2:3, :] * cur


def _conv_rows_prompt(i, tiles_per_seq, n_rows, halo_ref, st_ref, chunk_fn):
    @pl.when(i % tiles_per_seq == 0)
    def _():
        halo_ref[...] = jnp.zeros_like(halo_ref)

    halo = halo_ref[...]
    for rows in _row_chunks(n_rows):
        cur = chunk_fn(rows, lambda cur, halo=halo: _shift_rows(cur, halo))
        halo = cur[cur.shape[0] - 2:, :]
    halo_ref[...] = halo

    @pl.when(i % tiles_per_seq == tiles_per_seq - 1)
    def _():
        st_ref[...] = halo


def _convmix_kernel(n_m, tiles_per_seq, h_ref, hs_ref, wb_ref, wc_ref, wu_ref, cw_ref, p2_ref, p1_ref,
                    o_ref, st_ref, os_ref, pres_ref, wbf_ref, halo_ref):
    i = pl.program_id(1)
    bn = wb_ref.shape[1]

    @pl.when(i == 0)
    def _():
        wbf_ref[:, 0:bn] = wb_ref[...].astype(BF16)
        wbf_ref[:, bn:2 * bn] = wc_ref[...].astype(BF16)
        wbf_ref[:, 2 * bn:3 * bn] = wu_ref[...].astype(BF16)

    def project(x):
        proj = _dot(x, wbf_ref[...])
        return proj[:, 0:bn], proj[:, bn:2 * bn] * proj[:, 2 * bn:3 * bn]

    @pl.when(i < n_m)
    def _():
        def chunk(rows, shifts):
            cb, cur = project(h_ref[rows, :])
            s1, s2 = shifts(cur)
            o_ref[rows, :] = (cb * _conv3(cw_ref, cur, s1, s2)).astype(BF16)
            return cur

        _conv_rows_prompt(i, tiles_per_seq, h_ref.shape[0], halo_ref, st_ref, chunk)

    @pl.when(i == n_m)
    def _():
        cb, cur = project(hs_ref[...])
        pres_ref[...] = cur
        os_ref[...] = (cb * _conv3(cw_ref, cur, p1_ref[...], p2_ref[...])).astype(BF16)


def _convmix(h, hs, w_in, conv_w, prev2, prev1, *, bm, bn, seq_len):
    m, ms = h.shape[0], hs.shape[0]
    n_m = m // bm
    tiles_per_seq = seq_len // bm
    row = lambda j, i: (jnp.minimum(i, n_m - 1), 0)
    tile = pl.BlockSpec((bm, bn), lambda j, i: (jnp.minimum(i, n_m - 1), j))
    stile = pl.BlockSpec((ms, bn), lambda j, i: (0, j))
    wspec = lambda off: _wcol(D_MODEL, bn, lambda j, i: off + j * bn)
    return pl.pallas_call(
        functools.partial(_convmix_kernel, n_m, tiles_per_seq),
        grid=(CONV_WIDTH // bn, n_m + 1),
        in_specs=[
            pl.BlockSpec((bm, D_MODEL), row),
            _resident(hs),
            wspec(OFF_CB), wspec(OFF_CC), wspec(OFF_CU),
            pl.BlockSpec((pl.Squeezed(), 3, bn), lambda j, i: (0, 0, j)),
            stile, stile,
        ],
        out_specs=[
            tile,
            pl.BlockSpec((pl.Squeezed(), 2, bn), lambda j, i: (jnp.minimum(i, n_m - 1) // tiles_per_seq, 0, j)),
            stile, stile,
        ],
        out_shape=[_sds((m, CONV_WIDTH), BF16), _sds((m // seq_len, 2, CONV_WIDTH), F32),
                   _sds((ms, CONV_WIDTH), BF16), _sds((ms, CONV_WIDTH), F32)],
        scratch_shapes=[pltpu.VMEM((D_MODEL, 3 * bn), BF16), pltpu.VMEM((2, bn), F32)],
        compiler_params=_params(2),
        name="convmix",
    )(h, hs, w_in, w_in, w_in, conv_w, prev2, prev1)


def _merge_kernel(n_m, h_ref, a_ref, b_ref, hs_ref, as_ref, bs_ref, wga_ref, wgb_ref, wa_ref, wb_ref, bg_ref,
                  o_ref, os_ref, wg_bf_ref, wa_bf_ref, wb_bf_ref):
    i = pl.program_id(1)
    bn = wa_ref.shape[1]

    @pl.when(i == 0)
    def _():
        wg_bf_ref[:, 0:bn] = wga_ref[...].astype(BF16)
        wg_bf_ref[:, bn:2 * bn] = wgb_ref[...].astype(BF16)
        wa_bf_ref[...] = wa_ref[...].astype(BF16)
        wb_bf_ref[...] = wb_ref[...].astype(BF16)

    def rows_fn(h_ref, a_ref, b_ref, o_ref):
        for rows in _row_chunks(h_ref.shape[0]):
            gates = _dot(h_ref[rows, :], wg_bf_ref[...])
            ga = jax.nn.sigmoid(gates[:, 0:bn] + bg_ref[0:1, :])
            gb = jax.nn.sigmoid(gates[:, bn:2 * bn] + bg_ref[1:2, :])
            merged = ga * _dot(a_ref[rows, :], wa_bf_ref[...]) + gb * _dot(b_ref[rows, :], wb_bf_ref[...])
            o_ref[rows, :] = merged.astype(BF16)

    @pl.when(i < n_m)
    def _():
        rows_fn(h_ref, a_ref, b_ref, o_ref)

    @pl.when(i == n_m)
    def _():
        rows_fn(hs_ref, as_ref, bs_ref, os_ref)


def _merge(h, a, b, hs, a_s, b_s, w_in, w_attn_branch, w_conv_branch, b_gate, *, bm, bn):
    m, ms = h.shape[0], hs.shape[0]
    n_m = m // bm
    row = lambda j, i: (jnp.minimum(i, n_m - 1), 0)
    return pl.pallas_call(
        functools.partial(_merge_kernel, n_m),
        grid=(D_MODEL // bn, n_m + 1),
        in_specs=[
            pl.BlockSpec((bm, D_MODEL), row),
            pl.BlockSpec((bm, ATTN_WIDTH), row),
            pl.BlockSpec((bm, CONV_WIDTH), row),
            _resident(hs), _resident(a_s), _resident(b_s),
            _wcol(D_MODEL, bn, lambda j, i: OFF_GA + j * bn),
            _wcol(D_MODEL, bn, lambda j, i: OFF_GB + j * bn),
            pl.BlockSpec((pl.Squeezed(), ATTN_WIDTH, bn), lambda j, i: (0, 0, j)),
            pl.BlockSpec((pl.Squeezed(), CONV_WIDTH, bn), lambda j, i: (0, 0, j)),
            pl.BlockSpec((pl.Squeezed(), 2, bn), lambda j, i: (0, 0, j)),
        ],
        out_specs=[pl.BlockSpec((bm, bn), lambda j, i: (jnp.minimum(i, n_m - 1), j)),
                   pl.BlockSpec((ms, bn), lambda j, i: (0, j))],
        out_shape=[_sds((m, D_MODEL), BF16), _sds((ms, D_MODEL), BF16)],
        scratch_shapes=[pltpu.VMEM((D_MODEL, 2 * bn), BF16), pltpu.VMEM((ATTN_WIDTH, bn), BF16),
                        pltpu.VMEM((CONV_WIDTH, bn), BF16)],
        compiler_params=_params(2),
        name="merge",
    )(h, a, b, hs, a_s, b_s, w_in, w_in, w_attn_branch, w_conv_branch, b_gate)


def _oproj_kernel(n_m, mg_ref, x_ref, mgs_ref, xs_ref, wo_ref, gpost_ref, gpre_ref,
                  x1_ref, h2_ref, x1s_ref, h2s_ref):
    i = pl.program_id(0)

    def rows_fn(mg_ref, x_ref, x1_ref, h2_ref):
        for rows in _row_chunks(x_ref.shape[0]):
            x1 = x_ref[rows, :] + _rms(_dot(mg_ref[rows, :], wo_ref[...]), gpost_ref[...])
            x1_ref[rows, :] = x1
            h2_ref[rows, :] = _rms(x1, gpre_ref[...]).astype(BF16)

    @pl.when(i < n_m)
    def _():
        rows_fn(mg_ref, x_ref, x1_ref, h2_ref)

    @pl.when(i == n_m)
    def _():
        rows_fn(mgs_ref, xs_ref, x1s_ref, h2s_ref)


def _oproj(merged, x, merged_s, xs, w_o, g_post, g_pre, *, bm):
    m, ms = x.shape[0], xs.shape[0]
    n_m = m // bm
    row = lambda i: (jnp.minimum(i, n_m - 1), 0)
    tile = pl.BlockSpec((bm, D_MODEL), row)
    return pl.pallas_call(
        functools.partial(_oproj_kernel, n_m),
        grid=(n_m + 1,),
        in_specs=[
            tile, tile, _resident(merged_s), _resident(xs),
            pl.BlockSpec((pl.Squeezed(), D_MODEL, D_MODEL), lambda i: (0, 0, 0),
                         pipeline_mode=pl.Buffered(1)),
            _resident(g_post), _resident(g_pre),
        ],
        out_specs=[tile, tile, _resident(xs), _resident(merged_s)],
        out_shape=[_sds((m, D_MODEL), F32), _sds((m, D_MODEL), BF16),
                   _sds((ms, D_MODEL), F32), _sds((ms, D_MODEL), BF16)],
        compiler_params=_params(1),
        name="oproj_norm",
    )(merged, x, merged_s, xs, w_o, g_post, g_pre)


def _ffn_up_kernel(n_m, tiles_per_seq, h_ref, hs_ref, wa_ref, wg_ref, cw_ref, cb_ref, p2_ref, p1_ref,
                   o_ref, st_ref, os_ref, ups_ref, wbf_ref, halo_ref):
    i = pl.program_id(1)
    bn = wa_ref.shape[1]

    @pl.when(i == 0)
    def _():
        wbf_ref[:, 0:bn] = wa_ref[...].astype(BF16)
        wbf_ref[:, bn:2 * bn] = wg_ref[...].astype(BF16)

    def project(x):
        proj = _dot(x, wbf_ref[...])
        return proj[:, 0:bn], proj[:, bn:2 * bn]

    def gated(cur, s1, s2, gate):
        act = jax.nn.gelu(_conv3(cw_ref, cur, s1, s2) + cb_ref[...], approximate=True)
        return (act * gate).astype(BF16)

    @pl.when(i < n_m)
    def _():
        def chunk(rows, shifts):
            cur, gate = project(h_ref[rows, :])
            s1, s2 = shifts(cur)
            o_ref[rows, :] = gated(cur, s1, s2, gate)
            return cur

        _conv_rows_prompt(i, tiles_per_seq, h_ref.shape[0], halo_ref, st_ref, chunk)

    @pl.when(i == n_m)
    def _():
        cur, gate = project(hs_ref[...])
        ups_ref[...] = cur
        os_ref[...] = gated(cur, p1_ref[...], p2_ref[...], gate)


def _ffn_up(h2, h2s, w_ffn_up, ffn_conv_w, ffn_conv_b, prev2, prev1, *, bm, bn, seq_len):
    m, ms = h2.shape[0], h2s.shape[0]
    n_m = m // bm
    tiles_per_seq = seq_len // bm
    n_tiles = D_FF // bn
    row = lambda j, i: (jnp.minimum(i, n_m - 1), 0)
    tile = pl.BlockSpec((bm, bn), lambda j, i: (jnp.minimum(i, n_m - 1), j))
    stile = pl.BlockSpec((ms, bn), lambda j, i: (0, j))
    return pl.pallas_call(
        functools.partial(_ffn_up_kernel, n_m, tiles_per_seq),
        grid=(n_tiles, n_m + 1),
        in_specs=[
            pl.BlockSpec((bm, D_MODEL), row),
            _resident(h2s),
            pl.BlockSpec((pl.Squeezed(), D_MODEL, bn), lambda j, i: (0, 0, j)),
            pl.BlockSpec((pl.Squeezed(), D_MODEL, bn), lambda j, i: (0, 0, n_tiles + j)),
            pl.BlockSpec((pl.Squeezed(), 3, bn), lambda j, i: (0, 0, j)),
            pl.BlockSpec((1, bn), lambda j, i: (0, j)),
            stile, stile,
        ],
        out_specs=[
            tile,
            pl.BlockSpec((pl.Squeezed(), 2, bn), lambda j, i: (jnp.minimum(i, n_m - 1) // tiles_per_seq, 0, j)),
            stile, stile,
        ],
        out_shape=[_sds((m, D_FF), BF16), _sds((m // seq_len, 2, D_FF), F32),
                   _sds((ms, D_FF), BF16), _sds((ms, D_FF), F32)],
        scratch_shapes=[pltpu.VMEM((D_MODEL, 2 * bn), BF16), pltpu.VMEM((2, bn), F32)],
        compiler_params=_params(2),
        name="ffn_up",
    )(h2, h2s, w_ffn_up, w_ffn_up, ffn_conv_w, ffn_conv_b, prev2, prev1)


def _ffn_down_kernel(n_m, a_ref, as_ref, w_ref, o_ref, os_ref, wbf_ref):
    i = pl.program_id(1)

    @pl.when(i == 0)
    def _():
        wbf_ref[...] = w_ref[...].astype(BF16)

    @pl.when(i < n_m)
    def _():
        for rows in _row_chunks(a_ref.shape[0]):
            o_ref[rows, :] = _dot(a_ref[rows, :], wbf_ref[...]).astype(BF16)

    @pl.when(i == n_m)
    def _():
        os_ref[...] = _dot(as_ref[...], wbf_ref[...]).astype(BF16)


def _ffn_down(act, act_s, w_ffn_down, *, bm, bn):
    m, ms = act.shape[0], act_s.shape[0]
    n_m = m // bm
    return pl.pallas_call(
        functools.partial(_ffn_down_kernel, n_m),
        grid=(D_MODEL // bn, n_m + 1),
        in_specs=[
            pl.BlockSpec((bm, D_FF), lambda j, i: (jnp.minimum(i, n_m - 1), 0)),
            _resident(act_s),
            pl.BlockSpec((pl.Squeezed(), D_FF, bn), lambda j, i: (0, 0, j)),
        ],
        out_specs=[pl.BlockSpec((bm, bn), lambda j, i: (jnp.minimum(i, n_m - 1), j)),
                   pl.BlockSpec((ms, bn), lambda j, i: (0, j))],
        out_shape=[_sds((m, D_MODEL), BF16), _sds((ms, D_MODEL), BF16)],
        scratch_shapes=[pltpu.VMEM((D_FF, bn), BF16)],
        compiler_params=_params(2),
        name="ffn_down",
    )(act, act_s, w_ffn_down)


def _final_kernel(n_m, f_ref, x1_ref, p_ref, fs_ref, x1s_ref, ps_ref, g_ref, wg_ref, wp_ref, y_ref, ys_ref):
    i = pl.program_id(0)

    def rows_fn(f_ref, x1_ref, p_ref, y_ref):
        for rows in _row_chunks(x1_ref.shape[0]):
            x2 = x1_ref[rows, :] + _rms(f_ref[rows, :].astype(F32), g_ref[...])
            gate = jax.nn.sigmoid(_dot(x2.astype(BF16), wg_ref[...]))
            y_ref[rows, :] = x2 + gate * _dot(p_ref[rows, :].astype(BF16), wp_ref[...])

    @pl.when(i < n_m)
    def _():
        rows_fn(f_ref, x1_ref, p_ref, y_ref)

    @pl.when(i == n_m)
    def _():
        rows_fn(fs_ref, x1s_ref, ps_ref, ys_ref)


def _final(f, x1, p, f_s, x1_s, p_s, g_post, w_ple_gate, w_ple, *, bm):
    m, ms = x1.shape[0], x1_s.shape[0]
    n_m = m // bm
    row = lambda i: (jnp.minimum(i, n_m - 1), 0)
    return pl.pallas_call(
        functools.partial(_final_kernel, n_m),
        grid=(n_m + 1,),
        in_specs=[
            pl.BlockSpec((bm, D_MODEL), row),
            pl.BlockSpec((bm, D_MODEL), row),
            pl.BlockSpec((bm, PLE_DIM), row),
            _resident(f_s), _resident(x1_s), _resident(p_s),
            _resident(g_post),
            pl.BlockSpec((pl.Squeezed(), D_MODEL, D_MODEL), lambda i: (0, 0, 0),
                         pipeline_mode=pl.Buffered(1)),
            pl.BlockSpec((pl.Squeezed(), PLE_DIM, D_MODEL), lambda i: (0, 0, 0),
                         pipeline_mode=pl.Buffered(1)),
        ],
        out_specs=[pl.BlockSpec((bm, D_MODEL), row), _resident(x1_s)],
        out_shape=[_sds((m, D_MODEL), F32), _sds((ms, D_MODEL), F32)],
        compiler_params=_params(1),
        name="ffn_norm_ple",
    )(f, x1, p, f_s, x1_s, p_s, g_post, w_ple_gate, w_ple)


def _rope_tables(pos):
    half = HEAD_DIM // 2
    inv = ROPE_THETA ** (-jnp.arange(half, dtype=F32) / half)
    ang = pos.astype(F32)[:, None] * inv[None, :]
    cos = jnp.cos(ang)
    sin = jnp.sin(ang)
    reps = LANES // HEAD_DIM
    return (jnp.tile(jnp.concatenate([cos, cos], axis=1), (1, reps)),
            jnp.tile(jnp.concatenate([-sin, sin], axis=1), (1, reps)))


TILES = dict(qkv=512, convmix=dict(bm=1024, bn=256), merge=dict(bm=512, bn=512), oproj=512,
             ffn_up=dict(bm=1024, bn=512), ffn_down=dict(bm=512, bn=512), final=512)


def kernel(x_prompt, x_sample, state_attn_k, state_attn_v, state_conv, state_ffn_conv, p_prompt, p_sample,
           norm_mix_pre, norm_mix_post, norm_ffn_pre, norm_ffn_post, w_in, attn_sinks, conv_w, w_attn_branch,
           w_conv_branch, b_gate, w_o, w_ffn_up, ffn_conv_w, ffn_conv_b, w_ffn_down, w_ple, w_ple_gate):
    assert w_in.shape[0] == 1, "weights are indexed with a squeezed depth axis of size 1"
    batch, seq, _ = x_prompt.shape
    ns, dec_seq, _ = x_sample.shape
    assert dec_seq == 1
    m = batch * seq
    x = x_prompt.reshape(m, D_MODEL)
    xs = x_sample.reshape(ns, D_MODEL)
    rope_p = _rope_tables(jnp.arange(seq, dtype=jnp.int32))
    rope_s = _rope_tables(jnp.full((ns,), PAST_LEN, dtype=jnp.int32))

    h, q, k, v, hs, qs, ks, vs = _qkv(x, xs, norm_mix_pre, w_in, rope_p, rope_s, bm=TILES["qkv"])

    a = _attn_prompt(attn_sinks.reshape(N_HEADS), q, k, v, n_seq=batch, seq_len=seq)
    a3, new_ks, new_vs = _attn_sample(
        attn_sinks.reshape(N_HEADS, 1), qs.reshape(ns, N_HEADS, HEAD_DIM), ks.reshape(ns, 1, KV_WIDTH),
        vs.reshape(ns, 1, KV_WIDTH), state_attn_k.reshape(ns, WINDOW, KV_WIDTH),
        state_attn_v.reshape(ns, WINDOW, KV_WIDTH))
    a_s = a3.reshape(ns, ATTN_WIDTH)

    b, new_conv_p, b_s, pre_s = _convmix(h, hs, w_in, conv_w, state_conv[0, :, 0], state_conv[0, :, 1],
                                         seq_len=seq, **TILES["convmix"])
    merged, merged_s = _merge(h, a, b, hs, a_s, b_s, w_in, w_attn_branch, w_conv_branch, b_gate, **TILES["merge"])
    x1, h2, x1_s, h2_s = _oproj(merged, x, merged_s, xs, w_o.astype(BF16), norm_mix_post, norm_ffn_pre,
                                bm=TILES["oproj"])
    act, new_ffn_p, act_s, up_s = _ffn_up(h2, h2_s, w_ffn_up, ffn_conv_w, ffn_conv_b, state_ffn_conv[0, :, 0],
                                          state_ffn_conv[0, :, 1], seq_len=seq, **TILES["ffn_up"])
    f, f_s = _ffn_down(act, act_s, w_ffn_down, **TILES["ffn_down"])
    y, y_s = _final(f, x1, p_prompt.reshape(m, PLE_DIM), f_s, x1_s, p_sample.reshape(ns, PLE_DIM),
                    norm_ffn_post, w_ple_gate.astype(BF16), w_ple.astype(BF16), bm=TILES["final"])

    kv_tail = lambda t: t.reshape(batch, seq, N_KV_HEADS, HEAD_DIM)[:, -WINDOW:]
    kv_state = lambda t: t.reshape(ns, WINDOW, N_KV_HEADS, HEAD_DIM)
    return (y.reshape(batch, seq, D_MODEL), y_s.reshape(ns, dec_seq, D_MODEL),
            kv_tail(k)[None], kv_tail(v)[None], new_conv_p[None], new_ffn_p[None],
            kv_state(new_ks)[None], kv_state(new_vs)[None],
            jnp.stack([state_conv[0, :, 1], pre_s], axis=1)[None],
            jnp.stack([state_ffn_conv[0, :, 1], up_s], axis=1)[None])
```

```python
import functools

import jax
import jax.numpy as jnp
from jax import lax
from jax.experimental import pallas as pl
from jax.experimental.pallas import tpu as pltpu

D_MODEL = 2048
N_HEADS = 16
N_KV_HEADS = 2
HEAD_DIM = 64
GROUP = N_HEADS // N_KV_HEADS
ATTN_WIDTH = N_HEADS * HEAD_DIM
KV_WIDTH = N_KV_HEADS * HEAD_DIM
QKV_WIDTH = ATTN_WIDTH + 2 * KV_WIDTH
WINDOW = 128
PAST_LEN = 16384
ROPE_THETA = 10000.0
CONV_WIDTH = D_MODEL // 2
D_FF = 5632
PLE_DIM = 256
RMS_EPS = 1e-6
OFF_CB = QKV_WIDTH
OFF_CC = OFF_CB + CONV_WIDTH
OFF_CU = OFF_CC + CONV_WIDTH
OFF_GA = OFF_CU + CONV_WIDTH
OFF_GB = OFF_GA + D_MODEL

LANES = 128
MASKED = -1e30
VMEM_LIMIT = 56 * 1024 * 1024
ROW_CHUNK = 256

BF16 = jnp.bfloat16
F32 = jnp.float32


def _params(n_axes):
    return pltpu.CompilerParams(dimension_semantics=("arbitrary",) * n_axes,
                                vmem_limit_bytes=VMEM_LIMIT)


def _rms(x, g):
    return x * lax.rsqrt(jnp.mean(x * x, axis=-1, keepdims=True) + RMS_EPS) * g


def _dot(a, b):
    return jnp.dot(a, b, preferred_element_type=F32)


def _row_chunks(bm):
    rc = min(bm, ROW_CHUNK)
    return [pl.ds(c * rc, rc) for c in range(bm // rc)]


def _wcol(k_rows, block_cols, col_fn):
    return pl.BlockSpec((pl.Squeezed(), pl.Element(k_rows), pl.Element(block_cols)),
                        lambda *g: (0, 0, pl.multiple_of(col_fn(*g), LANES)))


def _resident(arr):
    zeros = (0,) * arr.ndim
    return pl.BlockSpec(arr.shape, lambda *g: zeros)


def _prompt_tile(i):
    return jnp.maximum(i - 1, 0)


def _sds(shape, dtype):
    return jax.ShapeDtypeStruct(shape, dtype)


def _qkv_rows(x_ref, g_ref, wbf_ref, cos_ref, sin_ref, h_ref, q_ref, k_ref, v_ref):
    n_q = ATTN_WIDTH // LANES
    for rows in _row_chunks(x_ref.shape[0]):
        h = _rms(x_ref[rows, :], g_ref[...]).astype(BF16)
        h_ref[rows, :] = h
        proj = _dot(h, wbf_ref[...])
        cos = cos_ref[rows, :]
        sin = sin_ref[rows, :]
        lane = lax.broadcasted_iota(jnp.int32, cos.shape, 1)
        first_half = (lane % HEAD_DIM) < (HEAD_DIM // 2)
        for c in range(n_q + 1):
            xg = proj[:, c * LANES:(c + 1) * LANES]
            partner = jnp.where(first_half, pltpu.roll(xg, LANES - HEAD_DIM // 2, 1),
                                pltpu.roll(xg, HEAD_DIM // 2, 1))
            yg = xg * cos + partner * sin
            if c < n_q:
                q_ref[rows, c * LANES:(c + 1) * LANES] = (yg * (HEAD_DIM ** -0.5)).astype(BF16)
            else:
                k_ref[rows, :] = yg
        v_ref[rows, :] = proj[:, ATTN_WIDTH + KV_WIDTH:QKV_WIDTH]


def _qkv_kernel(x_ref, xs_ref, g_ref, w_ref, cos_ref, sin_ref, coss_ref, sins_ref,
                h_ref, q_ref, k_ref, v_ref, hs_ref, qs_ref, ks_ref, vs_ref, wbf_ref):
    i = pl.program_id(0)

    @pl.when(i == 0)
    def _():
        wbf_ref[...] = w_ref[...].astype(BF16)

    @pl.when(i > 0)
    def _():
        _qkv_rows(x_ref, g_ref, wbf_ref, cos_ref, sin_ref, h_ref, q_ref, k_ref, v_ref)

    @pl.when(i == 0)
    def _():
        _qkv_rows(xs_ref, g_ref, wbf_ref, coss_ref, sins_ref, hs_ref, qs_ref, ks_ref, vs_ref)


def _qkv(x, xs, g, w_in, rope_p, rope_s, *, bm):
    m, ms = x.shape[0], xs.shape[0]
    n_m = m // bm
    t_tiles = rope_p[0].shape[0] // bm
    row = lambda i: (_prompt_tile(i), 0)
    pos = lambda i: (_prompt_tile(i) % t_tiles, 0)
    outs = lambda rows: [_sds((rows, D_MODEL), BF16), _sds((rows, ATTN_WIDTH), BF16),
                         _sds((rows, KV_WIDTH), F32), _sds((rows, KV_WIDTH), F32)]
    out_p, out_s = outs(m), outs(ms)
    return pl.pallas_call(
        _qkv_kernel,
        grid=(n_m + 1,),
        in_specs=[
            pl.BlockSpec((bm, D_MODEL), row),
            _resident(xs),
            _resident(g),
            pl.BlockSpec((pl.Squeezed(), D_MODEL, QKV_WIDTH), lambda i: (0, 0, 0),
                         pipeline_mode=pl.Buffered(1)),
            pl.BlockSpec((bm, LANES), pos),
            pl.BlockSpec((bm, LANES), pos),
            _resident(rope_s[0]),
            _resident(rope_s[1]),
        ],
        out_specs=[pl.BlockSpec((bm, s.shape[1]), row) for s in out_p] + [_resident(s) for s in out_s],
        out_shape=out_p + out_s,
        scratch_shapes=[pltpu.VMEM((D_MODEL, QKV_WIDTH), BF16)],
        compiler_params=_params(1),
        name="qkv_rope",
    )(x, xs, g, w_in, rope_p[0], rope_p[1], rope_s[0], rope_s[1])


def _attn_prompt_kernel(sinks_ref, q_ref, kc_ref, kp_ref, vc_ref, vp_ref, o_ref):
    blk = pl.program_id(1)
    qi = lax.broadcasted_iota(jnp.int32, (WINDOW, 2 * WINDOW), 0)
    ci = lax.broadcasted_iota(jnp.int32, (WINDOW, 2 * WINDOW), 1)
    rel = WINDOW + qi - ci
    mask = (rel >= 0) & (rel < WINDOW) & ((blk > 0) | (ci >= WINDOW))
    for kh in range(N_KV_HEADS):
        cols = slice(kh * HEAD_DIM, (kh + 1) * HEAD_DIM)
        k = jnp.concatenate([kp_ref[:, cols], kc_ref[:, cols]], axis=0).astype(BF16)
        v = jnp.concatenate([vp_ref[:, cols], vc_ref[:, cols]], axis=0).astype(BF16)
        heads = range(kh * GROUP, (kh + 1) * GROUP)
        qs = jnp.concatenate([q_ref[:, h * HEAD_DIM:(h + 1) * HEAD_DIM] for h in heads], axis=0)
        s_all = lax.dot_general(qs, k, (((1,), (1,)), ((), ())), preferred_element_type=F32)
        ps, dens = [], []
        for g, h in enumerate(heads):
            s = jnp.where(mask, s_all[g * WINDOW:(g + 1) * WINDOW, :], MASKED)
            sink = sinks_ref[h]
            m = jnp.maximum(jnp.max(s, axis=-1, keepdims=True), sink)
            p = jnp.exp(s - m)
            dens.append(jnp.sum(p, axis=-1, keepdims=True) + jnp.exp(sink - m))
            ps.append(p.astype(BF16))
        o_all = _dot(jnp.concatenate(ps, axis=0), v)
        for g, h in enumerate(heads):
            o = o_all[g * WINDOW:(g + 1) * WINDOW, :] / dens[g]
            o_ref[:, h * HEAD_DIM:(h + 1) * HEAD_DIM] = o.astype(BF16)


def _attn_prompt(sinks, q, k, v, *, n_seq, seq_len):
    nb = seq_len // WINDOW
    cur = lambda n, b: (n * nb + b, 0)
    prev = lambda n, b: (jnp.maximum(n * nb + b - 1, 0), 0)
    return pl.pallas_call(
        _attn_prompt_kernel,
        grid=(n_seq, nb),
        in_specs=[
            pl.BlockSpec(memory_space=pltpu.SMEM),
            pl.BlockSpec((WINDOW, ATTN_WIDTH), cur),
            pl.BlockSpec((WINDOW, KV_WIDTH), cur),
            pl.BlockSpec((WINDOW, KV_WIDTH), prev),
            pl.BlockSpec((WINDOW, KV_WIDTH), cur),
            pl.BlockSpec((WINDOW, KV_WIDTH), prev),
        ],
        out_specs=pl.BlockSpec((WINDOW, ATTN_WIDTH), cur),
        out_shape=_sds((n_seq * seq_len, ATTN_WIDTH), BF16),
        compiler_params=_params(2),
        name="attn_prompt",
    )(sinks, q, k, k, v, v)


def _attn_sample_kernel(sinks_ref, q_ref, kn_ref, vn_ref, sk_ref, sv_ref, a_ref, nk_ref, nv_ref):
    nk_ref[:, 0:WINDOW - 1, :] = sk_ref[:, 1:WINDOW, :]
    nk_ref[:, WINDOW - 1:WINDOW, :] = kn_ref[...]
    nv_ref[:, 0:WINDOW - 1, :] = sv_ref[:, 1:WINDOW, :]
    nv_ref[:, WINDOW - 1:WINDOW, :] = vn_ref[...]
    for kh in range(N_KV_HEADS):
        cols = slice(kh * HEAD_DIM, (kh + 1) * HEAD_DIM)
        heads = slice(kh * GROUP, (kh + 1) * GROUP)
        k = nk_ref[:, :, cols].astype(BF16)
        v = nv_ref[:, :, cols].astype(BF16)
        s = jnp.einsum("ngd,ncd->ngc", q_ref[:, heads, :], k, preferred_element_type=F32)
        sink = sinks_ref[heads, :][None]
        m = jnp.maximum(jnp.max(s, axis=-1, keepdims=True), sink)
        p = jnp.exp(s - m)
        den = jnp.sum(p, axis=-1, keepdims=True) + jnp.exp(sink - m)
        o = jnp.einsum("ngc,ncd->ngd", p.astype(BF16), v, preferred_element_type=F32) / den
        a_ref[:, heads, :] = o.astype(BF16)


def _attn_sample(sinks_col, q3, k_new3, v_new3, state_k, state_v):
    n = q3.shape[0]
    return pl.pallas_call(
        _attn_sample_kernel,
        out_shape=[_sds((n, N_HEADS, HEAD_DIM), BF16), _sds((n, WINDOW, KV_WIDTH), F32),
                   _sds((n, WINDOW, KV_WIDTH), F32)],
        compiler_params=pltpu.CompilerParams(vmem_limit_bytes=VMEM_LIMIT),
        name="attn_sample",
    )(sinks_col, q3, k_new3, v_new3, state_k, state_v)


def _shift_rows(cur, halo):
    rid = lax.broadcasted_iota(jnp.int32, cur.shape, 0)
    h0 = halo[0:1, :]
    h1 = halo[1:2, :]
    s1 = jnp.where(rid == 0, h1, pltpu.roll(cur, 1, 0))
    s2 = jnp.where(rid == 0, h0, jnp.where(rid == 1, h1, pltpu.roll(cur, 2, 0)))
    return s1, s2


def _conv3(cw_ref, cur, s1, s2):
    return cw_ref[0:1, :] * s2 + cw_ref[1:2, :] * s1 + cw_ref[2:3, :] * cur


def _conv_rows_prompt(i, tiles_per_seq, n_rows, halo_ref, st_ref, chunk_fn):
    @pl.when(i % tiles_per_seq == 0)
    def _():
        halo_ref[...] = jnp.zeros_like(halo_ref)

    halo = halo_ref[...]
    for rows in _row_chunks(n_rows):
        cur = chunk_fn(rows, lambda cur, halo=halo: _shift_rows(cur, halo))
        halo = cur[cur.shape[0] - 2:, :]
    halo_ref[...] = halo

    @pl.when(i % tiles_per_seq == tiles_per_seq - 1)
    def _():
        st_ref[...] = halo


def _convmix_kernel(tiles_per_seq, h_ref, hs_ref, wb_ref, wc_ref, wu_ref, cw_ref, p2_ref, p1_ref,
                    o_ref, st_ref, os_ref, pres_ref, wbf_ref, halo_ref):
    i = pl.program_id(1)
    bn = wb_ref.shape[1]

    @pl.when(i == 0)
    def _():
        wbf_ref[:, 0:bn] = wb_ref[...].astype(BF16)
        wbf_ref[:, bn:2 * bn] = wc_ref[...].astype(BF16)
        wbf_ref[:, 2 * bn:3 * bn] = wu_ref[...].astype(BF16)

    def project(x):
        proj = _dot(x, wbf_ref[...])
        return proj[:, 0:bn], proj[:, bn:2 * bn] * proj[:, 2 * bn:3 * bn]

    @pl.when(i > 0)
    def _():
        def chunk(rows, shifts):
            cb, cur = project(h_ref[rows, :])
            s1, s2 = shifts(cur)
            o_ref[rows, :] = (cb * _conv3(cw_ref, cur, s1, s2)).astype(BF16)
            return cur

        _conv_rows_prompt(i - 1, tiles_per_seq, h_ref.shape[0], halo_ref, st_ref, chunk)

    @pl.when(i == 0)
    def _():
        cb, cur = project(hs_ref[...])
        pres_ref[...] = cur
        os_ref[...] = (cb * _conv3(cw_ref, cur, p1_ref[...], p2_ref[...])).astype(BF16)


def _convmix(h, hs, w_in, conv_w, prev2, prev1, *, bm, bn, seq_len):
    m, ms = h.shape[0], hs.shape[0]
    n_m = m // bm
    tiles_per_seq = seq_len // bm
    row = lambda j, i: (_prompt_tile(i), 0)
    tile = pl.BlockSpec((bm, bn), lambda j, i: (_prompt_tile(i), j))
    stile = pl.BlockSpec((ms, bn), lambda j, i: (0, j))
    wspec = lambda off: _wcol(D_MODEL, bn, lambda j, i: off + j * bn)
    return pl.pallas_call(
        functools.partial(_convmix_kernel, tiles_per_seq),
        grid=(CONV_WIDTH // bn, n_m + 1),
        in_specs=[
            pl.BlockSpec((bm, D_MODEL), row),
            _resident(hs),
            wspec(OFF_CB), wspec(OFF_CC), wspec(OFF_CU),
            pl.BlockSpec((pl.Squeezed(), 3, bn), lambda j, i: (0, 0, j)),
            stile, stile,
        ],
        out_specs=[
            tile,
            pl.BlockSpec((pl.Squeezed(), 2, bn), lambda j, i: (_prompt_tile(i) // tiles_per_seq, 0, j)),
            stile, stile,
        ],
        out_shape=[_sds((m, CONV_WIDTH), BF16), _sds((m // seq_len, 2, CONV_WIDTH), F32),
                   _sds((ms, CONV_WIDTH), BF16), _sds((ms, CONV_WIDTH), F32)],
        scratch_shapes=[pltpu.VMEM((D_MODEL, 3 * bn), BF16), pltpu.VMEM((2, bn), F32)],
        compiler_params=_params(2),
        name="convmix",
    )(h, hs, w_in, w_in, w_in, conv_w, prev2, prev1)


def _merge_kernel(h_ref, a_ref, b_ref, hs_ref, as_ref, bs_ref, wga_ref, wgb_ref, wa_ref, wb_ref, bg_ref,
                  o_ref, os_ref, wg_bf_ref, wa_bf_ref, wb_bf_ref):
    i = pl.program_id(1)
    bn = wa_ref.shape[1]

    @pl.when(i == 0)
    def _():
        wg_bf_ref[:, 0:bn] = wga_ref[...].astype(BF16)
        wg_bf_ref[:, bn:2 * bn] = wgb_ref[...].astype(BF16)
        wa_bf_ref[...] = wa_ref[...].astype(BF16)
        wb_bf_ref[...] = wb_ref[...].astype(BF16)

    def rows_fn(h_ref, a_ref, b_ref, o_ref):
        for rows in _row_chunks(h_ref.shape[0]):
            gates = _dot(h_ref[rows, :], wg_bf_ref[...])
            ga = jax.nn.sigmoid(gates[:, 0:bn] + bg_ref[0:1, :])
            gb = jax.nn.sigmoid(gates[:, bn:2 * bn] + bg_ref[1:2, :])
            merged = ga * _dot(a_ref[rows, :], wa_bf_ref[...]) + gb * _dot(b_ref[rows, :], wb_bf_ref[...])
            o_ref[rows, :] = merged.astype(BF16)

    @pl.when(i > 0)
    def _():
        rows_fn(h_ref, a_ref, b_ref, o_ref)

    @pl.when(i == 0)
    def _():
        rows_fn(hs_ref, as_ref, bs_ref, os_ref)


def _merge(h, a, b, hs, a_s, b_s, w_in, w_attn_branch, w_conv_branch, b_gate, *, bm, bn):
    m, ms = h.shape[0], hs.shape[0]
    n_m = m // bm
    row = lambda j, i: (_prompt_tile(i), 0)
    return pl.pallas_call(
        _merge_kernel,
        grid=(D_MODEL // bn, n_m + 1),
        in_specs=[
            pl.BlockSpec((bm, D_MODEL), row),
            pl.BlockSpec((bm, ATTN_WIDTH), row),
            pl.BlockSpec((bm, CONV_WIDTH), row),
            _resident(hs), _resident(a_s), _resident(b_s),
            _wcol(D_MODEL, bn, lambda j, i: OFF_GA + j * bn),
            _wcol(D_MODEL, bn, lambda j, i: OFF_GB + j * bn),
            pl.BlockSpec((pl.Squeezed(), ATTN_WIDTH, bn), lambda j, i: (0, 0, j)),
            pl.BlockSpec((pl.Squeezed(), CONV_WIDTH, bn), lambda j, i: (0, 0, j)),
            pl.BlockSpec((pl.Squeezed(), 2, bn), lambda j, i: (0, 0, j)),
        ],
        out_specs=[pl.BlockSpec((bm, bn), lambda j, i: (_prompt_tile(i), j)),
                   pl.BlockSpec((ms, bn), lambda j, i: (0, j))],
        out_shape=[_sds((m, D_MODEL), BF16), _sds((ms, D_MODEL), BF16)],
        scratch_shapes=[pltpu.VMEM((D_MODEL, 2 * bn), BF16), pltpu.VMEM((ATTN_WIDTH, bn), BF16),
                        pltpu.VMEM((CONV_WIDTH, bn), BF16)],
        compiler_params=_params(2),
        name="merge",
    )(h, a, b, hs, a_s, b_s, w_in, w_in, w_attn_branch, w_conv_branch, b_gate)


def _oproj_kernel(mg_ref, x_ref, mgs_ref, xs_ref, wo_ref, gpost_ref, gpre_ref,
                  x1_ref, h2_ref, x1s_ref, h2s_ref):
    i = pl.program_id(0)

    def rows_fn(mg_ref, x_ref, x1_ref, h2_ref):
        for rows in _row_chunks(x_ref.shape[0]):
            x1 = x_ref[rows, :] + _rms(_dot(mg_ref[rows, :], wo_ref[...]), gpost_ref[...])
            x1_ref[rows, :] = x1
            h2_ref[rows, :] = _rms(x1, gpre_ref[...]).astype(BF16)

    @pl.when(i > 0)
    def _():
        rows_fn(mg_ref, x_ref, x1_ref, h2_ref)

    @pl.when(i == 0)
    def _():
        rows_fn(mgs_ref, xs_ref, x1s_ref, h2s_ref)


def _oproj(merged, x, merged_s, xs, w_o, g_post, g_pre, *, bm):
    m, ms = x.shape[0], xs.shape[0]
    n_m = m // bm
    row = lambda i: (_prompt_tile(i), 0)
    tile = pl.BlockSpec((bm, D_MODEL), row)
    return pl.pallas_call(
        _oproj_kernel,
        grid=(n_m + 1,),
        in_specs=[
            tile, tile, _resident(merged_s), _resident(xs),
            pl.BlockSpec((pl.Squeezed(), D_MODEL, D_MODEL), lambda i: (0, 0, 0),
                         pipeline_mode=pl.Buffered(1)),
            _resident(g_post), _resident(g_pre),
        ],
        out_specs=[tile, tile, _resident(xs), _resident(merged_s)],
        out_shape=[_sds((m, D_MODEL), F32), _sds((m, D_MODEL), BF16),
                   _sds((ms, D_MODEL), F32), _sds((ms, D_MODEL), BF16)],
        compiler_params=_params(1),
        name="oproj_norm",
    )(merged, x, merged_s, xs, w_o, g_post, g_pre)


def _ffn_up_kernel(tiles_per_seq, h_ref, hs_ref, wa_ref, wg_ref, cw_ref, cb_ref, p2_ref, p1_ref,
                   o_ref, st_ref, os_ref, ups_ref, wbf_ref, halo_ref):
    i = pl.program_id(1)
    bn = wa_ref.shape[1]

    @pl.when(i == 0)
    def _():
        wbf_ref[:, 0:bn] = wa_ref[...].astype(BF16)
        wbf_ref[:, bn:2 * bn] = wg_ref[...].astype(BF16)

    def project(x):
        proj = _dot(x, wbf_ref[...])
        return proj[:, 0:bn], proj[:, bn:2 * bn]

    def gated(cur, s1, s2, gate):
        act = jax.nn.gelu(_conv3(cw_ref, cur, s1, s2) + cb_ref[...], approximate=True)
        return (act * gate).astype(BF16)

    @pl.when(i > 0)
    def _():
        def chunk(rows, shifts):
            cur, gate = project(h_ref[rows, :])
            s1, s2 = shifts(cur)
            o_ref[rows, :] = gated(cur, s1, s2, gate)
            return cur

        _conv_rows_prompt(i - 1, tiles_per_seq, h_ref.shape[0], halo_ref, st_ref, chunk)

    @pl.when(i == 0)
    def _():
        cur, gate = project(hs_ref[...])
        ups_ref[...] = cur
        os_ref[...] = gated(cur, p1_ref[...], p2_ref[...], gate)


def _ffn_up(h2, h2s, w_ffn_up, ffn_conv_w, ffn_conv_b, prev2, prev1, *, bm, bn, seq_len):
    m, ms = h2.shape[0], h2s.shape[0]
    n_m = m // bm
    tiles_per_seq = seq_len // bm
    n_tiles = D_FF // bn
    row = lambda j, i: (_prompt_tile(i), 0)
    tile = pl.BlockSpec((bm, bn), lambda j, i: (_prompt_tile(i), j))
    stile = pl.BlockSpec((ms, bn), lambda j, i: (0, j))
    return pl.pallas_call(
        functools.partial(_ffn_up_kernel, tiles_per_seq),
        grid=(n_tiles, n_m + 1),
        in_specs=[
            pl.BlockSpec((bm, D_MODEL), row),
            _resident(h2s),
            pl.BlockSpec((pl.Squeezed(), D_MODEL, bn), lambda j, i: (0, 0, j)),
            pl.BlockSpec((pl.Squeezed(), D_MODEL, bn), lambda j, i: (0, 0, n_tiles + j)),
            pl.BlockSpec((pl.Squeezed(), 3, bn), lambda j, i: (0, 0, j)),
            pl.BlockSpec((1, bn), lambda j, i: (0, j)),
            stile, stile,
        ],
        out_specs=[
            tile,
            pl.BlockSpec((pl.Squeezed(), 2, bn), lambda j, i: (_prompt_tile(i) // tiles_per_seq, 0, j)),
            stile, stile,
        ],
        out_shape=[_sds((m, D_FF), BF16), _sds((m // seq_len, 2, D_FF), F32),
                   _sds((ms, D_FF), BF16), _sds((ms, D_FF), F32)],
        scratch_shapes=[pltpu.VMEM((D_MODEL, 2 * bn), BF16), pltpu.VMEM((2, bn), F32)],
        compiler_params=_params(2),
        name="ffn_up",
    )(h2, h2s, w_ffn_up, w_ffn_up, ffn_conv_w, ffn_conv_b, prev2, prev1)


def _ffn_down_kernel(a_ref, as_ref, w_ref, o_ref, os_ref, wbf_ref):
    i = pl.program_id(1)

    @pl.when(i == 0)
    def _():
        wbf_ref[...] = w_ref[...].astype(BF16)

    @pl.when(i > 0)
    def _():
        for rows in _row_chunks(a_ref.shape[0]):
            o_ref[rows, :] = _dot(a_ref[rows, :], wbf_ref[...]).astype(BF16)

    @pl.when(i == 0)
    def _():
        os_ref[...] = _dot(as_ref[...], wbf_ref[...]).astype(BF16)


def _ffn_down(act, act_s, w_ffn_down, *, bm, bn):
    m, ms = act.shape[0], act_s.shape[0]
    n_m = m // bm
    return pl.pallas_call(
        _ffn_down_kernel,
        grid=(D_MODEL // bn, n_m + 1),
        in_specs=[
            pl.BlockSpec((bm, D_FF), lambda j, i: (_prompt_tile(i), 0)),
            _resident(act_s),
            pl.BlockSpec((pl.Squeezed(), D_FF, bn), lambda j, i: (0, 0, j)),
        ],
        out_specs=[pl.BlockSpec((bm, bn), lambda j, i: (_prompt_tile(i), j)),
                   pl.BlockSpec((ms, bn), lambda j, i: (0, j))],
        out_shape=[_sds((m, D_MODEL), BF16), _sds((ms, D_MODEL), BF16)],
        scratch_shapes=[pltpu.VMEM((D_FF, bn), BF16)],
        compiler_params=_params(2),
        name="ffn_down",
    )(act, act_s, w_ffn_down)


def _final_kernel(f_ref, x1_ref, p_ref, fs_ref, x1s_ref, ps_ref, g_ref, wg_ref, wp_ref, y_ref, ys_ref):
    i = pl.program_id(0)

    def rows_fn(f_ref, x1_ref, p_ref, y_ref):
        for rows in _row_chunks(x1_ref.shape[0]):
            x2 = x1_ref[rows, :] + _rms(f_ref[rows, :].astype(F32), g_ref[...])
            gate = jax.nn.sigmoid(_dot(x2.astype(BF16), wg_ref[...]))
            y_ref[rows, :] = x2 + gate * _dot(p_ref[rows, :].astype(BF16), wp_ref[...])

    @pl.when(i > 0)
    def _():
        rows_fn(f_ref, x1_ref, p_ref, y_ref)

    @pl.when(i == 0)
    def _():
        rows_fn(fs_ref, x1s_ref, ps_ref, ys_ref)


def _final(f, x1, p, f_s, x1_s, p_s, g_post, w_ple_gate, w_ple, *, bm):
    m, ms = x1.shape[0], x1_s.shape[0]
    n_m = m // bm
    row = lambda i: (_prompt_tile(i), 0)
    return pl.pallas_call(
        _final_kernel,
        grid=(n_m + 1,),
        in_specs=[
            pl.BlockSpec((bm, D_MODEL), row),
            pl.BlockSpec((bm, D_MODEL), row),
            pl.BlockSpec((bm, PLE_DIM), row),
            _resident(f_s), _resident(x1_s), _resident(p_s),
            _resident(g_post),
            pl.BlockSpec((pl.Squeezed(), D_MODEL, D_MODEL), lambda i: (0, 0, 0),
                         pipeline_mode=pl.Buffered(1)),
            pl.BlockSpec((pl.Squeezed(), PLE_DIM, D_MODEL), lambda i: (0, 0, 0),
                         pipeline_mode=pl.Buffered(1)),
        ],
        out_specs=[pl.BlockSpec((bm, D_MODEL), row), _resident(x1_s)],
        out_shape=[_sds((m, D_MODEL), F32), _sds((ms, D_MODEL), F32)],
        compiler_params=_params(1),
        name="ffn_norm_ple",
    )(f, x1, p, f_s, x1_s, p_s, g_post, w_ple_gate, w_ple)


def _rope_tables(pos):
    half = HEAD_DIM // 2
    inv = ROPE_THETA ** (-jnp.arange(half, dtype=F32) / half)
    ang = pos.astype(F32)[:, None] * inv[None, :]
    cos = jnp.cos(ang)
    sin = jnp.sin(ang)
    reps = LANES // HEAD_DIM
    return (jnp.tile(jnp.concatenate([cos, cos], axis=1), (1, reps)),
            jnp.tile(jnp.concatenate([-sin, sin], axis=1), (1, reps)))


TILES = dict(qkv=512, convmix=dict(bm=1024, bn=256), merge=dict(bm=512, bn=512), oproj=512,
             ffn_up=dict(bm=1024, bn=512), ffn_down=dict(bm=512, bn=512), final=512)


def kernel(x_prompt, x_sample, state_attn_k, state_attn_v, state_conv, state_ffn_conv, p_prompt, p_sample,
           norm_mix_pre, norm_mix_post, norm_ffn_pre, norm_ffn_post, w_in, attn_sinks, conv_w, w_attn_branch,
           w_conv_branch, b_gate, w_o, w_ffn_up, ffn_conv_w, ffn_conv_b, w_ffn_down, w_ple, w_ple_gate):
    assert w_in.shape[0] == 1, "weights are indexed with a squeezed depth axis of size 1"
    batch, seq, _ = x_prompt.shape
    ns, dec_seq, _ = x_sample.shape
    assert dec_seq == 1
    m = batch * seq
    x = x_prompt.reshape(m, D_MODEL)
    xs = x_sample.reshape(ns, D_MODEL)
    rope_p = _rope_tables(jnp.arange(seq, dtype=jnp.int32))
    rope_s = _rope_tables(jnp.full((ns,), PAST_LEN, dtype=jnp.int32))

    h, q, k, v, hs, qs, ks, vs = _qkv(x, xs, norm_mix_pre, w_in, rope_p, rope_s, bm=TILES["qkv"])

    a = _attn_prompt(attn_sinks.reshape(N_HEADS), q, k, v, n_seq=batch, seq_len=seq)
    a3, new_ks, new_vs = _attn_sample(
        attn_sinks.reshape(N_HEADS, 1), qs.reshape(ns, N_HEADS, HEAD_DIM), ks.reshape(ns, 1, KV_WIDTH),
        vs.reshape(ns, 1, KV_WIDTH), state_attn_k.reshape(ns, WINDOW, KV_WIDTH),
        state_attn_v.reshape(ns, WINDOW, KV_WIDTH))
    a_s = a3.reshape(ns, ATTN_WIDTH)

    b, new_conv_p, b_s, pre_s = _convmix(h, hs, w_in, conv_w, state_conv[0, :, 0], state_conv[0, :, 1],
                                         seq_len=seq, **TILES["convmix"])
    merged, merged_s = _merge(h, a, b, hs, a_s, b_s, w_in, w_attn_branch, w_conv_branch, b_gate, **TILES["merge"])
    x1, h2, x1_s, h2_s = _oproj(merged, x, merged_s, xs, w_o.astype(BF16), norm_mix_post, norm_ffn_pre,
                                bm=TILES["oproj"])
    act, new_ffn_p, act_s, up_s = _ffn_up(h2, h2_s, w_ffn_up, ffn_conv_w, ffn_conv_b, state_ffn_conv[0, :, 0],
                                          state_ffn_conv[0, :, 1], seq_len=seq, **TILES["ffn_up"])
    f, f_s = _ffn_down(act, act_s, w_ffn_down, **TILES["ffn_down"])
    y, y_s = _final(f, x1, p_prompt.reshape(m, PLE_DIM), f_s, x1_s, p_sample.reshape(ns, PLE_DIM),
                    norm_ffn_post, w_ple_gate.astype(BF16), w_ple.astype(BF16), bm=TILES["final"])

    kv_tail = lambda t: t.reshape(batch, seq, N_KV_HEADS, HEAD_DIM)[:, -WINDOW:]
    kv_state = lambda t: t.reshape(ns, WINDOW, N_KV_HEADS, HEAD_DIM)
    return (y.reshape(batch, seq, D_MODEL), y_s.reshape(ns, dec_seq, D_MODEL),
            kv_tail(k)[None], kv_tail(v)[None], new_conv_p[None], new_ffn_p[None],
            kv_state(new_ks)[None], kv_state(new_vs)[None],
            jnp.stack([state_conv[0, :, 1], pre_s], axis=1)[None],
            jnp.stack([state_ffn_conv[0, :, 1], up_s], axis=1)[None])
```

```python
import functools

import jax
import jax.numpy as jnp
import numpy as np
from jax import lax
from jax.experimental import pallas as pl
from jax.experimental.pallas import tpu as pltpu

D_MODEL = 2048
N_HEADS = 16
N_KV_HEADS = 2
HEAD_DIM = 64
GROUP = N_HEADS // N_KV_HEADS
ATTN_WIDTH = N_HEADS * HEAD_DIM
KV_WIDTH = N_KV_HEADS * HEAD_DIM
QKV_WIDTH = ATTN_WIDTH + 2 * KV_WIDTH
WINDOW = 128
PAST_LEN = 16384
ROPE_THETA = 10000.0
CONV_WIDTH = D_MODEL // 2
D_FF = 5632
PLE_DIM = 256
RMS_EPS = 1e-6
OFF_CB = QKV_WIDTH
OFF_CC = OFF_CB + CONV_WIDTH
OFF_CU = OFF_CC + CONV_WIDTH
OFF_GA = OFF_CU + CONV_WIDTH
OFF_GB = OFF_GA + D_MODEL

LANES = 128
MASKED = -1e30
VMEM_LIMIT = 56 * 1024 * 1024
ROW_CHUNK = 256

BF16 = jnp.bfloat16
F32 = jnp.float32


def _params(n_axes):
    return pltpu.CompilerParams(dimension_semantics=("arbitrary",) * n_axes,
                                vmem_limit_bytes=VMEM_LIMIT)


def _rms(x, g):
    return x * lax.rsqrt(jnp.mean(x * x, axis=-1, keepdims=True) + RMS_EPS) * g


def _dot(a, b):
    return jnp.dot(a, b, preferred_element_type=F32)


def _walk_chunks(i, bm, rc, chunk_fn):
    rc = min(bm, rc)

    def walk(first):
        for c in range(bm // rc):
            chunk_fn(pl.ds(c * rc, rc), first and c == 0)

    @pl.when(i == 0)
    def _():
        walk(True)

    @pl.when(i > 0)
    def _():
        walk(False)


def _ride(ref, rows, sample_ref, ride):
    x = ref[rows, :]
    return jnp.concatenate([x, sample_ref[...]], axis=0) if ride else x


def _wcol(k_rows, block_cols, col_fn):
    return pl.BlockSpec((pl.Squeezed(), pl.Element(k_rows), pl.Element(block_cols)),
                        lambda *g: (0, 0, pl.multiple_of(col_fn(*g), LANES)))


def _resident(arr):
    zeros = (0,) * arr.ndim
    return pl.BlockSpec(arr.shape, lambda *g: zeros)


def _sds(shape, dtype):
    return jax.ShapeDtypeStruct(shape, dtype)


def _qkv_kernel(x_ref, xs_ref, g_ref, w_ref, cos_ref, sin_ref, coss_ref, sins_ref,
                h_ref, q_ref, k_ref, v_ref, hs_ref, qs_ref, ks_ref, vs_ref, wbf_ref):
    i = pl.program_id(0)
    n_q = ATTN_WIDTH // LANES

    @pl.when(i == 0)
    def _():
        wbf_ref[...] = w_ref[...].astype(BF16)

    def chunk(rows, ride):
        n = rows.size

        def put(ref, sample_ref, cols, val):
            ref[rows, cols] = val[:n]
            if ride:
                sample_ref[:, cols] = val[n:]

        h = _rms(_ride(x_ref, rows, xs_ref, ride), g_ref[...]).astype(BF16)
        put(h_ref, hs_ref, slice(None), h)
        proj = _dot(h, wbf_ref[...])
        cos = _ride(cos_ref, rows, coss_ref, ride)
        sin = _ride(sin_ref, rows, sins_ref, ride)
        lane = lax.broadcasted_iota(jnp.int32, cos.shape, 1)
        first_half = (lane % HEAD_DIM) < (HEAD_DIM // 2)
        for c in range(n_q + 1):
            cols = slice(c * LANES, (c + 1) * LANES)
            xg = proj[:, cols]
            partner = jnp.where(first_half, pltpu.roll(xg, LANES - HEAD_DIM // 2, 1),
                                pltpu.roll(xg, HEAD_DIM // 2, 1))
            yg = xg * cos + partner * sin
            if c < n_q:
                put(q_ref, qs_ref, cols, (yg * (HEAD_DIM ** -0.5)).astype(BF16))
            else:
                put(k_ref, ks_ref, slice(None), yg)
        put(v_ref, vs_ref, slice(None), proj[:, ATTN_WIDTH + KV_WIDTH:QKV_WIDTH])

    _walk_chunks(i, x_ref.shape[0], ROW_CHUNK, chunk)


def _qkv(x, xs, g, w_in, rope_p, rope_s, *, bm):
    m, ms = x.shape[0], xs.shape[0]
    t_tiles = rope_p[0].shape[0] // bm
    row = lambda i: (i, 0)
    pos = lambda i: (i % t_tiles, 0)
    outs = lambda rows: [_sds((rows, D_MODEL), BF16), _sds((rows, ATTN_WIDTH), BF16),
                         _sds((rows, KV_WIDTH), F32), _sds((rows, KV_WIDTH), F32)]
    out_p, out_s = outs(m), outs(ms)
    return pl.pallas_call(
        _qkv_kernel,
        grid=(m // bm,),
        in_specs=[
            pl.BlockSpec((bm, D_MODEL), row),
            _resident(xs),
            _resident(g),
            pl.BlockSpec((pl.Squeezed(), D_MODEL, QKV_WIDTH), lambda i: (0, 0, 0),
                         pipeline_mode=pl.Buffered(1)),
            pl.BlockSpec((bm, LANES), pos),
            pl.BlockSpec((bm, LANES), pos),
            _resident(rope_s[0]),
            _resident(rope_s[1]),
        ],
        out_specs=[pl.BlockSpec((bm, s.shape[1]), row) for s in out_p] + [_resident(s) for s in out_s],
        out_shape=out_p + out_s,
        scratch_shapes=[pltpu.VMEM((D_MODEL, QKV_WIDTH), BF16)],
        compiler_params=_params(1),
        name="qkv_rope",
    )(x, xs, g, w_in, rope_p[0], rope_p[1], rope_s[0], rope_s[1])


def _attn_prompt_kernel(sinks_ref, q_ref, kc_ref, kp_ref, vc_ref, vp_ref, o_ref):
    n_sub = q_ref.shape[0] // WINDOW
    qi = lax.broadcasted_iota(jnp.int32, (WINDOW, WINDOW), 0)
    ci = lax.broadcasted_iota(jnp.int32, (WINDOW, WINDOW), 1)
    own = ci <= qi
    first_valid = own | (pl.program_id(1) > 0)
    k_all = jnp.concatenate([kp_ref[...], kc_ref[...]], axis=0).astype(BF16)
    v_all = jnp.concatenate([vp_ref[...], vc_ref[...]], axis=0).astype(BF16)
    for sub in range(n_sub):
        q_rows = slice(sub * WINDOW, (sub + 1) * WINDOW)
        kv_rows = slice(sub * WINDOW, (sub + 2) * WINDOW)
        for kh in range(N_KV_HEADS):
            cols = slice(kh * HEAD_DIM, (kh + 1) * HEAD_DIM)
            k = k_all[kv_rows, cols]
            v = v_all[kv_rows, cols]
            heads = range(kh * GROUP, (kh + 1) * GROUP)
            qs = jnp.concatenate([q_ref[q_rows, h * HEAD_DIM:(h + 1) * HEAD_DIM] for h in heads], axis=0)
            s_all = lax.dot_general(qs, k, (((1,), (1,)), ((), ())), preferred_element_type=F32)
            ps, dens = [], []
            for g, h in enumerate(heads):
                s2 = s_all[g * WINDOW:(g + 1) * WINDOW, :]
                s = jnp.where(own, s2[:, WINDOW:], s2[:, :WINDOW])
                if sub == 0:
                    s = jnp.where(first_valid, s, MASKED)
                sink = sinks_ref[h]
                m = jnp.maximum(jnp.max(s, axis=-1, keepdims=True), sink)
                p = jnp.exp(s - m)
                dens.append(jnp.sum(p, axis=-1, keepdims=True) + jnp.exp(sink - m))
                ps.append(jnp.concatenate([jnp.where(own, 0.0, p), jnp.where(own, p, 0.0)], axis=1).astype(BF16))
            o_all = _dot(jnp.concatenate(ps, axis=0), v)
            for g, h in enumerate(heads):
                o = o_all[g * WINDOW:(g + 1) * WINDOW, :] / dens[g]
                o_ref[q_rows, h * HEAD_DIM:(h + 1) * HEAD_DIM] = o.astype(BF16)


def _attn_prompt(sinks, q, k, v, *, n_seq, seq_len, bq):
    nb = seq_len // bq
    sub = bq // WINDOW
    cur = lambda n, b: (n * nb + b, 0)
    prev = lambda n, b: (jnp.maximum((n * nb + b) * sub - 1, 0), 0)
    return pl.pallas_call(
        _attn_prompt_kernel,
        grid=(n_seq, nb),
        in_specs=[
            pl.BlockSpec(memory_space=pltpu.SMEM),
            pl.BlockSpec((bq, ATTN_WIDTH), cur),
            pl.BlockSpec((bq, KV_WIDTH), cur),
            pl.BlockSpec((WINDOW, KV_WIDTH), prev),
            pl.BlockSpec((bq, KV_WIDTH), cur),
            pl.BlockSpec((WINDOW, KV_WIDTH), prev),
        ],
        out_specs=pl.BlockSpec((bq, ATTN_WIDTH), cur),
        out_shape=_sds((n_seq * seq_len, ATTN_WIDTH), BF16),
        compiler_params=_params(2),
        name="attn_prompt",
    )(sinks, q, k, k, v, v)


def _attn_sample_kernel(sinks_ref, q_ref, kn_ref, vn_ref, sk_ref, sv_ref, a_ref, nk_ref, nv_ref):
    nk_ref[:, 0:WINDOW - 1, :] = sk_ref[:, 1:WINDOW, :]
    nk_ref[:, WINDOW - 1:WINDOW, :] = kn_ref[...]
    nv_ref[:, 0:WINDOW - 1, :] = sv_ref[:, 1:WINDOW, :]
    nv_ref[:, WINDOW - 1:WINDOW, :] = vn_ref[...]
    for kh in range(N_KV_HEADS):
        cols = slice(kh * HEAD_DIM, (kh + 1) * HEAD_DIM)
        heads = slice(kh * GROUP, (kh + 1) * GROUP)
        k = nk_ref[:, :, cols].astype(BF16)
        v = nv_ref[:, :, cols].astype(BF16)
        s = jnp.einsum("ngd,ncd->ngc", q_ref[:, heads, :], k, preferred_element_type=F32)
        sink = sinks_ref[heads, :][None]
        m = jnp.maximum(jnp.max(s, axis=-1, keepdims=True), sink)
        p = jnp.exp(s - m)
        den = jnp.sum(p, axis=-1, keepdims=True) + jnp.exp(sink - m)
        o = jnp.einsum("ngc,ncd->ngd", p.astype(BF16), v, preferred_element_type=F32) / den
        a_ref[:, heads, :] = o.astype(BF16)


def _attn_sample(sinks_col, q3, k_new3, v_new3, state_k, state_v):
    n = q3.shape[0]
    return pl.pallas_call(
        _attn_sample_kernel,
        out_shape=[_sds((n, N_HEADS, HEAD_DIM), BF16), _sds((n, WINDOW, KV_WIDTH), F32),
                   _sds((n, WINDOW, KV_WIDTH), F32)],
        compiler_params=pltpu.CompilerParams(vmem_limit_bytes=VMEM_LIMIT),
        name="attn_sample",
    )(sinks_col, q3, k_new3, v_new3, state_k, state_v)


def _shift_rows(cur, halo):
    rid = lax.broadcasted_iota(jnp.int32, cur.shape, 0)
    h0 = halo[0:1, :]
    h1 = halo[1:2, :]
    s1 = jnp.where(rid == 0, h1, pltpu.roll(cur, 1, 0))
    s2 = jnp.where(rid == 0, h0, jnp.where(rid == 1, h1, pltpu.roll(cur, 2, 0)))
    return s1, s2


def _conv3(cw, cur, s1, s2):
    return cw[0:1, :] * s2 + cw[1:2, :] * s1 + cw[2:3, :] * cur


def _conv_tile(i, tiles_per_seq, split, bm, halo_ref, st_ref, group_fn):
    gw = halo_ref.shape[1] // split
    gcols = [slice(grp * gw, (grp + 1) * gw) for grp in range(split)]

    @pl.when(i % tiles_per_seq == 0)
    def _():
        halo_ref[...] = jnp.zeros_like(halo_ref)

    halos = {}

    def chunk(rows, ride):
        for grp in range(split):
            halo = halos.get(grp)
            if halo is None:
                halo = halo_ref[:, gcols[grp]]
            cur = group_fn(rows, ride, grp, lambda cur, halo=halo: _shift_rows(cur, halo))
            halos[grp] = cur[cur.shape[0] - 2:, :]
        if rows.start + rows.size == bm:
            for grp in range(split):
                halo_ref[:, gcols[grp]] = halos.pop(grp)

    _walk_chunks(i, bm, ROW_CHUNK, chunk)

    @pl.when(i % tiles_per_seq == tiles_per_seq - 1)
    def _():
        st_ref[...] = halo_ref[...]


def _convmix_kernel(tiles_per_seq, split, h_ref, hs_ref, wb_ref, wc_ref, wu_ref, cw_ref, p2_ref, p1_ref,
                    o_ref, st_ref, os_ref, pres_ref, wbf_ref, halo_ref):
    i = pl.program_id(1)
    gw = wb_ref.shape[1] // split

    @pl.when(i == 0)
    def _():
        for grp in range(split):
            src = slice(grp * gw, (grp + 1) * gw)
            for part, w_ref in enumerate((wb_ref, wc_ref, wu_ref)):
                wbf_ref[:, (3 * grp + part) * gw:(3 * grp + part + 1) * gw] = w_ref[:, src].astype(BF16)

    def group(rows, ride, grp, shifts):
        n = rows.size
        cols = slice(grp * gw, (grp + 1) * gw)
        proj = _dot(_ride(h_ref, rows, hs_ref, ride), wbf_ref[:, 3 * grp * gw:3 * (grp + 1) * gw])
        cb, cur = proj[:, 0:gw], proj[:, gw:2 * gw] * proj[:, 2 * gw:3 * gw]
        cw = cw_ref[:, cols]
        s1, s2 = shifts(cur[:n])
        o_ref[rows, cols] = (cb[:n] * _conv3(cw, cur[:n], s1, s2)).astype(BF16)
        if ride:
            pres_ref[:, cols] = cur[n:]
            os_ref[:, cols] = (cb[n:] * _conv3(cw, cur[n:], p1_ref[:, cols], p2_ref[:, cols])).astype(BF16)
        return cur[:n]

    _conv_tile(i, tiles_per_seq, split, h_ref.shape[0], halo_ref, st_ref, group)


def _convmix(h, hs, w_in, conv_w, prev2, prev1, *, bm, bn, split, seq_len):
    m, ms = h.shape[0], hs.shape[0]
    tiles_per_seq = seq_len // bm
    tile = pl.BlockSpec((bm, bn), lambda j, i: (i, j))
    stile = pl.BlockSpec((ms, bn), lambda j, i: (0, j))
    wspec = lambda off: _wcol(D_MODEL, bn, lambda j, i: off + j * bn)
    return pl.pallas_call(
        functools.partial(_convmix_kernel, tiles_per_seq, split),
        grid=(CONV_WIDTH // bn, m // bm),
        in_specs=[
            pl.BlockSpec((bm, D_MODEL), lambda j, i: (i, 0)),
            _resident(hs),
            wspec(OFF_CB), wspec(OFF_CC), wspec(OFF_CU),
            pl.BlockSpec((pl.Squeezed(), 3, bn), lambda j, i: (0, 0, j)),
            stile, stile,
        ],
        out_specs=[
            tile,
            pl.BlockSpec((pl.Squeezed(), 2, bn), lambda j, i: (i // tiles_per_seq, 0, j)),
            stile, stile,
        ],
        out_shape=[_sds((m, CONV_WIDTH), BF16), _sds((m // seq_len, 2, CONV_WIDTH), F32),
                   _sds((ms, CONV_WIDTH), BF16), _sds((ms, CONV_WIDTH), F32)],
        scratch_shapes=[pltpu.VMEM((D_MODEL, 3 * bn), BF16), pltpu.VMEM((2, bn), F32)],
        compiler_params=_params(2),
        name="convmix",
    )(h, hs, w_in, w_in, w_in, conv_w, prev2, prev1)


def _merge_kernel(split, h_ref, a_ref, b_ref, hs_ref, as_ref, bs_ref, wga_ref, wgb_ref, wa_ref, wb_ref, bg_ref,
                  o_ref, os_ref, wg_bf_ref, wa_bf_ref, wb_bf_ref):
    i = pl.program_id(1)
    gw = wa_ref.shape[1] // split

    @pl.when(i == 0)
    def _():
        for grp in range(split):
            src = slice(grp * gw, (grp + 1) * gw)
            wg_bf_ref[:, 2 * grp * gw:(2 * grp + 1) * gw] = wga_ref[:, src].astype(BF16)
            wg_bf_ref[:, (2 * grp + 1) * gw:(2 * grp + 2) * gw] = wgb_ref[:, src].astype(BF16)
        wa_bf_ref[...] = wa_ref[...].astype(BF16)
        wb_bf_ref[...] = wb_ref[...].astype(BF16)

    def chunk(rows, ride):
        n = rows.size
        h = _ride(h_ref, rows, hs_ref, ride)
        a = _ride(a_ref, rows, as_ref, ride)
        b = _ride(b_ref, rows, bs_ref, ride)
        for grp in range(split):
            cols = slice(grp * gw, (grp + 1) * gw)
            gates = _dot(h, wg_bf_ref[:, 2 * grp * gw:2 * (grp + 1) * gw])
            ga = jax.nn.sigmoid(gates[:, 0:gw] + bg_ref[0:1, cols])
            gb = jax.nn.sigmoid(gates[:, gw:2 * gw] + bg_ref[1:2, cols])
            merged = (ga * _dot(a, wa_bf_ref[:, cols]) + gb * _dot(b, wb_bf_ref[:, cols])).astype(BF16)
            o_ref[rows, cols] = merged[:n]
            if ride:
                os_ref[:, cols] = merged[n:]

    _walk_chunks(i, h_ref.shape[0], ROW_CHUNK, chunk)


def _merge(h, a, b, hs, a_s, b_s, w_in, w_attn_branch, w_conv_branch, b_gate, *, bm, bn, split):
    m, ms = h.shape[0], hs.shape[0]
    row = lambda j, i: (i, 0)
    return pl.pallas_call(
        functools.partial(_merge_kernel, split),
        grid=(D_MODEL // bn, m // bm),
        in_specs=[
            pl.BlockSpec((bm, D_MODEL), row),
            pl.BlockSpec((bm, ATTN_WIDTH), row),
            pl.BlockSpec((bm, CONV_WIDTH), row),
            _resident(hs), _resident(a_s), _resident(b_s),
            _wcol(D_MODEL, bn, lambda j, i: OFF_GA + j * bn),
            _wcol(D_MODEL, bn, lambda j, i: OFF_GB + j * bn),
            pl.BlockSpec((pl.Squeezed(), ATTN_WIDTH, bn), lambda j, i: (0, 0, j)),
            pl.BlockSpec((pl.Squeezed(), CONV_WIDTH, bn), lambda j, i: (0, 0, j)),
            pl.BlockSpec((pl.Squeezed(), 2, bn), lambda j, i: (0, 0, j)),
        ],
        out_specs=[pl.BlockSpec((bm, bn), lambda j, i: (i, j)),
                   pl.BlockSpec((ms, bn), lambda j, i: (0, j))],
        out_shape=[_sds((m, D_MODEL), BF16), _sds((ms, D_MODEL), BF16)],
        scratch_shapes=[pltpu.VMEM((D_MODEL, 2 * bn), BF16), pltpu.VMEM((ATTN_WIDTH, bn), BF16),
                        pltpu.VMEM((CONV_WIDTH, bn), BF16)],
        compiler_params=_params(2),
        name="merge",
    )(h, a, b, hs, a_s, b_s, w_in, w_in, w_attn_branch, w_conv_branch, b_gate)


def _oproj_kernel(rc, mg_ref, x_ref, mgs_ref, xs_ref, wo_ref, gpost_ref, gpre_ref,
                  x1_ref, h2_ref, x1s_ref, h2s_ref):
    def chunk(rows, ride):
        n = rows.size
        y = _dot(_ride(mg_ref, rows, mgs_ref, ride), wo_ref[...])
        x1 = _ride(x_ref, rows, xs_ref, ride) + _rms(y, gpost_ref[...])
        h2 = _rms(x1, gpre_ref[...]).astype(BF16)
        x1_ref[rows, :] = x1[:n]
        h2_ref[rows, :] = h2[:n]
        if ride:
            x1s_ref[...] = x1[n:]
            h2s_ref[...] = h2[n:]

    _walk_chunks(pl.program_id(0), x_ref.shape[0], rc, chunk)


def _oproj(merged, x, merged_s, xs, w_o, g_post, g_pre, *, bm, rc):
    m, ms = x.shape[0], xs.shape[0]
    tile = pl.BlockSpec((bm, D_MODEL), lambda i: (i, 0))
    return pl.pallas_call(
        functools.partial(_oproj_kernel, rc),
        grid=(m // bm,),
        in_specs=[
            tile, tile, _resident(merged_s), _resident(xs),
            pl.BlockSpec((pl.Squeezed(), D_MODEL, D_MODEL), lambda i: (0, 0, 0),
                         pipeline_mode=pl.Buffered(1)),
            _resident(g_post), _resident(g_pre),
        ],
        out_specs=[tile, tile, _resident(xs), _resident(merged_s)],
        out_shape=[_sds((m, D_MODEL), F32), _sds((m, D_MODEL), BF16),
                   _sds((ms, D_MODEL), F32), _sds((ms, D_MODEL), BF16)],
        compiler_params=_params(1),
        name="oproj_norm",
    )(merged, x, merged_s, xs, w_o, g_post, g_pre)


def _ffn_up_kernel(tiles_per_seq, split, h_ref, hs_ref, wa_ref, wg_ref, cw_ref, cb_ref, p2_ref, p1_ref,
                   o_ref, st_ref, os_ref, ups_ref, wbf_ref, halo_ref):
    i = pl.program_id(1)
    gw = wa_ref.shape[1] // split

    @pl.when(i == 0)
    def _():
        for grp in range(split):
            src = slice(grp * gw, (grp + 1) * gw)
            wbf_ref[:, 2 * grp * gw:(2 * grp + 1) * gw] = wa_ref[:, src].astype(BF16)
            wbf_ref[:, (2 * grp + 1) * gw:(2 * grp + 2) * gw] = wg_ref[:, src].astype(BF16)

    def group(rows, ride, grp, shifts):
        n = rows.size
        cols = slice(grp * gw, (grp + 1) * gw)
        proj = _dot(_ride(h_ref, rows, hs_ref, ride), wbf_ref[:, 2 * grp * gw:2 * (grp + 1) * gw])
        cur, gate = proj[:, 0:gw], proj[:, gw:2 * gw]
        cw = cw_ref[:, cols]

        def gated(cur, s1, s2, gate):
            act = jax.nn.gelu(_conv3(cw, cur, s1, s2) + cb_ref[:, cols], approximate=True)
            return (act * gate).astype(BF16)

        s1, s2 = shifts(cur[:n])
        o_ref[rows, cols] = gated(cur[:n], s1, s2, gate[:n])
        if ride:
            ups_ref[:, cols] = cur[n:]
            os_ref[:, cols] = gated(cur[n:], p1_ref[:, cols], p2_ref[:, cols], gate[n:])
        return cur[:n]

    _conv_tile(i, tiles_per_seq, split, h_ref.shape[0], halo_ref, st_ref, group)


def _ffn_up(h2, h2s, w_ffn_up, ffn_conv_w, ffn_conv_b, prev2, prev1, *, bm, bn, split, seq_len):
    m, ms = h2.shape[0], h2s.shape[0]
    tiles_per_seq = seq_len // bm
    n_tiles = D_FF // bn
    tile = pl.BlockSpec((bm, bn), lambda j, i: (i, j))
    stile = pl.BlockSpec((ms, bn), lambda j, i: (0, j))
    return pl.pallas_call(
        functools.partial(_ffn_up_kernel, tiles_per_seq, split),
        grid=(n_tiles, m // bm),
        in_specs=[
            pl.BlockSpec((bm, D_MODEL), lambda j, i: (i, 0)),
            _resident(h2s),
            pl.BlockSpec((pl.Squeezed(), D_MODEL, bn), lambda j, i: (0, 0, j)),
            pl.BlockSpec((pl.Squeezed(), D_MODEL, bn), lambda j, i: (0, 0, n_tiles + j)),
            pl.BlockSpec((pl.Squeezed(), 3, bn), lambda j, i: (0, 0, j)),
            pl.BlockSpec((1, bn), lambda j, i: (0, j)),
            stile, stile,
        ],
        out_specs=[
            tile,
            pl.BlockSpec((pl.Squeezed(), 2, bn), lambda j, i: (i // tiles_per_seq, 0, j)),
            stile, stile,
        ],
        out_shape=[_sds((m, D_FF), BF16), _sds((m // seq_len, 2, D_FF), F32),
                   _sds((ms, D_FF), BF16), _sds((ms, D_FF), F32)],
        scratch_shapes=[pltpu.VMEM((D_MODEL, 2 * bn), BF16), pltpu.VMEM((2, bn), F32)],
        compiler_params=_params(2),
        name="ffn_up",
    )(h2, h2s, w_ffn_up, w_ffn_up, ffn_conv_w, ffn_conv_b, prev2, prev1)


def _ffn_down_kernel(a_ref, as_ref, w_ref, o_ref, os_ref, wbf_ref):
    i = pl.program_id(1)

    @pl.when(i == 0)
    def _():
        wbf_ref[...] = w_ref[...].astype(BF16)

    def chunk(rows, ride):
        n = rows.size
        o = _dot(_ride(a_ref, rows, as_ref, ride), wbf_ref[...]).astype(BF16)
        o_ref[rows, :] = o[:n]
        if ride:
            os_ref[...] = o[n:]

    _walk_chunks(i, a_ref.shape[0], ROW_CHUNK, chunk)


def _ffn_down(act, act_s, w_ffn_down, *, bm, bn):
    m, ms = act.shape[0], act_s.shape[0]
    return pl.pallas_call(
        _ffn_down_kernel,
        grid=(D_MODEL // bn, m // bm),
        in_specs=[
            pl.BlockSpec((bm, D_FF), lambda j, i: (i, 0)),
            _resident(act_s),
            pl.BlockSpec((pl.Squeezed(), D_FF, bn), lambda j, i: (0, 0, j)),
        ],
        out_specs=[pl.BlockSpec((bm, bn), lambda j, i: (i, j)),
                   pl.BlockSpec((ms, bn), lambda j, i: (0, j))],
        out_shape=[_sds((m, D_MODEL), BF16), _sds((ms, D_MODEL), BF16)],
        scratch_shapes=[pltpu.VMEM((D_FF, bn), BF16)],
        compiler_params=_params(2),
        name="ffn_down",
    )(act, act_s, w_ffn_down)


def _final_kernel(f_ref, x1_ref, p_ref, fs_ref, x1s_ref, ps_ref, g_ref, wg_ref, wp_ref, y_ref, ys_ref):
    def chunk(rows, ride):
        n = rows.size
        f = _ride(f_ref, rows, fs_ref, ride).astype(F32)
        x2 = _ride(x1_ref, rows, x1s_ref, ride) + _rms(f, g_ref[...])
        gate = jax.nn.sigmoid(_dot(x2.astype(BF16), wg_ref[...]))
        y = x2 + gate * _dot(_ride(p_ref, rows, ps_ref, ride).astype(BF16), wp_ref[...])
        y_ref[rows, :] = y[:n]
        if ride:
            ys_ref[...] = y[n:]

    _walk_chunks(pl.program_id(0), x1_ref.shape[0], ROW_CHUNK, chunk)


def _final(f, x1, p, f_s, x1_s, p_s, g_post, w_ple_gate, w_ple, *, bm):
    m, ms = x1.shape[0], x1_s.shape[0]
    row = lambda i: (i, 0)
    return pl.pallas_call(
        _final_kernel,
        grid=(m // bm,),
        in_specs=[
            pl.BlockSpec((bm, D_MODEL), row),
            pl.BlockSpec((bm, D_MODEL), row),
            pl.BlockSpec((bm, PLE_DIM), row),
            _resident(f_s), _resident(x1_s), _resident(p_s),
            _resident(g_post),
            pl.BlockSpec((pl.Squeezed(), D_MODEL, D_MODEL), lambda i: (0, 0, 0),
                         pipeline_mode=pl.Buffered(1)),
            pl.BlockSpec((pl.Squeezed(), PLE_DIM, D_MODEL), lambda i: (0, 0, 0),
                         pipeline_mode=pl.Buffered(1)),
        ],
        out_specs=[pl.BlockSpec((bm, D_MODEL), row), _resident(x1_s)],
        out_shape=[_sds((m, D_MODEL), F32), _sds((ms, D_MODEL), F32)],
        compiler_params=_params(1),
        name="ffn_norm_ple",
    )(f, x1, p, f_s, x1_s, p_s, g_post, w_ple_gate, w_ple)


def _rope_tables(pos):
    half = HEAD_DIM // 2
    inv = ROPE_THETA ** (-np.arange(half, dtype=np.float64) / half)
    ang = np.asarray(pos, dtype=np.float64)[:, None] * inv[None, :]
    cos, sin = np.cos(ang), np.sin(ang)
    reps = LANES // HEAD_DIM
    return (jnp.asarray(np.tile(np.concatenate([cos, cos], axis=1), (1, reps)), dtype=F32),
            jnp.asarray(np.tile(np.concatenate([-sin, sin], axis=1), (1, reps)), dtype=F32))


TILES = dict(qkv=512, attn=128, convmix=dict(bm=1024, bn=512, split=2), merge=dict(bm=512, bn=512, split=2),
             oproj=dict(bm=512, rc=128), ffn_up=dict(bm=1024, bn=512, split=1), ffn_down=dict(bm=512, bn=512),
             final=512)


def kernel(x_prompt, x_sample, state_attn_k, state_attn_v, state_conv, state_ffn_conv, p_prompt, p_sample,
           norm_mix_pre, norm_mix_post, norm_ffn_pre, norm_ffn_post, w_in, attn_sinks, conv_w, w_attn_branch,
           w_conv_branch, b_gate, w_o, w_ffn_up, ffn_conv_w, ffn_conv_b, w_ffn_down, w_ple, w_ple_gate):
    assert w_in.shape[0] == 1, "weights are indexed with a squeezed depth axis of size 1"
    batch, seq, _ = x_prompt.shape
    ns, dec_seq, _ = x_sample.shape
    assert dec_seq == 1
    m = batch * seq
    x = x_prompt.reshape(m, D_MODEL)
    xs = x_sample.reshape(ns, D_MODEL)
    rope_p = _rope_tables(np.arange(seq))
    rope_s = _rope_tables(np.full((ns,), PAST_LEN))

    h, q, k, v, hs, qs, ks, vs = _qkv(x, xs, norm_mix_pre, w_in, rope_p, rope_s, bm=TILES["qkv"])

    a = _attn_prompt(attn_sinks.reshape(N_HEADS), q, k, v, n_seq=batch, seq_len=seq, bq=TILES["attn"])
    a3, new_ks, new_vs = _attn_sample(
        attn_sinks.reshape(N_HEADS, 1), qs.reshape(ns, N_HEADS, HEAD_DIM), ks.reshape(ns, 1, KV_WIDTH),
        vs.reshape(ns, 1, KV_WIDTH), state_attn_k.reshape(ns, WINDOW, KV_WIDTH),
        state_attn_v.reshape(ns, WINDOW, KV_WIDTH))
    a_s = a3.reshape(ns, ATTN_WIDTH)

    b, new_conv_p, b_s, pre_s = _convmix(h, hs, w_in, conv_w, state_conv[0, :, 0], state_conv[0, :, 1],
                                         seq_len=seq, **TILES["convmix"])
    merged, merged_s = _merge(h, a, b, hs, a_s, b_s, w_in, w_attn_branch, w_conv_branch, b_gate, **TILES["merge"])
    x1, h2, x1_s, h2_s = _oproj(merged, x, merged_s, xs, w_o.astype(BF16), norm_mix_post, norm_ffn_pre,
                                **TILES["oproj"])
    act, new_ffn_p, act_s, up_s = _ffn_up(h2, h2_s, w_ffn_up, ffn_conv_w, ffn_conv_b, state_ffn_conv[0, :, 0],
                                          state_ffn_conv[0, :, 1], seq_len=seq, **TILES["ffn_up"])
    f, f_s = _ffn_down(act, act_s, w_ffn_down, **TILES["ffn_down"])
    y, y_s = _final(f, x1, p_prompt.reshape(m, PLE_DIM), f_s, x1_s, p_sample.reshape(ns, PLE_DIM),
                    norm_ffn_post, w_ple_gate.astype(BF16), w_ple.astype(BF16), bm=TILES["final"])

    kv_tail = lambda t: t.reshape(batch, seq, N_KV_HEADS, HEAD_DIM)[:, -WINDOW:]
    kv_state = lambda t: t.reshape(ns, WINDOW, N_KV_HEADS, HEAD_DIM)
    return (y.reshape(batch, seq, D_MODEL), y_s.reshape(ns, dec_seq, D_MODEL),
            kv_tail(k)[None], kv_tail(v)[None], new_conv_p[None], new_ffn_p[None],
            kv_state(new_ks)[None], kv_state(new_vs)[None],
            jnp.stack([state_conv[0, :, 1], pre_s], axis=1)[None],
            jnp.stack([state_ffn_conv[0, :, 1], up_s], axis=1)[None])
```

```python
import functools

import jax
import jax.numpy as jnp
import numpy as np
from jax import lax
from jax.experimental import pallas as pl
from jax.experimental.pallas import tpu as pltpu

D_MODEL = 2048
N_HEADS = 16
N_KV_HEADS = 2
HEAD_DIM = 64
GROUP = N_HEADS // N_KV_HEADS
ATTN_WIDTH = N_HEADS * HEAD_DIM
KV_WIDTH = N_KV_HEADS * HEAD_DIM
QKV_WIDTH = ATTN_WIDTH + 2 * KV_WIDTH
WINDOW = 128
PAST_LEN = 16384
ROPE_THETA = 10000.0
CONV_WIDTH = D_MODEL // 2
D_FF = 5632
PLE_DIM = 256
RMS_EPS = 1e-6
OFF_CB = QKV_WIDTH
OFF_CC = OFF_CB + CONV_WIDTH
OFF_CU = OFF_CC + CONV_WIDTH
OFF_GA = OFF_CU + CONV_WIDTH
OFF_GB = OFF_GA + D_MODEL

LANES = 128
MASKED = -1e30
VMEM_LIMIT = 56 * 1024 * 1024

BF16 = jnp.bfloat16
F32 = jnp.float32


def _params(n_axes):
    return pltpu.CompilerParams(dimension_semantics=("arbitrary",) * n_axes,
                                vmem_limit_bytes=VMEM_LIMIT)


def _rms(x, g):
    return x * lax.rsqrt(jnp.mean(x * x, axis=-1, keepdims=True) + RMS_EPS) * g


def _dot(a, b):
    return jnp.dot(a, b, preferred_element_type=F32)


def _walk_chunks(i, bm, rc, chunk_fn):
    rc = min(bm, rc)

    def walk(first):
        for c in range(bm // rc):
            chunk_fn(pl.ds(c * rc, rc), first and c == 0)

    @pl.when(i == 0)
    def _():
        walk(True)

    @pl.when(i > 0)
    def _():
        walk(False)


def _ride(ref, rows, sample_ref, ride):
    x = ref[rows, :]
    return jnp.concatenate([x, sample_ref[...]], axis=0) if ride else x


def _wcol(k_rows, block_cols, col_fn):
    return pl.BlockSpec((pl.Squeezed(), pl.Element(k_rows), pl.Element(block_cols)),
                        lambda *g: (0, 0, pl.multiple_of(col_fn(*g), LANES)))


def _resident(arr):
    zeros = (0,) * arr.ndim
    return pl.BlockSpec(arr.shape, lambda *g: zeros)


def _sds(shape, dtype):
    return jax.ShapeDtypeStruct(shape, dtype)


def _qkv_kernel(rc, x_ref, xs_ref, g_ref, w_ref, cos_ref, sin_ref, coss_ref, sins_ref,
                h_ref, q_ref, k_ref, v_ref, hs_ref, qs_ref, ks_ref, vs_ref, wbf_ref):
    i = pl.program_id(0)
    n_q = ATTN_WIDTH // LANES

    @pl.when(i == 0)
    def _():
        wbf_ref[...] = w_ref[...].astype(BF16)

    def chunk(rows, ride):
        n = rows.size

        def put(ref, sample_ref, cols, val):
            ref[rows, cols] = val[:n]
            if ride:
                sample_ref[:, cols] = val[n:]

        h = _rms(_ride(x_ref, rows, xs_ref, ride), g_ref[...]).astype(BF16)
        put(h_ref, hs_ref, slice(None), h)
        proj = _dot(h, wbf_ref[...])
        cos = _ride(cos_ref, rows, coss_ref, ride)
        sin = _ride(sin_ref, rows, sins_ref, ride)
        lane = lax.broadcasted_iota(jnp.int32, cos.shape, 1)
        first_half = (lane % HEAD_DIM) < (HEAD_DIM // 2)
        for c in range(n_q + 1):
            cols = slice(c * LANES, (c + 1) * LANES)
            xg = proj[:, cols]
            partner = jnp.where(first_half, pltpu.roll(xg, LANES - HEAD_DIM // 2, 1),
                                pltpu.roll(xg, HEAD_DIM // 2, 1))
            yg = xg * cos + partner * sin
            if c < n_q:
                put(q_ref, qs_ref, cols, (yg * (HEAD_DIM ** -0.5)).astype(BF16))
            else:
                put(k_ref, ks_ref, slice(None), yg)
        put(v_ref, vs_ref, slice(None), proj[:, ATTN_WIDTH + KV_WIDTH:QKV_WIDTH])

    _walk_chunks(i, x_ref.shape[0], rc, chunk)


def _qkv(x, xs, g, w_in, rope_p, rope_s, *, bm, rc):
    m, ms = x.shape[0], xs.shape[0]
    t_tiles = rope_p[0].shape[0] // bm
    row = lambda i: (i, 0)
    pos = lambda i: (i % t_tiles, 0)
    outs = lambda rows: [_sds((rows, D_MODEL), BF16), _sds((rows, ATTN_WIDTH), BF16),
                         _sds((rows, KV_WIDTH), F32), _sds((rows, KV_WIDTH), F32)]
    out_p, out_s = outs(m), outs(ms)
    return pl.pallas_call(
        functools.partial(_qkv_kernel, rc),
        grid=(m // bm,),
        in_specs=[
            pl.BlockSpec((bm, D_MODEL), row),
            _resident(xs),
            _resident(g),
            pl.BlockSpec((pl.Squeezed(), D_MODEL, QKV_WIDTH), lambda i: (0, 0, 0),
                         pipeline_mode=pl.Buffered(1)),
            pl.BlockSpec((bm, LANES), pos),
            pl.BlockSpec((bm, LANES), pos),
            _resident(rope_s[0]),
            _resident(rope_s[1]),
        ],
        out_specs=[pl.BlockSpec((bm, s.shape[1]), row) for s in out_p] + [_resident(s) for s in out_s],
        out_shape=out_p + out_s,
        scratch_shapes=[pltpu.VMEM((D_MODEL, QKV_WIDTH), BF16)],
        compiler_params=_params(1),
        name="qkv_rope",
    )(x, xs, g, w_in, rope_p[0], rope_p[1], rope_s[0], rope_s[1])


def _attn_prompt_kernel(sinks_ref, q_ref, kc_ref, kp_ref, vc_ref, vp_ref, o_ref):
    n_sub = q_ref.shape[0] // WINDOW
    qi = lax.broadcasted_iota(jnp.int32, (WINDOW, WINDOW), 0)
    ci = lax.broadcasted_iota(jnp.int32, (WINDOW, WINDOW), 1)
    own = ci <= qi
    first_valid = own | (pl.program_id(1) > 0)
    k_all = jnp.concatenate([kp_ref[...], kc_ref[...]], axis=0).astype(BF16)
    v_all = jnp.concatenate([vp_ref[...], vc_ref[...]], axis=0).astype(BF16)
    for sub in range(n_sub):
        q_rows = slice(sub * WINDOW, (sub + 1) * WINDOW)
        kv_rows = slice(sub * WINDOW, (sub + 2) * WINDOW)
        for kh in range(N_KV_HEADS):
            cols = slice(kh * HEAD_DIM, (kh + 1) * HEAD_DIM)
            k = k_all[kv_rows, cols]
            v = v_all[kv_rows, cols]
            heads = range(kh * GROUP, (kh + 1) * GROUP)
            qs = jnp.concatenate([q_ref[q_rows, h * HEAD_DIM:(h + 1) * HEAD_DIM] for h in heads], axis=0)
            s_all = lax.dot_general(qs, k, (((1,), (1,)), ((), ())), preferred_element_type=F32)
            ps, dens = [], []
            for g, h in enumerate(heads):
                s2 = s_all[g * WINDOW:(g + 1) * WINDOW, :]
                s = jnp.where(own, s2[:, WINDOW:], s2[:, :WINDOW])
                if sub == 0:
                    s = jnp.where(first_valid, s, MASKED)
                sink = sinks_ref[h]
                m = jnp.maximum(jnp.max(s, axis=-1, keepdims=True), sink)
                p = jnp.exp(s - m)
                dens.append(jnp.sum(p, axis=-1, keepdims=True) + jnp.exp(sink - m))
                ps.append(jnp.concatenate([jnp.where(own, 0.0, p), jnp.where(own, p, 0.0)], axis=1).astype(BF16))
            o_all = _dot(jnp.concatenate(ps, axis=0), v)
            for g, h in enumerate(heads):
                o = o_all[g * WINDOW:(g + 1) * WINDOW, :] / dens[g]
                o_ref[q_rows, h * HEAD_DIM:(h + 1) * HEAD_DIM] = o.astype(BF16)


def _attn_prompt(sinks, q, k, v, *, n_seq, seq_len, bq):
    nb = seq_len // bq
    sub = bq // WINDOW
    cur = lambda n, b: (n * nb + b, 0)
    prev = lambda n, b: (jnp.maximum((n * nb + b) * sub - 1, 0), 0)
    return pl.pallas_call(
        _attn_prompt_kernel,
        grid=(n_seq, nb),
        in_specs=[
            pl.BlockSpec(memory_space=pltpu.SMEM),
            pl.BlockSpec((bq, ATTN_WIDTH), cur),
            pl.BlockSpec((bq, KV_WIDTH), cur),
            pl.BlockSpec((WINDOW, KV_WIDTH), prev),
            pl.BlockSpec((bq, KV_WIDTH), cur),
            pl.BlockSpec((WINDOW, KV_WIDTH), prev),
        ],
        out_specs=pl.BlockSpec((bq, ATTN_WIDTH), cur),
        out_shape=_sds((n_seq * seq_len, ATTN_WIDTH), BF16),
        compiler_params=_params(2),
        name="attn_prompt",
    )(sinks, q, k, k, v, v)


def _attn_sample_kernel(sinks_ref, q_ref, kn_ref, vn_ref, sk_ref, sv_ref, a_ref, nk_ref, nv_ref):
    nk_ref[:, 0:WINDOW - 1, :] = sk_ref[:, 1:WINDOW, :]
    nk_ref[:, WINDOW - 1:WINDOW, :] = kn_ref[...]
    nv_ref[:, 0:WINDOW - 1, :] = sv_ref[:, 1:WINDOW, :]
    nv_ref[:, WINDOW - 1:WINDOW, :] = vn_ref[...]
    for kh in range(N_KV_HEADS):
        cols = slice(kh * HEAD_DIM, (kh + 1) * HEAD_DIM)
        heads = slice(kh * GROUP, (kh + 1) * GROUP)
        k = nk_ref[:, :, cols].astype(BF16)
        v = nv_ref[:, :, cols].astype(BF16)
        s = jnp.einsum("ngd,ncd->ngc", q_ref[:, heads, :], k, preferred_element_type=F32)
        sink = sinks_ref[heads, :][None]
        m = jnp.maximum(jnp.max(s, axis=-1, keepdims=True), sink)
        p = jnp.exp(s - m)
        den = jnp.sum(p, axis=-1, keepdims=True) + jnp.exp(sink - m)
        o = jnp.einsum("ngc,ncd->ngd", p.astype(BF16), v, preferred_element_type=F32) / den
        a_ref[:, heads, :] = o.astype(BF16)


def _attn_sample(sinks_col, q3, k_new3, v_new3, state_k, state_v):
    n = q3.shape[0]
    return pl.pallas_call(
        _attn_sample_kernel,
        out_shape=[_sds((n, N_HEADS, HEAD_DIM), BF16), _sds((n, WINDOW, KV_WIDTH), F32),
                   _sds((n, WINDOW, KV_WIDTH), F32)],
        compiler_params=pltpu.CompilerParams(vmem_limit_bytes=VMEM_LIMIT),
        name="attn_sample",
    )(sinks_col, q3, k_new3, v_new3, state_k, state_v)


def _shift_rows(cur, halo):
    rid = lax.broadcasted_iota(jnp.int32, cur.shape, 0)
    h0 = halo[0:1, :]
    h1 = halo[1:2, :]
    s1 = jnp.where(rid == 0, h1, pltpu.roll(cur, 1, 0))
    s2 = jnp.where(rid == 0, h0, jnp.where(rid == 1, h1, pltpu.roll(cur, 2, 0)))
    return s1, s2


def _conv3(cw, cur, s1, s2):
    return cw[0:1, :] * s2 + cw[1:2, :] * s1 + cw[2:3, :] * cur


def _conv_tile(i, tiles_per_seq, split, rc, bm, halo_ref, st_ref, group_fn):
    gw = halo_ref.shape[1] // split
    gcols = [slice(grp * gw, (grp + 1) * gw) for grp in range(split)]

    @pl.when(i % tiles_per_seq == 0)
    def _():
        halo_ref[...] = jnp.zeros_like(halo_ref)

    halos = {}

    def chunk(rows, ride):
        for grp in range(split):
            halo = halos.get(grp)
            if halo is None:
                halo = halo_ref[:, gcols[grp]]
            cur = group_fn(rows, ride, grp, lambda cur, halo=halo: _shift_rows(cur, halo))
            halos[grp] = cur[cur.shape[0] - 2:, :]
        if rows.start + rows.size == bm:
            for grp in range(split):
                halo_ref[:, gcols[grp]] = halos.pop(grp)

    _walk_chunks(i, bm, rc, chunk)

    @pl.when(i % tiles_per_seq == tiles_per_seq - 1)
    def _():
        st_ref[...] = halo_ref[...]


def _convmix_kernel(tiles_per_seq, split, rc, h_ref, hs_ref, wb_ref, wc_ref, wu_ref, cw_ref, p2_ref, p1_ref,
                    o_ref, st_ref, os_ref, pres_ref, wbf_ref, halo_ref):
    i = pl.program_id(1)
    gw = wb_ref.shape[1] // split

    @pl.when(i == 0)
    def _():
        for grp in range(split):
            src = slice(grp * gw, (grp + 1) * gw)
            for part, w_ref in enumerate((wb_ref, wc_ref, wu_ref)):
                wbf_ref[:, (3 * grp + part) * gw:(3 * grp + part + 1) * gw] = w_ref[:, src].astype(BF16)

    def group(rows, ride, grp, shifts):
        n = rows.size
        cols = slice(grp * gw, (grp + 1) * gw)
        proj = _dot(_ride(h_ref, rows, hs_ref, ride), wbf_ref[:, 3 * grp * gw:3 * (grp + 1) * gw])
        cb, cur = proj[:, 0:gw], proj[:, gw:2 * gw] * proj[:, 2 * gw:3 * gw]
        cw = cw_ref[:, cols]
        s1, s2 = shifts(cur[:n])
        o_ref[rows, cols] = (cb[:n] * _conv3(cw, cur[:n], s1, s2)).astype(BF16)
        if ride:
            pres_ref[:, cols] = cur[n:]
            os_ref[:, cols] = (cb[n:] * _conv3(cw, cur[n:], p1_ref[:, cols], p2_ref[:, cols])).astype(BF16)
        return cur[:n]

    _conv_tile(i, tiles_per_seq, split, rc, h_ref.shape[0], halo_ref, st_ref, group)


def _convmix(h, hs, w_in, conv_w, prev2, prev1, *, bm, bn, split, rc, seq_len):
    m, ms = h.shape[0], hs.shape[0]
    tiles_per_seq = seq_len // bm
    tile = pl.BlockSpec((bm, bn), lambda j, i: (i, j))
    stile = pl.BlockSpec((ms, bn), lambda j, i: (0, j))
    wspec = lambda off: _wcol(D_MODEL, bn, lambda j, i: off + j * bn)
    return pl.pallas_call(
        functools.partial(_convmix_kernel, tiles_per_seq, split, rc),
        grid=(CONV_WIDTH // bn, m // bm),
        in_specs=[
            pl.BlockSpec((bm, D_MODEL), lambda j, i: (i, 0)),
            _resident(hs),
            wspec(OFF_CB), wspec(OFF_CC), wspec(OFF_CU),
            pl.BlockSpec((pl.Squeezed(), 3, bn), lambda j, i: (0, 0, j)),
            stile, stile,
        ],
        out_specs=[
            tile,
            pl.BlockSpec((pl.Squeezed(), 2, bn), lambda j, i: (i // tiles_per_seq, 0, j)),
            stile, stile,
        ],
        out_shape=[_sds((m, CONV_WIDTH), BF16), _sds((m // seq_len, 2, CONV_WIDTH), F32),
                   _sds((ms, CONV_WIDTH), BF16), _sds((ms, CONV_WIDTH), F32)],
        scratch_shapes=[pltpu.VMEM((D_MODEL, 3 * bn), BF16), pltpu.VMEM((2, bn), F32)],
        compiler_params=_params(2),
        name="convmix",
    )(h, hs, w_in, w_in, w_in, conv_w, prev2, prev1)


def _merge_kernel(split, rc, h_ref, a_ref, b_ref, hs_ref, as_ref, bs_ref, wga_ref, wgb_ref, wa_ref, wb_ref, bg_ref,
                  o_ref, os_ref, wg_bf_ref, wa_bf_ref, wb_bf_ref):
    i = pl.program_id(1)
    gw = wa_ref.shape[1] // split

    @pl.when(i == 0)
    def _():
        for grp in range(split):
            src = slice(grp * gw, (grp + 1) * gw)
            wg_bf_ref[:, 2 * grp * gw:(2 * grp + 1) * gw] = wga_ref[:, src].astype(BF16)
            wg_bf_ref[:, (2 * grp + 1) * gw:(2 * grp + 2) * gw] = wgb_ref[:, src].astype(BF16)
        wa_bf_ref[...] = wa_ref[...].astype(BF16)
        wb_bf_ref[...] = wb_ref[...].astype(BF16)

    def chunk(rows, ride):
        n = rows.size
        h = _ride(h_ref, rows, hs_ref, ride)
        a = _ride(a_ref, rows, as_ref, ride)
        b = _ride(b_ref, rows, bs_ref, ride)
        for grp in range(split):
            cols = slice(grp * gw, (grp + 1) * gw)
            gates = _dot(h, wg_bf_ref[:, 2 * grp * gw:2 * (grp + 1) * gw])
            ga = jax.nn.sigmoid(gates[:, 0:gw] + bg_ref[0:1, cols])
            gb = jax.nn.sigmoid(gates[:, gw:2 * gw] + bg_ref[1:2, cols])
            merged = (ga * _dot(a, wa_bf_ref[:, cols]) + gb * _dot(b, wb_bf_ref[:, cols])).astype(BF16)
            o_ref[rows, cols] = merged[:n]
            if ride:
                os_ref[:, cols] = merged[n:]

    _walk_chunks(i, h_ref.shape[0], rc, chunk)


def _merge(h, a, b, hs, a_s, b_s, w_in, w_attn_branch, w_conv_branch, b_gate, *, bm, bn, split, rc):
    m, ms = h.shape[0], hs.shape[0]
    row = lambda j, i: (i, 0)
    return pl.pallas_call(
        functools.partial(_merge_kernel, split, rc),
        grid=(D_MODEL // bn, m // bm),
        in_specs=[
            pl.BlockSpec((bm, D_MODEL), row),
            pl.BlockSpec((bm, ATTN_WIDTH), row),
            pl.BlockSpec((bm, CONV_WIDTH), row),
            _resident(hs), _resident(a_s), _resident(b_s),
            _wcol(D_MODEL, bn, lambda j, i: OFF_GA + j * bn),
            _wcol(D_MODEL, bn, lambda j, i: OFF_GB + j * bn),
            pl.BlockSpec((pl.Squeezed(), ATTN_WIDTH, bn), lambda j, i: (0, 0, j)),
            pl.BlockSpec((pl.Squeezed(), CONV_WIDTH, bn), lambda j, i: (0, 0, j)),
            pl.BlockSpec((pl.Squeezed(), 2, bn), lambda j, i: (0, 0, j)),
        ],
        out_specs=[pl.BlockSpec((bm, bn), lambda j, i: (i, j)),
                   pl.BlockSpec((ms, bn), lambda j, i: (0, j))],
        out_shape=[_sds((m, D_MODEL), BF16), _sds((ms, D_MODEL), BF16)],
        scratch_shapes=[pltpu.VMEM((D_MODEL, 2 * bn), BF16), pltpu.VMEM((ATTN_WIDTH, bn), BF16),
                        pltpu.VMEM((CONV_WIDTH, bn), BF16)],
        compiler_params=_params(2),
        name="merge",
    )(h, a, b, hs, a_s, b_s, w_in, w_in, w_attn_branch, w_conv_branch, b_gate)


def _oproj_kernel(rc, mg_ref, x_ref, mgs_ref, xs_ref, wo_ref, gpost_ref, gpre_ref,
                  x1_ref, h2_ref, x1s_ref, h2s_ref):
    def chunk(rows, ride):
        n = rows.size
        y = _dot(_ride(mg_ref, rows, mgs_ref, ride), wo_ref[...])
        x1 = _ride(x_ref, rows, xs_ref, ride) + _rms(y, gpost_ref[...])
        h2 = _rms(x1, gpre_ref[...]).astype(BF16)
        x1_ref[rows, :] = x1[:n]
        h2_ref[rows, :] = h2[:n]
        if ride:
            x1s_ref[...] = x1[n:]
            h2s_ref[...] = h2[n:]

    _walk_chunks(pl.program_id(0), x_ref.shape[0], rc, chunk)


def _oproj(merged, x, merged_s, xs, w_o, g_post, g_pre, *, bm, rc):
    m, ms = x.shape[0], xs.shape[0]
    tile = pl.BlockSpec((bm, D_MODEL), lambda i: (i, 0))
    return pl.pallas_call(
        functools.partial(_oproj_kernel, rc),
        grid=(m // bm,),
        in_specs=[
            tile, tile, _resident(merged_s), _resident(xs),
            pl.BlockSpec((pl.Squeezed(), D_MODEL, D_MODEL), lambda i: (0, 0, 0),
                         pipeline_mode=pl.Buffered(1)),
            _resident(g_post), _resident(g_pre),
        ],
        out_specs=[tile, tile, _resident(xs), _resident(merged_s)],
        out_shape=[_sds((m, D_MODEL), F32), _sds((m, D_MODEL), BF16),
                   _sds((ms, D_MODEL), F32), _sds((ms, D_MODEL), BF16)],
        compiler_params=_params(1),
        name="oproj_norm",
    )(merged, x, merged_s, xs, w_o, g_post, g_pre)


def _ffn_up_kernel(tiles_per_seq, split, rc, h_ref, hs_ref, wa_ref, wg_ref, cw_ref, cb_ref, p2_ref, p1_ref,
                   o_ref, st_ref, os_ref, ups_ref, wbf_ref, halo_ref):
    i = pl.program_id(1)
    gw = wa_ref.shape[1] // split

    @pl.when(i == 0)
    def _():
        for grp in range(split):
            src = slice(grp * gw, (grp + 1) * gw)
            wbf_ref[:, 2 * grp * gw:(2 * grp + 1) * gw] = wa_ref[:, src].astype(BF16)
            wbf_ref[:, (2 * grp + 1) * gw:(2 * grp + 2) * gw] = wg_ref[:, src].astype(BF16)

    def group(rows, ride, grp, shifts):
        n = rows.size
        cols = slice(grp * gw, (grp + 1) * gw)
        proj = _dot(_ride(h_ref, rows, hs_ref, ride), wbf_ref[:, 2 * grp * gw:2 * (grp + 1) * gw])
        cur, gate = proj[:, 0:gw], proj[:, gw:2 * gw]
        cw = cw_ref[:, cols]

        def gated(cur, s1, s2, gate):
            act = jax.nn.gelu(_conv3(cw, cur, s1, s2) + cb_ref[:, cols], approximate=True)
            return (act * gate).astype(BF16)

        s1, s2 = shifts(cur[:n])
        o_ref[rows, cols] = gated(cur[:n], s1, s2, gate[:n])
        if ride:
            ups_ref[:, cols] = cur[n:]
            os_ref[:, cols] = gated(cur[n:], p1_ref[:, cols], p2_ref[:, cols], gate[n:])
        return cur[:n]

    _conv_tile(i, tiles_per_seq, split, rc, h_ref.shape[0], halo_ref, st_ref, group)


def _ffn_up(h2, h2s, w_ffn_up, ffn_conv_w, ffn_conv_b, prev2, prev1, *, bm, bn, split, rc, seq_len):
    m, ms = h2.shape[0], h2s.shape[0]
    tiles_per_seq = seq_len // bm
    n_tiles = D_FF // bn
    tile = pl.BlockSpec((bm, bn), lambda j, i: (i, j))
    stile = pl.BlockSpec((ms, bn), lambda j, i: (0, j))
    return pl.pallas_call(
        functools.partial(_ffn_up_kernel, tiles_per_seq, split, rc),
        grid=(n_tiles, m // bm),
        in_specs=[
            pl.BlockSpec((bm, D_MODEL), lambda j, i: (i, 0)),
            _resident(h2s),
            pl.BlockSpec((pl.Squeezed(), D_MODEL, bn), lambda j, i: (0, 0, j)),
            pl.BlockSpec((pl.Squeezed(), D_MODEL, bn), lambda j, i: (0, 0, n_tiles + j)),
            pl.BlockSpec((pl.Squeezed(), 3, bn), lambda j, i: (0, 0, j)),
            pl.BlockSpec((1, bn), lambda j, i: (0, j)),
            stile, stile,
        ],
        out_specs=[
            tile,
            pl.BlockSpec((pl.Squeezed(), 2, bn), lambda j, i: (i // tiles_per_seq, 0, j)),
            stile, stile,
        ],
        out_shape=[_sds((m, D_FF), BF16), _sds((m // seq_len, 2, D_FF), F32),
                   _sds((ms, D_FF), BF16), _sds((ms, D_FF), F32)],
        scratch_shapes=[pltpu.VMEM((D_MODEL, 2 * bn), BF16), pltpu.VMEM((2, bn), F32)],
        compiler_params=_params(2),
        name="ffn_up",
    )(h2, h2s, w_ffn_up, w_ffn_up, ffn_conv_w, ffn_conv_b, prev2, prev1)


def _ffn_down_kernel(rc, a_ref, as_ref, w_ref, o_ref, os_ref, wbf_ref):
    i = pl.program_id(1)

    @pl.when(i == 0)
    def _():
        wbf_ref[...] = w_ref[...].astype(BF16)

    def chunk(rows, ride):
        n = rows.size
        o = _dot(_ride(a_ref, rows, as_ref, ride), wbf_ref[...]).astype(BF16)
        o_ref[rows, :] = o[:n]
        if ride:
            os_ref[...] = o[n:]

    _walk_chunks(i, a_ref.shape[0], rc, chunk)


def _ffn_down(act, act_s, w_ffn_down, *, bm, bn, rc):
    m, ms = act.shape[0], act_s.shape[0]
    return pl.pallas_call(
        functools.partial(_ffn_down_kernel, rc),
        grid=(D_MODEL // bn, m // bm),
        in_specs=[
            pl.BlockSpec((bm, D_FF), lambda j, i: (i, 0)),
            _resident(act_s),
            pl.BlockSpec((pl.Squeezed(), D_FF, bn), lambda j, i: (0, 0, j)),
        ],
        out_specs=[pl.BlockSpec((bm, bn), lambda j, i: (i, j)),
                   pl.BlockSpec((ms, bn), lambda j, i: (0, j))],
        out_shape=[_sds((m, D_MODEL), BF16), _sds((ms, D_MODEL), BF16)],
        scratch_shapes=[pltpu.VMEM((D_FF, bn), BF16)],
        compiler_params=_params(2),
        name="ffn_down",
    )(act, act_s, w_ffn_down)


def _final_kernel(rc, f_ref, x1_ref, p_ref, fs_ref, x1s_ref, ps_ref, g_ref, wg_ref, wp_ref, y_ref, ys_ref):
    def chunk(rows, ride):
        n = rows.size
        f = _ride(f_ref, rows, fs_ref, ride).astype(F32)
        x2 = _ride(x1_ref, rows, x1s_ref, ride) + _rms(f, g_ref[...])
        gate = jax.nn.sigmoid(_dot(x2.astype(BF16), wg_ref[...]))
        y = x2 + gate * _dot(_ride(p_ref, rows, ps_ref, ride).astype(BF16), wp_ref[...])
        y_ref[rows, :] = y[:n]
        if ride:
            ys_ref[...] = y[n:]

    _walk_chunks(pl.program_id(0), x1_ref.shape[0], rc, chunk)


def _final(f, x1, p, f_s, x1_s, p_s, g_post, w_ple_gate, w_ple, *, bm, rc):
    m, ms = x1.shape[0], x1_s.shape[0]
    row = lambda i: (i, 0)
    return pl.pallas_call(
        functools.partial(_final_kernel, rc),
        grid=(m // bm,),
        in_specs=[
            pl.BlockSpec((bm, D_MODEL), row),
            pl.BlockSpec((bm, D_MODEL), row),
            pl.BlockSpec((bm, PLE_DIM), row),
            _resident(f_s), _resident(x1_s), _resident(p_s),
            _resident(g_post),
            pl.BlockSpec((pl.Squeezed(), D_MODEL, D_MODEL), lambda i: (0, 0, 0),
                         pipeline_mode=pl.Buffered(1)),
            pl.BlockSpec((pl.Squeezed(), PLE_DIM, D_MODEL), lambda i: (0, 0, 0),
                         pipeline_mode=pl.Buffered(1)),
        ],
        out_specs=[pl.BlockSpec((bm, D_MODEL), row), _resident(x1_s)],
        out_shape=[_sds((m, D_MODEL), F32), _sds((ms, D_MODEL), F32)],
        compiler_params=_params(1),
        name="ffn_norm_ple",
    )(f, x1, p, f_s, x1_s, p_s, g_post, w_ple_gate, w_ple)


def _rope_tables(pos):
    half = HEAD_DIM // 2
    inv = ROPE_THETA ** (-np.arange(half, dtype=np.float64) / half)
    ang = np.asarray(pos, dtype=np.float64)[:, None] * inv[None, :]
    cos, sin = np.cos(ang), np.sin(ang)
    reps = LANES // HEAD_DIM
    return (jnp.asarray(np.tile(np.concatenate([cos, cos], axis=1), (1, reps)), dtype=F32),
            jnp.asarray(np.tile(np.concatenate([-sin, sin], axis=1), (1, reps)), dtype=F32))


TILES = dict(qkv=dict(bm=512, rc=256), attn=128, convmix=dict(bm=1024, bn=512, split=2, rc=512),
             merge=dict(bm=512, bn=512, split=2, rc=512), oproj=dict(bm=512, rc=128),
             ffn_up=dict(bm=1024, bn=512, split=1, rc=512), ffn_down=dict(bm=512, bn=512, rc=512),
             final=dict(bm=512, rc=256))


def kernel(x_prompt, x_sample, state_attn_k, state_attn_v, state_conv, state_ffn_conv, p_prompt, p_sample,
           norm_mix_pre, norm_mix_post, norm_ffn_pre, norm_ffn_post, w_in, attn_sinks, conv_w, w_attn_branch,
           w_conv_branch, b_gate, w_o, w_ffn_up, ffn_conv_w, ffn_conv_b, w_ffn_down, w_ple, w_ple_gate):
    assert w_in.shape[0] == 1, "weights are indexed with a squeezed depth axis of size 1"
    batch, seq, _ = x_prompt.shape
    ns, dec_seq, _ = x_sample.shape
    assert dec_seq == 1
    m = batch * seq
    x = x_prompt.reshape(m, D_MODEL)
    xs = x_sample.reshape(ns, D_MODEL)
    rope_p = _rope_tables(np.arange(seq))
    rope_s = _rope_tables(np.full((ns,), PAST_LEN))

    h, q, k, v, hs, qs, ks, vs = _qkv(x, xs, norm_mix_pre, w_in, rope_p, rope_s, **TILES["qkv"])

    a = _attn_prompt(attn_sinks.reshape(N_HEADS), q, k, v, n_seq=batch, seq_len=seq, bq=TILES["attn"])
    a3, new_ks, new_vs = _attn_sample(
        attn_sinks.reshape(N_HEADS, 1), qs.reshape(ns, N_HEADS, HEAD_DIM), ks.reshape(ns, 1, KV_WIDTH),
        vs.reshape(ns, 1, KV_WIDTH), state_attn_k.reshape(ns, WINDOW, KV_WIDTH),
        state_attn_v.reshape(ns, WINDOW, KV_WIDTH))
    a_s = a3.reshape(ns, ATTN_WIDTH)

    b, new_conv_p, b_s, pre_s = _convmix(h, hs, w_in, conv_w, state_conv[0, :, 0], state_conv[0, :, 1],
                                         seq_len=seq, **TILES["convmix"])
    merged, merged_s = _merge(h, a, b, hs, a_s, b_s, w_in, w_attn_branch, w_conv_branch, b_gate, **TILES["merge"])
    x1, h2, x1_s, h2_s = _oproj(merged, x, merged_s, xs, w_o.astype(BF16), norm_mix_post, norm_ffn_pre,
                                **TILES["oproj"])
    act, new_ffn_p, act_s, up_s = _ffn_up(h2, h2_s, w_ffn_up, ffn_conv_w, ffn_conv_b, state_ffn_conv[0, :, 0],
                                          state_ffn_conv[0, :, 1], seq_len=seq, **TILES["ffn_up"])
    f, f_s = _ffn_down(act, act_s, w_ffn_down, **TILES["ffn_down"])
    y, y_s = _final(f, x1, p_prompt.reshape(m, PLE_DIM), f_s, x1_s, p_sample.reshape(ns, PLE_DIM),
                    norm_ffn_post, w_ple_gate.astype(BF16), w_ple.astype(BF16), **TILES["final"])

    kv_tail = lambda t: t.reshape(batch, seq, N_KV_HEADS, HEAD_DIM)[:, -WINDOW:]
    kv_state = lambda t: t.reshape(ns, WINDOW, N_KV_HEADS, HEAD_DIM)
    return (y.reshape(batch, seq, D_MODEL), y_s.reshape(ns, dec_seq, D_MODEL),
            kv_tail(k)[None], kv_tail(v)[None], new_conv_p[None], new_ffn_p[None],
            kv_state(new_ks)[None], kv_state(new_vs)[None],
            jnp.stack([state_conv[0, :, 1], pre_s], axis=1)[None],
            jnp.stack([state_ffn_conv[0, :, 1], up_s], axis=1)[None])
```

```python
import functools

import jax
import jax.numpy as jnp
import numpy as np
from jax import lax
from jax.experimental import pallas as pl
from jax.experimental.pallas import tpu as pltpu

D_MODEL = 2048
N_HEADS = 16
N_KV_HEADS = 2
HEAD_DIM = 64
GROUP = N_HEADS // N_KV_HEADS
ATTN_WIDTH = N_HEADS * HEAD_DIM
KV_WIDTH = N_KV_HEADS * HEAD_DIM
QKV_WIDTH = ATTN_WIDTH + 2 * KV_WIDTH
WINDOW = 128
PAST_LEN = 16384
ROPE_THETA = 10000.0
CONV_WIDTH = D_MODEL // 2
D_FF = 5632
PLE_DIM = 256
RMS_EPS = 1e-6
OFF_CB = QKV_WIDTH
OFF_CC = OFF_CB + CONV_WIDTH
OFF_CU = OFF_CC + CONV_WIDTH
OFF_GA = OFF_CU + CONV_WIDTH
OFF_GB = OFF_GA + D_MODEL

LANES = 128
MASKED = -1e30
VMEM_LIMIT = 56 * 1024 * 1024

BF16 = jnp.bfloat16
F32 = jnp.float32


def _params(n_axes):
    return pltpu.CompilerParams(dimension_semantics=("arbitrary",) * n_axes,
                                vmem_limit_bytes=VMEM_LIMIT)


def _rms(x, g):
    return x * lax.rsqrt(jnp.mean(x * x, axis=-1, keepdims=True) + RMS_EPS) * g


def _dot(a, b):
    return jnp.dot(a, b, preferred_element_type=F32)


def _walk_chunks(i, bm, rc, chunk_fn):
    rc = min(bm, rc)

    def walk(first):
        for c in range(bm // rc):
            chunk_fn(pl.ds(c * rc, rc), first and c == 0)

    @pl.when(i == 0)
    def _():
        walk(True)

    @pl.when(i > 0)
    def _():
        walk(False)


def _ride(ref, rows, sample_ref, ride):
    x = ref[rows, :]
    return jnp.concatenate([x, sample_ref[...]], axis=0) if ride else x


def _wcol(k_rows, block_cols, col_fn):
    return pl.BlockSpec((pl.Squeezed(), pl.Element(k_rows), pl.Element(block_cols)),
                        lambda *g: (0, 0, pl.multiple_of(col_fn(*g), LANES)))


def _resident(arr):
    zeros = (0,) * arr.ndim
    return pl.BlockSpec(arr.shape, lambda *g: zeros)


def _sds(shape, dtype):
    return jax.ShapeDtypeStruct(shape, dtype)


def _qkv_kernel(rc, x_ref, xs_ref, g_ref, w_ref, cos_ref, sin_ref, coss_ref, sins_ref,
                h_ref, q_ref, k_ref, v_ref, hs_ref, qs_ref, ks_ref, vs_ref, wbf_ref):
    i = pl.program_id(0)
    n_q = ATTN_WIDTH // LANES

    @pl.when(i == 0)
    def _():
        wbf_ref[...] = w_ref[...].astype(BF16)

    def chunk(rows, ride):
        n = rows.size

        def put(ref, sample_ref, cols, val):
            ref[rows, cols] = val[:n]
            if ride:
                sample_ref[:, cols] = val[n:]

        h = _rms(_ride(x_ref, rows, xs_ref, ride), g_ref[...]).astype(BF16)
        put(h_ref, hs_ref, slice(None), h)
        proj = _dot(h, wbf_ref[...])
        cos = _ride(cos_ref, rows, coss_ref, ride)
        sin = _ride(sin_ref, rows, sins_ref, ride)
        lane = lax.broadcasted_iota(jnp.int32, cos.shape, 1)
        first_half = (lane % HEAD_DIM) < (HEAD_DIM // 2)
        for c in range(n_q + 1):
            cols = slice(c * LANES, (c + 1) * LANES)
            xg = proj[:, cols]
            partner = jnp.where(first_half, pltpu.roll(xg, LANES - HEAD_DIM // 2, 1),
                                pltpu.roll(xg, HEAD_DIM // 2, 1))
            yg = xg * cos + partner * sin
            if c < n_q:
                put(q_ref, qs_ref, cols, (yg * (HEAD_DIM ** -0.5)).astype(BF16))
            else:
                put(k_ref, ks_ref, slice(None), yg)
        put(v_ref, vs_ref, slice(None), proj[:, ATTN_WIDTH + KV_WIDTH:QKV_WIDTH])

    _walk_chunks(i, x_ref.shape[0], rc, chunk)


def _qkv(x, xs, g, w_in, rope_p, rope_s, *, bm, rc):
    m, ms = x.shape[0], xs.shape[0]
    t_tiles = rope_p[0].shape[0] // bm
    row = lambda i: (i, 0)
    pos = lambda i: (i % t_tiles, 0)
    outs = lambda rows: [_sds((rows, D_MODEL), BF16), _sds((rows, ATTN_WIDTH), BF16),
                         _sds((rows, KV_WIDTH), F32), _sds((rows, KV_WIDTH), F32)]
    out_p, out_s = outs(m), outs(ms)
    return pl.pallas_call(
        functools.partial(_qkv_kernel, rc),
        grid=(m // bm,),
        in_specs=[
            pl.BlockSpec((bm, D_MODEL), row),
            _resident(xs),
            _resident(g),
            pl.BlockSpec((pl.Squeezed(), D_MODEL, QKV_WIDTH), lambda i: (0, 0, 0),
                         pipeline_mode=pl.Buffered(1)),
            pl.BlockSpec((bm, LANES), pos),
            pl.BlockSpec((bm, LANES), pos),
            _resident(rope_s[0]),
            _resident(rope_s[1]),
        ],
        out_specs=[pl.BlockSpec((bm, s.shape[1]), row) for s in out_p] + [_resident(s) for s in out_s],
        out_shape=out_p + out_s,
        scratch_shapes=[pltpu.VMEM((D_MODEL, QKV_WIDTH), BF16)],
        compiler_params=_params(1),
        name="qkv_rope",
    )(x, xs, g, w_in, rope_p[0], rope_p[1], rope_s[0], rope_s[1])


def _attn_prompt_kernel(sinks_ref, q_ref, kc_ref, kp_ref, vc_ref, vp_ref, wo_ref, wg_ref, wp_ref,
                        o_ref, wo_bf_ref, wg_bf_ref, wp_bf_ref):
    wo_bf_ref[...] = wo_ref[...].astype(BF16)
    wg_bf_ref[...] = wg_ref[...].astype(BF16)

    @pl.when((pl.program_id(0) == 0) & (pl.program_id(1) == 0))
    def _():
        wp_bf_ref[...] = wp_ref[...].astype(BF16)

    n_sub = q_ref.shape[0] // WINDOW
    qi = lax.broadcasted_iota(jnp.int32, (WINDOW, WINDOW), 0)
    ci = lax.broadcasted_iota(jnp.int32, (WINDOW, WINDOW), 1)
    own = ci <= qi
    first_valid = own | (pl.program_id(1) > 0)
    k_all = jnp.concatenate([kp_ref[...], kc_ref[...]], axis=0).astype(BF16)
    v_all = jnp.concatenate([vp_ref[...], vc_ref[...]], axis=0).astype(BF16)
    for sub in range(n_sub):
        q_rows = slice(sub * WINDOW, (sub + 1) * WINDOW)
        kv_rows = slice(sub * WINDOW, (sub + 2) * WINDOW)
        for kh in range(N_KV_HEADS):
            cols = slice(kh * HEAD_DIM, (kh + 1) * HEAD_DIM)
            k = k_all[kv_rows, cols]
            v = v_all[kv_rows, cols]
            heads = range(kh * GROUP, (kh + 1) * GROUP)
            qs = jnp.concatenate([q_ref[q_rows, h * HEAD_DIM:(h + 1) * HEAD_DIM] for h in heads], axis=0)
            s_all = lax.dot_general(qs, k, (((1,), (1,)), ((), ())), preferred_element_type=F32)
            ps, dens = [], []
            for g, h in enumerate(heads):
                s2 = s_all[g * WINDOW:(g + 1) * WINDOW, :]
                s = jnp.where(own, s2[:, WINDOW:], s2[:, :WINDOW])
                if sub == 0:
                    s = jnp.where(first_valid, s, MASKED)
                sink = sinks_ref[h]
                m = jnp.maximum(jnp.max(s, axis=-1, keepdims=True), sink)
                p = jnp.exp(s - m)
                dens.append(jnp.sum(p, axis=-1, keepdims=True) + jnp.exp(sink - m))
                ps.append(jnp.concatenate([jnp.where(own, 0.0, p), jnp.where(own, p, 0.0)], axis=1).astype(BF16))
            o_all = _dot(jnp.concatenate(ps, axis=0), v)
            for g, h in enumerate(heads):
                o = o_all[g * WINDOW:(g + 1) * WINDOW, :] / dens[g]
                o_ref[q_rows, h * HEAD_DIM:(h + 1) * HEAD_DIM] = o.astype(BF16)


def _attn_prompt(sinks, q, k, v, w_o, w_ple_gate, w_ple, *, n_seq, seq_len, bq):
    nb = seq_len // bq
    sub = bq // WINDOW
    slab = D_MODEL // (n_seq * nb)
    cur = lambda n, b: (n * nb + b, 0)
    prev = lambda n, b: (jnp.maximum((n * nb + b) * sub - 1, 0), 0)
    wslab = pl.BlockSpec((pl.Squeezed(), slab, D_MODEL), lambda n, b: (0, n * nb + b, 0))
    return pl.pallas_call(
        _attn_prompt_kernel,
        grid=(n_seq, nb),
        in_specs=[
            pl.BlockSpec(memory_space=pltpu.SMEM),
            pl.BlockSpec((bq, ATTN_WIDTH), cur),
            pl.BlockSpec((bq, KV_WIDTH), cur),
            pl.BlockSpec((WINDOW, KV_WIDTH), prev),
            pl.BlockSpec((bq, KV_WIDTH), cur),
            pl.BlockSpec((WINDOW, KV_WIDTH), prev),
            wslab, wslab, _resident(w_ple),
        ],
        out_specs=[pl.BlockSpec((bq, ATTN_WIDTH), cur), wslab, wslab, _resident(w_ple)],
        out_shape=[_sds((n_seq * seq_len, ATTN_WIDTH), BF16), _sds(w_o.shape, BF16), _sds(w_ple_gate.shape, BF16),
                   _sds(w_ple.shape, BF16)],
        compiler_params=_params(2),
        name="attn_prompt",
    )(sinks, q, k, k, v, v, w_o, w_ple_gate, w_ple)


def _attn_sample_kernel(sinks_ref, q_ref, kn_ref, vn_ref, sk_ref, sv_ref, a_ref, nk_ref, nv_ref):
    nk_ref[:, 0:WINDOW - 1, :] = sk_ref[:, 1:WINDOW, :]
    nk_ref[:, WINDOW - 1:WINDOW, :] = kn_ref[...]
    nv_ref[:, 0:WINDOW - 1, :] = sv_ref[:, 1:WINDOW, :]
    nv_ref[:, WINDOW - 1:WINDOW, :] = vn_ref[...]
    for kh in range(N_KV_HEADS):
        cols = slice(kh * HEAD_DIM, (kh + 1) * HEAD_DIM)
        heads = slice(kh * GROUP, (kh + 1) * GROUP)
        k = nk_ref[:, :, cols].astype(BF16)
        v = nv_ref[:, :, cols].astype(BF16)
        s = jnp.einsum("ngd,ncd->ngc", q_ref[:, heads, :], k, preferred_element_type=F32)
        sink = sinks_ref[heads, :][None]
        m = jnp.maximum(jnp.max(s, axis=-1, keepdims=True), sink)
        p = jnp.exp(s - m)
        den = jnp.sum(p, axis=-1, keepdims=True) + jnp.exp(sink - m)
        o = jnp.einsum("ngc,ncd->ngd", p.astype(BF16), v, preferred_element_type=F32) / den
        a_ref[:, heads, :] = o.astype(BF16)


def _attn_sample(sinks_col, q3, k_new3, v_new3, state_k, state_v):
    n = q3.shape[0]
    return pl.pallas_call(
        _attn_sample_kernel,
        out_shape=[_sds((n, N_HEADS, HEAD_DIM), BF16), _sds((n, WINDOW, KV_WIDTH), F32),
                   _sds((n, WINDOW, KV_WIDTH), F32)],
        compiler_params=pltpu.CompilerParams(vmem_limit_bytes=VMEM_LIMIT),
        name="attn_sample",
    )(sinks_col, q3, k_new3, v_new3, state_k, state_v)


def _shift_rows(cur, halo):
    rid = lax.broadcasted_iota(jnp.int32, cur.shape, 0)
    h0 = halo[0:1, :]
    h1 = halo[1:2, :]
    s1 = jnp.where(rid == 0, h1, pltpu.roll(cur, 1, 0))
    s2 = jnp.where(rid == 0, h0, jnp.where(rid == 1, h1, pltpu.roll(cur, 2, 0)))
    return s1, s2


def _conv3(cw, cur, s1, s2):
    return cw[0:1, :] * s2 + cw[1:2, :] * s1 + cw[2:3, :] * cur


def _conv_tile(i, tiles_per_seq, split, rc, bm, halo_ref, st_ref, group_fn):
    gw = halo_ref.shape[1] // split
    gcols = [slice(grp * gw, (grp + 1) * gw) for grp in range(split)]

    @pl.when(i % tiles_per_seq == 0)
    def _():
        halo_ref[...] = jnp.zeros_like(halo_ref)

    halos = {}

    def chunk(rows, ride):
        for grp in range(split):
            halo = halos.get(grp)
            if halo is None:
                halo = halo_ref[:, gcols[grp]]
            cur = group_fn(rows, ride, grp, lambda cur, halo=halo: _shift_rows(cur, halo))
            halos[grp] = cur[cur.shape[0] - 2:, :]
        if rows.start + rows.size == bm:
            for grp in range(split):
                halo_ref[:, gcols[grp]] = halos.pop(grp)

    _walk_chunks(i, bm, rc, chunk)

    @pl.when(i % tiles_per_seq == tiles_per_seq - 1)
    def _():
        st_ref[...] = halo_ref[...]


def _convmix_kernel(tiles_per_seq, split, rc, h_ref, hs_ref, wb_ref, wc_ref, wu_ref, cw_ref, p2_ref, p1_ref,
                    o_ref, st_ref, os_ref, pres_ref, wbf_ref, halo_ref):
    i = pl.program_id(1)
    gw = wb_ref.shape[1] // split

    @pl.when(i == 0)
    def _():
        for grp in range(split):
            src = slice(grp * gw, (grp + 1) * gw)
            for part, w_ref in enumerate((wb_ref, wc_ref, wu_ref)):
                wbf_ref[:, (3 * grp + part) * gw:(3 * grp + part + 1) * gw] = w_ref[:, src].astype(BF16)

    def group(rows, ride, grp, shifts):
        n = rows.size
        cols = slice(grp * gw, (grp + 1) * gw)
        proj = _dot(_ride(h_ref, rows, hs_ref, ride), wbf_ref[:, 3 * grp * gw:3 * (grp + 1) * gw])
        cb, cur = proj[:, 0:gw], proj[:, gw:2 * gw] * proj[:, 2 * gw:3 * gw]
        cw = cw_ref[:, cols]
        s1, s2 = shifts(cur[:n])
        o_ref[rows, cols] = (cb[:n] * _conv3(cw, cur[:n], s1, s2)).astype(BF16)
        if ride:
            pres_ref[:, cols] = cur[n:]
            os_ref[:, cols] = (cb[n:] * _conv3(cw, cur[n:], p1_ref[:, cols], p2_ref[:, cols])).astype(BF16)
        return cur[:n]

    _conv_tile(i, tiles_per_seq, split, rc, h_ref.shape[0], halo_ref, st_ref, group)


def _convmix(h, hs, w_in, conv_w, prev2, prev1, *, bm, bn, split, rc, seq_len):
    m, ms = h.shape[0], hs.shape[0]
    tiles_per_seq = seq_len // bm
    tile = pl.BlockSpec((bm, bn), lambda j, i: (i, j))
    stile = pl.BlockSpec((ms, bn), lambda j, i: (0, j))
    wspec = lambda off: _wcol(D_MODEL, bn, lambda j, i: off + j * bn)
    return pl.pallas_call(
        functools.partial(_convmix_kernel, tiles_per_seq, split, rc),
        grid=(CONV_WIDTH // bn, m // bm),
        in_specs=[
            pl.BlockSpec((bm, D_MODEL), lambda j, i: (i, 0)),
            _resident(hs),
            wspec(OFF_CB), wspec(OFF_CC), wspec(OFF_CU),
            pl.BlockSpec((pl.Squeezed(), 3, bn), lambda j, i: (0, 0, j)),
            stile, stile,
        ],
        out_specs=[
            tile,
            pl.BlockSpec((pl.Squeezed(), 2, bn), lambda j, i: (i // tiles_per_seq, 0, j)),
            stile, stile,
        ],
        out_shape=[_sds((m, CONV_WIDTH), BF16), _sds((m // seq_len, 2, CONV_WIDTH), F32),
                   _sds((ms, CONV_WIDTH), BF16), _sds((ms, CONV_WIDTH), F32)],
        scratch_shapes=[pltpu.VMEM((D_MODEL, 3 * bn), BF16), pltpu.VMEM((2, bn), F32)],
        compiler_params=_params(2),
        name="convmix",
    )(h, hs, w_in, w_in, w_in, conv_w, prev2, prev1)


def _merge_kernel(split, rc, h_ref, a_ref, b_ref, hs_ref, as_ref, bs_ref, wga_ref, wgb_ref, wa_ref, wb_ref, bg_ref,
                  o_ref, os_ref, wg_bf_ref, wa_bf_ref, wb_bf_ref):
    i = pl.program_id(1)
    gw = wa_ref.shape[1] // split

    @pl.when(i == 0)
    def _():
        for grp in range(split):
            src = slice(grp * gw, (grp + 1) * gw)
            wg_bf_ref[:, 2 * grp * gw:(2 * grp + 1) * gw] = wga_ref[:, src].astype(BF16)
            wg_bf_ref[:, (2 * grp + 1) * gw:(2 * grp + 2) * gw] = wgb_ref[:, src].astype(BF16)
        wa_bf_ref[...] = wa_ref[...].astype(BF16)
        wb_bf_ref[...] = wb_ref[...].astype(BF16)

    def chunk(rows, ride):
        n = rows.size
        h = _ride(h_ref, rows, hs_ref, ride)
        a = _ride(a_ref, rows, as_ref, ride)
        b = _ride(b_ref, rows, bs_ref, ride)
        for grp in range(split):
            cols = slice(grp * gw, (grp + 1) * gw)
            gates = _dot(h, wg_bf_ref[:, 2 * grp * gw:2 * (grp + 1) * gw])
            ga = jax.nn.sigmoid(gates[:, 0:gw] + bg_ref[0:1, cols])
            gb = jax.nn.sigmoid(gates[:, gw:2 * gw] + bg_ref[1:2, cols])
            merged = (ga * _dot(a, wa_bf_ref[:, cols]) + gb * _dot(b, wb_bf_ref[:, cols])).astype(BF16)
            o_ref[rows, cols] = merged[:n]
            if ride:
                os_ref[:, cols] = merged[n:]

    _walk_chunks(i, h_ref.shape[0], rc, chunk)


def _merge(h, a, b, hs, a_s, b_s, w_in, w_attn_branch, w_conv_branch, b_gate, *, bm, bn, split, rc):
    m, ms = h.shape[0], hs.shape[0]
    row = lambda j, i: (i, 0)
    return pl.pallas_call(
        functools.partial(_merge_kernel, split, rc),
        grid=(D_MODEL // bn, m // bm),
        in_specs=[
            pl.BlockSpec((bm, D_MODEL), row),
            pl.BlockSpec((bm, ATTN_WIDTH), row),
            pl.BlockSpec((bm, CONV_WIDTH), row),
            _resident(hs), _resident(a_s), _resident(b_s),
            _wcol(D_MODEL, bn, lambda j, i: OFF_GA + j * bn),
            _wcol(D_MODEL, bn, lambda j, i: OFF_GB + j * bn),
            pl.BlockSpec((pl.Squeezed(), ATTN_WIDTH, bn), lambda j, i: (0, 0, j)),
            pl.BlockSpec((pl.Squeezed(), CONV_WIDTH, bn), lambda j, i: (0, 0, j)),
            pl.BlockSpec((pl.Squeezed(), 2, bn), lambda j, i: (0, 0, j)),
        ],
        out_specs=[pl.BlockSpec((bm, bn), lambda j, i: (i, j)),
                   pl.BlockSpec((ms, bn), lambda j, i: (0, j))],
        out_shape=[_sds((m, D_MODEL), BF16), _sds((ms, D_MODEL), BF16)],
        scratch_shapes=[pltpu.VMEM((D_MODEL, 2 * bn), BF16), pltpu.VMEM((ATTN_WIDTH, bn), BF16),
                        pltpu.VMEM((CONV_WIDTH, bn), BF16)],
        compiler_params=_params(2),
        name="merge",
    )(h, a, b, hs, a_s, b_s, w_in, w_in, w_attn_branch, w_conv_branch, b_gate)


def _oproj_kernel(rc, mg_ref, x_ref, mgs_ref, xs_ref, wo_ref, gpost_ref, gpre_ref,
                  x1_ref, h2_ref, x1s_ref, h2s_ref):
    def chunk(rows, ride):
        n = rows.size
        y = _dot(_ride(mg_ref, rows, mgs_ref, ride), wo_ref[...])
        x1 = _ride(x_ref, rows, xs_ref, ride) + _rms(y, gpost_ref[...])
        h2 = _rms(x1, gpre_ref[...]).astype(BF16)
        x1_ref[rows, :] = x1[:n]
        h2_ref[rows, :] = h2[:n]
        if ride:
            x1s_ref[...] = x1[n:]
            h2s_ref[...] = h2[n:]

    _walk_chunks(pl.program_id(0), x_ref.shape[0], rc, chunk)


def _oproj(merged, x, merged_s, xs, w_o, g_post, g_pre, *, bm, rc):
    m, ms = x.shape[0], xs.shape[0]
    tile = pl.BlockSpec((bm, D_MODEL), lambda i: (i, 0))
    return pl.pallas_call(
        functools.partial(_oproj_kernel, rc),
        grid=(m // bm,),
        in_specs=[
            tile, tile, _resident(merged_s), _resident(xs),
            pl.BlockSpec((pl.Squeezed(), D_MODEL, D_MODEL), lambda i: (0, 0, 0),
                         pipeline_mode=pl.Buffered(1)),
            _resident(g_post), _resident(g_pre),
        ],
        out_specs=[tile, tile, _resident(xs), _resident(merged_s)],
        out_shape=[_sds((m, D_MODEL), F32), _sds((m, D_MODEL), BF16),
                   _sds((ms, D_MODEL), F32), _sds((ms, D_MODEL), BF16)],
        compiler_params=_params(1),
        name="oproj_norm",
    )(merged, x, merged_s, xs, w_o, g_post, g_pre)


def _ffn_up_kernel(tiles_per_seq, split, rc, h_ref, hs_ref, wa_ref, wg_ref, cw_ref, cb_ref, p2_ref, p1_ref,
                   o_ref, st_ref, os_ref, ups_ref, wbf_ref, halo_ref):
    i = pl.program_id(1)
    gw = wa_ref.shape[1] // split

    @pl.when(i == 0)
    def _():
        for grp in range(split):
            src = slice(grp * gw, (grp + 1) * gw)
            wbf_ref[:, 2 * grp * gw:(2 * grp + 1) * gw] = wa_ref[:, src].astype(BF16)
            wbf_ref[:, (2 * grp + 1) * gw:(2 * grp + 2) * gw] = wg_ref[:, src].astype(BF16)

    def group(rows, ride, grp, shifts):
        n = rows.size
        cols = slice(grp * gw, (grp + 1) * gw)
        proj = _dot(_ride(h_ref, rows, hs_ref, ride), wbf_ref[:, 2 * grp * gw:2 * (grp + 1) * gw])
        cur, gate = proj[:, 0:gw], proj[:, gw:2 * gw]
        cw = cw_ref[:, cols]

        def gated(cur, s1, s2, gate):
            act = jax.nn.gelu(_conv3(cw, cur, s1, s2) + cb_ref[:, cols], approximate=True)
            return (act * gate).astype(BF16)

        s1, s2 = shifts(cur[:n])
        o_ref[rows, cols] = gated(cur[:n], s1, s2, gate[:n])
        if ride:
            ups_ref[:, cols] = cur[n:]
            os_ref[:, cols] = gated(cur[n:], p1_ref[:, cols], p2_ref[:, cols], gate[n:])
        return cur[:n]

    _conv_tile(i, tiles_per_seq, split, rc, h_ref.shape[0], halo_ref, st_ref, group)


def _ffn_up(h2, h2s, w_ffn_up, ffn_conv_w, ffn_conv_b, prev2, prev1, *, bm, bn, split, rc, seq_len):
    m, ms = h2.shape[0], h2s.shape[0]
    tiles_per_seq = seq_len // bm
    n_tiles = D_FF // bn
    tile = pl.BlockSpec((bm, bn), lambda j, i: (i, j))
    stile = pl.BlockSpec((ms, bn), lambda j, i: (0, j))
    return pl.pallas_call(
        functools.partial(_ffn_up_kernel, tiles_per_seq, split, rc),
        grid=(n_tiles, m // bm),
        in_specs=[
            pl.BlockSpec((bm, D_MODEL), lambda j, i: (i, 0)),
            _resident(h2s),
            pl.BlockSpec((pl.Squeezed(), D_MODEL, bn), lambda j, i: (0, 0, j)),
            pl.BlockSpec((pl.Squeezed(), D_MODEL, bn), lambda j, i: (0, 0, n_tiles + j)),
            pl.BlockSpec((pl.Squeezed(), 3, bn), lambda j, i: (0, 0, j)),
            pl.BlockSpec((1, bn), lambda j, i: (0, j)),
            stile, stile,
        ],
        out_specs=[
            tile,
            pl.BlockSpec((pl.Squeezed(), 2, bn), lambda j, i: (i // tiles_per_seq, 0, j)),
            stile, stile,
        ],
        out_shape=[_sds((m, D_FF), BF16), _sds((m // seq_len, 2, D_FF), F32),
                   _sds((ms, D_FF), BF16), _sds((ms, D_FF), F32)],
        scratch_shapes=[pltpu.VMEM((D_MODEL, 2 * bn), BF16), pltpu.VMEM((2, bn), F32)],
        compiler_params=_params(2),
        name="ffn_up",
    )(h2, h2s, w_ffn_up, w_ffn_up, ffn_conv_w, ffn_conv_b, prev2, prev1)


def _ffn_down_kernel(rc, a_ref, as_ref, w_ref, o_ref, os_ref, wbf_ref):
    i = pl.program_id(1)

    @pl.when(i == 0)
    def _():
        wbf_ref[...] = w_ref[...].astype(BF16)

    def chunk(rows, ride):
        n = rows.size
        o = _dot(_ride(a_ref, rows, as_ref, ride), wbf_ref[...]).astype(BF16)
        o_ref[rows, :] = o[:n]
        if ride:
            os_ref[...] = o[n:]

    _walk_chunks(i, a_ref.shape[0], rc, chunk)


def _ffn_down(act, act_s, w_ffn_down, *, bm, bn, rc):
    m, ms = act.shape[0], act_s.shape[0]
    return pl.pallas_call(
        functools.partial(_ffn_down_kernel, rc),
        grid=(D_MODEL // bn, m // bm),
        in_specs=[
            pl.BlockSpec((bm, D_FF), lambda j, i: (i, 0)),
            _resident(act_s),
            pl.BlockSpec((pl.Squeezed(), D_FF, bn), lambda j, i: (0, 0, j)),
        ],
        out_specs=[pl.BlockSpec((bm, bn), lambda j, i: (i, j)),
                   pl.BlockSpec((ms, bn), lambda j, i: (0, j))],
        out_shape=[_sds((m, D_MODEL), BF16), _sds((ms, D_MODEL), BF16)],
        scratch_shapes=[pltpu.VMEM((D_FF, bn), BF16)],
        compiler_params=_params(2),
        name="ffn_down",
    )(act, act_s, w_ffn_down)


def _final_kernel(rc, f_ref, x1_ref, p_ref, fs_ref, x1s_ref, ps_ref, g_ref, wg_ref, wp_ref, y_ref, ys_ref):
    def chunk(rows, ride):
        n = rows.size
        f = _ride(f_ref, rows, fs_ref, ride).astype(F32)
        x2 = _ride(x1_ref, rows, x1s_ref, ride) + _rms(f, g_ref[...])
        gate = jax.nn.sigmoid(_dot(x2.astype(BF16), wg_ref[...]))
        y = x2 + gate * _dot(_ride(p_ref, rows, ps_ref, ride).astype(BF16), wp_ref[...])
        y_ref[rows, :] = y[:n]
        if ride:
            ys_ref[...] = y[n:]

    _walk_chunks(pl.program_id(0), x1_ref.shape[0], rc, chunk)


def _final(f, x1, p, f_s, x1_s, p_s, g_post, w_ple_gate, w_ple, *, bm, rc):
    m, ms = x1.shape[0], x1_s.shape[0]
    row = lambda i: (i, 0)
    return pl.pallas_call(
        functools.partial(_final_kernel, rc),
        grid=(m // bm,),
        in_specs=[
            pl.BlockSpec((bm, D_MODEL), row),
            pl.BlockSpec((bm, D_MODEL), row),
            pl.BlockSpec((bm, PLE_DIM), row),
            _resident(f_s), _resident(x1_s), _resident(p_s),
            _resident(g_post),
            pl.BlockSpec((pl.Squeezed(), D_MODEL, D_MODEL), lambda i: (0, 0, 0),
                         pipeline_mode=pl.Buffered(1)),
            pl.BlockSpec((pl.Squeezed(), PLE_DIM, D_MODEL), lambda i: (0, 0, 0),
                         pipeline_mode=pl.Buffered(1)),
        ],
        out_specs=[pl.BlockSpec((bm, D_MODEL), row), _resident(x1_s)],
        out_shape=[_sds((m, D_MODEL), F32), _sds((ms, D_MODEL), F32)],
        compiler_params=_params(1),
        name="ffn_norm_ple",
    )(f, x1, p, f_s, x1_s, p_s, g_post, w_ple_gate, w_ple)


def _rope_tables(pos):
    half = HEAD_DIM // 2
    inv = ROPE_THETA ** (-np.arange(half, dtype=np.float64) / half)
    ang = np.asarray(pos, dtype=np.float64)[:, None] * inv[None, :]
    cos, sin = np.cos(ang), np.sin(ang)
    reps = LANES // HEAD_DIM
    return (jnp.asarray(np.tile(np.concatenate([cos, cos], axis=1), (1, reps)), dtype=F32),
            jnp.asarray(np.tile(np.concatenate([-sin, sin], axis=1), (1, reps)), dtype=F32))


TILES = dict(qkv=dict(bm=512, rc=256), attn=128, convmix=dict(bm=1024, bn=512, split=2, rc=512),
             merge=dict(bm=512, bn=512, split=2, rc=512), oproj=dict(bm=512, rc=128),
             ffn_up=dict(bm=1024, bn=512, split=1, rc=512), ffn_down=dict(bm=512, bn=512, rc=512),
             final=dict(bm=512, rc=256))


def kernel(x_prompt, x_sample, state_attn_k, state_attn_v, state_conv, state_ffn_conv, p_prompt, p_sample,
           norm_mix_pre, norm_mix_post, norm_ffn_pre, norm_ffn_post, w_in, attn_sinks, conv_w, w_attn_branch,
           w_conv_branch, b_gate, w_o, w_ffn_up, ffn_conv_w, ffn_conv_b, w_ffn_down, w_ple, w_ple_gate):
    assert w_in.shape[0] == 1, "weights are indexed with a squeezed depth axis of size 1"
    batch, seq, _ = x_prompt.shape
    ns, dec_seq, _ = x_sample.shape
    assert dec_seq == 1
    m = batch * seq
    x = x_prompt.reshape(m, D_MODEL)
    xs = x_sample.reshape(ns, D_MODEL)
    rope_p = _rope_tables(np.arange(seq))
    rope_s = _rope_tables(np.full((ns,), PAST_LEN))

    h, q, k, v, hs, qs, ks, vs = _qkv(x, xs, norm_mix_pre, w_in, rope_p, rope_s, **TILES["qkv"])

    a, w_o_bf, w_ple_gate_bf, w_ple_bf = _attn_prompt(attn_sinks.reshape(N_HEADS), q, k, v, w_o, w_ple_gate, w_ple,
                                                      n_seq=batch, seq_len=seq, bq=TILES["attn"])
    a3, new_ks, new_vs = _attn_sample(
        attn_sinks.reshape(N_HEADS, 1), qs.reshape(ns, N_HEADS, HEAD_DIM), ks.reshape(ns, 1, KV_WIDTH),
        vs.reshape(ns, 1, KV_WIDTH), state_attn_k.reshape(ns, WINDOW, KV_WIDTH),
        state_attn_v.reshape(ns, WINDOW, KV_WIDTH))
    a_s = a3.reshape(ns, ATTN_WIDTH)

    b, new_conv_p, b_s, pre_s = _convmix(h, hs, w_in, conv_w, state_conv[0, :, 0], state_conv[0, :, 1],
                                         seq_len=seq, **TILES["convmix"])
    merged, merged_s = _merge(h, a, b, hs, a_s, b_s, w_in, w_attn_branch, w_conv_branch, b_gate, **TILES["merge"])
    x1, h2, x1_s, h2_s = _oproj(merged, x, merged_s, xs, w_o_bf, norm_mix_post, norm_ffn_pre,
                                **TILES["oproj"])
    act, new_ffn_p, act_s, up_s = _ffn_up(h2, h2_s, w_ffn_up, ffn_conv_w, ffn_conv_b, state_ffn_conv[0, :, 0],
                                          state_ffn_conv[0, :, 1], seq_len=seq, **TILES["ffn_up"])
    f, f_s = _ffn_down(act, act_s, w_ffn_down, **TILES["ffn_down"])
    y, y_s = _final(f, x1, p_prompt.reshape(m, PLE_DIM), f_s, x1_s, p_sample.reshape(ns, PLE_DIM),
                    norm_ffn_post, w_ple_gate_bf, w_ple_bf, **TILES["final"])

    kv_tail = lambda t: t.reshape(batch, seq, N_KV_HEADS, HEAD_DIM)[:, -WINDOW:]
    kv_state = lambda t: t.reshape(ns, WINDOW, N_KV_HEADS, HEAD_DIM)
    return (y.reshape(batch, seq, D_MODEL), y_s.reshape(ns, dec_seq, D_MODEL),
            kv_tail(k)[None], kv_tail(v)[None], new_conv_p[None], new_ffn_p[None],
            kv_state(new_ks)[None], kv_state(new_vs)[None],
            jnp.stack([state_conv[0, :, 1], pre_s], axis=1)[None],
            jnp.stack([state_ffn_conv[0, :, 1], up_s], axis=1)[None])
```

```python
import functools

import jax
import jax.numpy as jnp
import numpy as np
from jax import lax
from jax.experimental import pallas as pl
from jax.experimental.pallas import tpu as pltpu

D_MODEL = 2048
N_HEADS = 16
N_KV_HEADS = 2
HEAD_DIM = 64
GROUP = N_HEADS // N_KV_HEADS
ATTN_WIDTH = N_HEADS * HEAD_DIM
KV_WIDTH = N_KV_HEADS * HEAD_DIM
QKV_WIDTH = ATTN_WIDTH + 2 * KV_WIDTH
WINDOW = 128
PAST_LEN = 16384
ROPE_THETA = 10000.0
CONV_WIDTH = D_MODEL // 2
D_FF = 5632
PLE_DIM = 256
RMS_EPS = 1e-6
OFF_CB = QKV_WIDTH
OFF_CC = OFF_CB + CONV_WIDTH
OFF_CU = OFF_CC + CONV_WIDTH
OFF_GA = OFF_CU + CONV_WIDTH
OFF_GB = OFF_GA + D_MODEL

LANES = 128
MASKED = -1e30
VMEM_LIMIT = 56 * 1024 * 1024

BF16 = jnp.bfloat16
F32 = jnp.float32


def _params(n_axes):
    return pltpu.CompilerParams(dimension_semantics=("arbitrary",) * n_axes,
                                vmem_limit_bytes=VMEM_LIMIT)


def _rms(x, g):
    return x * lax.rsqrt(jnp.mean(x * x, axis=-1, keepdims=True) + RMS_EPS) * g


def _dot(a, b):
    return jnp.dot(a, b, preferred_element_type=F32)


def _walk_chunks(i, bm, rc, chunk_fn):
    rc = min(bm, rc)

    def walk(first):
        for c in range(bm // rc):
            chunk_fn(pl.ds(c * rc, rc), first and c == 0)

    @pl.when(i == 0)
    def _():
        walk(True)

    @pl.when(i > 0)
    def _():
        walk(False)


def _ride(ref, rows, sample_ref, ride):
    x = ref[rows, :]
    return jnp.concatenate([x, sample_ref[...]], axis=0) if ride else x


def _wcol(k_rows, block_cols, col_fn):
    return pl.BlockSpec((pl.Squeezed(), pl.Element(k_rows), pl.Element(block_cols)),
                        lambda *g: (0, 0, pl.multiple_of(col_fn(*g), LANES)))


def _resident(arr):
    zeros = (0,) * arr.ndim
    return pl.BlockSpec(arr.shape, lambda *g: zeros)


def _sds(shape, dtype):
    return jax.ShapeDtypeStruct(shape, dtype)


def _qkv_kernel(rc, x_ref, xs_ref, g_ref, w_ref, cos_ref, sin_ref, coss_ref, sins_ref,
                h_ref, q_ref, k_ref, v_ref, hs_ref, qs_ref, ks_ref, vs_ref, wbf_ref):
    i = pl.program_id(0)
    n_q = ATTN_WIDTH // LANES

    @pl.when(i == 0)
    def _():
        wbf_ref[...] = w_ref[...].astype(BF16)

    def chunk(rows, ride):
        n = rows.size

        def put(ref, sample_ref, cols, val):
            ref[rows, cols] = val[:n]
            if ride:
                sample_ref[:, cols] = val[n:]

        h = _rms(_ride(x_ref, rows, xs_ref, ride), g_ref[...]).astype(BF16)
        put(h_ref, hs_ref, slice(None), h)
        proj = _dot(h, wbf_ref[...])
        cos = _ride(cos_ref, rows, coss_ref, ride)
        sin = _ride(sin_ref, rows, sins_ref, ride)
        lane = lax.broadcasted_iota(jnp.int32, cos.shape, 1)
        first_half = (lane % HEAD_DIM) < (HEAD_DIM // 2)
        for c in range(n_q + 1):
            cols = slice(c * LANES, (c + 1) * LANES)
            xg = proj[:, cols]
            partner = jnp.where(first_half, pltpu.roll(xg, LANES - HEAD_DIM // 2, 1),
                                pltpu.roll(xg, HEAD_DIM // 2, 1))
            yg = xg * cos + partner * sin
            if c < n_q:
                put(q_ref, qs_ref, cols, (yg * (HEAD_DIM ** -0.5)).astype(BF16))
            else:
                put(k_ref, ks_ref, slice(None), yg)
        put(v_ref, vs_ref, slice(None), proj[:, ATTN_WIDTH + KV_WIDTH:QKV_WIDTH])

    _walk_chunks(i, x_ref.shape[0], rc, chunk)


def _qkv(x, xs, g, w_in, rope_p, rope_s, *, bm, rc):
    m, ms = x.shape[0], xs.shape[0]
    t_tiles = rope_p[0].shape[0] // bm
    row = lambda i: (i, 0)
    pos = lambda i: (i % t_tiles, 0)
    outs = lambda rows: [_sds((rows, D_MODEL), BF16), _sds((rows, ATTN_WIDTH), BF16),
                         _sds((rows, KV_WIDTH), F32), _sds((rows, KV_WIDTH), F32)]
    out_p, out_s = outs(m), outs(ms)
    return pl.pallas_call(
        functools.partial(_qkv_kernel, rc),
        grid=(m // bm,),
        in_specs=[
            pl.BlockSpec((bm, D_MODEL), row),
            _resident(xs),
            _resident(g),
            pl.BlockSpec((pl.Squeezed(), D_MODEL, QKV_WIDTH), lambda i: (0, 0, 0),
                         pipeline_mode=pl.Buffered(1)),
            pl.BlockSpec((bm, LANES), pos),
            pl.BlockSpec((bm, LANES), pos),
            _resident(rope_s[0]),
            _resident(rope_s[1]),
        ],
        out_specs=[pl.BlockSpec((bm, s.shape[1]), row) for s in out_p] + [_resident(s) for s in out_s],
        out_shape=out_p + out_s,
        scratch_shapes=[pltpu.VMEM((D_MODEL, QKV_WIDTH), BF16)],
        compiler_params=_params(1),
        name="qkv_rope",
    )(x, xs, g, w_in, rope_p[0], rope_p[1], rope_s[0], rope_s[1])


def _attn_prompt_kernel(sinks_ref, q_ref, kc_ref, kp_ref, vc_ref, vp_ref, wo_ref, wg_ref, wp_ref,
                        o_ref, wo_bf_ref, wg_bf_ref, wp_bf_ref):
    wo_bf_ref[...] = wo_ref[...].astype(BF16)
    wg_bf_ref[...] = wg_ref[...].astype(BF16)

    @pl.when((pl.program_id(0) == 0) & (pl.program_id(1) == 0))
    def _():
        wp_bf_ref[...] = wp_ref[...].astype(BF16)

    n_sub = q_ref.shape[0] // WINDOW
    qi = lax.broadcasted_iota(jnp.int32, (WINDOW, WINDOW), 0)
    ci = lax.broadcasted_iota(jnp.int32, (WINDOW, WINDOW), 1)
    own = ci <= qi
    first_valid = own | (pl.program_id(1) > 0)
    k_all = jnp.concatenate([kp_ref[...], kc_ref[...]], axis=0).astype(BF16)
    v_all = jnp.concatenate([vp_ref[...], vc_ref[...]], axis=0).astype(BF16)
    for sub in range(n_sub):
        q_rows = slice(sub * WINDOW, (sub + 1) * WINDOW)
        kv_rows = slice(sub * WINDOW, (sub + 2) * WINDOW)
        for kh in range(N_KV_HEADS):
            cols = slice(kh * HEAD_DIM, (kh + 1) * HEAD_DIM)
            k = k_all[kv_rows, cols]
            v = v_all[kv_rows, cols]
            heads = range(kh * GROUP, (kh + 1) * GROUP)
            qs = jnp.concatenate([q_ref[q_rows, h * HEAD_DIM:(h + 1) * HEAD_DIM] for h in heads], axis=0)
            s_all = lax.dot_general(qs, k, (((1,), (1,)), ((), ())), preferred_element_type=F32)
            ps, dens = [], []
            for g, h in enumerate(heads):
                s2 = s_all[g * WINDOW:(g + 1) * WINDOW, :]
                s = jnp.where(own, s2[:, WINDOW:], s2[:, :WINDOW])
                if sub == 0:
                    s = jnp.where(first_valid, s, MASKED)
                sink = sinks_ref[h]
                m = jnp.maximum(jnp.max(s, axis=-1, keepdims=True), sink)
                p = jnp.exp(s - m)
                dens.append(jnp.sum(p, axis=-1, keepdims=True) + jnp.exp(sink - m))
                ps.append(jnp.concatenate([jnp.where(own, 0.0, p), jnp.where(own, p, 0.0)], axis=1).astype(BF16))
            o_all = _dot(jnp.concatenate(ps, axis=0), v)
            for g, h in enumerate(heads):
                o = o_all[g * WINDOW:(g + 1) * WINDOW, :] / dens[g]
                o_ref[q_rows, h * HEAD_DIM:(h + 1) * HEAD_DIM] = o.astype(BF16)


def _attn_prompt(sinks, q, k, v, w_o, w_ple_gate, w_ple, *, n_seq, seq_len, bq):
    nb = seq_len // bq
    sub = bq // WINDOW
    slab = D_MODEL // (n_seq * nb)
    cur = lambda n, b: (n * nb + b, 0)
    prev = lambda n, b: (jnp.maximum((n * nb + b) * sub - 1, 0), 0)
    wslab = pl.BlockSpec((pl.Squeezed(), slab, D_MODEL), lambda n, b: (0, n * nb + b, 0))
    return pl.pallas_call(
        _attn_prompt_kernel,
        grid=(n_seq, nb),
        in_specs=[
            pl.BlockSpec(memory_space=pltpu.SMEM),
            pl.BlockSpec((bq, ATTN_WIDTH), cur),
            pl.BlockSpec((bq, KV_WIDTH), cur),
            pl.BlockSpec((WINDOW, KV_WIDTH), prev),
            pl.BlockSpec((bq, KV_WIDTH), cur),
            pl.BlockSpec((WINDOW, KV_WIDTH), prev),
            wslab, wslab, _resident(w_ple),
        ],
        out_specs=[pl.BlockSpec((bq, ATTN_WIDTH), cur), wslab, wslab, _resident(w_ple)],
        out_shape=[_sds((n_seq * seq_len, ATTN_WIDTH), BF16), _sds(w_o.shape, BF16), _sds(w_ple_gate.shape, BF16),
                   _sds(w_ple.shape, BF16)],
        compiler_params=_params(2),
        name="attn_prompt",
    )(sinks, q, k, k, v, v, w_o, w_ple_gate, w_ple)


def _attn_sample_kernel(sinks_ref, q_ref, kn_ref, vn_ref, sk_ref, sv_ref, a_ref, nk_ref, nv_ref):
    nk_ref[:, 0:WINDOW - 1, :] = sk_ref[:, 1:WINDOW, :]
    nk_ref[:, WINDOW - 1:WINDOW, :] = kn_ref[...]
    nv_ref[:, 0:WINDOW - 1, :] = sv_ref[:, 1:WINDOW, :]
    nv_ref[:, WINDOW - 1:WINDOW, :] = vn_ref[...]
    for kh in range(N_KV_HEADS):
        cols = slice(kh * HEAD_DIM, (kh + 1) * HEAD_DIM)
        heads = slice(kh * GROUP, (kh + 1) * GROUP)
        k = nk_ref[:, :, cols].astype(BF16)
        v = nv_ref[:, :, cols].astype(BF16)
        s = jnp.einsum("ngd,ncd->ngc", q_ref[:, heads, :], k, preferred_element_type=F32)
        sink = sinks_ref[heads, :][None]
        m = jnp.maximum(jnp.max(s, axis=-1, keepdims=True), sink)
        p = jnp.exp(s - m)
        den = jnp.sum(p, axis=-1, keepdims=True) + jnp.exp(sink - m)
        o = jnp.einsum("ngc,ncd->ngd", p.astype(BF16), v, preferred_element_type=F32) / den
        a_ref[:, heads, :] = o.astype(BF16)


def _attn_sample(sinks_col, q3, k_new3, v_new3, state_k, state_v):
    n = q3.shape[0]
    return pl.pallas_call(
        _attn_sample_kernel,
        out_shape=[_sds((n, N_HEADS, HEAD_DIM), BF16), _sds((n, WINDOW, KV_WIDTH), F32),
                   _sds((n, WINDOW, KV_WIDTH), F32)],
        compiler_params=pltpu.CompilerParams(vmem_limit_bytes=VMEM_LIMIT),
        name="attn_sample",
    )(sinks_col, q3, k_new3, v_new3, state_k, state_v)


def _shift_rows(cur, halo):
    rid = lax.broadcasted_iota(jnp.int32, cur.shape, 0)
    h0 = halo[0:1, :]
    h1 = halo[1:2, :]
    s1 = jnp.where(rid == 0, h1, pltpu.roll(cur, 1, 0))
    s2 = jnp.where(rid == 0, h0, jnp.where(rid == 1, h1, pltpu.roll(cur, 2, 0)))
    return s1, s2


def _conv3(cw, cur, s1, s2):
    return cw[0:1, :] * s2 + cw[1:2, :] * s1 + cw[2:3, :] * cur


def _conv_tile(i, tiles_per_seq, split, rc, bm, halo_ref, st_ref, group_fn):
    gw = halo_ref.shape[1] // split
    gcols = [slice(grp * gw, (grp + 1) * gw) for grp in range(split)]

    @pl.when(i % tiles_per_seq == 0)
    def _():
        halo_ref[...] = jnp.zeros_like(halo_ref)

    halos = {}

    def chunk(rows, ride):
        for grp in range(split):
            halo = halos.get(grp)
            if halo is None:
                halo = halo_ref[:, gcols[grp]]
            cur = group_fn(rows, ride, grp, lambda cur, halo=halo: _shift_rows(cur, halo))
            halos[grp] = cur[cur.shape[0] - 2:, :]
        if rows.start + rows.size == bm:
            for grp in range(split):
                halo_ref[:, gcols[grp]] = halos.pop(grp)

    _walk_chunks(i, bm, rc, chunk)

    @pl.when(i % tiles_per_seq == tiles_per_seq - 1)
    def _():
        st_ref[...] = halo_ref[...]


def _convmix_kernel(tiles_per_seq, split, rc, h_ref, hs_ref, wb_ref, wc_ref, wu_ref, cw_ref, p2_ref, p1_ref,
                    o_ref, st_ref, os_ref, pres_ref, wbf_ref, halo_ref):
    i = pl.program_id(1)
    gw = wb_ref.shape[1] // split

    @pl.when(i == 0)
    def _():
        for grp in range(split):
            src = slice(grp * gw, (grp + 1) * gw)
            for part, w_ref in enumerate((wb_ref, wc_ref, wu_ref)):
                wbf_ref[:, (3 * grp + part) * gw:(3 * grp + part + 1) * gw] = w_ref[:, src].astype(BF16)

    def group(rows, ride, grp, shifts):
        n = rows.size
        cols = slice(grp * gw, (grp + 1) * gw)
        proj = _dot(_ride(h_ref, rows, hs_ref, ride), wbf_ref[:, 3 * grp * gw:3 * (grp + 1) * gw])
        cb, cur = proj[:, 0:gw], proj[:, gw:2 * gw] * proj[:, 2 * gw:3 * gw]
        cw = cw_ref[:, cols]
        s1, s2 = shifts(cur[:n])
        o_ref[rows, cols] = (cb[:n] * _conv3(cw, cur[:n], s1, s2)).astype(BF16)
        if ride:
            pres_ref[:, cols] = cur[n:]
            os_ref[:, cols] = (cb[n:] * _conv3(cw, cur[n:], p1_ref[:, cols], p2_ref[:, cols])).astype(BF16)
        return cur[:n]

    _conv_tile(i, tiles_per_seq, split, rc, h_ref.shape[0], halo_ref, st_ref, group)


def _convmix(h, hs, w_in, conv_w, prev2, prev1, *, bm, bn, split, rc, seq_len):
    m, ms = h.shape[0], hs.shape[0]
    tiles_per_seq = seq_len // bm
    tile = pl.BlockSpec((bm, bn), lambda j, i: (i, j))
    stile = pl.BlockSpec((ms, bn), lambda j, i: (0, j))
    wspec = lambda off: _wcol(D_MODEL, bn, lambda j, i: off + j * bn)
    return pl.pallas_call(
        functools.partial(_convmix_kernel, tiles_per_seq, split, rc),
        grid=(CONV_WIDTH // bn, m // bm),
        in_specs=[
            pl.BlockSpec((bm, D_MODEL), lambda j, i: (i, 0)),
            _resident(hs),
            wspec(OFF_CB), wspec(OFF_CC), wspec(OFF_CU),
            pl.BlockSpec((pl.Squeezed(), 3, bn), lambda j, i: (0, 0, j)),
            stile, stile,
        ],
        out_specs=[
            tile,
            pl.BlockSpec((pl.Squeezed(), 2, bn), lambda j, i: (i // tiles_per_seq, 0, j)),
            stile, stile,
        ],
        out_shape=[_sds((m, CONV_WIDTH), BF16), _sds((m // seq_len, 2, CONV_WIDTH), F32),
                   _sds((ms, CONV_WIDTH), BF16), _sds((ms, CONV_WIDTH), F32)],
        scratch_shapes=[pltpu.VMEM((D_MODEL, 3 * bn), BF16), pltpu.VMEM((2, bn), F32)],
        compiler_params=_params(2),
        name="convmix",
    )(h, hs, w_in, w_in, w_in, conv_w, prev2, prev1)


def _merge_kernel(split, rc, h_ref, a_ref, b_ref, hs_ref, as_ref, bs_ref, wga_ref, wgb_ref, wa_ref, wb_ref, bg_ref,
                  o_ref, os_ref, wg_bf_ref, wa_bf_ref, wb_bf_ref):
    i = pl.program_id(1)
    gw = wa_ref.shape[1] // split

    @pl.when(i == 0)
    def _():
        for grp in range(split):
            src = slice(grp * gw, (grp + 1) * gw)
            wg_bf_ref[:, 2 * grp * gw:(2 * grp + 1) * gw] = wga_ref[:, src].astype(BF16)
            wg_bf_ref[:, (2 * grp + 1) * gw:(2 * grp + 2) * gw] = wgb_ref[:, src].astype(BF16)
        wa_bf_ref[...] = wa_ref[...].astype(BF16)
        wb_bf_ref[...] = wb_ref[...].astype(BF16)

    def chunk(rows, ride):
        n = rows.size
        h = _ride(h_ref, rows, hs_ref, ride)
        a = _ride(a_ref, rows, as_ref, ride)
        b = _ride(b_ref, rows, bs_ref, ride)
        for grp in range(split):
            cols = slice(grp * gw, (grp + 1) * gw)
            gates = _dot(h, wg_bf_ref[:, 2 * grp * gw:2 * (grp + 1) * gw])
            ga = jax.nn.sigmoid(gates[:, 0:gw] + bg_ref[0:1, cols])
            gb = jax.nn.sigmoid(gates[:, gw:2 * gw] + bg_ref[1:2, cols])
            merged = (ga * _dot(a, wa_bf_ref[:, cols]) + gb * _dot(b, wb_bf_ref[:, cols])).astype(BF16)
            o_ref[rows, cols] = merged[:n]
            if ride:
                os_ref[:, cols] = merged[n:]

    _walk_chunks(i, h_ref.shape[0], rc, chunk)


def _merge(h, a, b, hs, a_s, b_s, w_in, w_attn_branch, w_conv_branch, b_gate, *, bm, bn, split, rc):
    m, ms = h.shape[0], hs.shape[0]
    row = lambda j, i: (i, 0)
    return pl.pallas_call(
        functools.partial(_merge_kernel, split, rc),
        grid=(D_MODEL // bn, m // bm),
        in_specs=[
            pl.BlockSpec((bm, D_MODEL), row),
            pl.BlockSpec((bm, ATTN_WIDTH), row),
            pl.BlockSpec((bm, CONV_WIDTH), row),
            _resident(hs), _resident(a_s), _resident(b_s),
            _wcol(D_MODEL, bn, lambda j, i: OFF_GA + j * bn),
            _wcol(D_MODEL, bn, lambda j, i: OFF_GB + j * bn),
            pl.BlockSpec((pl.Squeezed(), ATTN_WIDTH, bn), lambda j, i: (0, 0, j)),
            pl.BlockSpec((pl.Squeezed(), CONV_WIDTH, bn), lambda j, i: (0, 0, j)),
            pl.BlockSpec((pl.Squeezed(), 2, bn), lambda j, i: (0, 0, j)),
        ],
        out_specs=[pl.BlockSpec((bm, bn), lambda j, i: (i, j)),
                   pl.BlockSpec((ms, bn), lambda j, i: (0, j))],
        out_shape=[_sds((m, D_MODEL), BF16), _sds((ms, D_MODEL), BF16)],
        scratch_shapes=[pltpu.VMEM((D_MODEL, 2 * bn), BF16), pltpu.VMEM((ATTN_WIDTH, bn), BF16),
                        pltpu.VMEM((CONV_WIDTH, bn), BF16)],
        compiler_params=_params(2),
        name="merge",
    )(h, a, b, hs, a_s, b_s, w_in, w_in, w_attn_branch, w_conv_branch, b_gate)


def _oproj_kernel(rc, mg_ref, x_ref, mgs_ref, xs_ref, wo_ref, gpost_ref, gpre_ref,
                  x1_ref, h2_ref, x1s_ref, h2s_ref):
    def chunk(rows, ride):
        n = rows.size
        y = _dot(_ride(mg_ref, rows, mgs_ref, ride), wo_ref[...])
        x1 = _ride(x_ref, rows, xs_ref, ride) + _rms(y, gpost_ref[...])
        h2 = _rms(x1, gpre_ref[...]).astype(BF16)
        x1_ref[rows, :] = x1[:n]
        h2_ref[rows, :] = h2[:n]
        if ride:
            x1s_ref[...] = x1[n:]
            h2s_ref[...] = h2[n:]

    _walk_chunks(pl.program_id(0), x_ref.shape[0], rc, chunk)


def _oproj(merged, x, merged_s, xs, w_o, g_post, g_pre, *, bm, rc):
    m, ms = x.shape[0], xs.shape[0]
    tile = pl.BlockSpec((bm, D_MODEL), lambda i: (i, 0))
    return pl.pallas_call(
        functools.partial(_oproj_kernel, rc),
        grid=(m // bm,),
        in_specs=[
            tile, tile, _resident(merged_s), _resident(xs),
            pl.BlockSpec((pl.Squeezed(), D_MODEL, D_MODEL), lambda i: (0, 0, 0),
                         pipeline_mode=pl.Buffered(1)),
            _resident(g_post), _resident(g_pre),
        ],
        out_specs=[tile, tile, _resident(xs), _resident(merged_s)],
        out_shape=[_sds((m, D_MODEL), F32), _sds((m, D_MODEL), BF16),
                   _sds((ms, D_MODEL), F32), _sds((ms, D_MODEL), BF16)],
        compiler_params=_params(1),
        name="oproj_norm",
    )(merged, x, merged_s, xs, w_o, g_post, g_pre)


def _ffn_up_kernel(tiles_per_seq, split, rc, h_ref, hs_ref, wa_ref, wg_ref, cw_ref, cb_ref, p2_ref, p1_ref, wd_ref,
                   o_ref, st_ref, os_ref, ups_ref, wd_bf_ref, wbf_ref, halo_ref):
    i = pl.program_id(1)
    wd_bf_ref[...] = wd_ref[...].astype(BF16)
    gw = wa_ref.shape[1] // split

    @pl.when(i == 0)
    def _():
        for grp in range(split):
            src = slice(grp * gw, (grp + 1) * gw)
            wbf_ref[:, 2 * grp * gw:(2 * grp + 1) * gw] = wa_ref[:, src].astype(BF16)
            wbf_ref[:, (2 * grp + 1) * gw:(2 * grp + 2) * gw] = wg_ref[:, src].astype(BF16)

    def group(rows, ride, grp, shifts):
        n = rows.size
        cols = slice(grp * gw, (grp + 1) * gw)
        proj = _dot(_ride(h_ref, rows, hs_ref, ride), wbf_ref[:, 2 * grp * gw:2 * (grp + 1) * gw])
        cur, gate = proj[:, 0:gw], proj[:, gw:2 * gw]
        cw = cw_ref[:, cols]

        def gated(cur, s1, s2, gate):
            act = jax.nn.gelu(_conv3(cw, cur, s1, s2) + cb_ref[:, cols], approximate=True)
            return (act * gate).astype(BF16)

        s1, s2 = shifts(cur[:n])
        o_ref[rows, cols] = gated(cur[:n], s1, s2, gate[:n])
        if ride:
            ups_ref[:, cols] = cur[n:]
            os_ref[:, cols] = gated(cur[n:], p1_ref[:, cols], p2_ref[:, cols], gate[n:])
        return cur[:n]

    _conv_tile(i, tiles_per_seq, split, rc, h_ref.shape[0], halo_ref, st_ref, group)


def _ffn_up(h2, h2s, w_ffn_up, ffn_conv_w, ffn_conv_b, prev2, prev1, w_ffn_down, *, bm, bn, split, rc, seq_len):
    m, ms = h2.shape[0], h2s.shape[0]
    tiles_per_seq = seq_len // bm
    n_tiles = D_FF // bn
    n_m = m // bm
    slab = D_FF // (n_tiles * n_m)
    wslab = pl.BlockSpec((pl.Squeezed(), slab, D_MODEL), lambda j, i: (0, j * n_m + i, 0))
    tile = pl.BlockSpec((bm, bn), lambda j, i: (i, j))
    stile = pl.BlockSpec((ms, bn), lambda j, i: (0, j))
    return pl.pallas_call(
        functools.partial(_ffn_up_kernel, tiles_per_seq, split, rc),
        grid=(n_tiles, m // bm),
        in_specs=[
            pl.BlockSpec((bm, D_MODEL), lambda j, i: (i, 0)),
            _resident(h2s),
            pl.BlockSpec((pl.Squeezed(), D_MODEL, bn), lambda j, i: (0, 0, j)),
            pl.BlockSpec((pl.Squeezed(), D_MODEL, bn), lambda j, i: (0, 0, n_tiles + j)),
            pl.BlockSpec((pl.Squeezed(), 3, bn), lambda j, i: (0, 0, j)),
            pl.BlockSpec((1, bn), lambda j, i: (0, j)),
            stile, stile, wslab,
        ],
        out_specs=[
            tile,
            pl.BlockSpec((pl.Squeezed(), 2, bn), lambda j, i: (i // tiles_per_seq, 0, j)),
            stile, stile, wslab,
        ],
        out_shape=[_sds((m, D_FF), BF16), _sds((m // seq_len, 2, D_FF), F32),
                   _sds((ms, D_FF), BF16), _sds((ms, D_FF), F32), _sds(w_ffn_down.shape, BF16)],
        scratch_shapes=[pltpu.VMEM((D_MODEL, 2 * bn), BF16), pltpu.VMEM((2, bn), F32)],
        compiler_params=_params(2),
        name="ffn_up",
    )(h2, h2s, w_ffn_up, w_ffn_up, ffn_conv_w, ffn_conv_b, prev2, prev1, w_ffn_down)


def _ffn_down_kernel(rc, a_ref, as_ref, w_ref, o_ref, os_ref):
    def chunk(rows, ride):
        n = rows.size
        o = _dot(_ride(a_ref, rows, as_ref, ride), w_ref[...]).astype(BF16)
        o_ref[rows, :] = o[:n]
        if ride:
            os_ref[...] = o[n:]

    _walk_chunks(pl.program_id(1), a_ref.shape[0], rc, chunk)


def _ffn_down(act, act_s, w_ffn_down, *, bm, bn, rc):
    m, ms = act.shape[0], act_s.shape[0]
    return pl.pallas_call(
        functools.partial(_ffn_down_kernel, rc),
        grid=(D_MODEL // bn, m // bm),
        in_specs=[
            pl.BlockSpec((bm, D_FF), lambda j, i: (i, 0)),
            _resident(act_s),
            pl.BlockSpec((pl.Squeezed(), D_FF, bn), lambda j, i: (0, 0, j)),
        ],
        out_specs=[pl.BlockSpec((bm, bn), lambda j, i: (i, j)),
                   pl.BlockSpec((ms, bn), lambda j, i: (0, j))],
        out_shape=[_sds((m, D_MODEL), BF16), _sds((ms, D_MODEL), BF16)],
        compiler_params=_params(2),
        name="ffn_down",
    )(act, act_s, w_ffn_down)


def _final_kernel(rc, f_ref, x1_ref, p_ref, fs_ref, x1s_ref, ps_ref, g_ref, wg_ref, wp_ref, y_ref, ys_ref):
    def chunk(rows, ride):
        n = rows.size
        f = _ride(f_ref, rows, fs_ref, ride).astype(F32)
        x2 = _ride(x1_ref, rows, x1s_ref, ride) + _rms(f, g_ref[...])
        gate = jax.nn.sigmoid(_dot(x2.astype(BF16), wg_ref[...]))
        y = x2 + gate * _dot(_ride(p_ref, rows, ps_ref, ride).astype(BF16), wp_ref[...])
        y_ref[rows, :] = y[:n]
        if ride:
            ys_ref[...] = y[n:]

    _walk_chunks(pl.program_id(0), x1_ref.shape[0], rc, chunk)


def _final(f, x1, p, f_s, x1_s, p_s, g_post, w_ple_gate, w_ple, *, bm, rc):
    m, ms = x1.shape[0], x1_s.shape[0]
    row = lambda i: (i, 0)
    return pl.pallas_call(
        functools.partial(_final_kernel, rc),
        grid=(m // bm,),
        in_specs=[
            pl.BlockSpec((bm, D_MODEL), row),
            pl.BlockSpec((bm, D_MODEL), row),
            pl.BlockSpec((bm, PLE_DIM), row),
            _resident(f_s), _resident(x1_s), _resident(p_s),
            _resident(g_post),
            pl.BlockSpec((pl.Squeezed(), D_MODEL, D_MODEL), lambda i: (0, 0, 0),
                         pipeline_mode=pl.Buffered(1)),
            pl.BlockSpec((pl.Squeezed(), PLE_DIM, D_MODEL), lambda i: (0, 0, 0),
                         pipeline_mode=pl.Buffered(1)),
        ],
        out_specs=[pl.BlockSpec((bm, D_MODEL), row), _resident(x1_s)],
        out_shape=[_sds((m, D_MODEL), F32), _sds((ms, D_MODEL), F32)],
        compiler_params=_params(1),
        name="ffn_norm_ple",
    )(f, x1, p, f_s, x1_s, p_s, g_post, w_ple_gate, w_ple)


def _rope_tables(pos):
    half = HEAD_DIM // 2
    inv = ROPE_THETA ** (-np.arange(half, dtype=np.float64) / half)
    ang = np.asarray(pos, dtype=np.float64)[:, None] * inv[None, :]
    cos, sin = np.cos(ang), np.sin(ang)
    reps = LANES // HEAD_DIM
    return (jnp.asarray(np.tile(np.concatenate([cos, cos], axis=1), (1, reps)), dtype=F32),
            jnp.asarray(np.tile(np.concatenate([-sin, sin], axis=1), (1, reps)), dtype=F32))


TILES = dict(qkv=dict(bm=512, rc=256), attn=128, convmix=dict(bm=1024, bn=512, split=2, rc=512),
             merge=dict(bm=512, bn=512, split=2, rc=512), oproj=dict(bm=512, rc=128),
             ffn_up=dict(bm=1024, bn=512, split=1, rc=512), ffn_down=dict(bm=1024, bn=512, rc=1024),
             final=dict(bm=512, rc=256))


def kernel(x_prompt, x_sample, state_attn_k, state_attn_v, state_conv, state_ffn_conv, p_prompt, p_sample,
           norm_mix_pre, norm_mix_post, norm_ffn_pre, norm_ffn_post, w_in, attn_sinks, conv_w, w_attn_branch,
           w_conv_branch, b_gate, w_o, w_ffn_up, ffn_conv_w, ffn_conv_b, w_ffn_down, w_ple, w_ple_gate):
    assert w_in.shape[0] == 1, "weights are indexed with a squeezed depth axis of size 1"
    batch, seq, _ = x_prompt.shape
    ns, dec_seq, _ = x_sample.shape
    assert dec_seq == 1
    m = batch * seq
    x = x_prompt.reshape(m, D_MODEL)
    xs = x_sample.reshape(ns, D_MODEL)
    rope_p = _rope_tables(np.arange(seq))
    rope_s = _rope_tables(np.full((ns,), PAST_LEN))

    h, q, k, v, hs, qs, ks, vs = _qkv(x, xs, norm_mix_pre, w_in, rope_p, rope_s, **TILES["qkv"])

    a, w_o_bf, w_ple_gate_bf, w_ple_bf = _attn_prompt(attn_sinks.reshape(N_HEADS), q, k, v, w_o, w_ple_gate, w_ple,
                                                      n_seq=batch, seq_len=seq, bq=TILES["attn"])
    a3, new_ks, new_vs = _attn_sample(
        attn_sinks.reshape(N_HEADS, 1), qs.reshape(ns, N_HEADS, HEAD_DIM), ks.reshape(ns, 1, KV_WIDTH),
        vs.reshape(ns, 1, KV_WIDTH), state_attn_k.reshape(ns, WINDOW, KV_WIDTH),
        state_attn_v.reshape(ns, WINDOW, KV_WIDTH))
    a_s = a3.reshape(ns, ATTN_WIDTH)

    b, new_conv_p, b_s, pre_s = _convmix(h, hs, w_in, conv_w, state_conv[0, :, 0], state_conv[0, :, 1],
                                         seq_len=seq, **TILES["convmix"])
    merged, merged_s = _merge(h, a, b, hs, a_s, b_s, w_in, w_attn_branch, w_conv_branch, b_gate, **TILES["merge"])
    x1, h2, x1_s, h2_s = _oproj(merged, x, merged_s, xs, w_o_bf, norm_mix_post, norm_ffn_pre,
                                **TILES["oproj"])
    act, new_ffn_p, act_s, up_s, w_ffn_down_bf = _ffn_up(
        h2, h2_s, w_ffn_up, ffn_conv_w, ffn_conv_b, state_ffn_conv[0, :, 0], state_ffn_conv[0, :, 1], w_ffn_down,
        seq_len=seq, **TILES["ffn_up"])
    f, f_s = _ffn_down(act, act_s, w_ffn_down_bf, **TILES["ffn_down"])
    y, y_s = _final(f, x1, p_prompt.reshape(m, PLE_DIM), f_s, x1_s, p_sample.reshape(ns, PLE_DIM),
                    norm_ffn_post, w_ple_gate_bf, w_ple_bf, **TILES["final"])

    kv_tail = lambda t: t.reshape(batch, seq, N_KV_HEADS, HEAD_DIM)[:, -WINDOW:]
    kv_state = lambda t: t.reshape(ns, WINDOW, N_KV_HEADS, HEAD_DIM)
    return (y.reshape(batch, seq, D_MODEL), y_s.reshape(ns, dec_seq, D_MODEL),
            kv_tail(k)[None], kv_tail(v)[None], new_conv_p[None], new_ffn_p[None],
            kv_state(new_ks)[None], kv_state(new_vs)[None],
            jnp.stack([state_conv[0, :, 1], pre_s], axis=1)[None],
            jnp.stack([state_ffn_conv[0, :, 1], up_s], axis=1)[None])
```

```python
import functools

import jax
import jax.numpy as jnp
import numpy as np
from jax import lax
from jax.experimental import pallas as pl
from jax.experimental.pallas import tpu as pltpu

D_MODEL = 2048
N_HEADS = 16
N_KV_HEADS = 2
HEAD_DIM = 64
GROUP = N_HEADS // N_KV_HEADS
ATTN_WIDTH = N_HEADS * HEAD_DIM
KV_WIDTH = N_KV_HEADS * HEAD_DIM
QKV_WIDTH = ATTN_WIDTH + 2 * KV_WIDTH
WINDOW = 128
PAST_LEN = 16384
ROPE_THETA = 10000.0
CONV_WIDTH = D_MODEL // 2
D_FF = 5632
PLE_DIM = 256
RMS_EPS = 1e-6
OFF_CB = QKV_WIDTH
OFF_CC = OFF_CB + CONV_WIDTH
OFF_CU = OFF_CC + CONV_WIDTH
OFF_GA = OFF_CU + CONV_WIDTH
OFF_GB = OFF_GA + D_MODEL

LANES = 128
UNIT_HEADS = GROUP
MASKED = -1e30
VMEM_LIMIT = 56 * 1024 * 1024

BF16 = jnp.bfloat16
F32 = jnp.float32


def _params(n_axes):
    return pltpu.CompilerParams(dimension_semantics=("arbitrary",) * n_axes,
                                vmem_limit_bytes=VMEM_LIMIT)


def _rms(x, g):
    return x * lax.rsqrt(jnp.mean(x * x, axis=-1, keepdims=True) + RMS_EPS) * g


def _dot(a, b):
    return jnp.dot(a, b, preferred_element_type=F32)


def _walk_chunks(i, bm, rc, chunk_fn):
    rc = min(bm, rc)

    def walk(first):
        for c in range(bm // rc):
            chunk_fn(pl.ds(c * rc, rc), first and c == 0)

    @pl.when(i == 0)
    def _():
        walk(True)

    @pl.when(i > 0)
    def _():
        walk(False)


def _ride(ref, rows, sample_ref, ride):
    x = ref[rows, :]
    return jnp.concatenate([x, sample_ref[...]], axis=0) if ride else x


def _wcol(k_rows, block_cols, col_fn):
    return pl.BlockSpec((pl.Squeezed(), pl.Element(k_rows), pl.Element(block_cols)),
                        lambda *g: (0, 0, pl.multiple_of(col_fn(*g), LANES)))


def _resident(arr):
    zeros = (0,) * arr.ndim
    return pl.BlockSpec(arr.shape, lambda *g: zeros)


def _sds(shape, dtype):
    return jax.ShapeDtypeStruct(shape, dtype)


def _qkv_kernel(rc, x_ref, xs_ref, g_ref, w_ref, cos_ref, sin_ref, coss_ref, sins_ref,
                h_ref, q_ref, k_ref, v_ref, hs_ref, qs_ref, ks_ref, vs_ref, wbf_ref):
    i = pl.program_id(0)
    n_q = ATTN_WIDTH // LANES

    @pl.when(i == 0)
    def _():
        wbf_ref[...] = w_ref[...].astype(BF16)

    def chunk(rows, ride):
        n = rows.size

        def put(ref, sample_ref, cols, val):
            ref[rows, cols] = val[:n]
            if ride:
                sample_ref[:, cols] = val[n:]

        h = _rms(_ride(x_ref, rows, xs_ref, ride), g_ref[...]).astype(BF16)
        put(h_ref, hs_ref, slice(None), h)
        proj = _dot(h, wbf_ref[...])
        cos = _ride(cos_ref, rows, coss_ref, ride)
        sin = _ride(sin_ref, rows, sins_ref, ride)
        lane = lax.broadcasted_iota(jnp.int32, cos.shape, 1)
        first_half = (lane % HEAD_DIM) < (HEAD_DIM // 2)
        for c in range(n_q + 1):
            cols = slice(c * LANES, (c + 1) * LANES)
            xg = proj[:, cols]
            partner = jnp.where(first_half, pltpu.roll(xg, LANES - HEAD_DIM // 2, 1),
                                pltpu.roll(xg, HEAD_DIM // 2, 1))
            yg = xg * cos + partner * sin
            if c < n_q:
                put(q_ref, qs_ref, cols, (yg * (HEAD_DIM ** -0.5)).astype(BF16))
            else:
                put(k_ref, ks_ref, slice(None), yg)
        put(v_ref, vs_ref, slice(None), proj[:, ATTN_WIDTH + KV_WIDTH:QKV_WIDTH])

    _walk_chunks(i, x_ref.shape[0], rc, chunk)


def _qkv(x, xs, g, w_in, rope_p, rope_s, *, bm, rc):
    m, ms = x.shape[0], xs.shape[0]
    t_tiles = rope_p[0].shape[0] // bm
    row = lambda i: (i, 0)
    pos = lambda i: (i % t_tiles, 0)
    outs = lambda rows: [_sds((rows, D_MODEL), BF16), _sds((rows, ATTN_WIDTH), BF16),
                         _sds((rows, KV_WIDTH), F32), _sds((rows, KV_WIDTH), F32)]
    out_p, out_s = outs(m), outs(ms)
    return pl.pallas_call(
        functools.partial(_qkv_kernel, rc),
        grid=(m // bm,),
        in_specs=[
            pl.BlockSpec((bm, D_MODEL), row),
            _resident(xs),
            _resident(g),
            pl.BlockSpec((pl.Squeezed(), D_MODEL, QKV_WIDTH), lambda i: (0, 0, 0),
                         pipeline_mode=pl.Buffered(1)),
            pl.BlockSpec((bm, LANES), pos),
            pl.BlockSpec((bm, LANES), pos),
            _resident(rope_s[0]),
            _resident(rope_s[1]),
        ],
        out_specs=[pl.BlockSpec((bm, s.shape[1]), row) for s in out_p] + [_resident(s) for s in out_s],
        out_shape=out_p + out_s,
        scratch_shapes=[pltpu.VMEM((D_MODEL, QKV_WIDTH), BF16)],
        compiler_params=_params(1),
        name="qkv_rope",
    )(x, xs, g, w_in, rope_p[0], rope_p[1], rope_s[0], rope_s[1])


def _attn_prompt_kernel(sinks_ref, q_ref, kc_ref, kp_ref, vc_ref, vp_ref, wo_ref, wg_ref, wp_ref,
                        o_ref, wo_bf_ref, wg_bf_ref, wp_bf_ref):
    wo_bf_ref[...] = wo_ref[...].astype(BF16)
    wg_bf_ref[...] = wg_ref[...].astype(BF16)

    @pl.when((pl.program_id(0) == 0) & (pl.program_id(1) == 0))
    def _():
        wp_bf_ref[...] = wp_ref[...].astype(BF16)

    n_sub = q_ref.shape[0] // WINDOW
    qi = lax.broadcasted_iota(jnp.int32, (WINDOW, WINDOW), 0)
    ci = lax.broadcasted_iota(jnp.int32, (WINDOW, WINDOW), 1)
    own = ci <= qi
    first_valid = own | (pl.program_id(1) > 0)
    k_all = jnp.concatenate([kp_ref[...], kc_ref[...]], axis=0).astype(BF16)
    v_all = jnp.concatenate([vp_ref[...], vc_ref[...]], axis=0).astype(BF16)
    units = [(sub, hg) for sub in range(n_sub) for hg in range(N_HEADS // UNIT_HEADS)]
    heads_of = lambda hg: range(hg * UNIT_HEADS, (hg + 1) * UNIT_HEADS)
    q_rows = lambda sub: slice(sub * WINDOW, (sub + 1) * WINDOW)
    kv_rows = lambda sub: slice(sub * WINDOW, (sub + 2) * WINDOW)
    kv_cols = lambda hg: slice((hg * UNIT_HEADS // GROUP) * HEAD_DIM, (hg * UNIT_HEADS // GROUP + 1) * HEAD_DIM)

    scores = []
    for sub, kh in units:
        qs = jnp.concatenate([q_ref[q_rows(sub), h * HEAD_DIM:(h + 1) * HEAD_DIM] for h in heads_of(kh)], axis=0)
        scores.append(lax.dot_general(qs, k_all[kv_rows(sub), kv_cols(kh)], (((1,), (1,)), ((), ())),
                                      preferred_element_type=F32))
    probs, dens = [], []
    for (sub, kh), s_all in zip(units, scores):
        ps, ds = [], []
        for g, h in enumerate(heads_of(kh)):
            s2 = s_all[g * WINDOW:(g + 1) * WINDOW, :]
            s = jnp.where(own, s2[:, WINDOW:], s2[:, :WINDOW])
            if sub == 0:
                s = jnp.where(first_valid, s, MASKED)
            sink = sinks_ref[h]
            m = jnp.maximum(jnp.max(s, axis=-1, keepdims=True), sink)
            p = jnp.exp(s - m)
            ds.append(jnp.sum(p, axis=-1, keepdims=True) + jnp.exp(sink - m))
            ps.append(jnp.concatenate([jnp.where(own, 0.0, p), jnp.where(own, p, 0.0)], axis=1).astype(BF16))
        probs.append(jnp.concatenate(ps, axis=0))
        dens.append(ds)
    outs = [_dot(p_all, v_all[kv_rows(sub), kv_cols(kh)]) for (sub, kh), p_all in zip(units, probs)]
    for (sub, kh), o_all, ds in zip(units, outs, dens):
        for g, h in enumerate(heads_of(kh)):
            o = o_all[g * WINDOW:(g + 1) * WINDOW, :] / ds[g]
            o_ref[q_rows(sub), h * HEAD_DIM:(h + 1) * HEAD_DIM] = o.astype(BF16)


def _attn_prompt(sinks, q, k, v, w_o, w_ple_gate, w_ple, *, n_seq, seq_len, bq):
    nb = seq_len // bq
    sub = bq // WINDOW
    slab = D_MODEL // (n_seq * nb)
    cur = lambda n, b: (n * nb + b, 0)
    prev = lambda n, b: (jnp.maximum((n * nb + b) * sub - 1, 0), 0)
    wslab = pl.BlockSpec((pl.Squeezed(), slab, D_MODEL), lambda n, b: (0, n * nb + b, 0))
    return pl.pallas_call(
        _attn_prompt_kernel,
        grid=(n_seq, nb),
        in_specs=[
            pl.BlockSpec(memory_space=pltpu.SMEM),
            pl.BlockSpec((bq, ATTN_WIDTH), cur),
            pl.BlockSpec((bq, KV_WIDTH), cur),
            pl.BlockSpec((WINDOW, KV_WIDTH), prev),
            pl.BlockSpec((bq, KV_WIDTH), cur),
            pl.BlockSpec((WINDOW, KV_WIDTH), prev),
            wslab, wslab, _resident(w_ple),
        ],
        out_specs=[pl.BlockSpec((bq, ATTN_WIDTH), cur), wslab, wslab, _resident(w_ple)],
        out_shape=[_sds((n_seq * seq_len, ATTN_WIDTH), BF16), _sds(w_o.shape, BF16), _sds(w_ple_gate.shape, BF16),
                   _sds(w_ple.shape, BF16)],
        compiler_params=_params(2),
        name="attn_prompt",
    )(sinks, q, k, k, v, v, w_o, w_ple_gate, w_ple)


def _attn_sample_kernel(sinks_ref, q_ref, kn_ref, vn_ref, sk_ref, sv_ref, a_ref, nk_ref, nv_ref):
    nk_ref[:, 0:WINDOW - 1, :] = sk_ref[:, 1:WINDOW, :]
    nk_ref[:, WINDOW - 1:WINDOW, :] = kn_ref[...]
    nv_ref[:, 0:WINDOW - 1, :] = sv_ref[:, 1:WINDOW, :]
    nv_ref[:, WINDOW - 1:WINDOW, :] = vn_ref[...]
    for kh in range(N_KV_HEADS):
        cols = slice(kh * HEAD_DIM, (kh + 1) * HEAD_DIM)
        heads = slice(kh * GROUP, (kh + 1) * GROUP)
        k = nk_ref[:, :, cols].astype(BF16)
        v = nv_ref[:, :, cols].astype(BF16)
        s = jnp.einsum("ngd,ncd->ngc", q_ref[:, heads, :], k, preferred_element_type=F32)
        sink = sinks_ref[heads, :][None]
        m = jnp.maximum(jnp.max(s, axis=-1, keepdims=True), sink)
        p = jnp.exp(s - m)
        den = jnp.sum(p, axis=-1, keepdims=True) + jnp.exp(sink - m)
        o = jnp.einsum("ngc,ncd->ngd", p.astype(BF16), v, preferred_element_type=F32) / den
        a_ref[:, heads, :] = o.astype(BF16)


def _attn_sample(sinks_col, q3, k_new3, v_new3, state_k, state_v):
    n = q3.shape[0]
    return pl.pallas_call(
        _attn_sample_kernel,
        out_shape=[_sds((n, N_HEADS, HEAD_DIM), BF16), _sds((n, WINDOW, KV_WIDTH), F32),
                   _sds((n, WINDOW, KV_WIDTH), F32)],
        compiler_params=pltpu.CompilerParams(vmem_limit_bytes=VMEM_LIMIT),
        name="attn_sample",
    )(sinks_col, q3, k_new3, v_new3, state_k, state_v)


def _shift_rows(cur, halo):
    rid = lax.broadcasted_iota(jnp.int32, cur.shape, 0)
    h0 = halo[0:1, :]
    h1 = halo[1:2, :]
    s1 = jnp.where(rid == 0, h1, pltpu.roll(cur, 1, 0))
    s2 = jnp.where(rid == 0, h0, jnp.where(rid == 1, h1, pltpu.roll(cur, 2, 0)))
    return s1, s2


def _conv3(cw, cur, s1, s2):
    return cw[0:1, :] * s2 + cw[1:2, :] * s1 + cw[2:3, :] * cur


def _conv_tile(i, tiles_per_seq, split, parts, rc, bm, halo_ref, st_ref, project, group_fn):
    gw = halo_ref.shape[1] // split
    gcols = [slice(grp * gw, (grp + 1) * gw) for grp in range(split)]

    @pl.when(i % tiles_per_seq == 0)
    def _():
        halo_ref[...] = jnp.zeros_like(halo_ref)

    halos = {}

    def chunk(rows, ride):
        last = rows.start + rows.size == bm
        full = None if last else project(rows, ride, slice(None))
        for grp in range(split):
            pcols = slice(parts * grp * gw, parts * (grp + 1) * gw)
            proj = project(rows, ride, pcols) if last else full[:, pcols]
            halo = halos.get(grp)
            if halo is None:
                halo = halo_ref[:, gcols[grp]]
            cur = group_fn(rows, ride, grp, proj, lambda cur, halo=halo: _shift_rows(cur, halo))
            halos[grp] = cur[cur.shape[0] - 2:, :]
        if last:
            for grp in range(split):
                halo_ref[:, gcols[grp]] = halos.pop(grp)

    _walk_chunks(i, bm, rc, chunk)

    @pl.when(i % tiles_per_seq == tiles_per_seq - 1)
    def _():
        st_ref[...] = halo_ref[...]


def _convmix_kernel(tiles_per_seq, split, rc, h_ref, hs_ref, wb_ref, wc_ref, wu_ref, cw_ref, p2_ref, p1_ref,
                    o_ref, st_ref, os_ref, pres_ref, wbf_ref, halo_ref):
    i = pl.program_id(1)
    gw = wb_ref.shape[1] // split

    @pl.when(i == 0)
    def _():
        for grp in range(split):
            src = slice(grp * gw, (grp + 1) * gw)
            for part, w_ref in enumerate((wb_ref, wc_ref, wu_ref)):
                wbf_ref[:, (3 * grp + part) * gw:(3 * grp + part + 1) * gw] = w_ref[:, src].astype(BF16)

    def project(rows, ride, pcols):
        return _dot(_ride(h_ref, rows, hs_ref, ride), wbf_ref[:, pcols])

    def group(rows, ride, grp, proj, shifts):
        n = rows.size
        cols = slice(grp * gw, (grp + 1) * gw)
        cb, cur = proj[:, 0:gw], proj[:, gw:2 * gw] * proj[:, 2 * gw:3 * gw]
        cw = cw_ref[:, cols]
        s1, s2 = shifts(cur[:n])
        o_ref[rows, cols] = (cb[:n] * _conv3(cw, cur[:n], s1, s2)).astype(BF16)
        if ride:
            pres_ref[:, cols] = cur[n:]
            os_ref[:, cols] = (cb[n:] * _conv3(cw, cur[n:], p1_ref[:, cols], p2_ref[:, cols])).astype(BF16)
        return cur[:n]

    _conv_tile(i, tiles_per_seq, split, 3, rc, h_ref.shape[0], halo_ref, st_ref, project, group)


def _convmix(h, hs, w_in, conv_w, prev2, prev1, *, bm, bn, split, rc, seq_len):
    m, ms = h.shape[0], hs.shape[0]
    tiles_per_seq = seq_len // bm
    tile = pl.BlockSpec((bm, bn), lambda j, i: (i, j))
    stile = pl.BlockSpec((ms, bn), lambda j, i: (0, j))
    wspec = lambda off: _wcol(D_MODEL, bn, lambda j, i: off + j * bn)
    return pl.pallas_call(
        functools.partial(_convmix_kernel, tiles_per_seq, split, rc),
        grid=(CONV_WIDTH // bn, m // bm),
        in_specs=[
            pl.BlockSpec((bm, D_MODEL), lambda j, i: (i, 0)),
            _resident(hs),
            wspec(OFF_CB), wspec(OFF_CC), wspec(OFF_CU),
            pl.BlockSpec((pl.Squeezed(), 3, bn), lambda j, i: (0, 0, j)),
            stile, stile,
        ],
        out_specs=[
            tile,
            pl.BlockSpec((pl.Squeezed(), 2, bn), lambda j, i: (i // tiles_per_seq, 0, j)),
            stile, stile,
        ],
        out_shape=[_sds((m, CONV_WIDTH), BF16), _sds((m // seq_len, 2, CONV_WIDTH), F32),
                   _sds((ms, CONV_WIDTH), BF16), _sds((ms, CONV_WIDTH), F32)],
        scratch_shapes=[pltpu.VMEM((D_MODEL, 3 * bn), BF16), pltpu.VMEM((2, bn), F32)],
        compiler_params=_params(2),
        name="convmix",
    )(h, hs, w_in, w_in, w_in, conv_w, prev2, prev1)


def _merge_kernel(split, rc, h_ref, a_ref, b_ref, hs_ref, as_ref, bs_ref, wga_ref, wgb_ref, wa_ref, wb_ref, bg_ref,
                  o_ref, os_ref, wg_bf_ref, wa_bf_ref, wb_bf_ref):
    i = pl.program_id(1)
    gw = wa_ref.shape[1] // split

    @pl.when(i == 0)
    def _():
        for grp in range(split):
            src = slice(grp * gw, (grp + 1) * gw)
            wg_bf_ref[:, 2 * grp * gw:(2 * grp + 1) * gw] = wga_ref[:, src].astype(BF16)
            wg_bf_ref[:, (2 * grp + 1) * gw:(2 * grp + 2) * gw] = wgb_ref[:, src].astype(BF16)
        wa_bf_ref[...] = wa_ref[...].astype(BF16)
        wb_bf_ref[...] = wb_ref[...].astype(BF16)

    def chunk(rows, ride):
        n = rows.size
        h = _ride(h_ref, rows, hs_ref, ride)
        a = _ride(a_ref, rows, as_ref, ride)
        b = _ride(b_ref, rows, bs_ref, ride)
        for grp in range(split):
            cols = slice(grp * gw, (grp + 1) * gw)
            gates = _dot(h, wg_bf_ref[:, 2 * grp * gw:2 * (grp + 1) * gw])
            ga = jax.nn.sigmoid(gates[:, 0:gw] + bg_ref[0:1, cols])
            gb = jax.nn.sigmoid(gates[:, gw:2 * gw] + bg_ref[1:2, cols])
            merged = (ga * _dot(a, wa_bf_ref[:, cols]) + gb * _dot(b, wb_bf_ref[:, cols])).astype(BF16)
            o_ref[rows, cols] = merged[:n]
            if ride:
                os_ref[:, cols] = merged[n:]

    _walk_chunks(i, h_ref.shape[0], rc, chunk)


def _merge(h, a, b, hs, a_s, b_s, w_in, w_attn_branch, w_conv_branch, b_gate, *, bm, bn, split, rc):
    m, ms = h.shape[0], hs.shape[0]
    row = lambda j, i: (i, 0)
    return pl.pallas_call(
        functools.partial(_merge_kernel, split, rc),
        grid=(D_MODEL // bn, m // bm),
        in_specs=[
            pl.BlockSpec((bm, D_MODEL), row),
            pl.BlockSpec((bm, ATTN_WIDTH), row),
            pl.BlockSpec((bm, CONV_WIDTH), row),
            _resident(hs), _resident(a_s), _resident(b_s),
            _wcol(D_MODEL, bn, lambda j, i: OFF_GA + j * bn),
            _wcol(D_MODEL, bn, lambda j, i: OFF_GB + j * bn),
            pl.BlockSpec((pl.Squeezed(), ATTN_WIDTH, bn), lambda j, i: (0, 0, j)),
            pl.BlockSpec((pl.Squeezed(), CONV_WIDTH, bn), lambda j, i: (0, 0, j)),
            pl.BlockSpec((pl.Squeezed(), 2, bn), lambda j, i: (0, 0, j)),
        ],
        out_specs=[pl.BlockSpec((bm, bn), lambda j, i: (i, j)),
                   pl.BlockSpec((ms, bn), lambda j, i: (0, j))],
        out_shape=[_sds((m, D_MODEL), BF16), _sds((ms, D_MODEL), BF16)],
        scratch_shapes=[pltpu.VMEM((D_MODEL, 2 * bn), BF16), pltpu.VMEM((ATTN_WIDTH, bn), BF16),
                        pltpu.VMEM((CONV_WIDTH, bn), BF16)],
        compiler_params=_params(2),
        name="merge",
    )(h, a, b, hs, a_s, b_s, w_in, w_in, w_attn_branch, w_conv_branch, b_gate)


def _oproj_kernel(rc, mg_ref, x_ref, mgs_ref, xs_ref, wo_ref, gpost_ref, gpre_ref,
                  x1_ref, h2_ref, x1s_ref, h2s_ref):
    def chunk(rows, ride):
        n = rows.size
        y = _dot(_ride(mg_ref, rows, mgs_ref, ride), wo_ref[...])
        x1 = _ride(x_ref, rows, xs_ref, ride) + _rms(y, gpost_ref[...])
        h2 = _rms(x1, gpre_ref[...]).astype(BF16)
        x1_ref[rows, :] = x1[:n]
        h2_ref[rows, :] = h2[:n]
        if ride:
            x1s_ref[...] = x1[n:]
            h2s_ref[...] = h2[n:]

    _walk_chunks(pl.program_id(0), x_ref.shape[0], rc, chunk)


def _oproj(merged, x, merged_s, xs, w_o, g_post, g_pre, *, bm, rc):
    m, ms = x.shape[0], xs.shape[0]
    tile = pl.BlockSpec((bm, D_MODEL), lambda i: (i, 0))
    return pl.pallas_call(
        functools.partial(_oproj_kernel, rc),
        grid=(m // bm,),
        in_specs=[
            tile, tile, _resident(merged_s), _resident(xs),
            pl.BlockSpec((pl.Squeezed(), D_MODEL, D_MODEL), lambda i: (0, 0, 0),
                         pipeline_mode=pl.Buffered(1)),
            _resident(g_post), _resident(g_pre),
        ],
        out_specs=[tile, tile, _resident(xs), _resident(merged_s)],
        out_shape=[_sds((m, D_MODEL), F32), _sds((m, D_MODEL), BF16),
                   _sds((ms, D_MODEL), F32), _sds((ms, D_MODEL), BF16)],
        compiler_params=_params(1),
        name="oproj_norm",
    )(merged, x, merged_s, xs, w_o, g_post, g_pre)


def _ffn_up_kernel(tiles_per_seq, split, rc, h_ref, hs_ref, wa_ref, wg_ref, cw_ref, cb_ref, p2_ref, p1_ref, wd_ref,
                   o_ref, st_ref, os_ref, ups_ref, wd_bf_ref, wbf_ref, halo_ref):
    i = pl.program_id(1)
    wd_bf_ref[...] = wd_ref[...].astype(BF16)
    gw = wa_ref.shape[1] // split

    @pl.when(i == 0)
    def _():
        for grp in range(split):
            src = slice(grp * gw, (grp + 1) * gw)
            wbf_ref[:, 2 * grp * gw:(2 * grp + 1) * gw] = wa_ref[:, src].astype(BF16)
            wbf_ref[:, (2 * grp + 1) * gw:(2 * grp + 2) * gw] = wg_ref[:, src].astype(BF16)

    def project(rows, ride, pcols):
        return _dot(_ride(h_ref, rows, hs_ref, ride), wbf_ref[:, pcols])

    def group(rows, ride, grp, proj, shifts):
        n = rows.size
        cols = slice(grp * gw, (grp + 1) * gw)
        cur, gate = proj[:, 0:gw], proj[:, gw:2 * gw]
        cw = cw_ref[:, cols]

        def gated(cur, s1, s2, gate):
            act = jax.nn.gelu(_conv3(cw, cur, s1, s2) + cb_ref[:, cols], approximate=True)
            return (act * gate).astype(BF16)

        s1, s2 = shifts(cur[:n])
        o_ref[rows, cols] = gated(cur[:n], s1, s2, gate[:n])
        if ride:
            ups_ref[:, cols] = cur[n:]
            os_ref[:, cols] = gated(cur[n:], p1_ref[:, cols], p2_ref[:, cols], gate[n:])
        return cur[:n]

    _conv_tile(i, tiles_per_seq, split, 2, rc, h_ref.shape[0], halo_ref, st_ref, project, group)


def _ffn_up(h2, h2s, w_ffn_up, ffn_conv_w, ffn_conv_b, prev2, prev1, w_ffn_down, *, bm, bn, split, rc, seq_len):
    m, ms = h2.shape[0], h2s.shape[0]
    tiles_per_seq = seq_len // bm
    n_tiles = D_FF // bn
    n_m = m // bm
    slab = D_FF // (n_tiles * n_m)
    wslab = pl.BlockSpec((pl.Squeezed(), slab, D_MODEL), lambda j, i: (0, j * n_m + i, 0))
    tile = pl.BlockSpec((bm, bn), lambda j, i: (i, j))
    stile = pl.BlockSpec((ms, bn), lambda j, i: (0, j))
    return pl.pallas_call(
        functools.partial(_ffn_up_kernel, tiles_per_seq, split, rc),
        grid=(n_tiles, m // bm),
        in_specs=[
            pl.BlockSpec((bm, D_MODEL), lambda j, i: (i, 0)),
            _resident(h2s),
            pl.BlockSpec((pl.Squeezed(), D_MODEL, bn), lambda j, i: (0, 0, j)),
            pl.BlockSpec((pl.Squeezed(), D_MODEL, bn), lambda j, i: (0, 0, n_tiles + j)),
            pl.BlockSpec((pl.Squeezed(), 3, bn), lambda j, i: (0, 0, j)),
            pl.BlockSpec((1, bn), lambda j, i: (0, j)),
            stile, stile, wslab,
        ],
        out_specs=[
            tile,
            pl.BlockSpec((pl.Squeezed(), 2, bn), lambda j, i: (i // tiles_per_seq, 0, j)),
            stile, stile, wslab,
        ],
        out_shape=[_sds((m, D_FF), BF16), _sds((m // seq_len, 2, D_FF), F32),
                   _sds((ms, D_FF), BF16), _sds((ms, D_FF), F32), _sds(w_ffn_down.shape, BF16)],
        scratch_shapes=[pltpu.VMEM((D_MODEL, 2 * bn), BF16), pltpu.VMEM((2, bn), F32)],
        compiler_params=_params(2),
        name="ffn_up",
    )(h2, h2s, w_ffn_up, w_ffn_up, ffn_conv_w, ffn_conv_b, prev2, prev1, w_ffn_down)


def _ffn_down_kernel(rc, a_ref, as_ref, w_ref, o_ref, os_ref):
    def chunk(rows, ride):
        n = rows.size
        o = _dot(_ride(a_ref, rows, as_ref, ride), w_ref[...]).astype(BF16)
        o_ref[rows, :] = o[:n]
        if ride:
            os_ref[...] = o[n:]

    _walk_chunks(pl.program_id(1), a_ref.shape[0], rc, chunk)


def _ffn_down(act, act_s, w_ffn_down, *, bm, bn, rc):
    m, ms = act.shape[0], act_s.shape[0]
    return pl.pallas_call(
        functools.partial(_ffn_down_kernel, rc),
        grid=(D_MODEL // bn, m // bm),
        in_specs=[
            pl.BlockSpec((bm, D_FF), lambda j, i: (i, 0)),
            _resident(act_s),
            pl.BlockSpec((pl.Squeezed(), D_FF, bn), lambda j, i: (0, 0, j)),
        ],
        out_specs=[pl.BlockSpec((bm, bn), lambda j, i: (i, j)),
                   pl.BlockSpec((ms, bn), lambda j, i: (0, j))],
        out_shape=[_sds((m, D_MODEL), BF16), _sds((ms, D_MODEL), BF16)],
        compiler_params=_params(2),
        name="ffn_down",
    )(act, act_s, w_ffn_down)


def _final_kernel(rc, f_ref, x1_ref, p_ref, fs_ref, x1s_ref, ps_ref, g_ref, wg_ref, wp_ref, y_ref, ys_ref):
    def chunk(rows, ride):
        n = rows.size
        f = _ride(f_ref, rows, fs_ref, ride).astype(F32)
        x2 = _ride(x1_ref, rows, x1s_ref, ride) + _rms(f, g_ref[...])
        gate = jax.nn.sigmoid(_dot(x2.astype(BF16), wg_ref[...]))
        y = x2 + gate * _dot(_ride(p_ref, rows, ps_ref, ride).astype(BF16), wp_ref[...])
        y_ref[rows, :] = y[:n]
        if ride:
            ys_ref[...] = y[n:]

    _walk_chunks(pl.program_id(0), x1_ref.shape[0], rc, chunk)


def _final(f, x1, p, f_s, x1_s, p_s, g_post, w_ple_gate, w_ple, *, bm, rc):
    m, ms = x1.shape[0], x1_s.shape[0]
    row = lambda i: (i, 0)
    return pl.pallas_call(
        functools.partial(_final_kernel, rc),
        grid=(m // bm,),
        in_specs=[
            pl.BlockSpec((bm, D_MODEL), row),
            pl.BlockSpec((bm, D_MODEL), row),
            pl.BlockSpec((bm, PLE_DIM), row),
            _resident(f_s), _resident(x1_s), _resident(p_s),
            _resident(g_post),
            pl.BlockSpec((pl.Squeezed(), D_MODEL, D_MODEL), lambda i: (0, 0, 0),
                         pipeline_mode=pl.Buffered(1)),
            pl.BlockSpec((pl.Squeezed(), PLE_DIM, D_MODEL), lambda i: (0, 0, 0),
                         pipeline_mode=pl.Buffered(1)),
        ],
        out_specs=[pl.BlockSpec((bm, D_MODEL), row), _resident(x1_s)],
        out_shape=[_sds((m, D_MODEL), F32), _sds((ms, D_MODEL), F32)],
        compiler_params=_params(1),
        name="ffn_norm_ple",
    )(f, x1, p, f_s, x1_s, p_s, g_post, w_ple_gate, w_ple)


def _rope_tables(pos):
    half = HEAD_DIM // 2
    inv = ROPE_THETA ** (-np.arange(half, dtype=np.float64) / half)
    ang = np.asarray(pos, dtype=np.float64)[:, None] * inv[None, :]
    cos, sin = np.cos(ang), np.sin(ang)
    reps = LANES // HEAD_DIM
    return (jnp.asarray(np.tile(np.concatenate([cos, cos], axis=1), (1, reps)), dtype=F32),
            jnp.asarray(np.tile(np.concatenate([-sin, sin], axis=1), (1, reps)), dtype=F32))


TILES = dict(qkv=dict(bm=1024, rc=256), attn=128, convmix=dict(bm=1024, bn=512, split=2, rc=512),
             merge=dict(bm=512, bn=512, split=2, rc=512), oproj=dict(bm=512, rc=128),
             ffn_up=dict(bm=1024, bn=512, split=1, rc=512), ffn_down=dict(bm=1024, bn=512, rc=1024),
             final=dict(bm=512, rc=256))


def kernel(x_prompt, x_sample, state_attn_k, state_attn_v, state_conv, state_ffn_conv, p_prompt, p_sample,
           norm_mix_pre, norm_mix_post, norm_ffn_pre, norm_ffn_post, w_in, attn_sinks, conv_w, w_attn_branch,
           w_conv_branch, b_gate, w_o, w_ffn_up, ffn_conv_w, ffn_conv_b, w_ffn_down, w_ple, w_ple_gate):
    assert w_in.shape[0] == 1, "weights are indexed with a squeezed depth axis of size 1"
    batch, seq, _ = x_prompt.shape
    ns, dec_seq, _ = x_sample.shape
    assert dec_seq == 1
    m = batch * seq
    x = x_prompt.reshape(m, D_MODEL)
    xs = x_sample.reshape(ns, D_MODEL)
    rope_p = _rope_tables(np.arange(seq))
    rope_s = _rope_tables(np.full((ns,), PAST_LEN))

    h, q, k, v, hs, qs, ks, vs = _qkv(x, xs, norm_mix_pre, w_in, rope_p, rope_s, **TILES["qkv"])

    a, w_o_bf, w_ple_gate_bf, w_ple_bf = _attn_prompt(attn_sinks.reshape(N_HEADS), q, k, v, w_o, w_ple_gate, w_ple,
                                                      n_seq=batch, seq_len=seq, bq=TILES["attn"])
    a3, new_ks, new_vs = _attn_sample(
        attn_sinks.reshape(N_HEADS, 1), qs.reshape(ns, N_HEADS, HEAD_DIM), ks.reshape(ns, 1, KV_WIDTH),
        vs.reshape(ns, 1, KV_WIDTH), state_attn_k.reshape(ns, WINDOW, KV_WIDTH),
        state_attn_v.reshape(ns, WINDOW, KV_WIDTH))
    a_s = a3.reshape(ns, ATTN_WIDTH)

    b, new_conv_p, b_s, pre_s = _convmix(h, hs, w_in, conv_w, state_conv[0, :, 0], state_conv[0, :, 1],
                                         seq_len=seq, **TILES["convmix"])
    merged, merged_s = _merge(h, a, b, hs, a_s, b_s, w_in, w_attn_branch, w_conv_branch, b_gate, **TILES["merge"])
    x1, h2, x1_s, h2_s = _oproj(merged, x, merged_s, xs, w_o_bf, norm_mix_post, norm_ffn_pre,
                                **TILES["oproj"])
    act, new_ffn_p, act_s, up_s, w_ffn_down_bf = _ffn_up(
        h2, h2_s, w_ffn_up, ffn_conv_w, ffn_conv_b, state_ffn_conv[0, :, 0], state_ffn_conv[0, :, 1], w_ffn_down,
        seq_len=seq, **TILES["ffn_up"])
    f, f_s = _ffn_down(act, act_s, w_ffn_down_bf, **TILES["ffn_down"])
    y, y_s = _final(f, x1, p_prompt.reshape(m, PLE_DIM), f_s, x1_s, p_sample.reshape(ns, PLE_DIM),
                    norm_ffn_post, w_ple_gate_bf, w_ple_bf, **TILES["final"])

    kv_tail = lambda t: t.reshape(batch, seq, N_KV_HEADS, HEAD_DIM)[:, -WINDOW:]
    kv_state = lambda t: t.reshape(ns, WINDOW, N_KV_HEADS, HEAD_DIM)
    return (y.reshape(batch, seq, D_MODEL), y_s.reshape(ns, dec_seq, D_MODEL),
            kv_tail(k)[None], kv_tail(v)[None], new_conv_p[None], new_ffn_p[None],
            kv_state(new_ks)[None], kv_state(new_vs)[None],
            jnp.stack([state_conv[0, :, 1], pre_s], axis=1)[None],
            jnp.stack([state_ffn_conv[0, :, 1], up_s], axis=1)[None])
```

```python
import functools

import jax
import jax.numpy as jnp
import numpy as np
from jax import lax
from jax.experimental import pallas as pl
from jax.experimental.pallas import tpu as pltpu

D_MODEL = 2048
N_HEADS = 16
N_KV_HEADS = 2
HEAD_DIM = 64
GROUP = N_HEADS // N_KV_HEADS
ATTN_WIDTH = N_HEADS * HEAD_DIM
KV_WIDTH = N_KV_HEADS * HEAD_DIM
QKV_WIDTH = ATTN_WIDTH + 2 * KV_WIDTH
WINDOW = 128
PAST_LEN = 16384
ROPE_THETA = 10000.0
CONV_WIDTH = D_MODEL // 2
D_FF = 5632
PLE_DIM = 256
RMS_EPS = 1e-6
OFF_CB = QKV_WIDTH
OFF_CC = OFF_CB + CONV_WIDTH
OFF_CU = OFF_CC + CONV_WIDTH
OFF_GA = OFF_CU + CONV_WIDTH
OFF_GB = OFF_GA + D_MODEL

LANES = 128
UNIT_HEADS = GROUP
MASKED = -1e30
VMEM_LIMIT = 56 * 1024 * 1024

BF16 = jnp.bfloat16
F32 = jnp.float32


def _params(n_axes):
    return pltpu.CompilerParams(dimension_semantics=("arbitrary",) * n_axes,
                                vmem_limit_bytes=VMEM_LIMIT)


def _rms(x, g):
    return x * lax.rsqrt(jnp.mean(x * x, axis=-1, keepdims=True) + RMS_EPS) * g


def _dot(a, b):
    return jnp.dot(a, b, preferred_element_type=F32)


def _walk_chunks(i, bm, rc, chunk_fn):
    rc = min(bm, rc)

    def walk(first):
        for c in range(bm // rc):
            chunk_fn(pl.ds(c * rc, rc), first and c == 0)

    @pl.when(i == 0)
    def _():
        walk(True)

    @pl.when(i > 0)
    def _():
        walk(False)


def _ride(ref, rows, sample_ref, ride):
    x = ref[rows, :]
    return jnp.concatenate([x, sample_ref[...]], axis=0) if ride else x


def _wcol(k_rows, block_cols, col_fn):
    return pl.BlockSpec((pl.Squeezed(), pl.Element(k_rows), pl.Element(block_cols)),
                        lambda *g: (0, 0, pl.multiple_of(col_fn(*g), LANES)))


def _resident(arr):
    zeros = (0,) * arr.ndim
    return pl.BlockSpec(arr.shape, lambda *g: zeros)


def _sds(shape, dtype):
    return jax.ShapeDtypeStruct(shape, dtype)


def _qkv_kernel(rc, x_ref, xs_ref, g_ref, w_ref, cos_ref, sin_ref, coss_ref, sins_ref,
                h_ref, q_ref, k_ref, v_ref, hs_ref, qs_ref, ks_ref, vs_ref, wbf_ref):
    i = pl.program_id(0)
    n_q = ATTN_WIDTH // LANES

    @pl.when(i == 0)
    def _():
        wbf_ref[...] = w_ref[...].astype(BF16)

    def chunk(rows, ride):
        n = rows.size

        def put(ref, sample_ref, cols, val):
            ref[rows, cols] = val[:n]
            if ride:
                sample_ref[:, cols] = val[n:]

        h = _rms(_ride(x_ref, rows, xs_ref, ride), g_ref[...]).astype(BF16)
        put(h_ref, hs_ref, slice(None), h)
        proj = _dot(h, wbf_ref[...])
        cos = _ride(cos_ref, rows, coss_ref, ride)
        sin = _ride(sin_ref, rows, sins_ref, ride)
        lane = lax.broadcasted_iota(jnp.int32, cos.shape, 1)
        first_half = (lane % HEAD_DIM) < (HEAD_DIM // 2)
        for c in range(n_q + 1):
            cols = slice(c * LANES, (c + 1) * LANES)
            xg = proj[:, cols]
            partner = jnp.where(first_half, pltpu.roll(xg, LANES - HEAD_DIM // 2, 1),
                                pltpu.roll(xg, HEAD_DIM // 2, 1))
            yg = xg * cos + partner * sin
            if c < n_q:
                put(q_ref, qs_ref, cols, (yg * (HEAD_DIM ** -0.5)).astype(BF16))
            else:
                put(k_ref, ks_ref, slice(None), yg)
        put(v_ref, vs_ref, slice(None), proj[:, ATTN_WIDTH + KV_WIDTH:QKV_WIDTH])

    _walk_chunks(i, x_ref.shape[0], rc, chunk)


def _qkv(x, xs, g, w_in, rope_p, rope_s, *, bm, rc):
    m, ms = x.shape[0], xs.shape[0]
    t_tiles = rope_p[0].shape[0] // bm
    row = lambda i: (i, 0)
    pos = lambda i: (i % t_tiles, 0)
    outs = lambda rows: [_sds((rows, D_MODEL), BF16), _sds((rows, ATTN_WIDTH), BF16),
                         _sds((rows, KV_WIDTH), F32), _sds((rows, KV_WIDTH), F32)]
    out_p, out_s = outs(m), outs(ms)
    return pl.pallas_call(
        functools.partial(_qkv_kernel, rc),
        grid=(m // bm,),
        in_specs=[
            pl.BlockSpec((bm, D_MODEL), row),
            _resident(xs),
            _resident(g),
            pl.BlockSpec((pl.Squeezed(), D_MODEL, QKV_WIDTH), lambda i: (0, 0, 0),
                         pipeline_mode=pl.Buffered(1)),
            pl.BlockSpec((bm, LANES), pos),
            pl.BlockSpec((bm, LANES), pos),
            _resident(rope_s[0]),
            _resident(rope_s[1]),
        ],
        out_specs=[pl.BlockSpec((bm, s.shape[1]), row) for s in out_p] + [_resident(s) for s in out_s],
        out_shape=out_p + out_s,
        scratch_shapes=[pltpu.VMEM((D_MODEL, QKV_WIDTH), BF16)],
        compiler_params=_params(1),
        name="qkv_rope",
    )(x, xs, g, w_in, rope_p[0], rope_p[1], rope_s[0], rope_s[1])


def _attn_prompt_kernel(gate_group, sinks_ref, q_ref, kc_ref, kp_ref, vc_ref, vp_ref, wo_ref, wg_ref, wp_ref,
                        wgate_ref, wa_ref, wb_ref, o_ref, wo_bf_ref, wg_bf_ref, wp_bf_ref, wgate_bf_ref, wa_bf_ref,
                        wb_bf_ref):
    wo_bf_ref[...] = wo_ref[...].astype(BF16)
    wg_bf_ref[...] = wg_ref[...].astype(BF16)
    wa_bf_ref[...] = wa_ref[...].astype(BF16)
    wb_bf_ref[...] = wb_ref[...].astype(BF16)
    for t in range(D_MODEL // gate_group):
        src = slice(t * gate_group, (t + 1) * gate_group)
        wgate_bf_ref[:, 2 * t * gate_group:(2 * t + 1) * gate_group] = wgate_ref[:, src].astype(BF16)
        wgate_bf_ref[:, (2 * t + 1) * gate_group:(2 * t + 2) * gate_group] = (
            wgate_ref[:, D_MODEL + t * gate_group:D_MODEL + (t + 1) * gate_group].astype(BF16))

    @pl.when((pl.program_id(0) == 0) & (pl.program_id(1) == 0))
    def _():
        wp_bf_ref[...] = wp_ref[...].astype(BF16)

    n_sub = q_ref.shape[0] // WINDOW
    qi = lax.broadcasted_iota(jnp.int32, (WINDOW, WINDOW), 0)
    ci = lax.broadcasted_iota(jnp.int32, (WINDOW, WINDOW), 1)
    own = ci <= qi
    first_valid = own | (pl.program_id(1) > 0)
    k_all = jnp.concatenate([kp_ref[...], kc_ref[...]], axis=0).astype(BF16)
    v_all = jnp.concatenate([vp_ref[...], vc_ref[...]], axis=0).astype(BF16)
    units = [(sub, hg) for sub in range(n_sub) for hg in range(N_HEADS // UNIT_HEADS)]
    heads_of = lambda hg: range(hg * UNIT_HEADS, (hg + 1) * UNIT_HEADS)
    q_rows = lambda sub: slice(sub * WINDOW, (sub + 1) * WINDOW)
    kv_rows = lambda sub: slice(sub * WINDOW, (sub + 2) * WINDOW)
    kv_cols = lambda hg: slice((hg * UNIT_HEADS // GROUP) * HEAD_DIM, (hg * UNIT_HEADS // GROUP + 1) * HEAD_DIM)

    scores = []
    for sub, kh in units:
        qs = jnp.concatenate([q_ref[q_rows(sub), h * HEAD_DIM:(h + 1) * HEAD_DIM] for h in heads_of(kh)], axis=0)
        scores.append(lax.dot_general(qs, k_all[kv_rows(sub), kv_cols(kh)], (((1,), (1,)), ((), ())),
                                      preferred_element_type=F32))
    probs, dens = [], []
    for (sub, kh), s_all in zip(units, scores):
        ps, ds = [], []
        for g, h in enumerate(heads_of(kh)):
            s2 = s_all[g * WINDOW:(g + 1) * WINDOW, :]
            s = jnp.where(own, s2[:, WINDOW:], s2[:, :WINDOW])
            if sub == 0:
                s = jnp.where(first_valid, s, MASKED)
            sink = sinks_ref[h]
            m = jnp.maximum(jnp.max(s, axis=-1, keepdims=True), sink)
            p = jnp.exp(s - m)
            ds.append(jnp.sum(p, axis=-1, keepdims=True) + jnp.exp(sink - m))
            ps.append(jnp.concatenate([jnp.where(own, 0.0, p), jnp.where(own, p, 0.0)], axis=1).astype(BF16))
        probs.append(jnp.concatenate(ps, axis=0))
        dens.append(ds)
    outs = [_dot(p_all, v_all[kv_rows(sub), kv_cols(kh)]) for (sub, kh), p_all in zip(units, probs)]
    for (sub, kh), o_all, ds in zip(units, outs, dens):
        for g, h in enumerate(heads_of(kh)):
            o = o_all[g * WINDOW:(g + 1) * WINDOW, :] / ds[g]
            o_ref[q_rows(sub), h * HEAD_DIM:(h + 1) * HEAD_DIM] = o.astype(BF16)


def _attn_prompt(sinks, q, k, v, w_o, w_ple_gate, w_ple, w_in, w_attn_branch, w_conv_branch, *, n_seq, seq_len, bq,
                 gate_group):
    nb = seq_len // bq
    sub = bq // WINDOW
    steps = n_seq * nb
    slab, bslab = D_MODEL // steps, ATTN_WIDTH // steps
    cur = lambda n, b: (n * nb + b, 0)
    prev = lambda n, b: (jnp.maximum((n * nb + b) * sub - 1, 0), 0)
    wslab = pl.BlockSpec((pl.Squeezed(), slab, D_MODEL), lambda n, b: (0, n * nb + b, 0))
    bspec = pl.BlockSpec((pl.Squeezed(), bslab, D_MODEL), lambda n, b: (0, n * nb + b, 0))
    gate_in = pl.BlockSpec((pl.Squeezed(), pl.Element(slab), pl.Element(2 * D_MODEL)),
                           lambda n, b: (0, pl.multiple_of((n * nb + b) * slab, slab), OFF_GA))
    gate_out = pl.BlockSpec((slab, 2 * D_MODEL), lambda n, b: (n * nb + b, 0))
    return pl.pallas_call(
        functools.partial(_attn_prompt_kernel, gate_group),
        grid=(n_seq, nb),
        in_specs=[
            pl.BlockSpec(memory_space=pltpu.SMEM),
            pl.BlockSpec((bq, ATTN_WIDTH), cur),
            pl.BlockSpec((bq, KV_WIDTH), cur),
            pl.BlockSpec((WINDOW, KV_WIDTH), prev),
            pl.BlockSpec((bq, KV_WIDTH), cur),
            pl.BlockSpec((WINDOW, KV_WIDTH), prev),
            wslab, wslab, _resident(w_ple), gate_in, bspec, bspec,
        ],
        out_specs=[pl.BlockSpec((bq, ATTN_WIDTH), cur), wslab, wslab, _resident(w_ple), gate_out, bspec, bspec],
        out_shape=[_sds((n_seq * seq_len, ATTN_WIDTH), BF16), _sds(w_o.shape, BF16), _sds(w_ple_gate.shape, BF16),
                   _sds(w_ple.shape, BF16), _sds((D_MODEL, 2 * D_MODEL), BF16), _sds(w_attn_branch.shape, BF16),
                   _sds(w_conv_branch.shape, BF16)],
        compiler_params=_params(2),
        name="attn_prompt",
    )(sinks, q, k, k, v, v, w_o, w_ple_gate, w_ple, w_in, w_attn_branch, w_conv_branch)


def _attn_sample_kernel(sinks_ref, q_ref, kn_ref, vn_ref, sk_ref, sv_ref, a_ref, nk_ref, nv_ref):
    nk_ref[:, 0:WINDOW - 1, :] = sk_ref[:, 1:WINDOW, :]
    nk_ref[:, WINDOW - 1:WINDOW, :] = kn_ref[...]
    nv_ref[:, 0:WINDOW - 1, :] = sv_ref[:, 1:WINDOW, :]
    nv_ref[:, WINDOW - 1:WINDOW, :] = vn_ref[...]
    for kh in range(N_KV_HEADS):
        cols = slice(kh * HEAD_DIM, (kh + 1) * HEAD_DIM)
        heads = slice(kh * GROUP, (kh + 1) * GROUP)
        k = nk_ref[:, :, cols].astype(BF16)
        v = nv_ref[:, :, cols].astype(BF16)
        s = jnp.einsum("ngd,ncd->ngc", q_ref[:, heads, :], k, preferred_element_type=F32)
        sink = sinks_ref[heads, :][None]
        m = jnp.maximum(jnp.max(s, axis=-1, keepdims=True), sink)
        p = jnp.exp(s - m)
        den = jnp.sum(p, axis=-1, keepdims=True) + jnp.exp(sink - m)
        o = jnp.einsum("ngc,ncd->ngd", p.astype(BF16), v, preferred_element_type=F32) / den
        a_ref[:, heads, :] = o.astype(BF16)


def _attn_sample(sinks_col, q3, k_new3, v_new3, state_k, state_v):
    n = q3.shape[0]
    return pl.pallas_call(
        _attn_sample_kernel,
        out_shape=[_sds((n, N_HEADS, HEAD_DIM), BF16), _sds((n, WINDOW, KV_WIDTH), F32),
                   _sds((n, WINDOW, KV_WIDTH), F32)],
        compiler_params=pltpu.CompilerParams(vmem_limit_bytes=VMEM_LIMIT),
        name="attn_sample",
    )(sinks_col, q3, k_new3, v_new3, state_k, state_v)


def _shift_rows(cur, halo):
    rid = lax.broadcasted_iota(jnp.int32, cur.shape, 0)
    h0 = halo[0:1, :]
    h1 = halo[1:2, :]
    s1 = jnp.where(rid == 0, h1, pltpu.roll(cur, 1, 0))
    s2 = jnp.where(rid == 0, h0, jnp.where(rid == 1, h1, pltpu.roll(cur, 2, 0)))
    return s1, s2


def _conv3(cw, cur, s1, s2):
    return cw[0:1, :] * s2 + cw[1:2, :] * s1 + cw[2:3, :] * cur


def _conv_tile(i, tiles_per_seq, split, parts, rc, bm, halo_ref, st_ref, project, group_fn):
    gw = halo_ref.shape[1] // split
    gcols = [slice(grp * gw, (grp + 1) * gw) for grp in range(split)]

    @pl.when(i % tiles_per_seq == 0)
    def _():
        halo_ref[...] = jnp.zeros_like(halo_ref)

    halos = {}

    def chunk(rows, ride):
        last = rows.start + rows.size == bm
        full = None if last else project(rows, ride, slice(None))
        for grp in range(split):
            pcols = slice(parts * grp * gw, parts * (grp + 1) * gw)
            proj = project(rows, ride, pcols) if last else full[:, pcols]
            halo = halos.get(grp)
            if halo is None:
                halo = halo_ref[:, gcols[grp]]
            cur = group_fn(rows, ride, grp, proj, lambda cur, halo=halo: _shift_rows(cur, halo))
            halos[grp] = cur[cur.shape[0] - 2:, :]
        if last:
            for grp in range(split):
                halo_ref[:, gcols[grp]] = halos.pop(grp)

    _walk_chunks(i, bm, rc, chunk)

    @pl.when(i % tiles_per_seq == tiles_per_seq - 1)
    def _():
        st_ref[...] = halo_ref[...]


def _convmix_kernel(tiles_per_seq, split, rc, h_ref, hs_ref, wb_ref, wc_ref, wu_ref, cw_ref, p2_ref, p1_ref,
                    o_ref, st_ref, os_ref, pres_ref, wbf_ref, halo_ref):
    i = pl.program_id(1)
    gw = wb_ref.shape[1] // split

    @pl.when(i == 0)
    def _():
        for grp in range(split):
            src = slice(grp * gw, (grp + 1) * gw)
            for part, w_ref in enumerate((wb_ref, wc_ref, wu_ref)):
                wbf_ref[:, (3 * grp + part) * gw:(3 * grp + part + 1) * gw] = w_ref[:, src].astype(BF16)

    def project(rows, ride, pcols):
        return _dot(_ride(h_ref, rows, hs_ref, ride), wbf_ref[:, pcols])

    def group(rows, ride, grp, proj, shifts):
        n = rows.size
        cols = slice(grp * gw, (grp + 1) * gw)
        cb, cur = proj[:, 0:gw], proj[:, gw:2 * gw] * proj[:, 2 * gw:3 * gw]
        cw = cw_ref[:, cols]
        s1, s2 = shifts(cur[:n])
        o_ref[rows, cols] = (cb[:n] * _conv3(cw, cur[:n], s1, s2)).astype(BF16)
        if ride:
            pres_ref[:, cols] = cur[n:]
            os_ref[:, cols] = (cb[n:] * _conv3(cw, cur[n:], p1_ref[:, cols], p2_ref[:, cols])).astype(BF16)
        return cur[:n]

    _conv_tile(i, tiles_per_seq, split, 3, rc, h_ref.shape[0], halo_ref, st_ref, project, group)


def _convmix(h, hs, w_in, conv_w, prev2, prev1, *, bm, bn, split, rc, seq_len):
    m, ms = h.shape[0], hs.shape[0]
    tiles_per_seq = seq_len // bm
    tile = pl.BlockSpec((bm, bn), lambda j, i: (i, j))
    stile = pl.BlockSpec((ms, bn), lambda j, i: (0, j))
    wspec = lambda off: _wcol(D_MODEL, bn, lambda j, i: off + j * bn)
    return pl.pallas_call(
        functools.partial(_convmix_kernel, tiles_per_seq, split, rc),
        grid=(CONV_WIDTH // bn, m // bm),
        in_specs=[
            pl.BlockSpec((bm, D_MODEL), lambda j, i: (i, 0)),
            _resident(hs),
            wspec(OFF_CB), wspec(OFF_CC), wspec(OFF_CU),
            pl.BlockSpec((pl.Squeezed(), 3, bn), lambda j, i: (0, 0, j)),
            stile, stile,
        ],
        out_specs=[
            tile,
            pl.BlockSpec((pl.Squeezed(), 2, bn), lambda j, i: (i // tiles_per_seq, 0, j)),
            stile, stile,
        ],
        out_shape=[_sds((m, CONV_WIDTH), BF16), _sds((m // seq_len, 2, CONV_WIDTH), F32),
                   _sds((ms, CONV_WIDTH), BF16), _sds((ms, CONV_WIDTH), F32)],
        scratch_shapes=[pltpu.VMEM((D_MODEL, 3 * bn), BF16), pltpu.VMEM((2, bn), F32)],
        compiler_params=_params(2),
        name="convmix",
    )(h, hs, w_in, w_in, w_in, conv_w, prev2, prev1)


def _merge_kernel(split, rc, h_ref, a_ref, b_ref, hs_ref, as_ref, bs_ref, wg_ref, wa_ref, wb_ref, bg_ref,
                  o_ref, os_ref):
    gw = wa_ref.shape[1] // split

    def chunk(rows, ride):
        n = rows.size
        h = _ride(h_ref, rows, hs_ref, ride)
        a = _ride(a_ref, rows, as_ref, ride)
        b = _ride(b_ref, rows, bs_ref, ride)
        for grp in range(split):
            cols = slice(grp * gw, (grp + 1) * gw)
            gates = _dot(h, wg_ref[:, 2 * grp * gw:2 * (grp + 1) * gw])
            ga = jax.nn.sigmoid(gates[:, 0:gw] + bg_ref[0:1, cols])
            gb = jax.nn.sigmoid(gates[:, gw:2 * gw] + bg_ref[1:2, cols])
            merged = (ga * _dot(a, wa_ref[:, cols]) + gb * _dot(b, wb_ref[:, cols])).astype(BF16)
            o_ref[rows, cols] = merged[:n]
            if ride:
                os_ref[:, cols] = merged[n:]

    _walk_chunks(pl.program_id(1), h_ref.shape[0], rc, chunk)


def _merge(h, a, b, hs, a_s, b_s, w_gates, w_attn_branch, w_conv_branch, b_gate, *, bm, bn, split, rc):
    m, ms = h.shape[0], hs.shape[0]
    row = lambda j, i: (i, 0)
    return pl.pallas_call(
        functools.partial(_merge_kernel, split, rc),
        grid=(D_MODEL // bn, m // bm),
        in_specs=[
            pl.BlockSpec((bm, D_MODEL), row),
            pl.BlockSpec((bm, ATTN_WIDTH), row),
            pl.BlockSpec((bm, CONV_WIDTH), row),
            _resident(hs), _resident(a_s), _resident(b_s),
            pl.BlockSpec((D_MODEL, 2 * bn), lambda j, i: (0, j)),
            pl.BlockSpec((pl.Squeezed(), ATTN_WIDTH, bn), lambda j, i: (0, 0, j)),
            pl.BlockSpec((pl.Squeezed(), CONV_WIDTH, bn), lambda j, i: (0, 0, j)),
            pl.BlockSpec((pl.Squeezed(), 2, bn), lambda j, i: (0, 0, j)),
        ],
        out_specs=[pl.BlockSpec((bm, bn), lambda j, i: (i, j)),
                   pl.BlockSpec((ms, bn), lambda j, i: (0, j))],
        out_shape=[_sds((m, D_MODEL), BF16), _sds((ms, D_MODEL), BF16)],
        compiler_params=_params(2),
        name="merge",
    )(h, a, b, hs, a_s, b_s, w_gates, w_attn_branch, w_conv_branch, b_gate)


def _oproj_kernel(rc, mg_ref, x_ref, mgs_ref, xs_ref, wo_ref, gpost_ref, gpre_ref,
                  x1_ref, h2_ref, x1s_ref, h2s_ref):
    def chunk(rows, ride):
        n = rows.size
        y = _dot(_ride(mg_ref, rows, mgs_ref, ride), wo_ref[...])
        x1 = _ride(x_ref, rows, xs_ref, ride) + _rms(y, gpost_ref[...])
        h2 = _rms(x1, gpre_ref[...]).astype(BF16)
        x1_ref[rows, :] = x1[:n]
        h2_ref[rows, :] = h2[:n]
        if ride:
            x1s_ref[...] = x1[n:]
            h2s_ref[...] = h2[n:]

    _walk_chunks(pl.program_id(0), x_ref.shape[0], rc, chunk)


def _oproj(merged, x, merged_s, xs, w_o, g_post, g_pre, *, bm, rc):
    m, ms = x.shape[0], xs.shape[0]
    tile = pl.BlockSpec((bm, D_MODEL), lambda i: (i, 0))
    return pl.pallas_call(
        functools.partial(_oproj_kernel, rc),
        grid=(m // bm,),
        in_specs=[
            tile, tile, _resident(merged_s), _resident(xs),
            pl.BlockSpec((pl.Squeezed(), D_MODEL, D_MODEL), lambda i: (0, 0, 0),
                         pipeline_mode=pl.Buffered(1)),
            _resident(g_post), _resident(g_pre),
        ],
        out_specs=[tile, tile, _resident(xs), _resident(merged_s)],
        out_shape=[_sds((m, D_MODEL), F32), _sds((m, D_MODEL), BF16),
                   _sds((ms, D_MODEL), F32), _sds((ms, D_MODEL), BF16)],
        compiler_params=_params(1),
        name="oproj_norm",
    )(merged, x, merged_s, xs, w_o, g_post, g_pre)


def _ffn_up_kernel(tiles_per_seq, split, rc, h_ref, hs_ref, wa_ref, wg_ref, cw_ref, cb_ref, p2_ref, p1_ref, wd_ref,
                   o_ref, st_ref, os_ref, ups_ref, wd_bf_ref, wbf_ref, halo_ref):
    i = pl.program_id(1)
    wd_bf_ref[...] = wd_ref[...].astype(BF16)
    gw = wa_ref.shape[1] // split

    @pl.when(i == 0)
    def _():
        for grp in range(split):
            src = slice(grp * gw, (grp + 1) * gw)
            wbf_ref[:, 2 * grp * gw:(2 * grp + 1) * gw] = wa_ref[:, src].astype(BF16)
            wbf_ref[:, (2 * grp + 1) * gw:(2 * grp + 2) * gw] = wg_ref[:, src].astype(BF16)

    def project(rows, ride, pcols):
        return _dot(_ride(h_ref, rows, hs_ref, ride), wbf_ref[:, pcols])

    def group(rows, ride, grp, proj, shifts):
        n = rows.size
        cols = slice(grp * gw, (grp + 1) * gw)
        cur, gate = proj[:, 0:gw], proj[:, gw:2 * gw]
        cw = cw_ref[:, cols]

        def gated(cur, s1, s2, gate):
            act = jax.nn.gelu(_conv3(cw, cur, s1, s2) + cb_ref[:, cols], approximate=True)
            return (act * gate).astype(BF16)

        s1, s2 = shifts(cur[:n])
        o_ref[rows, cols] = gated(cur[:n], s1, s2, gate[:n])
        if ride:
            ups_ref[:, cols] = cur[n:]
            os_ref[:, cols] = gated(cur[n:], p1_ref[:, cols], p2_ref[:, cols], gate[n:])
        return cur[:n]

    _conv_tile(i, tiles_per_seq, split, 2, rc, h_ref.shape[0], halo_ref, st_ref, project, group)


def _ffn_up(h2, h2s, w_ffn_up, ffn_conv_w, ffn_conv_b, prev2, prev1, w_ffn_down, *, bm, bn, split, rc, seq_len):
    m, ms = h2.shape[0], h2s.shape[0]
    tiles_per_seq = seq_len // bm
    n_tiles = D_FF // bn
    n_m = m // bm
    slab = D_FF // (n_tiles * n_m)
    wslab = pl.BlockSpec((pl.Squeezed(), slab, D_MODEL), lambda j, i: (0, j * n_m + i, 0))
    tile = pl.BlockSpec((bm, bn), lambda j, i: (i, j))
    stile = pl.BlockSpec((ms, bn), lambda j, i: (0, j))
    return pl.pallas_call(
        functools.partial(_ffn_up_kernel, tiles_per_seq, split, rc),
        grid=(n_tiles, m // bm),
        in_specs=[
            pl.BlockSpec((bm, D_MODEL), lambda j, i: (i, 0)),
            _resident(h2s),
            pl.BlockSpec((pl.Squeezed(), D_MODEL, bn), lambda j, i: (0, 0, j)),
            pl.BlockSpec((pl.Squeezed(), D_MODEL, bn), lambda j, i: (0, 0, n_tiles + j)),
            pl.BlockSpec((pl.Squeezed(), 3, bn), lambda j, i: (0, 0, j)),
            pl.BlockSpec((1, bn), lambda j, i: (0, j)),
            stile, stile, wslab,
        ],
        out_specs=[
            tile,
            pl.BlockSpec((pl.Squeezed(), 2, bn), lambda j, i: (i // tiles_per_seq, 0, j)),
            stile, stile, wslab,
        ],
        out_shape=[_sds((m, D_FF), BF16), _sds((m // seq_len, 2, D_FF), F32),
                   _sds((ms, D_FF), BF16), _sds((ms, D_FF), F32), _sds(w_ffn_down.shape, BF16)],
        scratch_shapes=[pltpu.VMEM((D_MODEL, 2 * bn), BF16), pltpu.VMEM((2, bn), F32)],
        compiler_params=_params(2),
        name="ffn_up",
    )(h2, h2s, w_ffn_up, w_ffn_up, ffn_conv_w, ffn_conv_b, prev2, prev1, w_ffn_down)


def _ffn_down_kernel(rc, a_ref, as_ref, w_ref, o_ref, os_ref):
    def chunk(rows, ride):
        n = rows.size
        o = _dot(_ride(a_ref, rows, as_ref, ride), w_ref[...]).astype(BF16)
        o_ref[rows, :] = o[:n]
        if ride:
            os_ref[...] = o[n:]

    _walk_chunks(pl.program_id(1), a_ref.shape[0], rc, chunk)


def _ffn_down(act, act_s, w_ffn_down, *, bm, bn, rc):
    m, ms = act.shape[0], act_s.shape[0]
    return pl.pallas_call(
        functools.partial(_ffn_down_kernel, rc),
        grid=(D_MODEL // bn, m // bm),
        in_specs=[
            pl.BlockSpec((bm, D_FF), lambda j, i: (i, 0)),
            _resident(act_s),
            pl.BlockSpec((pl.Squeezed(), D_FF, bn), lambda j, i: (0, 0, j)),
        ],
        out_specs=[pl.BlockSpec((bm, bn), lambda j, i: (i, j)),
                   pl.BlockSpec((ms, bn), lambda j, i: (0, j))],
        out_shape=[_sds((m, D_MODEL), BF16), _sds((ms, D_MODEL), BF16)],
        compiler_params=_params(2),
        name="ffn_down",
    )(act, act_s, w_ffn_down)


def _final_kernel(rc, f_ref, x1_ref, p_ref, fs_ref, x1s_ref, ps_ref, g_ref, wg_ref, wp_ref, y_ref, ys_ref):
    def chunk(rows, ride):
        n = rows.size
        f = _ride(f_ref, rows, fs_ref, ride).astype(F32)
        x2 = _ride(x1_ref, rows, x1s_ref, ride) + _rms(f, g_ref[...])
        gate = jax.nn.sigmoid(_dot(x2.astype(BF16), wg_ref[...]))
        y = x2 + gate * _dot(_ride(p_ref, rows, ps_ref, ride).astype(BF16), wp_ref[...])
        y_ref[rows, :] = y[:n]
        if ride:
            ys_ref[...] = y[n:]

    _walk_chunks(pl.program_id(0), x1_ref.shape[0], rc, chunk)


def _final(f, x1, p, f_s, x1_s, p_s, g_post, w_ple_gate, w_ple, *, bm, rc):
    m, ms = x1.shape[0], x1_s.shape[0]
    row = lambda i: (i, 0)
    return pl.pallas_call(
        functools.partial(_final_kernel, rc),
        grid=(m // bm,),
        in_specs=[
            pl.BlockSpec((bm, D_MODEL), row),
            pl.BlockSpec((bm, D_MODEL), row),
            pl.BlockSpec((bm, PLE_DIM), row),
            _resident(f_s), _resident(x1_s), _resident(p_s),
            _resident(g_post),
            pl.BlockSpec((pl.Squeezed(), D_MODEL, D_MODEL), lambda i: (0, 0, 0),
                         pipeline_mode=pl.Buffered(1)),
            pl.BlockSpec((pl.Squeezed(), PLE_DIM, D_MODEL), lambda i: (0, 0, 0),
                         pipeline_mode=pl.Buffered(1)),
        ],
        out_specs=[pl.BlockSpec((bm, D_MODEL), row), _resident(x1_s)],
        out_shape=[_sds((m, D_MODEL), F32), _sds((ms, D_MODEL), F32)],
        compiler_params=_params(1),
        name="ffn_norm_ple",
    )(f, x1, p, f_s, x1_s, p_s, g_post, w_ple_gate, w_ple)


def _rope_tables(pos):
    half = HEAD_DIM // 2
    inv = ROPE_THETA ** (-np.arange(half, dtype=np.float64) / half)
    ang = np.asarray(pos, dtype=np.float64)[:, None] * inv[None, :]
    cos, sin = np.cos(ang), np.sin(ang)
    reps = LANES // HEAD_DIM
    return (jnp.asarray(np.tile(np.concatenate([cos, cos], axis=1), (1, reps)), dtype=F32),
            jnp.asarray(np.tile(np.concatenate([-sin, sin], axis=1), (1, reps)), dtype=F32))


TILES = dict(qkv=dict(bm=1024, rc=256), attn=128, convmix=dict(bm=1024, bn=512, split=2, rc=512),
             merge=dict(bm=1024, bn=512, split=2, rc=1024), oproj=dict(bm=512, rc=128),
             ffn_up=dict(bm=1024, bn=512, split=1, rc=512), ffn_down=dict(bm=1024, bn=512, rc=1024),
             final=dict(bm=512, rc=256))


def kernel(x_prompt, x_sample, state_attn_k, state_attn_v, state_conv, state_ffn_conv, p_prompt, p_sample,
           norm_mix_pre, norm_mix_post, norm_ffn_pre, norm_ffn_post, w_in, attn_sinks, conv_w, w_attn_branch,
           w_conv_branch, b_gate, w_o, w_ffn_up, ffn_conv_w, ffn_conv_b, w_ffn_down, w_ple, w_ple_gate):
    assert w_in.shape[0] == 1, "weights are indexed with a squeezed depth axis of size 1"
    batch, seq, _ = x_prompt.shape
    ns, dec_seq, _ = x_sample.shape
    assert dec_seq == 1
    m = batch * seq
    x = x_prompt.reshape(m, D_MODEL)
    xs = x_sample.reshape(ns, D_MODEL)
    rope_p = _rope_tables(np.arange(seq))
    rope_s = _rope_tables(np.full((ns,), PAST_LEN))

    h, q, k, v, hs, qs, ks, vs = _qkv(x, xs, norm_mix_pre, w_in, rope_p, rope_s, **TILES["qkv"])

    merge_tiles = TILES["merge"]
    a, w_o_bf, w_ple_gate_bf, w_ple_bf, w_gates_bf, w_attn_branch_bf, w_conv_branch_bf = _attn_prompt(
        attn_sinks.reshape(N_HEADS), q, k, v, w_o, w_ple_gate, w_ple, w_in, w_attn_branch, w_conv_branch,
        n_seq=batch, seq_len=seq, bq=TILES["attn"], gate_group=merge_tiles["bn"] // merge_tiles["split"])
    a3, new_ks, new_vs = _attn_sample(
        attn_sinks.reshape(N_HEADS, 1), qs.reshape(ns, N_HEADS, HEAD_DIM), ks.reshape(ns, 1, KV_WIDTH),
        vs.reshape(ns, 1, KV_WIDTH), state_attn_k.reshape(ns, WINDOW, KV_WIDTH),
        state_attn_v.reshape(ns, WINDOW, KV_WIDTH))
    a_s = a3.reshape(ns, ATTN_WIDTH)

    b, new_conv_p, b_s, pre_s = _convmix(h, hs, w_in, conv_w, state_conv[0, :, 0], state_conv[0, :, 1],
                                         seq_len=seq, **TILES["convmix"])
    merged, merged_s = _merge(h, a, b, hs, a_s, b_s, w_gates_bf, w_attn_branch_bf, w_conv_branch_bf, b_gate,
                              **merge_tiles)
    x1, h2, x1_s, h2_s = _oproj(merged, x, merged_s, xs, w_o_bf, norm_mix_post, norm_ffn_pre,
                                **TILES["oproj"])
    act, new_ffn_p, act_s, up_s, w_ffn_down_bf = _ffn_up(
        h2, h2_s, w_ffn_up, ffn_conv_w, ffn_conv_b, state_ffn_conv[0, :, 0], state_ffn_conv[0, :, 1], w_ffn_down,
        seq_len=seq, **TILES["ffn_up"])
    f, f_s = _ffn_down(act, act_s, w_ffn_down_bf, **TILES["ffn_down"])
    y, y_s = _final(f, x1, p_prompt.reshape(m, PLE_DIM), f_s, x1_s, p_sample.reshape(ns, PLE_DIM),
                    norm_ffn_post, w_ple_gate_bf, w_ple_bf, **TILES["final"])

    kv_tail = lambda t: t.reshape(batch, seq, KV_WIDTH)[:, seq - WINDOW:].reshape(batch, WINDOW, N_KV_HEADS, HEAD_DIM)
    kv_state = lambda t: t.reshape(ns, WINDOW, N_KV_HEADS, HEAD_DIM)
    return (y.reshape(batch, seq, D_MODEL), y_s.reshape(ns, dec_seq, D_MODEL),
            kv_tail(k)[None], kv_tail(v)[None], new_conv_p[None], new_ffn_p[None],
            kv_state(new_ks)[None], kv_state(new_vs)[None],
            jnp.stack([state_conv[0, :, 1], pre_s], axis=1)[None],
            jnp.stack([state_ffn_conv[0, :, 1], up_s], axis=1)[None])
```

```python
import functools

import jax
import jax.numpy as jnp
import numpy as np
from jax import lax
from jax.experimental import pallas as pl
from jax.experimental.pallas import tpu as pltpu

D_MODEL = 2048
N_HEADS = 16
N_KV_HEADS = 2
HEAD_DIM = 64
GROUP = N_HEADS // N_KV_HEADS
ATTN_WIDTH = N_HEADS * HEAD_DIM
KV_WIDTH = N_KV_HEADS * HEAD_DIM
QKV_WIDTH = ATTN_WIDTH + 2 * KV_WIDTH
WINDOW = 128
PAST_LEN = 16384
ROPE_THETA = 10000.0
CONV_WIDTH = D_MODEL // 2
D_FF = 5632
PLE_DIM = 256
RMS_EPS = 1e-6
OFF_CB = QKV_WIDTH
OFF_CC = OFF_CB + CONV_WIDTH
OFF_CU = OFF_CC + CONV_WIDTH
OFF_GA = OFF_CU + CONV_WIDTH
OFF_GB = OFF_GA + D_MODEL

LANES = 128
UNIT_HEADS = GROUP
MASKED = -1e30
VMEM_LIMIT = 56 * 1024 * 1024

BF16 = jnp.bfloat16
F32 = jnp.float32


def _params(n_axes):
    return pltpu.CompilerParams(dimension_semantics=("arbitrary",) * n_axes,
                                vmem_limit_bytes=VMEM_LIMIT)


def _rms(x, g):
    return x * lax.rsqrt(jnp.mean(x * x, axis=-1, keepdims=True) + RMS_EPS) * g


def _dot(a, b):
    return jnp.dot(a, b, preferred_element_type=F32)


def _walk_chunks(i, bm, rc, chunk_fn):
    rc = min(bm, rc)

    def walk(first):
        for c in range(bm // rc):
            chunk_fn(pl.ds(c * rc, rc), first and c == 0)

    @pl.when(i == 0)
    def _():
        walk(True)

    @pl.when(i > 0)
    def _():
        walk(False)


def _ride(ref, rows, sample_ref, ride):
    x = ref[rows, :]
    return jnp.concatenate([x, sample_ref[...]], axis=0) if ride else x


def _wcol(k_rows, block_cols, col_fn):
    return pl.BlockSpec((pl.Squeezed(), pl.Element(k_rows), pl.Element(block_cols)),
                        lambda *g: (0, 0, pl.multiple_of(col_fn(*g), LANES)))


def _resident(arr):
    zeros = (0,) * arr.ndim
    return pl.BlockSpec(arr.shape, lambda *g: zeros)


def _sds(shape, dtype):
    return jax.ShapeDtypeStruct(shape, dtype)


def _qkv_kernel(rc, x_ref, xs_ref, g_ref, w_ref, cos_ref, sin_ref, coss_ref, sins_ref,
                h_ref, q_ref, k_ref, v_ref, hs_ref, qs_ref, ks_ref, vs_ref, wbf_ref):
    i = pl.program_id(0)
    n_q = ATTN_WIDTH // LANES

    @pl.when(i == 0)
    def _():
        wbf_ref[...] = w_ref[...].astype(BF16)

    def chunk(rows, ride):
        n = rows.size

        def put(ref, sample_ref, cols, val):
            ref[rows, cols] = val[:n]
            if ride:
                sample_ref[:, cols] = val[n:]

        h = _rms(_ride(x_ref, rows, xs_ref, ride), g_ref[...]).astype(BF16)
        put(h_ref, hs_ref, slice(None), h)
        proj = _dot(h, wbf_ref[...])
        cos = _ride(cos_ref, rows, coss_ref, ride)
        sin = _ride(sin_ref, rows, sins_ref, ride)
        lane = lax.broadcasted_iota(jnp.int32, cos.shape, 1)
        first_half = (lane % HEAD_DIM) < (HEAD_DIM // 2)
        for c in range(n_q + 1):
            cols = slice(c * LANES, (c + 1) * LANES)
            xg = proj[:, cols]
            partner = jnp.where(first_half, pltpu.roll(xg, LANES - HEAD_DIM // 2, 1),
                                pltpu.roll(xg, HEAD_DIM // 2, 1))
            yg = xg * cos + partner * sin
            if c < n_q:
                put(q_ref, qs_ref, cols, (yg * (HEAD_DIM ** -0.5)).astype(BF16))
            else:
                put(k_ref, ks_ref, slice(None), yg)
        put(v_ref, vs_ref, slice(None), proj[:, ATTN_WIDTH + KV_WIDTH:QKV_WIDTH])

    _walk_chunks(i, x_ref.shape[0], rc, chunk)


def _qkv(x, xs, g, w_in, rope_p, rope_s, *, bm, rc):
    m, ms = x.shape[0], xs.shape[0]
    t_tiles = rope_p[0].shape[0] // bm
    row = lambda i: (i, 0)
    pos = lambda i: (i % t_tiles, 0)
    outs = lambda rows: [_sds((rows, D_MODEL), BF16), _sds((rows, ATTN_WIDTH), BF16),
                         _sds((rows, KV_WIDTH), F32), _sds((rows, KV_WIDTH), F32)]
    out_p, out_s = outs(m), outs(ms)
    return pl.pallas_call(
        functools.partial(_qkv_kernel, rc),
        grid=(m // bm,),
        in_specs=[
            pl.BlockSpec((bm, D_MODEL), row),
            _resident(xs),
            _resident(g),
            pl.BlockSpec((pl.Squeezed(), D_MODEL, QKV_WIDTH), lambda i: (0, 0, 0),
                         pipeline_mode=pl.Buffered(1)),
            pl.BlockSpec((bm, LANES), pos),
            pl.BlockSpec((bm, LANES), pos),
            _resident(rope_s[0]),
            _resident(rope_s[1]),
        ],
        out_specs=[pl.BlockSpec((bm, s.shape[1]), row) for s in out_p] + [_resident(s) for s in out_s],
        out_shape=out_p + out_s,
        scratch_shapes=[pltpu.VMEM((D_MODEL, QKV_WIDTH), BF16)],
        compiler_params=_params(1),
        name="qkv_rope",
    )(x, xs, g, w_in, rope_p[0], rope_p[1], rope_s[0], rope_s[1])


def _attn_prompt_kernel(gate_group, sinks_ref, q_ref, kc_ref, kp_ref, vc_ref, vp_ref, wo_ref, wg_ref, wp_ref,
                        wgate_ref, wa_ref, wb_ref, o_ref, wo_bf_ref, wg_bf_ref, wp_bf_ref, wgate_bf_ref, wa_bf_ref,
                        wb_bf_ref):
    wo_bf_ref[...] = wo_ref[...].astype(BF16)
    wg_bf_ref[...] = wg_ref[...].astype(BF16)
    wa_bf_ref[...] = wa_ref[...].astype(BF16)
    wb_bf_ref[...] = wb_ref[...].astype(BF16)
    for t in range(D_MODEL // gate_group):
        src = slice(t * gate_group, (t + 1) * gate_group)
        wgate_bf_ref[:, 2 * t * gate_group:(2 * t + 1) * gate_group] = wgate_ref[:, src].astype(BF16)
        wgate_bf_ref[:, (2 * t + 1) * gate_group:(2 * t + 2) * gate_group] = (
            wgate_ref[:, D_MODEL + t * gate_group:D_MODEL + (t + 1) * gate_group].astype(BF16))

    @pl.when((pl.program_id(0) == 0) & (pl.program_id(1) == 0))
    def _():
        wp_bf_ref[...] = wp_ref[...].astype(BF16)

    n_sub = q_ref.shape[0] // WINDOW
    qi = lax.broadcasted_iota(jnp.int32, (WINDOW, WINDOW), 0)
    ci = lax.broadcasted_iota(jnp.int32, (WINDOW, WINDOW), 1)
    own = ci <= qi
    first_valid = own | (pl.program_id(1) > 0)
    k_all = jnp.concatenate([kp_ref[...], kc_ref[...]], axis=0).astype(BF16)
    v_all = jnp.concatenate([vp_ref[...], vc_ref[...]], axis=0).astype(BF16)
    units = [(sub, hg) for sub in range(n_sub) for hg in range(N_HEADS // UNIT_HEADS)]
    heads_of = lambda hg: range(hg * UNIT_HEADS, (hg + 1) * UNIT_HEADS)
    q_rows = lambda sub: slice(sub * WINDOW, (sub + 1) * WINDOW)
    kv_rows = lambda sub: slice(sub * WINDOW, (sub + 2) * WINDOW)
    kv_cols = lambda hg: slice((hg * UNIT_HEADS // GROUP) * HEAD_DIM, (hg * UNIT_HEADS // GROUP + 1) * HEAD_DIM)

    scores = []
    for sub, kh in units:
        qs = jnp.concatenate([q_ref[q_rows(sub), h * HEAD_DIM:(h + 1) * HEAD_DIM] for h in heads_of(kh)], axis=0)
        scores.append(lax.dot_general(qs, k_all[kv_rows(sub), kv_cols(kh)], (((1,), (1,)), ((), ())),
                                      preferred_element_type=F32))
    probs, dens = [], []
    for (sub, kh), s_all in zip(units, scores):
        ps, ds = [], []
        for g, h in enumerate(heads_of(kh)):
            s2 = s_all[g * WINDOW:(g + 1) * WINDOW, :]
            s = jnp.where(own, s2[:, WINDOW:], s2[:, :WINDOW])
            if sub == 0:
                s = jnp.where(first_valid, s, MASKED)
            sink = sinks_ref[h]
            m = jnp.maximum(jnp.max(s, axis=-1, keepdims=True), sink)
            p = jnp.exp(s - m)
            ds.append(jnp.sum(p, axis=-1, keepdims=True) + jnp.exp(sink - m))
            ps.append(jnp.concatenate([jnp.where(own, 0.0, p), jnp.where(own, p, 0.0)], axis=1).astype(BF16))
        probs.append(jnp.concatenate(ps, axis=0))
        dens.append(ds)
    outs = [_dot(p_all, v_all[kv_rows(sub), kv_cols(kh)]) for (sub, kh), p_all in zip(units, probs)]
    for (sub, kh), o_all, ds in zip(units, outs, dens):
        for g, h in enumerate(heads_of(kh)):
            o = o_all[g * WINDOW:(g + 1) * WINDOW, :] / ds[g]
            o_ref[q_rows(sub), h * HEAD_DIM:(h + 1) * HEAD_DIM] = o.astype(BF16)


def _attn_prompt(sinks, q, k, v, w_o, w_ple_gate, w_ple, w_in, w_attn_branch, w_conv_branch, *, n_seq, seq_len, bq,
                 gate_group):
    nb = seq_len // bq
    sub = bq // WINDOW
    steps = n_seq * nb
    slab, bslab = D_MODEL // steps, ATTN_WIDTH // steps
    cur = lambda n, b: (n * nb + b, 0)
    prev = lambda n, b: (jnp.maximum((n * nb + b) * sub - 1, 0), 0)
    wslab = pl.BlockSpec((pl.Squeezed(), slab, D_MODEL), lambda n, b: (0, n * nb + b, 0))
    bspec = pl.BlockSpec((pl.Squeezed(), bslab, D_MODEL), lambda n, b: (0, n * nb + b, 0))
    gate_in = pl.BlockSpec((pl.Squeezed(), pl.Element(slab), pl.Element(2 * D_MODEL)),
                           lambda n, b: (0, pl.multiple_of((n * nb + b) * slab, slab), OFF_GA))
    gate_out = pl.BlockSpec((slab, 2 * D_MODEL), lambda n, b: (n * nb + b, 0))
    return pl.pallas_call(
        functools.partial(_attn_prompt_kernel, gate_group),
        grid=(n_seq, nb),
        in_specs=[
            pl.BlockSpec(memory_space=pltpu.SMEM),
            pl.BlockSpec((bq, ATTN_WIDTH), cur),
            pl.BlockSpec((bq, KV_WIDTH), cur),
            pl.BlockSpec((WINDOW, KV_WIDTH), prev),
            pl.BlockSpec((bq, KV_WIDTH), cur),
            pl.BlockSpec((WINDOW, KV_WIDTH), prev),
            wslab, wslab, _resident(w_ple), gate_in, bspec, bspec,
        ],
        out_specs=[pl.BlockSpec((bq, ATTN_WIDTH), cur), wslab, wslab, _resident(w_ple), gate_out, bspec, bspec],
        out_shape=[_sds((n_seq * seq_len, ATTN_WIDTH), BF16), _sds(w_o.shape, BF16), _sds(w_ple_gate.shape, BF16),
                   _sds(w_ple.shape, BF16), _sds((D_MODEL, 2 * D_MODEL), BF16), _sds(w_attn_branch.shape, BF16),
                   _sds(w_conv_branch.shape, BF16)],
        compiler_params=_params(2),
        name="attn_prompt",
    )(sinks, q, k, k, v, v, w_o, w_ple_gate, w_ple, w_in, w_attn_branch, w_conv_branch)


def _attn_sample_kernel(sinks_ref, q_ref, kn_ref, vn_ref, sk_ref, sv_ref, a_ref, nk_ref, nv_ref):
    nk_ref[:, 0:WINDOW - 1, :] = sk_ref[:, 1:WINDOW, :]
    nk_ref[:, WINDOW - 1:WINDOW, :] = kn_ref[...]
    nv_ref[:, 0:WINDOW - 1, :] = sv_ref[:, 1:WINDOW, :]
    nv_ref[:, WINDOW - 1:WINDOW, :] = vn_ref[...]
    for kh in range(N_KV_HEADS):
        cols = slice(kh * HEAD_DIM, (kh + 1) * HEAD_DIM)
        heads = slice(kh * GROUP, (kh + 1) * GROUP)
        k = nk_ref[:, :, cols].astype(BF16)
        v = nv_ref[:, :, cols].astype(BF16)
        s = jnp.einsum("ngd,ncd->ngc", q_ref[:, heads, :], k, preferred_element_type=F32)
        sink = sinks_ref[heads, :][None]
        m = jnp.maximum(jnp.max(s, axis=-1, keepdims=True), sink)
        p = jnp.exp(s - m)
        den = jnp.sum(p, axis=-1, keepdims=True) + jnp.exp(sink - m)
        o = jnp.einsum("ngc,ncd->ngd", p.astype(BF16), v, preferred_element_type=F32) / den
        a_ref[:, heads, :] = o.astype(BF16)


def _attn_sample(sinks_col, q3, k_new3, v_new3, state_k, state_v):
    n = q3.shape[0]
    return pl.pallas_call(
        _attn_sample_kernel,
        out_shape=[_sds((n, N_HEADS, HEAD_DIM), BF16), _sds((n, WINDOW, KV_WIDTH), F32),
                   _sds((n, WINDOW, KV_WIDTH), F32)],
        compiler_params=pltpu.CompilerParams(vmem_limit_bytes=VMEM_LIMIT),
        name="attn_sample",
    )(sinks_col, q3, k_new3, v_new3, state_k, state_v)


def _shift_rows(cur, halo):
    rid = lax.broadcasted_iota(jnp.int32, cur.shape, 0)
    h0 = halo[0:1, :]
    h1 = halo[1:2, :]
    s1 = jnp.where(rid == 0, h1, pltpu.roll(cur, 1, 0))
    s2 = jnp.where(rid == 0, h0, jnp.where(rid == 1, h1, pltpu.roll(cur, 2, 0)))
    return s1, s2


def _conv3(cw, cur, s1, s2):
    return cw[0:1, :] * s2 + cw[1:2, :] * s1 + cw[2:3, :] * cur


def _conv_tile(i, tiles_per_seq, split, parts, rc, bm, halo_ref, st_ref, project, group_fn):
    gw = halo_ref.shape[1] // split
    gcols = [slice(grp * gw, (grp + 1) * gw) for grp in range(split)]

    @pl.when(i % tiles_per_seq == 0)
    def _():
        halo_ref[...] = jnp.zeros_like(halo_ref)

    halos = {}

    def chunk(rows, ride):
        last = rows.start + rows.size == bm
        full = None if last else project(rows, ride, slice(None))
        for grp in range(split):
            pcols = slice(parts * grp * gw, parts * (grp + 1) * gw)
            proj = project(rows, ride, pcols) if last else full[:, pcols]
            halo = halos.get(grp)
            if halo is None:
                halo = halo_ref[:, gcols[grp]]
            cur = group_fn(rows, ride, grp, proj, lambda cur, halo=halo: _shift_rows(cur, halo))
            halos[grp] = cur[cur.shape[0] - 2:, :]
        if last:
            for grp in range(split):
                halo_ref[:, gcols[grp]] = halos.pop(grp)

    _walk_chunks(i, bm, rc, chunk)

    @pl.when(i % tiles_per_seq == tiles_per_seq - 1)
    def _():
        st_ref[...] = halo_ref[...]


def _convmix_kernel(tiles_per_seq, split, rc, h_ref, hs_ref, wb_ref, wc_ref, wu_ref, cw_ref, p2_ref, p1_ref,
                    o_ref, st_ref, os_ref, pres_ref, wbf_ref, halo_ref):
    i = pl.program_id(1)
    gw = wb_ref.shape[1] // split

    @pl.when(i == 0)
    def _():
        for grp in range(split):
            src = slice(grp * gw, (grp + 1) * gw)
            for part, w_ref in enumerate((wb_ref, wc_ref, wu_ref)):
                wbf_ref[:, (3 * grp + part) * gw:(3 * grp + part + 1) * gw] = w_ref[:, src].astype(BF16)

    def project(rows, ride, pcols):
        return _dot(_ride(h_ref, rows, hs_ref, ride), wbf_ref[:, pcols])

    def group(rows, ride, grp, proj, shifts):
        n = rows.size
        cols = slice(grp * gw, (grp + 1) * gw)
        cb, cur = proj[:, 0:gw], proj[:, gw:2 * gw] * proj[:, 2 * gw:3 * gw]
        cw = cw_ref[:, cols]
        s1, s2 = shifts(cur[:n])
        o_ref[rows, cols] = (cb[:n] * _conv3(cw, cur[:n], s1, s2)).astype(BF16)
        if ride:
            pres_ref[:, cols] = cur[n:]
            os_ref[:, cols] = (cb[n:] * _conv3(cw, cur[n:], p1_ref[:, cols], p2_ref[:, cols])).astype(BF16)
        return cur[:n]

    _conv_tile(i, tiles_per_seq, split, 3, rc, h_ref.shape[0], halo_ref, st_ref, project, group)


def _convmix(h, hs, w_in, conv_w, prev2, prev1, *, bm, bn, split, rc, seq_len):
    m, ms = h.shape[0], hs.shape[0]
    tiles_per_seq = seq_len // bm
    tile = pl.BlockSpec((bm, bn), lambda j, i: (i, j))
    stile = pl.BlockSpec((ms, bn), lambda j, i: (0, j))
    wspec = lambda off: _wcol(D_MODEL, bn, lambda j, i: off + j * bn)
    return pl.pallas_call(
        functools.partial(_convmix_kernel, tiles_per_seq, split, rc),
        grid=(CONV_WIDTH // bn, m // bm),
        in_specs=[
            pl.BlockSpec((bm, D_MODEL), lambda j, i: (i, 0)),
            _resident(hs),
            wspec(OFF_CB), wspec(OFF_CC), wspec(OFF_CU),
            pl.BlockSpec((pl.Squeezed(), 3, bn), lambda j, i: (0, 0, j)),
            stile, stile,
        ],
        out_specs=[
            tile,
            pl.BlockSpec((pl.Squeezed(), 2, bn), lambda j, i: (i // tiles_per_seq, 0, j)),
            stile, stile,
        ],
        out_shape=[_sds((m, CONV_WIDTH), BF16), _sds((m // seq_len, 2, CONV_WIDTH), F32),
                   _sds((ms, CONV_WIDTH), BF16), _sds((ms, CONV_WIDTH), F32)],
        scratch_shapes=[pltpu.VMEM((D_MODEL, 3 * bn), BF16), pltpu.VMEM((2, bn), F32)],
        compiler_params=_params(2),
        name="convmix",
    )(h, hs, w_in, w_in, w_in, conv_w, prev2, prev1)


def _merge_kernel(split, rc, h_ref, a_ref, b_ref, hs_ref, as_ref, bs_ref, wg_ref, wa_ref, wb_ref, bg_ref,
                  o_ref, os_ref):
    gw = wa_ref.shape[1] // split

    def chunk(rows, ride):
        n = rows.size
        h = _ride(h_ref, rows, hs_ref, ride)
        a = _ride(a_ref, rows, as_ref, ride)
        b = _ride(b_ref, rows, bs_ref, ride)
        for grp in range(split):
            cols = slice(grp * gw, (grp + 1) * gw)
            gates = _dot(h, wg_ref[:, 2 * grp * gw:2 * (grp + 1) * gw])
            ga = jax.nn.sigmoid(gates[:, 0:gw] + bg_ref[0:1, cols])
            gb = jax.nn.sigmoid(gates[:, gw:2 * gw] + bg_ref[1:2, cols])
            merged = (ga * _dot(a, wa_ref[:, cols]) + gb * _dot(b, wb_ref[:, cols])).astype(BF16)
            o_ref[rows, cols] = merged[:n]
            if ride:
                os_ref[:, cols] = merged[n:]

    _walk_chunks(pl.program_id(1), h_ref.shape[0], rc, chunk)


def _merge(h, a, b, hs, a_s, b_s, w_gates, w_attn_branch, w_conv_branch, b_gate, *, bm, bn, split, rc):
    m, ms = h.shape[0], hs.shape[0]
    row = lambda j, i: (i, 0)
    return pl.pallas_call(
        functools.partial(_merge_kernel, split, rc),
        grid=(D_MODEL // bn, m // bm),
        in_specs=[
            pl.BlockSpec((bm, D_MODEL), row),
            pl.BlockSpec((bm, ATTN_WIDTH), row),
            pl.BlockSpec((bm, CONV_WIDTH), row),
            _resident(hs), _resident(a_s), _resident(b_s),
            pl.BlockSpec((D_MODEL, 2 * bn), lambda j, i: (0, j)),
            pl.BlockSpec((pl.Squeezed(), ATTN_WIDTH, bn), lambda j, i: (0, 0, j)),
            pl.BlockSpec((pl.Squeezed(), CONV_WIDTH, bn), lambda j, i: (0, 0, j)),
            pl.BlockSpec((pl.Squeezed(), 2, bn), lambda j, i: (0, 0, j)),
        ],
        out_specs=[pl.BlockSpec((bm, bn), lambda j, i: (i, j)),
                   pl.BlockSpec((ms, bn), lambda j, i: (0, j))],
        out_shape=[_sds((m, D_MODEL), BF16), _sds((ms, D_MODEL), BF16)],
        compiler_params=_params(2),
        name="merge",
    )(h, a, b, hs, a_s, b_s, w_gates, w_attn_branch, w_conv_branch, b_gate)


def _oproj_kernel(rc, mg_ref, x_ref, mgs_ref, xs_ref, wo_ref, gpost_ref, gpre_ref,
                  x1_ref, h2_ref, x1s_ref, h2s_ref):
    def chunk(rows, ride):
        n = rows.size
        y = _dot(_ride(mg_ref, rows, mgs_ref, ride), wo_ref[...])
        x1 = _ride(x_ref, rows, xs_ref, ride) + _rms(y, gpost_ref[...])
        h2 = _rms(x1, gpre_ref[...]).astype(BF16)
        x1_ref[rows, :] = x1[:n]
        h2_ref[rows, :] = h2[:n]
        if ride:
            x1s_ref[...] = x1[n:]
            h2s_ref[...] = h2[n:]

    _walk_chunks(pl.program_id(0), x_ref.shape[0], rc, chunk)


def _oproj(merged, x, merged_s, xs, w_o, g_post, g_pre, *, bm, rc):
    m, ms = x.shape[0], xs.shape[0]
    tile = pl.BlockSpec((bm, D_MODEL), lambda i: (i, 0))
    return pl.pallas_call(
        functools.partial(_oproj_kernel, rc),
        grid=(m // bm,),
        in_specs=[
            tile, tile, _resident(merged_s), _resident(xs),
            pl.BlockSpec((pl.Squeezed(), D_MODEL, D_MODEL), lambda i: (0, 0, 0),
                         pipeline_mode=pl.Buffered(1)),
            _resident(g_post), _resident(g_pre),
        ],
        out_specs=[tile, tile, _resident(xs), _resident(merged_s)],
        out_shape=[_sds((m, D_MODEL), F32), _sds((m, D_MODEL), BF16),
                   _sds((ms, D_MODEL), F32), _sds((ms, D_MODEL), BF16)],
        compiler_params=_params(1),
        name="oproj_norm",
    )(merged, x, merged_s, xs, w_o, g_post, g_pre)


def _ffn_up_kernel(tiles_per_seq, split, rc, h_ref, hs_ref, wa_ref, wg_ref, cw_ref, cb_ref, p2_ref, p1_ref, wd_ref,
                   o_ref, st_ref, os_ref, ups_ref, wd_bf_ref, wbf_ref, halo_ref):
    i = pl.program_id(1)
    wd_bf_ref[...] = wd_ref[...].astype(BF16)
    gw = wa_ref.shape[1] // split

    @pl.when(i == 0)
    def _():
        for grp in range(split):
            src = slice(grp * gw, (grp + 1) * gw)
            wbf_ref[:, 2 * grp * gw:(2 * grp + 1) * gw] = wa_ref[:, src].astype(BF16)
            wbf_ref[:, (2 * grp + 1) * gw:(2 * grp + 2) * gw] = wg_ref[:, src].astype(BF16)

    def project(rows, ride, pcols):
        return _dot(_ride(h_ref, rows, hs_ref, ride), wbf_ref[:, pcols])

    def group(rows, ride, grp, proj, shifts):
        n = rows.size
        cols = slice(grp * gw, (grp + 1) * gw)
        cur, gate = proj[:, 0:gw], proj[:, gw:2 * gw]
        cw = cw_ref[:, cols]

        def gated(cur, s1, s2, gate):
            act = jax.nn.gelu(_conv3(cw, cur, s1, s2) + cb_ref[:, cols], approximate=True)
            return (act * gate).astype(BF16)

        s1, s2 = shifts(cur[:n])
        o_ref[rows, cols] = gated(cur[:n], s1, s2, gate[:n])
        if ride:
            ups_ref[:, cols] = cur[n:]
            os_ref[:, cols] = gated(cur[n:], p1_ref[:, cols], p2_ref[:, cols], gate[n:])
        return cur[:n]

    _conv_tile(i, tiles_per_seq, split, 2, rc, h_ref.shape[0], halo_ref, st_ref, project, group)


def _ffn_up(h2, h2s, w_ffn_up, ffn_conv_w, ffn_conv_b, prev2, prev1, w_ffn_down, *, bm, bn, split, rc, seq_len):
    m, ms = h2.shape[0], h2s.shape[0]
    tiles_per_seq = seq_len // bm
    n_tiles = D_FF // bn
    n_m = m // bm
    slab = D_FF // (n_tiles * n_m)
    wslab = pl.BlockSpec((pl.Squeezed(), slab, D_MODEL), lambda j, i: (0, j * n_m + i, 0))
    tile = pl.BlockSpec((bm, bn), lambda j, i: (i, j))
    stile = pl.BlockSpec((ms, bn), lambda j, i: (0, j))
    return pl.pallas_call(
        functools.partial(_ffn_up_kernel, tiles_per_seq, split, rc),
        grid=(n_tiles, m // bm),
        in_specs=[
            pl.BlockSpec((bm, D_MODEL), lambda j, i: (i, 0)),
            _resident(h2s),
            pl.BlockSpec((pl.Squeezed(), D_MODEL, bn), lambda j, i: (0, 0, j)),
            pl.BlockSpec((pl.Squeezed(), D_MODEL, bn), lambda j, i: (0, 0, n_tiles + j)),
            pl.BlockSpec((pl.Squeezed(), 3, bn), lambda j, i: (0, 0, j)),
            pl.BlockSpec((1, bn), lambda j, i: (0, j)),
            stile, stile, wslab,
        ],
        out_specs=[
            tile,
            pl.BlockSpec((pl.Squeezed(), 2, bn), lambda j, i: (i // tiles_per_seq, 0, j)),
            stile, stile, wslab,
        ],
        out_shape=[_sds((m, D_FF), BF16), _sds((m // seq_len, 2, D_FF), F32),
                   _sds((ms, D_FF), BF16), _sds((ms, D_FF), F32), _sds(w_ffn_down.shape, BF16)],
        scratch_shapes=[pltpu.VMEM((D_MODEL, 2 * bn), BF16), pltpu.VMEM((2, bn), F32)],
        compiler_params=_params(2),
        name="ffn_up",
    )(h2, h2s, w_ffn_up, w_ffn_up, ffn_conv_w, ffn_conv_b, prev2, prev1, w_ffn_down)


def _ffn_down_kernel(rc, a_ref, as_ref, w_ref, o_ref, os_ref):
    def chunk(rows, ride):
        n = rows.size
        o = _dot(_ride(a_ref, rows, as_ref, ride), w_ref[...]).astype(BF16)
        o_ref[rows, :] = o[:n]
        if ride:
            os_ref[...] = o[n:]

    _walk_chunks(pl.program_id(1), a_ref.shape[0], rc, chunk)


def _ffn_down(act, act_s, w_ffn_down, *, bm, bn, rc):
    m, ms = act.shape[0], act_s.shape[0]
    return pl.pallas_call(
        functools.partial(_ffn_down_kernel, rc),
        grid=(D_MODEL // bn, m // bm),
        in_specs=[
            pl.BlockSpec((bm, D_FF), lambda j, i: (i, 0)),
            _resident(act_s),
            pl.BlockSpec((pl.Squeezed(), D_FF, bn), lambda j, i: (0, 0, j)),
        ],
        out_specs=[pl.BlockSpec((bm, bn), lambda j, i: (i, j)),
                   pl.BlockSpec((ms, bn), lambda j, i: (0, j))],
        out_shape=[_sds((m, D_MODEL), BF16), _sds((ms, D_MODEL), BF16)],
        compiler_params=_params(2),
        name="ffn_down",
    )(act, act_s, w_ffn_down)


def _final_kernel(rc, f_ref, x1_ref, p_ref, fs_ref, x1s_ref, ps_ref, g_ref, wg_ref, wp_ref, y_ref, ys_ref):
    def chunk(rows, ride):
        n = rows.size
        f = _ride(f_ref, rows, fs_ref, ride).astype(F32)
        x2 = _ride(x1_ref, rows, x1s_ref, ride) + _rms(f, g_ref[...])
        gate = jax.nn.sigmoid(_dot(x2.astype(BF16), wg_ref[...]))
        y = x2 + gate * _dot(_ride(p_ref, rows, ps_ref, ride).astype(BF16), wp_ref[...])
        y_ref[rows, :] = y[:n]
        if ride:
            ys_ref[...] = y[n:]

    _walk_chunks(pl.program_id(0), x1_ref.shape[0], rc, chunk)


def _final(f, x1, p, f_s, x1_s, p_s, g_post, w_ple_gate, w_ple, *, bm, rc):
    m, ms = x1.shape[0], x1_s.shape[0]
    row = lambda i: (i, 0)
    return pl.pallas_call(
        functools.partial(_final_kernel, rc),
        grid=(m // bm,),
        in_specs=[
            pl.BlockSpec((bm, D_MODEL), row),
            pl.BlockSpec((bm, D_MODEL), row),
            pl.BlockSpec((bm, PLE_DIM), row),
            _resident(f_s), _resident(x1_s), _resident(p_s),
            _resident(g_post),
            pl.BlockSpec((pl.Squeezed(), D_MODEL, D_MODEL), lambda i: (0, 0, 0),
                         pipeline_mode=pl.Buffered(1)),
            pl.BlockSpec((pl.Squeezed(), PLE_DIM, D_MODEL), lambda i: (0, 0, 0),
                         pipeline_mode=pl.Buffered(1)),
        ],
        out_specs=[pl.BlockSpec((bm, D_MODEL), row), _resident(x1_s)],
        out_shape=[_sds((m, D_MODEL), F32), _sds((ms, D_MODEL), F32)],
        compiler_params=_params(1),
        name="ffn_norm_ple",
    )(f, x1, p, f_s, x1_s, p_s, g_post, w_ple_gate, w_ple)


def _rope_tables(pos):
    half = HEAD_DIM // 2
    inv = ROPE_THETA ** (-np.arange(half, dtype=np.float64) / half)
    ang = np.asarray(pos, dtype=np.float64)[:, None] * inv[None, :]
    cos, sin = np.cos(ang), np.sin(ang)
    reps = LANES // HEAD_DIM
    return (jnp.asarray(np.tile(np.concatenate([cos, cos], axis=1), (1, reps)), dtype=F32),
            jnp.asarray(np.tile(np.concatenate([-sin, sin], axis=1), (1, reps)), dtype=F32))


TILES = dict(qkv=dict(bm=1024, rc=256), attn=128, convmix=dict(bm=1024, bn=512, split=2, rc=1024),
             merge=dict(bm=1024, bn=512, split=2, rc=1024), oproj=dict(bm=512, rc=128),
             ffn_up=dict(bm=2048, bn=512, split=1, rc=512), ffn_down=dict(bm=1024, bn=512, rc=1024),
             final=dict(bm=512, rc=256))


def kernel(x_prompt, x_sample, state_attn_k, state_attn_v, state_conv, state_ffn_conv, p_prompt, p_sample,
           norm_mix_pre, norm_mix_post, norm_ffn_pre, norm_ffn_post, w_in, attn_sinks, conv_w, w_attn_branch,
           w_conv_branch, b_gate, w_o, w_ffn_up, ffn_conv_w, ffn_conv_b, w_ffn_down, w_ple, w_ple_gate):
    assert w_in.shape[0] == 1, "weights are indexed with a squeezed depth axis of size 1"
    batch, seq, _ = x_prompt.shape
    ns, dec_seq, _ = x_sample.shape
    assert dec_seq == 1
    m = batch * seq
    x = x_prompt.reshape(m, D_MODEL)
    xs = x_sample.reshape(ns, D_MODEL)
    rope_p = _rope_tables(np.arange(seq))
    rope_s = _rope_tables(np.full((ns,), PAST_LEN))

    h, q, k, v, hs, qs, ks, vs = _qkv(x, xs, norm_mix_pre, w_in, rope_p, rope_s, **TILES["qkv"])

    merge_tiles = TILES["merge"]
    a, w_o_bf, w_ple_gate_bf, w_ple_bf, w_gates_bf, w_attn_branch_bf, w_conv_branch_bf = _attn_prompt(
        attn_sinks.reshape(N_HEADS), q, k, v, w_o, w_ple_gate, w_ple, w_in, w_attn_branch, w_conv_branch,
        n_seq=batch, seq_len=seq, bq=TILES["attn"], gate_group=merge_tiles["bn"] // merge_tiles["split"])
    a3, new_ks, new_vs = _attn_sample(
        attn_sinks.reshape(N_HEADS, 1), qs.reshape(ns, N_HEADS, HEAD_DIM), ks.reshape(ns, 1, KV_WIDTH),
        vs.reshape(ns, 1, KV_WIDTH), state_attn_k.reshape(ns, WINDOW, KV_WIDTH),
        state_attn_v.reshape(ns, WINDOW, KV_WIDTH))
    a_s = a3.reshape(ns, ATTN_WIDTH)

    b, new_conv_p, b_s, pre_s = _convmix(h, hs, w_in, conv_w, state_conv[0, :, 0], state_conv[0, :, 1],
                                         seq_len=seq, **TILES["convmix"])
    merged, merged_s = _merge(h, a, b, hs, a_s, b_s, w_gates_bf, w_attn_branch_bf, w_conv_branch_bf, b_gate,
                              **merge_tiles)
    x1, h2, x1_s, h2_s = _oproj(merged, x, merged_s, xs, w_o_bf, norm_mix_post, norm_ffn_pre,
                                **TILES["oproj"])
    act, new_ffn_p, act_s, up_s, w_ffn_down_bf = _ffn_up(
        h2, h2_s, w_ffn_up, ffn_conv_w, ffn_conv_b, state_ffn_conv[0, :, 0], state_ffn_conv[0, :, 1], w_ffn_down,
        seq_len=seq, **TILES["ffn_up"])
    f, f_s = _ffn_down(act, act_s, w_ffn_down_bf, **TILES["ffn_down"])
    y, y_s = _final(f, x1, p_prompt.reshape(m, PLE_DIM), f_s, x1_s, p_sample.reshape(ns, PLE_DIM),
                    norm_ffn_post, w_ple_gate_bf, w_ple_bf, **TILES["final"])

    kv_tail = lambda t: t.reshape(batch, seq, KV_WIDTH)[:, seq - WINDOW:].reshape(batch, WINDOW, N_KV_HEADS, HEAD_DIM)
    kv_state = lambda t: t.reshape(ns, WINDOW, N_KV_HEADS, HEAD_DIM)
    return (y.reshape(batch, seq, D_MODEL), y_s.reshape(ns, dec_seq, D_MODEL),
            kv_tail(k)[None], kv_tail(v)[None], new_conv_p[None], new_ffn_p[None],
            kv_state(new_ks)[None], kv_state(new_vs)[None],
            jnp.stack([state_conv[0, :, 1], pre_s], axis=1)[None],
            jnp.stack([state_ffn_conv[0, :, 1], up_s], axis=1)[None])
```

```python
import functools

import jax
import jax.numpy as jnp
import numpy as np
from jax import lax
from jax.experimental import pallas as pl
from jax.experimental.pallas import tpu as pltpu

D_MODEL = 2048
N_HEADS = 16
N_KV_HEADS = 2
HEAD_DIM = 64
GROUP = N_HEADS // N_KV_HEADS
ATTN_WIDTH = N_HEADS * HEAD_DIM
KV_WIDTH = N_KV_HEADS * HEAD_DIM
QKV_WIDTH = ATTN_WIDTH + 2 * KV_WIDTH
WINDOW = 128
PAST_LEN = 16384
ROPE_THETA = 10000.0
CONV_WIDTH = D_MODEL // 2
D_FF = 5632
PLE_DIM = 256
RMS_EPS = 1e-6
OFF_CB = QKV_WIDTH
OFF_CC = OFF_CB + CONV_WIDTH
OFF_CU = OFF_CC + CONV_WIDTH
OFF_GA = OFF_CU + CONV_WIDTH
OFF_GB = OFF_GA + D_MODEL

LANES = 128
UNIT_HEADS = GROUP
MASKED = -1e30
VMEM_LIMIT = 56 * 1024 * 1024

BF16 = jnp.bfloat16
F32 = jnp.float32


def _params(n_axes):
    return pltpu.CompilerParams(dimension_semantics=("arbitrary",) * n_axes,
                                vmem_limit_bytes=VMEM_LIMIT)


def _rms(x, g):
    return x * lax.rsqrt(jnp.mean(x * x, axis=-1, keepdims=True) + RMS_EPS) * g


def _dot(a, b):
    return jnp.dot(a, b, preferred_element_type=F32)


def _walk_chunks(i, bm, rc, chunk_fn):
    rc = min(bm, rc)

    def walk(first):
        for c in range(bm // rc):
            chunk_fn(pl.ds(c * rc, rc), first and c == 0)

    @pl.when(i == 0)
    def _():
        walk(True)

    @pl.when(i > 0)
    def _():
        walk(False)


def _ride(ref, rows, sample_ref, ride):
    x = ref[rows, :]
    return jnp.concatenate([x, sample_ref[...]], axis=0) if ride else x


def _cast_gate_slab(src_ref, dst_ref, group):
    for t in range(D_MODEL // group):
        src = slice(t * group, (t + 1) * group)
        dst_ref[:, 2 * t * group:(2 * t + 1) * group] = src_ref[:, src].astype(BF16)
        dst_ref[:, (2 * t + 1) * group:(2 * t + 2) * group] = (
            src_ref[:, D_MODEL + t * group:D_MODEL + (t + 1) * group].astype(BF16))


def _gate_slab_specs(slab, step_fn):
    gate_in = pl.BlockSpec((pl.Squeezed(), pl.Element(slab), pl.Element(2 * D_MODEL)),
                           lambda *g: (0, pl.multiple_of(step_fn(*g) * slab, slab), OFF_GA))
    gate_out = pl.BlockSpec((slab, 2 * D_MODEL), lambda *g: (step_fn(*g), 0))
    return gate_in, gate_out


def _wcol(k_rows, block_cols, col_fn):
    return pl.BlockSpec((pl.Squeezed(), pl.Element(k_rows), pl.Element(block_cols)),
                        lambda *g: (0, 0, pl.multiple_of(col_fn(*g), LANES)))


def _resident(arr):
    zeros = (0,) * arr.ndim
    return pl.BlockSpec(arr.shape, lambda *g: zeros)


def _sds(shape, dtype):
    return jax.ShapeDtypeStruct(shape, dtype)


def _qkv_kernel(rc, x_ref, xs_ref, g_ref, w_ref, cos_ref, sin_ref, coss_ref, sins_ref,
                h_ref, q_ref, k_ref, v_ref, hs_ref, qs_ref, ks_ref, vs_ref, wbf_ref):
    i = pl.program_id(0)
    n_q = ATTN_WIDTH // LANES

    @pl.when(i == 0)
    def _():
        wbf_ref[...] = w_ref[...].astype(BF16)

    def chunk(rows, ride):
        n = rows.size

        def put(ref, sample_ref, cols, val):
            ref[rows, cols] = val[:n]
            if ride:
                sample_ref[:, cols] = val[n:]

        h = _rms(_ride(x_ref, rows, xs_ref, ride), g_ref[...]).astype(BF16)
        put(h_ref, hs_ref, slice(None), h)
        proj = _dot(h, wbf_ref[...])
        cos = _ride(cos_ref, rows, coss_ref, ride)
        sin = _ride(sin_ref, rows, sins_ref, ride)
        lane = lax.broadcasted_iota(jnp.int32, cos.shape, 1)
        first_half = (lane % HEAD_DIM) < (HEAD_DIM // 2)
        for c in range(n_q + 1):
            cols = slice(c * LANES, (c + 1) * LANES)
            xg = proj[:, cols]
            partner = jnp.where(first_half, pltpu.roll(xg, LANES - HEAD_DIM // 2, 1),
                                pltpu.roll(xg, HEAD_DIM // 2, 1))
            yg = xg * cos + partner * sin
            if c < n_q:
                put(q_ref, qs_ref, cols, (yg * (HEAD_DIM ** -0.5)).astype(BF16))
            else:
                put(k_ref, ks_ref, slice(None), yg)
        put(v_ref, vs_ref, slice(None), proj[:, ATTN_WIDTH + KV_WIDTH:QKV_WIDTH])

    _walk_chunks(i, x_ref.shape[0], rc, chunk)


def _qkv(x, xs, g, w_in, rope_p, rope_s, *, bm, rc):
    m, ms = x.shape[0], xs.shape[0]
    t_tiles = rope_p[0].shape[0] // bm
    row = lambda i: (i, 0)
    pos = lambda i: (i % t_tiles, 0)
    outs = lambda rows: [_sds((rows, D_MODEL), BF16), _sds((rows, ATTN_WIDTH), BF16),
                         _sds((rows, KV_WIDTH), F32), _sds((rows, KV_WIDTH), F32)]
    out_p, out_s = outs(m), outs(ms)
    return pl.pallas_call(
        functools.partial(_qkv_kernel, rc),
        grid=(m // bm,),
        in_specs=[
            pl.BlockSpec((bm, D_MODEL), row),
            _resident(xs),
            _resident(g),
            pl.BlockSpec((pl.Squeezed(), D_MODEL, QKV_WIDTH), lambda i: (0, 0, 0),
                         pipeline_mode=pl.Buffered(1)),
            pl.BlockSpec((bm, LANES), pos),
            pl.BlockSpec((bm, LANES), pos),
            _resident(rope_s[0]),
            _resident(rope_s[1]),
        ],
        out_specs=[pl.BlockSpec((bm, s.shape[1]), row) for s in out_p] + [_resident(s) for s in out_s],
        out_shape=out_p + out_s,
        scratch_shapes=[pltpu.VMEM((D_MODEL, QKV_WIDTH), BF16)],
        compiler_params=_params(1),
        name="qkv_rope",
    )(x, xs, g, w_in, rope_p[0], rope_p[1], rope_s[0], rope_s[1])


def _attn_prompt_kernel(sinks_ref, q_ref, kc_ref, kp_ref, vc_ref, vp_ref, wo_ref, wg_ref, wp_ref, wa_ref, wb_ref,
                        o_ref, wo_bf_ref, wg_bf_ref, wp_bf_ref, wa_bf_ref, wb_bf_ref):
    wo_bf_ref[...] = wo_ref[...].astype(BF16)
    wg_bf_ref[...] = wg_ref[...].astype(BF16)
    wa_bf_ref[...] = wa_ref[...].astype(BF16)
    wb_bf_ref[...] = wb_ref[...].astype(BF16)

    @pl.when((pl.program_id(0) == 0) & (pl.program_id(1) == 0))
    def _():
        wp_bf_ref[...] = wp_ref[...].astype(BF16)

    n_sub = q_ref.shape[0] // WINDOW
    qi = lax.broadcasted_iota(jnp.int32, (WINDOW, WINDOW), 0)
    ci = lax.broadcasted_iota(jnp.int32, (WINDOW, WINDOW), 1)
    own = ci <= qi
    first_valid = own | (pl.program_id(1) > 0)
    k_all = jnp.concatenate([kp_ref[...], kc_ref[...]], axis=0).astype(BF16)
    v_all = jnp.concatenate([vp_ref[...], vc_ref[...]], axis=0).astype(BF16)
    units = [(sub, hg) for sub in range(n_sub) for hg in range(N_HEADS // UNIT_HEADS)]
    heads_of = lambda hg: range(hg * UNIT_HEADS, (hg + 1) * UNIT_HEADS)
    q_rows = lambda sub: slice(sub * WINDOW, (sub + 1) * WINDOW)
    kv_rows = lambda sub: slice(sub * WINDOW, (sub + 2) * WINDOW)
    kv_cols = lambda hg: slice((hg * UNIT_HEADS // GROUP) * HEAD_DIM, (hg * UNIT_HEADS // GROUP + 1) * HEAD_DIM)

    scores = []
    for sub, kh in units:
        qs = jnp.concatenate([q_ref[q_rows(sub), h * HEAD_DIM:(h + 1) * HEAD_DIM] for h in heads_of(kh)], axis=0)
        scores.append(lax.dot_general(qs, k_all[kv_rows(sub), kv_cols(kh)], (((1,), (1,)), ((), ())),
                                      preferred_element_type=F32))
    probs, dens = [], []
    for (sub, kh), s_all in zip(units, scores):
        ps, ds = [], []
        for g, h in enumerate(heads_of(kh)):
            s2 = s_all[g * WINDOW:(g + 1) * WINDOW, :]
            s = jnp.where(own, s2[:, WINDOW:], s2[:, :WINDOW])
            if sub == 0:
                s = jnp.where(first_valid, s, MASKED)
            sink = sinks_ref[h]
            m = jnp.maximum(jnp.max(s, axis=-1, keepdims=True), sink)
            p = jnp.exp(s - m)
            ds.append(jnp.sum(p, axis=-1, keepdims=True) + jnp.exp(sink - m))
            ps.append(jnp.concatenate([jnp.where(own, 0.0, p), jnp.where(own, p, 0.0)], axis=1).astype(BF16))
        probs.append(jnp.concatenate(ps, axis=0))
        dens.append(ds)
    outs = [_dot(p_all, v_all[kv_rows(sub), kv_cols(kh)]) for (sub, kh), p_all in zip(units, probs)]
    for (sub, kh), o_all, ds in zip(units, outs, dens):
        for g, h in enumerate(heads_of(kh)):
            o = o_all[g * WINDOW:(g + 1) * WINDOW, :] / ds[g]
            o_ref[q_rows(sub), h * HEAD_DIM:(h + 1) * HEAD_DIM] = o.astype(BF16)


def _attn_prompt(sinks, q, k, v, w_o, w_ple_gate, w_ple, w_attn_branch, w_conv_branch, *, n_seq, seq_len, bq):
    nb = seq_len // bq
    sub = bq // WINDOW
    steps = n_seq * nb
    slab, bslab = D_MODEL // steps, ATTN_WIDTH // steps
    cur = lambda n, b: (n * nb + b, 0)
    prev = lambda n, b: (jnp.maximum((n * nb + b) * sub - 1, 0), 0)
    wslab = pl.BlockSpec((pl.Squeezed(), slab, D_MODEL), lambda n, b: (0, n * nb + b, 0))
    bspec = pl.BlockSpec((pl.Squeezed(), bslab, D_MODEL), lambda n, b: (0, n * nb + b, 0))
    return pl.pallas_call(
        _attn_prompt_kernel,
        grid=(n_seq, nb),
        in_specs=[
            pl.BlockSpec(memory_space=pltpu.SMEM),
            pl.BlockSpec((bq, ATTN_WIDTH), cur),
            pl.BlockSpec((bq, KV_WIDTH), cur),
            pl.BlockSpec((WINDOW, KV_WIDTH), prev),
            pl.BlockSpec((bq, KV_WIDTH), cur),
            pl.BlockSpec((WINDOW, KV_WIDTH), prev),
            wslab, wslab, _resident(w_ple), bspec, bspec,
        ],
        out_specs=[pl.BlockSpec((bq, ATTN_WIDTH), cur), wslab, wslab, _resident(w_ple), bspec, bspec],
        out_shape=[_sds((n_seq * seq_len, ATTN_WIDTH), BF16), _sds(w_o.shape, BF16), _sds(w_ple_gate.shape, BF16),
                   _sds(w_ple.shape, BF16), _sds(w_attn_branch.shape, BF16), _sds(w_conv_branch.shape, BF16)],
        compiler_params=_params(2),
        name="attn_prompt",
    )(sinks, q, k, k, v, v, w_o, w_ple_gate, w_ple, w_attn_branch, w_conv_branch)


def _attn_sample_kernel(sinks_ref, q_ref, kn_ref, vn_ref, sk_ref, sv_ref, a_ref, nk_ref, nv_ref):
    nk_ref[:, 0:WINDOW - 1, :] = sk_ref[:, 1:WINDOW, :]
    nk_ref[:, WINDOW - 1:WINDOW, :] = kn_ref[...]
    nv_ref[:, 0:WINDOW - 1, :] = sv_ref[:, 1:WINDOW, :]
    nv_ref[:, WINDOW - 1:WINDOW, :] = vn_ref[...]
    for kh in range(N_KV_HEADS):
        cols = slice(kh * HEAD_DIM, (kh + 1) * HEAD_DIM)
        heads = slice(kh * GROUP, (kh + 1) * GROUP)
        k = nk_ref[:, :, cols].astype(BF16)
        v = nv_ref[:, :, cols].astype(BF16)
        s = jnp.einsum("ngd,ncd->ngc", q_ref[:, heads, :], k, preferred_element_type=F32)
        sink = sinks_ref[heads, :][None]
        m = jnp.maximum(jnp.max(s, axis=-1, keepdims=True), sink)
        p = jnp.exp(s - m)
        den = jnp.sum(p, axis=-1, keepdims=True) + jnp.exp(sink - m)
        o = jnp.einsum("ngc,ncd->ngd", p.astype(BF16), v, preferred_element_type=F32) / den
        a_ref[:, heads, :] = o.astype(BF16)


def _attn_sample(sinks_col, q3, k_new3, v_new3, state_k, state_v):
    n = q3.shape[0]
    return pl.pallas_call(
        _attn_sample_kernel,
        out_shape=[_sds((n, N_HEADS, HEAD_DIM), BF16), _sds((n, WINDOW, KV_WIDTH), F32),
                   _sds((n, WINDOW, KV_WIDTH), F32)],
        compiler_params=pltpu.CompilerParams(vmem_limit_bytes=VMEM_LIMIT),
        name="attn_sample",
    )(sinks_col, q3, k_new3, v_new3, state_k, state_v)


def _shift_rows(cur, halo):
    rid = lax.broadcasted_iota(jnp.int32, cur.shape, 0)
    h0 = halo[0:1, :]
    h1 = halo[1:2, :]
    s1 = jnp.where(rid == 0, h1, pltpu.roll(cur, 1, 0))
    s2 = jnp.where(rid == 0, h0, jnp.where(rid == 1, h1, pltpu.roll(cur, 2, 0)))
    return s1, s2


def _conv3(cw, cur, s1, s2):
    return cw[0:1, :] * s2 + cw[1:2, :] * s1 + cw[2:3, :] * cur


def _conv_tile(i, tiles_per_seq, split, parts, rc, bm, halo_ref, st_ref, project, group_fn):
    gw = halo_ref.shape[1] // split
    gcols = [slice(grp * gw, (grp + 1) * gw) for grp in range(split)]

    @pl.when(i % tiles_per_seq == 0)
    def _():
        halo_ref[...] = jnp.zeros_like(halo_ref)

    halos = {}

    def chunk(rows, ride):
        last = rows.start + rows.size == bm
        full = None if last else project(rows, ride, slice(None))
        for grp in range(split):
            pcols = slice(parts * grp * gw, parts * (grp + 1) * gw)
            proj = project(rows, ride, pcols) if last else full[:, pcols]
            halo = halos.get(grp)
            if halo is None:
                halo = halo_ref[:, gcols[grp]]
            cur = group_fn(rows, ride, grp, proj, lambda cur, halo=halo: _shift_rows(cur, halo))
            halos[grp] = cur[cur.shape[0] - 2:, :]
        if last:
            for grp in range(split):
                halo_ref[:, gcols[grp]] = halos.pop(grp)

    _walk_chunks(i, bm, rc, chunk)

    @pl.when(i % tiles_per_seq == tiles_per_seq - 1)
    def _():
        st_ref[...] = halo_ref[...]


def _convmix_kernel(tiles_per_seq, split, rc, gate_group, h_ref, hs_ref, wb_ref, wc_ref, wu_ref, cw_ref, p2_ref, p1_ref,
                    wgate_ref, o_ref, st_ref, os_ref, pres_ref, wgate_bf_ref, wbf_ref, halo_ref):
    i = pl.program_id(1)
    _cast_gate_slab(wgate_ref, wgate_bf_ref, gate_group)
    gw = wb_ref.shape[1] // split

    @pl.when(i == 0)
    def _():
        for grp in range(split):
            src = slice(grp * gw, (grp + 1) * gw)
            for part, w_ref in enumerate((wb_ref, wc_ref, wu_ref)):
                wbf_ref[:, (3 * grp + part) * gw:(3 * grp + part + 1) * gw] = w_ref[:, src].astype(BF16)

    def project(rows, ride, pcols):
        return _dot(_ride(h_ref, rows, hs_ref, ride), wbf_ref[:, pcols])

    def group(rows, ride, grp, proj, shifts):
        n = rows.size
        cols = slice(grp * gw, (grp + 1) * gw)
        cb, cur = proj[:, 0:gw], proj[:, gw:2 * gw] * proj[:, 2 * gw:3 * gw]
        cw = cw_ref[:, cols]
        s1, s2 = shifts(cur[:n])
        o_ref[rows, cols] = (cb[:n] * _conv3(cw, cur[:n], s1, s2)).astype(BF16)
        if ride:
            pres_ref[:, cols] = cur[n:]
            os_ref[:, cols] = (cb[n:] * _conv3(cw, cur[n:], p1_ref[:, cols], p2_ref[:, cols])).astype(BF16)
        return cur[:n]

    _conv_tile(i, tiles_per_seq, split, 3, rc, h_ref.shape[0], halo_ref, st_ref, project, group)


def _convmix(h, hs, w_in, conv_w, prev2, prev1, *, bm, bn, split, rc, seq_len, gate_group):
    m, ms = h.shape[0], hs.shape[0]
    tiles_per_seq = seq_len // bm
    n_m = m // bm
    gate_in, gate_out = _gate_slab_specs(D_MODEL // (CONV_WIDTH // bn * n_m), lambda j, i: j * n_m + i)
    tile = pl.BlockSpec((bm, bn), lambda j, i: (i, j))
    stile = pl.BlockSpec((ms, bn), lambda j, i: (0, j))
    wspec = lambda off: _wcol(D_MODEL, bn, lambda j, i: off + j * bn)
    return pl.pallas_call(
        functools.partial(_convmix_kernel, tiles_per_seq, split, rc, gate_group),
        grid=(CONV_WIDTH // bn, n_m),
        in_specs=[
            pl.BlockSpec((bm, D_MODEL), lambda j, i: (i, 0)),
            _resident(hs),
            wspec(OFF_CB), wspec(OFF_CC), wspec(OFF_CU),
            pl.BlockSpec((pl.Squeezed(), 3, bn), lambda j, i: (0, 0, j)),
            stile, stile, gate_in,
        ],
        out_specs=[
            tile,
            pl.BlockSpec((pl.Squeezed(), 2, bn), lambda j, i: (i // tiles_per_seq, 0, j)),
            stile, stile, gate_out,
        ],
        out_shape=[_sds((m, CONV_WIDTH), BF16), _sds((m // seq_len, 2, CONV_WIDTH), F32),
                   _sds((ms, CONV_WIDTH), BF16), _sds((ms, CONV_WIDTH), F32), _sds((D_MODEL, 2 * D_MODEL), BF16)],
        scratch_shapes=[pltpu.VMEM((D_MODEL, 3 * bn), BF16), pltpu.VMEM((2, bn), F32)],
        compiler_params=_params(2),
        name="convmix",
    )(h, hs, w_in, w_in, w_in, conv_w, prev2, prev1, w_in)


def _merge_kernel(split, rc, h_ref, a_ref, b_ref, hs_ref, as_ref, bs_ref, wg_ref, wa_ref, wb_ref, bg_ref,
                  o_ref, os_ref):
    gw = wa_ref.shape[1] // split

    def chunk(rows, ride):
        n = rows.size
        h = _ride(h_ref, rows, hs_ref, ride)
        a = _ride(a_ref, rows, as_ref, ride)
        b = _ride(b_ref, rows, bs_ref, ride)
        for grp in range(split):
            cols = slice(grp * gw, (grp + 1) * gw)
            gates = _dot(h, wg_ref[:, 2 * grp * gw:2 * (grp + 1) * gw])
            ga = jax.nn.sigmoid(gates[:, 0:gw] + bg_ref[0:1, cols])
            gb = jax.nn.sigmoid(gates[:, gw:2 * gw] + bg_ref[1:2, cols])
            merged = (ga * _dot(a, wa_ref[:, cols]) + gb * _dot(b, wb_ref[:, cols])).astype(BF16)
            o_ref[rows, cols] = merged[:n]
            if ride:
                os_ref[:, cols] = merged[n:]

    _walk_chunks(pl.program_id(1), h_ref.shape[0], rc, chunk)


def _merge(h, a, b, hs, a_s, b_s, w_gates, w_attn_branch, w_conv_branch, b_gate, *, bm, bn, split, rc):
    m, ms = h.shape[0], hs.shape[0]
    row = lambda j, i: (i, 0)
    return pl.pallas_call(
        functools.partial(_merge_kernel, split, rc),
        grid=(D_MODEL // bn, m // bm),
        in_specs=[
            pl.BlockSpec((bm, D_MODEL), row),
            pl.BlockSpec((bm, ATTN_WIDTH), row),
            pl.BlockSpec((bm, CONV_WIDTH), row),
            _resident(hs), _resident(a_s), _resident(b_s),
            pl.BlockSpec((D_MODEL, 2 * bn), lambda j, i: (0, j)),
            pl.BlockSpec((pl.Squeezed(), ATTN_WIDTH, bn), lambda j, i: (0, 0, j)),
            pl.BlockSpec((pl.Squeezed(), CONV_WIDTH, bn), lambda j, i: (0, 0, j)),
            pl.BlockSpec((pl.Squeezed(), 2, bn), lambda j, i: (0, 0, j)),
        ],
        out_specs=[pl.BlockSpec((bm, bn), lambda j, i: (i, j)),
                   pl.BlockSpec((ms, bn), lambda j, i: (0, j))],
        out_shape=[_sds((m, D_MODEL), BF16), _sds((ms, D_MODEL), BF16)],
        compiler_params=_params(2),
        name="merge",
    )(h, a, b, hs, a_s, b_s, w_gates, w_attn_branch, w_conv_branch, b_gate)


def _oproj_kernel(rc, mg_ref, x_ref, mgs_ref, xs_ref, wo_ref, gpost_ref, gpre_ref,
                  x1_ref, h2_ref, x1s_ref, h2s_ref):
    def chunk(rows, ride):
        n = rows.size
        y = _dot(_ride(mg_ref, rows, mgs_ref, ride), wo_ref[...])
        x1 = _ride(x_ref, rows, xs_ref, ride) + _rms(y, gpost_ref[...])
        h2 = _rms(x1, gpre_ref[...]).astype(BF16)
        x1_ref[rows, :] = x1[:n]
        h2_ref[rows, :] = h2[:n]
        if ride:
            x1s_ref[...] = x1[n:]
            h2s_ref[...] = h2[n:]

    _walk_chunks(pl.program_id(0), x_ref.shape[0], rc, chunk)


def _oproj(merged, x, merged_s, xs, w_o, g_post, g_pre, *, bm, rc):
    m, ms = x.shape[0], xs.shape[0]
    tile = pl.BlockSpec((bm, D_MODEL), lambda i: (i, 0))
    return pl.pallas_call(
        functools.partial(_oproj_kernel, rc),
        grid=(m // bm,),
        in_specs=[
            tile, tile, _resident(merged_s), _resident(xs),
            pl.BlockSpec((pl.Squeezed(), D_MODEL, D_MODEL), lambda i: (0, 0, 0),
                         pipeline_mode=pl.Buffered(1)),
            _resident(g_post), _resident(g_pre),
        ],
        out_specs=[tile, tile, _resident(xs), _resident(merged_s)],
        out_shape=[_sds((m, D_MODEL), F32), _sds((m, D_MODEL), BF16),
                   _sds((ms, D_MODEL), F32), _sds((ms, D_MODEL), BF16)],
        compiler_params=_params(1),
        name="oproj_norm",
    )(merged, x, merged_s, xs, w_o, g_post, g_pre)


def _ffn_up_kernel(tiles_per_seq, split, rc, h_ref, hs_ref, wa_ref, wg_ref, cw_ref, cb_ref, p2_ref, p1_ref, wd_ref,
                   o_ref, st_ref, os_ref, ups_ref, wd_bf_ref, wbf_ref, halo_ref):
    i = pl.program_id(1)
    wd_bf_ref[...] = wd_ref[...].astype(BF16)
    gw = wa_ref.shape[1] // split

    @pl.when(i == 0)
    def _():
        for grp in range(split):
            src = slice(grp * gw, (grp + 1) * gw)
            wbf_ref[:, 2 * grp * gw:(2 * grp + 1) * gw] = wa_ref[:, src].astype(BF16)
            wbf_ref[:, (2 * grp + 1) * gw:(2 * grp + 2) * gw] = wg_ref[:, src].astype(BF16)

    def project(rows, ride, pcols):
        return _dot(_ride(h_ref, rows, hs_ref, ride), wbf_ref[:, pcols])

    def group(rows, ride, grp, proj, shifts):
        n = rows.size
        cols = slice(grp * gw, (grp + 1) * gw)
        cur, gate = proj[:, 0:gw], proj[:, gw:2 * gw]
        cw = cw_ref[:, cols]

        def gated(cur, s1, s2, gate):
            act = jax.nn.gelu(_conv3(cw, cur, s1, s2) + cb_ref[:, cols], approximate=True)
            return (act * gate).astype(BF16)

        s1, s2 = shifts(cur[:n])
        o_ref[rows, cols] = gated(cur[:n], s1, s2, gate[:n])
        if ride:
            ups_ref[:, cols] = cur[n:]
            os_ref[:, cols] = gated(cur[n:], p1_ref[:, cols], p2_ref[:, cols], gate[n:])
        return cur[:n]

    _conv_tile(i, tiles_per_seq, split, 2, rc, h_ref.shape[0], halo_ref, st_ref, project, group)


def _ffn_up(h2, h2s, w_ffn_up, ffn_conv_w, ffn_conv_b, prev2, prev1, w_ffn_down, *, bm, bn, split, rc, seq_len):
    m, ms = h2.shape[0], h2s.shape[0]
    tiles_per_seq = seq_len // bm
    n_tiles = D_FF // bn
    n_m = m // bm
    slab = D_FF // (n_tiles * n_m)
    wslab = pl.BlockSpec((pl.Squeezed(), slab, D_MODEL), lambda j, i: (0, j * n_m + i, 0))
    tile = pl.BlockSpec((bm, bn), lambda j, i: (i, j))
    stile = pl.BlockSpec((ms, bn), lambda j, i: (0, j))
    return pl.pallas_call(
        functools.partial(_ffn_up_kernel, tiles_per_seq, split, rc),
        grid=(n_tiles, m // bm),
        in_specs=[
            pl.BlockSpec((bm, D_MODEL), lambda j, i: (i, 0)),
            _resident(h2s),
            pl.BlockSpec((pl.Squeezed(), D_MODEL, bn), lambda j, i: (0, 0, j)),
            pl.BlockSpec((pl.Squeezed(), D_MODEL, bn), lambda j, i: (0, 0, n_tiles + j)),
            pl.BlockSpec((pl.Squeezed(), 3, bn), lambda j, i: (0, 0, j)),
            pl.BlockSpec((1, bn), lambda j, i: (0, j)),
            stile, stile, wslab,
        ],
        out_specs=[
            tile,
            pl.BlockSpec((pl.Squeezed(), 2, bn), lambda j, i: (i // tiles_per_seq, 0, j)),
            stile, stile, wslab,
        ],
        out_shape=[_sds((m, D_FF), BF16), _sds((m // seq_len, 2, D_FF), F32),
                   _sds((ms, D_FF), BF16), _sds((ms, D_FF), F32), _sds(w_ffn_down.shape, BF16)],
        scratch_shapes=[pltpu.VMEM((D_MODEL, 2 * bn), BF16), pltpu.VMEM((2, bn), F32)],
        compiler_params=_params(2),
        name="ffn_up",
    )(h2, h2s, w_ffn_up, w_ffn_up, ffn_conv_w, ffn_conv_b, prev2, prev1, w_ffn_down)


def _ffn_down_kernel(rc, a_ref, as_ref, w_ref, o_ref, os_ref):
    def chunk(rows, ride):
        n = rows.size
        o = _dot(_ride(a_ref, rows, as_ref, ride), w_ref[...]).astype(BF16)
        o_ref[rows, :] = o[:n]
        if ride:
            os_ref[...] = o[n:]

    _walk_chunks(pl.program_id(1), a_ref.shape[0], rc, chunk)


def _ffn_down(act, act_s, w_ffn_down, *, bm, bn, rc):
    m, ms = act.shape[0], act_s.shape[0]
    return pl.pallas_call(
        functools.partial(_ffn_down_kernel, rc),
        grid=(D_MODEL // bn, m // bm),
        in_specs=[
            pl.BlockSpec((bm, D_FF), lambda j, i: (i, 0)),
            _resident(act_s),
            pl.BlockSpec((pl.Squeezed(), D_FF, bn), lambda j, i: (0, 0, j)),
        ],
        out_specs=[pl.BlockSpec((bm, bn), lambda j, i: (i, j)),
                   pl.BlockSpec((ms, bn), lambda j, i: (0, j))],
        out_shape=[_sds((m, D_MODEL), BF16), _sds((ms, D_MODEL), BF16)],
        compiler_params=_params(2),
        name="ffn_down",
    )(act, act_s, w_ffn_down)


def _final_kernel(rc, f_ref, x1_ref, p_ref, fs_ref, x1s_ref, ps_ref, g_ref, wg_ref, wp_ref, y_ref, ys_ref):
    def chunk(rows, ride):
        n = rows.size
        f = _ride(f_ref, rows, fs_ref, ride).astype(F32)
        x2 = _ride(x1_ref, rows, x1s_ref, ride) + _rms(f, g_ref[...])
        gate = jax.nn.sigmoid(_dot(x2.astype(BF16), wg_ref[...]))
        y = x2 + gate * _dot(_ride(p_ref, rows, ps_ref, ride).astype(BF16), wp_ref[...])
        y_ref[rows, :] = y[:n]
        if ride:
            ys_ref[...] = y[n:]

    _walk_chunks(pl.program_id(0), x1_ref.shape[0], rc, chunk)


def _final(f, x1, p, f_s, x1_s, p_s, g_post, w_ple_gate, w_ple, *, bm, rc):
    m, ms = x1.shape[0], x1_s.shape[0]
    row = lambda i: (i, 0)
    return pl.pallas_call(
        functools.partial(_final_kernel, rc),
        grid=(m // bm,),
        in_specs=[
            pl.BlockSpec((bm, D_MODEL), row),
            pl.BlockSpec((bm, D_MODEL), row),
            pl.BlockSpec((bm, PLE_DIM), row),
            _resident(f_s), _resident(x1_s), _resident(p_s),
            _resident(g_post),
            pl.BlockSpec((pl.Squeezed(), D_MODEL, D_MODEL), lambda i: (0, 0, 0),
                         pipeline_mode=pl.Buffered(1)),
            pl.BlockSpec((pl.Squeezed(), PLE_DIM, D_MODEL), lambda i: (0, 0, 0),
                         pipeline_mode=pl.Buffered(1)),
        ],
        out_specs=[pl.BlockSpec((bm, D_MODEL), row), _resident(x1_s)],
        out_shape=[_sds((m, D_MODEL), F32), _sds((ms, D_MODEL), F32)],
        compiler_params=_params(1),
        name="ffn_norm_ple",
    )(f, x1, p, f_s, x1_s, p_s, g_post, w_ple_gate, w_ple)


def _rope_tables(pos):
    half = HEAD_DIM // 2
    inv = ROPE_THETA ** (-np.arange(half, dtype=np.float64) / half)
    ang = np.asarray(pos, dtype=np.float64)[:, None] * inv[None, :]
    cos, sin = np.cos(ang), np.sin(ang)
    reps = LANES // HEAD_DIM
    return (jnp.asarray(np.tile(np.concatenate([cos, cos], axis=1), (1, reps)), dtype=F32),
            jnp.asarray(np.tile(np.concatenate([-sin, sin], axis=1), (1, reps)), dtype=F32))


TILES = dict(qkv=dict(bm=1024, rc=256), attn=128, convmix=dict(bm=1024, bn=512, split=2, rc=512),
             merge=dict(bm=1024, bn=1024, split=4, rc=1024), oproj=dict(bm=512, rc=128),
             ffn_up=dict(bm=2048, bn=512, split=1, rc=512), ffn_down=dict(bm=1024, bn=512, rc=1024),
             final=dict(bm=512, rc=256))


def kernel(x_prompt, x_sample, state_attn_k, state_attn_v, state_conv, state_ffn_conv, p_prompt, p_sample,
           norm_mix_pre, norm_mix_post, norm_ffn_pre, norm_ffn_post, w_in, attn_sinks, conv_w, w_attn_branch,
           w_conv_branch, b_gate, w_o, w_ffn_up, ffn_conv_w, ffn_conv_b, w_ffn_down, w_ple, w_ple_gate):
    assert w_in.shape[0] == 1, "weights are indexed with a squeezed depth axis of size 1"
    batch, seq, _ = x_prompt.shape
    ns, dec_seq, _ = x_sample.shape
    assert dec_seq == 1
    m = batch * seq
    x = x_prompt.reshape(m, D_MODEL)
    xs = x_sample.reshape(ns, D_MODEL)
    rope_p = _rope_tables(np.arange(seq))
    rope_s = _rope_tables(np.full((ns,), PAST_LEN))

    h, q, k, v, hs, qs, ks, vs = _qkv(x, xs, norm_mix_pre, w_in, rope_p, rope_s, **TILES["qkv"])

    merge_tiles = TILES["merge"]
    a, w_o_bf, w_ple_gate_bf, w_ple_bf, w_attn_branch_bf, w_conv_branch_bf = _attn_prompt(
        attn_sinks.reshape(N_HEADS), q, k, v, w_o, w_ple_gate, w_ple, w_attn_branch, w_conv_branch,
        n_seq=batch, seq_len=seq, bq=TILES["attn"])
    a3, new_ks, new_vs = _attn_sample(
        attn_sinks.reshape(N_HEADS, 1), qs.reshape(ns, N_HEADS, HEAD_DIM), ks.reshape(ns, 1, KV_WIDTH),
        vs.reshape(ns, 1, KV_WIDTH), state_attn_k.reshape(ns, WINDOW, KV_WIDTH),
        state_attn_v.reshape(ns, WINDOW, KV_WIDTH))
    a_s = a3.reshape(ns, ATTN_WIDTH)

    b, new_conv_p, b_s, pre_s, w_gates_bf = _convmix(
        h, hs, w_in, conv_w, state_conv[0, :, 0], state_conv[0, :, 1], seq_len=seq,
        gate_group=merge_tiles["bn"] // merge_tiles["split"], **TILES["convmix"])
    merged, merged_s = _merge(h, a, b, hs, a_s, b_s, w_gates_bf, w_attn_branch_bf, w_conv_branch_bf, b_gate,
                              **merge_tiles)
    x1, h2, x1_s, h2_s = _oproj(merged, x, merged_s, xs, w_o_bf, norm_mix_post, norm_ffn_pre,
                                **TILES["oproj"])
    act, new_ffn_p, act_s, up_s, w_ffn_down_bf = _ffn_up(
        h2, h2_s, w_ffn_up, ffn_conv_w, ffn_conv_b, state_ffn_conv[0, :, 0], state_ffn_conv[0, :, 1], w_ffn_down,
        seq_len=seq, **TILES["ffn_up"])
    f, f_s = _ffn_down(act, act_s, w_ffn_down_bf, **TILES["ffn_down"])
    y, y_s = _final(f, x1, p_prompt.reshape(m, PLE_DIM), f_s, x1_s, p_sample.reshape(ns, PLE_DIM),
                    norm_ffn_post, w_ple_gate_bf, w_ple_bf, **TILES["final"])

    kv_tail = lambda t: t.reshape(batch, seq, KV_WIDTH)[:, seq - WINDOW:].reshape(batch, WINDOW, N_KV_HEADS, HEAD_DIM)
    kv_state = lambda t: t.reshape(ns, WINDOW, N_KV_HEADS, HEAD_DIM)
    return (y.reshape(batch, seq, D_MODEL), y_s.reshape(ns, dec_seq, D_MODEL),
            kv_tail(k)[None], kv_tail(v)[None], new_conv_p[None], new_ffn_p[None],
            kv_state(new_ks)[None], kv_state(new_vs)[None],
            jnp.stack([state_conv[0, :, 1], pre_s], axis=1)[None],
            jnp.stack([state_ffn_conv[0, :, 1], up_s], axis=1)[None])
```

```python
import functools

import jax
import jax.numpy as jnp
import numpy as np
from jax import lax
from jax.experimental import pallas as pl
from jax.experimental.pallas import tpu as pltpu

D_MODEL = 2048
N_HEADS = 16
N_KV_HEADS = 2
HEAD_DIM = 64
GROUP = N_HEADS // N_KV_HEADS
ATTN_WIDTH = N_HEADS * HEAD_DIM
KV_WIDTH = N_KV_HEADS * HEAD_DIM
QKV_WIDTH = ATTN_WIDTH + 2 * KV_WIDTH
WINDOW = 128
PAST_LEN = 16384
ROPE_THETA = 10000.0
CONV_WIDTH = D_MODEL // 2
D_FF = 5632
PLE_DIM = 256
RMS_EPS = 1e-6
OFF_CB = QKV_WIDTH
OFF_CC = OFF_CB + CONV_WIDTH
OFF_CU = OFF_CC + CONV_WIDTH
OFF_GA = OFF_CU + CONV_WIDTH
OFF_GB = OFF_GA + D_MODEL

LANES = 128
UNIT_HEADS = GROUP
MASKED = -1e30
VMEM_LIMIT = 56 * 1024 * 1024

BF16 = jnp.bfloat16
F32 = jnp.float32


def _params(n_axes):
    return pltpu.CompilerParams(dimension_semantics=("arbitrary",) * n_axes,
                                vmem_limit_bytes=VMEM_LIMIT)


def _rms(x, g):
    return x * lax.rsqrt(jnp.mean(x * x, axis=-1, keepdims=True) + RMS_EPS) * g


def _dot(a, b):
    return jnp.dot(a, b, preferred_element_type=F32)


def _walk_chunks(i, bm, rc, chunk_fn):
    rc = min(bm, rc)

    def walk(first):
        for c in range(bm // rc):
            chunk_fn(pl.ds(c * rc, rc), first and c == 0)

    @pl.when(i == 0)
    def _():
        walk(True)

    @pl.when(i > 0)
    def _():
        walk(False)


def _ride(ref, rows, sample_ref, ride):
    x = ref[rows, :]
    return jnp.concatenate([x, sample_ref[...]], axis=0) if ride else x


def _cast_gate_slab(src_ref, dst_ref, group):
    for t in range(D_MODEL // group):
        src = slice(t * group, (t + 1) * group)
        dst_ref[:, 2 * t * group:(2 * t + 1) * group] = src_ref[:, src].astype(BF16)
        dst_ref[:, (2 * t + 1) * group:(2 * t + 2) * group] = (
            src_ref[:, D_MODEL + t * group:D_MODEL + (t + 1) * group].astype(BF16))


def _gate_slab_specs(slab, step_fn):
    gate_in = pl.BlockSpec((pl.Squeezed(), pl.Element(slab), pl.Element(2 * D_MODEL)),
                           lambda *g: (0, pl.multiple_of(step_fn(*g) * slab, slab), OFF_GA))
    gate_out = pl.BlockSpec((slab, 2 * D_MODEL), lambda *g: (step_fn(*g), 0))
    return gate_in, gate_out


def _wcol(k_rows, block_cols, col_fn):
    return pl.BlockSpec((pl.Squeezed(), pl.Element(k_rows), pl.Element(block_cols)),
                        lambda *g: (0, 0, pl.multiple_of(col_fn(*g), LANES)))


def _resident(arr):
    zeros = (0,) * arr.ndim
    return pl.BlockSpec(arr.shape, lambda *g: zeros)


def _sds(shape, dtype):
    return jax.ShapeDtypeStruct(shape, dtype)


def _qkv_kernel(rc, x_ref, xs_ref, g_ref, w_ref, cos_ref, sin_ref, coss_ref, sins_ref,
                h_ref, q_ref, k_ref, v_ref, hs_ref, qs_ref, ks_ref, vs_ref, wbf_ref):
    i = pl.program_id(0)
    n_q = ATTN_WIDTH // LANES

    @pl.when(i == 0)
    def _():
        wbf_ref[...] = w_ref[...].astype(BF16)

    def chunk(rows, ride):
        n = rows.size

        def put(ref, sample_ref, cols, val):
            ref[rows, cols] = val[:n]
            if ride:
                sample_ref[:, cols] = val[n:]

        h = _rms(_ride(x_ref, rows, xs_ref, ride), g_ref[...]).astype(BF16)
        put(h_ref, hs_ref, slice(None), h)
        proj = _dot(h, wbf_ref[...])
        cos = _ride(cos_ref, rows, coss_ref, ride)
        sin = _ride(sin_ref, rows, sins_ref, ride)
        lane = lax.broadcasted_iota(jnp.int32, cos.shape, 1)
        first_half = (lane % HEAD_DIM) < (HEAD_DIM // 2)
        for c in range(n_q + 1):
            cols = slice(c * LANES, (c + 1) * LANES)
            xg = proj[:, cols]
            partner = jnp.where(first_half, pltpu.roll(xg, LANES - HEAD_DIM // 2, 1),
                                pltpu.roll(xg, HEAD_DIM // 2, 1))
            yg = xg * cos + partner * sin
            if c < n_q:
                put(q_ref, qs_ref, cols, (yg * (HEAD_DIM ** -0.5)).astype(BF16))
            else:
                put(k_ref, ks_ref, slice(None), yg)
        put(v_ref, vs_ref, slice(None), proj[:, ATTN_WIDTH + KV_WIDTH:QKV_WIDTH])

    _walk_chunks(i, x_ref.shape[0], rc, chunk)


def _qkv(x, xs, g, w_in, rope_p, rope_s, *, bm, rc):
    m, ms = x.shape[0], xs.shape[0]
    t_tiles = rope_p[0].shape[0] // bm
    row = lambda i: (i, 0)
    pos = lambda i: (i % t_tiles, 0)
    outs = lambda rows: [_sds((rows, D_MODEL), BF16), _sds((rows, ATTN_WIDTH), BF16),
                         _sds((rows, KV_WIDTH), F32), _sds((rows, KV_WIDTH), F32)]
    out_p, out_s = outs(m), outs(ms)
    return pl.pallas_call(
        functools.partial(_qkv_kernel, rc),
        grid=(m // bm,),
        in_specs=[
            pl.BlockSpec((bm, D_MODEL), row),
            _resident(xs),
            _resident(g),
            pl.BlockSpec((pl.Squeezed(), D_MODEL, QKV_WIDTH), lambda i: (0, 0, 0),
                         pipeline_mode=pl.Buffered(1)),
            pl.BlockSpec((bm, LANES), pos),
            pl.BlockSpec((bm, LANES), pos),
            _resident(rope_s[0]),
            _resident(rope_s[1]),
        ],
        out_specs=[pl.BlockSpec((bm, s.shape[1]), row) for s in out_p] + [_resident(s) for s in out_s],
        out_shape=out_p + out_s,
        scratch_shapes=[pltpu.VMEM((D_MODEL, QKV_WIDTH), BF16)],
        compiler_params=_params(1),
        name="qkv_rope",
    )(x, xs, g, w_in, rope_p[0], rope_p[1], rope_s[0], rope_s[1])


def _attn_prompt_kernel(sinks_ref, q_ref, kc_ref, kp_ref, vc_ref, vp_ref, wo_ref, wg_ref, wp_ref, wa_ref, wb_ref,
                        o_ref, wo_bf_ref, wg_bf_ref, wp_bf_ref, wa_bf_ref, wb_bf_ref):
    n_sub = q_ref.shape[0] // WINDOW
    qi = lax.broadcasted_iota(jnp.int32, (WINDOW, WINDOW), 0)
    ci = lax.broadcasted_iota(jnp.int32, (WINDOW, WINDOW), 1)
    own = ci <= qi
    first_valid = own | (pl.program_id(1) > 0)
    k_all = jnp.concatenate([kp_ref[...], kc_ref[...]], axis=0).astype(BF16)
    v_all = jnp.concatenate([vp_ref[...], vc_ref[...]], axis=0).astype(BF16)
    units = [(sub, hg) for sub in range(n_sub) for hg in range(N_HEADS // UNIT_HEADS)]
    heads_of = lambda hg: range(hg * UNIT_HEADS, (hg + 1) * UNIT_HEADS)
    q_rows = lambda sub: slice(sub * WINDOW, (sub + 1) * WINDOW)
    kv_rows = lambda sub: slice(sub * WINDOW, (sub + 2) * WINDOW)
    kv_cols = lambda hg: slice((hg * UNIT_HEADS // GROUP) * HEAD_DIM, (hg * UNIT_HEADS // GROUP + 1) * HEAD_DIM)

    scores = []
    for sub, kh in units:
        qs = jnp.concatenate([q_ref[q_rows(sub), h * HEAD_DIM:(h + 1) * HEAD_DIM] for h in heads_of(kh)], axis=0)
        scores.append(lax.dot_general(qs, k_all[kv_rows(sub), kv_cols(kh)], (((1,), (1,)), ((), ())),
                                      preferred_element_type=F32))
    probs, dens = [], []
    for (sub, kh), s_all in zip(units, scores):
        ps, ds = [], []
        for g, h in enumerate(heads_of(kh)):
            s2 = s_all[g * WINDOW:(g + 1) * WINDOW, :]
            s = jnp.where(own, s2[:, WINDOW:], s2[:, :WINDOW])
            if sub == 0:
                s = jnp.where(first_valid, s, MASKED)
            sink = sinks_ref[h]
            m = jnp.maximum(jnp.max(s, axis=-1, keepdims=True), sink)
            p = jnp.exp(s - m)
            ds.append(jnp.sum(p, axis=-1, keepdims=True) + jnp.exp(sink - m))
            ps.append(jnp.concatenate([jnp.where(own, 0.0, p), jnp.where(own, p, 0.0)], axis=1).astype(BF16))
        probs.append(jnp.concatenate(ps, axis=0))
        dens.append(ds)
    wo_bf_ref[...] = wo_ref[...].astype(BF16)
    wg_bf_ref[...] = wg_ref[...].astype(BF16)
    wa_bf_ref[...] = wa_ref[...].astype(BF16)
    wb_bf_ref[...] = wb_ref[...].astype(BF16)
    outs = [_dot(p_all, v_all[kv_rows(sub), kv_cols(kh)]) for (sub, kh), p_all in zip(units, probs)]
    for (sub, kh), o_all, ds in zip(units, outs, dens):
        for g, h in enumerate(heads_of(kh)):
            o = o_all[g * WINDOW:(g + 1) * WINDOW, :] / ds[g]
            o_ref[q_rows(sub), h * HEAD_DIM:(h + 1) * HEAD_DIM] = o.astype(BF16)

    @pl.when((pl.program_id(0) == 0) & (pl.program_id(1) == 0))
    def _():
        wp_bf_ref[...] = wp_ref[...].astype(BF16)


def _attn_prompt(sinks, q, k, v, w_o, w_ple_gate, w_ple, w_attn_branch, w_conv_branch, *, n_seq, seq_len, bq):
    nb = seq_len // bq
    sub = bq // WINDOW
    steps = n_seq * nb
    slab, bslab = D_MODEL // steps, ATTN_WIDTH // steps
    cur = lambda n, b: (n * nb + b, 0)
    prev = lambda n, b: (jnp.maximum((n * nb + b) * sub - 1, 0), 0)
    wslab = pl.BlockSpec((pl.Squeezed(), slab, D_MODEL), lambda n, b: (0, n * nb + b, 0))
    bspec = pl.BlockSpec((pl.Squeezed(), bslab, D_MODEL), lambda n, b: (0, n * nb + b, 0))
    return pl.pallas_call(
        _attn_prompt_kernel,
        grid=(n_seq, nb),
        in_specs=[
            pl.BlockSpec(memory_space=pltpu.SMEM),
            pl.BlockSpec((bq, ATTN_WIDTH), cur),
            pl.BlockSpec((bq, KV_WIDTH), cur),
            pl.BlockSpec((WINDOW, KV_WIDTH), prev),
            pl.BlockSpec((bq, KV_WIDTH), cur),
            pl.BlockSpec((WINDOW, KV_WIDTH), prev),
            wslab, wslab, _resident(w_ple), bspec, bspec,
        ],
        out_specs=[pl.BlockSpec((bq, ATTN_WIDTH), cur), wslab, wslab, _resident(w_ple), bspec, bspec],
        out_shape=[_sds((n_seq * seq_len, ATTN_WIDTH), BF16), _sds(w_o.shape, BF16), _sds(w_ple_gate.shape, BF16),
                   _sds(w_ple.shape, BF16), _sds(w_attn_branch.shape, BF16), _sds(w_conv_branch.shape, BF16)],
        compiler_params=_params(2),
        name="attn_prompt",
    )(sinks, q, k, k, v, v, w_o, w_ple_gate, w_ple, w_attn_branch, w_conv_branch)


def _attn_sample_kernel(sinks_ref, q_ref, kn_ref, vn_ref, sk_ref, sv_ref, a_ref, nk_ref, nv_ref):
    nk_ref[:, 0:WINDOW - 1, :] = sk_ref[:, 1:WINDOW, :]
    nk_ref[:, WINDOW - 1:WINDOW, :] = kn_ref[...]
    nv_ref[:, 0:WINDOW - 1, :] = sv_ref[:, 1:WINDOW, :]
    nv_ref[:, WINDOW - 1:WINDOW, :] = vn_ref[...]
    for kh in range(N_KV_HEADS):
        cols = slice(kh * HEAD_DIM, (kh + 1) * HEAD_DIM)
        heads = slice(kh * GROUP, (kh + 1) * GROUP)
        k = nk_ref[:, :, cols].astype(BF16)
        v = nv_ref[:, :, cols].astype(BF16)
        s = jnp.einsum("ngd,ncd->ngc", q_ref[:, heads, :], k, preferred_element_type=F32)
        sink = sinks_ref[heads, :][None]
        m = jnp.maximum(jnp.max(s, axis=-1, keepdims=True), sink)
        p = jnp.exp(s - m)
        den = jnp.sum(p, axis=-1, keepdims=True) + jnp.exp(sink - m)
        o = jnp.einsum("ngc,ncd->ngd", p.astype(BF16), v, preferred_element_type=F32) / den
        a_ref[:, heads, :] = o.astype(BF16)


def _attn_sample(sinks_col, q3, k_new3, v_new3, state_k, state_v):
    n = q3.shape[0]
    return pl.pallas_call(
        _attn_sample_kernel,
        out_shape=[_sds((n, N_HEADS, HEAD_DIM), BF16), _sds((n, WINDOW, KV_WIDTH), F32),
                   _sds((n, WINDOW, KV_WIDTH), F32)],
        compiler_params=pltpu.CompilerParams(vmem_limit_bytes=VMEM_LIMIT),
        name="attn_sample",
    )(sinks_col, q3, k_new3, v_new3, state_k, state_v)


def _shift_rows(cur, halo):
    rid = lax.broadcasted_iota(jnp.int32, cur.shape, 0)
    h0 = halo[0:1, :]
    h1 = halo[1:2, :]
    s1 = jnp.where(rid == 0, h1, pltpu.roll(cur, 1, 0))
    s2 = jnp.where(rid == 0, h0, jnp.where(rid == 1, h1, pltpu.roll(cur, 2, 0)))
    return s1, s2


def _conv3(cw, cur, s1, s2):
    return cw[0:1, :] * s2 + cw[1:2, :] * s1 + cw[2:3, :] * cur


def _conv_tile(i, tiles_per_seq, split, parts, rc, bm, halo_ref, st_ref, project, group_fn):
    gw = halo_ref.shape[1] // split
    gcols = [slice(grp * gw, (grp + 1) * gw) for grp in range(split)]

    @pl.when(i % tiles_per_seq == 0)
    def _():
        halo_ref[...] = jnp.zeros_like(halo_ref)

    halos = {}

    def chunk(rows, ride):
        last = rows.start + rows.size == bm
        full = None if last else project(rows, ride, slice(None))
        for grp in range(split):
            pcols = slice(parts * grp * gw, parts * (grp + 1) * gw)
            proj = project(rows, ride, pcols) if last else full[:, pcols]
            halo = halos.get(grp)
            if halo is None:
                halo = halo_ref[:, gcols[grp]]
            cur = group_fn(rows, ride, grp, proj, lambda cur, halo=halo: _shift_rows(cur, halo))
            halos[grp] = cur[cur.shape[0] - 2:, :]
        if last:
            for grp in range(split):
                halo_ref[:, gcols[grp]] = halos.pop(grp)

    _walk_chunks(i, bm, rc, chunk)

    @pl.when(i % tiles_per_seq == tiles_per_seq - 1)
    def _():
        st_ref[...] = halo_ref[...]


def _convmix_kernel(tiles_per_seq, split, rc, gate_group, h_ref, hs_ref, wb_ref, wc_ref, wu_ref, cw_ref, p2_ref, p1_ref,
                    wgate_ref, o_ref, st_ref, os_ref, pres_ref, wgate_bf_ref, wbf_ref, halo_ref):
    i = pl.program_id(1)
    _cast_gate_slab(wgate_ref, wgate_bf_ref, gate_group)
    gw = wb_ref.shape[1] // split

    @pl.when(i == 0)
    def _():
        for grp in range(split):
            src = slice(grp * gw, (grp + 1) * gw)
            for part, w_ref in enumerate((wb_ref, wc_ref, wu_ref)):
                wbf_ref[:, (3 * grp + part) * gw:(3 * grp + part + 1) * gw] = w_ref[:, src].astype(BF16)

    def project(rows, ride, pcols):
        return _dot(_ride(h_ref, rows, hs_ref, ride), wbf_ref[:, pcols])

    def group(rows, ride, grp, proj, shifts):
        n = rows.size
        cols = slice(grp * gw, (grp + 1) * gw)
        cb, cur = proj[:, 0:gw], proj[:, gw:2 * gw] * proj[:, 2 * gw:3 * gw]
        cw = cw_ref[:, cols]
        s1, s2 = shifts(cur[:n])
        o_ref[rows, cols] = (cb[:n] * _conv3(cw, cur[:n], s1, s2)).astype(BF16)
        if ride:
            pres_ref[:, cols] = cur[n:]
            os_ref[:, cols] = (cb[n:] * _conv3(cw, cur[n:], p1_ref[:, cols], p2_ref[:, cols])).astype(BF16)
        return cur[:n]

    _conv_tile(i, tiles_per_seq, split, 3, rc, h_ref.shape[0], halo_ref, st_ref, project, group)


def _convmix(h, hs, w_in, conv_w, prev2, prev1, *, bm, bn, split, rc, seq_len, gate_group):
    m, ms = h.shape[0], hs.shape[0]
    tiles_per_seq = seq_len // bm
    n_m = m // bm
    gate_in, gate_out = _gate_slab_specs(D_MODEL // (CONV_WIDTH // bn * n_m), lambda j, i: j * n_m + i)
    tile = pl.BlockSpec((bm, bn), lambda j, i: (i, j))
    stile = pl.BlockSpec((ms, bn), lambda j, i: (0, j))
    wspec = lambda off: _wcol(D_MODEL, bn, lambda j, i: off + j * bn)
    return pl.pallas_call(
        functools.partial(_convmix_kernel, tiles_per_seq, split, rc, gate_group),
        grid=(CONV_WIDTH // bn, n_m),
        in_specs=[
            pl.BlockSpec((bm, D_MODEL), lambda j, i: (i, 0)),
            _resident(hs),
            wspec(OFF_CB), wspec(OFF_CC), wspec(OFF_CU),
            pl.BlockSpec((pl.Squeezed(), 3, bn), lambda j, i: (0, 0, j)),
            stile, stile, gate_in,
        ],
        out_specs=[
            tile,
            pl.BlockSpec((pl.Squeezed(), 2, bn), lambda j, i: (i // tiles_per_seq, 0, j)),
            stile, stile, gate_out,
        ],
        out_shape=[_sds((m, CONV_WIDTH), BF16), _sds((m // seq_len, 2, CONV_WIDTH), F32),
                   _sds((ms, CONV_WIDTH), BF16), _sds((ms, CONV_WIDTH), F32), _sds((D_MODEL, 2 * D_MODEL), BF16)],
        scratch_shapes=[pltpu.VMEM((D_MODEL, 3 * bn), BF16), pltpu.VMEM((2, bn), F32)],
        compiler_params=_params(2),
        name="convmix",
    )(h, hs, w_in, w_in, w_in, conv_w, prev2, prev1, w_in)


def _merge_kernel(split, rc, h_ref, a_ref, b_ref, hs_ref, as_ref, bs_ref, wg_ref, wa_ref, wb_ref, bg_ref,
                  o_ref, os_ref):
    gw = wa_ref.shape[1] // split

    def chunk(rows, ride):
        n = rows.size
        h = _ride(h_ref, rows, hs_ref, ride)
        a = _ride(a_ref, rows, as_ref, ride)
        b = _ride(b_ref, rows, bs_ref, ride)
        for grp in range(split):
            cols = slice(grp * gw, (grp + 1) * gw)
            gates = _dot(h, wg_ref[:, 2 * grp * gw:2 * (grp + 1) * gw])
            ga = jax.nn.sigmoid(gates[:, 0:gw] + bg_ref[0:1, cols])
            gb = jax.nn.sigmoid(gates[:, gw:2 * gw] + bg_ref[1:2, cols])
            merged = (ga * _dot(a, wa_ref[:, cols]) + gb * _dot(b, wb_ref[:, cols])).astype(BF16)
            o_ref[rows, cols] = merged[:n]
            if ride:
                os_ref[:, cols] = merged[n:]

    _walk_chunks(pl.program_id(1), h_ref.shape[0], rc, chunk)


def _merge(h, a, b, hs, a_s, b_s, w_gates, w_attn_branch, w_conv_branch, b_gate, *, bm, bn, split, rc):
    m, ms = h.shape[0], hs.shape[0]
    row = lambda j, i: (i, 0)
    return pl.pallas_call(
        functools.partial(_merge_kernel, split, rc),
        grid=(D_MODEL // bn, m // bm),
        in_specs=[
            pl.BlockSpec((bm, D_MODEL), row),
            pl.BlockSpec((bm, ATTN_WIDTH), row),
            pl.BlockSpec((bm, CONV_WIDTH), row),
            _resident(hs), _resident(a_s), _resident(b_s),
            pl.BlockSpec((D_MODEL, 2 * bn), lambda j, i: (0, j)),
            pl.BlockSpec((pl.Squeezed(), ATTN_WIDTH, bn), lambda j, i: (0, 0, j)),
            pl.BlockSpec((pl.Squeezed(), CONV_WIDTH, bn), lambda j, i: (0, 0, j)),
            pl.BlockSpec((pl.Squeezed(), 2, bn), lambda j, i: (0, 0, j)),
        ],
        out_specs=[pl.BlockSpec((bm, bn), lambda j, i: (i, j)),
                   pl.BlockSpec((ms, bn), lambda j, i: (0, j))],
        out_shape=[_sds((m, D_MODEL), BF16), _sds((ms, D_MODEL), BF16)],
        compiler_params=_params(2),
        name="merge",
    )(h, a, b, hs, a_s, b_s, w_gates, w_attn_branch, w_conv_branch, b_gate)


def _oproj_kernel(rc, mg_ref, x_ref, mgs_ref, xs_ref, wo_ref, gpost_ref, gpre_ref,
                  x1_ref, h2_ref, x1s_ref, h2s_ref):
    def chunk(rows, ride):
        n = rows.size
        y = _dot(_ride(mg_ref, rows, mgs_ref, ride), wo_ref[...])
        x1 = _ride(x_ref, rows, xs_ref, ride) + _rms(y, gpost_ref[...])
        h2 = _rms(x1, gpre_ref[...]).astype(BF16)
        x1_ref[rows, :] = x1[:n]
        h2_ref[rows, :] = h2[:n]
        if ride:
            x1s_ref[...] = x1[n:]
            h2s_ref[...] = h2[n:]

    _walk_chunks(pl.program_id(0), x_ref.shape[0], rc, chunk)


def _oproj(merged, x, merged_s, xs, w_o, g_post, g_pre, *, bm, rc):
    m, ms = x.shape[0], xs.shape[0]
    tile = pl.BlockSpec((bm, D_MODEL), lambda i: (i, 0))
    return pl.pallas_call(
        functools.partial(_oproj_kernel, rc),
        grid=(m // bm,),
        in_specs=[
            tile, tile, _resident(merged_s), _resident(xs),
            pl.BlockSpec((pl.Squeezed(), D_MODEL, D_MODEL), lambda i: (0, 0, 0),
                         pipeline_mode=pl.Buffered(1)),
            _resident(g_post), _resident(g_pre),
        ],
        out_specs=[tile, tile, _resident(xs), _resident(merged_s)],
        out_shape=[_sds((m, D_MODEL), F32), _sds((m, D_MODEL), BF16),
                   _sds((ms, D_MODEL), F32), _sds((ms, D_MODEL), BF16)],
        compiler_params=_params(1),
        name="oproj_norm",
    )(merged, x, merged_s, xs, w_o, g_post, g_pre)


def _ffn_up_kernel(tiles_per_seq, split, rc, h_ref, hs_ref, wa_ref, wg_ref, cw_ref, cb_ref, p2_ref, p1_ref, wd_ref,
                   o_ref, st_ref, os_ref, ups_ref, wd_bf_ref, wbf_ref, halo_ref):
    i = pl.program_id(1)
    wd_bf_ref[...] = wd_ref[...].astype(BF16)
    gw = wa_ref.shape[1] // split

    @pl.when(i == 0)
    def _():
        for grp in range(split):
            src = slice(grp * gw, (grp + 1) * gw)
            wbf_ref[:, 2 * grp * gw:(2 * grp + 1) * gw] = wa_ref[:, src].astype(BF16)
            wbf_ref[:, (2 * grp + 1) * gw:(2 * grp + 2) * gw] = wg_ref[:, src].astype(BF16)

    def project(rows, ride, pcols):
        return _dot(_ride(h_ref, rows, hs_ref, ride), wbf_ref[:, pcols])

    def group(rows, ride, grp, proj, shifts):
        n = rows.size
        cols = slice(grp * gw, (grp + 1) * gw)
        cur, gate = proj[:, 0:gw], proj[:, gw:2 * gw]
        cw = cw_ref[:, cols]

        def gated(cur, s1, s2, gate):
            act = jax.nn.gelu(_conv3(cw, cur, s1, s2) + cb_ref[:, cols], approximate=True)
            return (act * gate).astype(BF16)

        s1, s2 = shifts(cur[:n])
        o_ref[rows, cols] = gated(cur[:n], s1, s2, gate[:n])
        if ride:
            ups_ref[:, cols] = cur[n:]
            os_ref[:, cols] = gated(cur[n:], p1_ref[:, cols], p2_ref[:, cols], gate[n:])
        return cur[:n]

    _conv_tile(i, tiles_per_seq, split, 2, rc, h_ref.shape[0], halo_ref, st_ref, project, group)


def _ffn_up(h2, h2s, w_ffn_up, ffn_conv_w, ffn_conv_b, prev2, prev1, w_ffn_down, *, bm, bn, split, rc, seq_len):
    m, ms = h2.shape[0], h2s.shape[0]
    tiles_per_seq = seq_len // bm
    n_tiles = D_FF // bn
    n_m = m // bm
    slab = D_FF // (n_tiles * n_m)
    wslab = pl.BlockSpec((pl.Squeezed(), slab, D_MODEL), lambda j, i: (0, j * n_m + i, 0))
    tile = pl.BlockSpec((bm, bn), lambda j, i: (i, j))
    stile = pl.BlockSpec((ms, bn), lambda j, i: (0, j))
    return pl.pallas_call(
        functools.partial(_ffn_up_kernel, tiles_per_seq, split, rc),
        grid=(n_tiles, m // bm),
        in_specs=[
            pl.BlockSpec((bm, D_MODEL), lambda j, i: (i, 0)),
            _resident(h2s),
            pl.BlockSpec((pl.Squeezed(), D_MODEL, bn), lambda j, i: (0, 0, j)),
            pl.BlockSpec((pl.Squeezed(), D_MODEL, bn), lambda j, i: (0, 0, n_tiles + j)),
            pl.BlockSpec((pl.Squeezed(), 3, bn), lambda j, i: (0, 0, j)),
            pl.BlockSpec((1, bn), lambda j, i: (0, j)),
            stile, stile, wslab,
        ],
        out_specs=[
            tile,
            pl.BlockSpec((pl.Squeezed(), 2, bn), lambda j, i: (i // tiles_per_seq, 0, j)),
            stile, stile, wslab,
        ],
        out_shape=[_sds((m, D_FF), BF16), _sds((m // seq_len, 2, D_FF), F32),
                   _sds((ms, D_FF), BF16), _sds((ms, D_FF), F32), _sds(w_ffn_down.shape, BF16)],
        scratch_shapes=[pltpu.VMEM((D_MODEL, 2 * bn), BF16), pltpu.VMEM((2, bn), F32)],
        compiler_params=_params(2),
        name="ffn_up",
    )(h2, h2s, w_ffn_up, w_ffn_up, ffn_conv_w, ffn_conv_b, prev2, prev1, w_ffn_down)


def _ffn_down_kernel(rc, a_ref, as_ref, w_ref, o_ref, os_ref):
    def chunk(rows, ride):
        n = rows.size
        o = _dot(_ride(a_ref, rows, as_ref, ride), w_ref[...]).astype(BF16)
        o_ref[rows, :] = o[:n]
        if ride:
            os_ref[...] = o[n:]

    _walk_chunks(pl.program_id(1), a_ref.shape[0], rc, chunk)


def _ffn_down(act, act_s, w_ffn_down, *, bm, bn, rc):
    m, ms = act.shape[0], act_s.shape[0]
    return pl.pallas_call(
        functools.partial(_ffn_down_kernel, rc),
        grid=(D_MODEL // bn, m // bm),
        in_specs=[
            pl.BlockSpec((bm, D_FF), lambda j, i: (i, 0)),
            _resident(act_s),
            pl.BlockSpec((pl.Squeezed(), D_FF, bn), lambda j, i: (0, 0, j)),
        ],
        out_specs=[pl.BlockSpec((bm, bn), lambda j, i: (i, j)),
                   pl.BlockSpec((ms, bn), lambda j, i: (0, j))],
        out_shape=[_sds((m, D_MODEL), BF16), _sds((ms, D_MODEL), BF16)],
        compiler_params=_params(2),
        name="ffn_down",
    )(act, act_s, w_ffn_down)


def _final_kernel(rc, f_ref, x1_ref, p_ref, fs_ref, x1s_ref, ps_ref, g_ref, wg_ref, wp_ref, y_ref, ys_ref):
    def chunk(rows, ride):
        n = rows.size
        f = _ride(f_ref, rows, fs_ref, ride).astype(F32)
        x2 = _ride(x1_ref, rows, x1s_ref, ride) + _rms(f, g_ref[...])
        gate = jax.nn.sigmoid(_dot(x2.astype(BF16), wg_ref[...]))
        y = x2 + gate * _dot(_ride(p_ref, rows, ps_ref, ride).astype(BF16), wp_ref[...])
        y_ref[rows, :] = y[:n]
        if ride:
            ys_ref[...] = y[n:]

    _walk_chunks(pl.program_id(0), x1_ref.shape[0], rc, chunk)


def _final(f, x1, p, f_s, x1_s, p_s, g_post, w_ple_gate, w_ple, *, bm, rc):
    m, ms = x1.shape[0], x1_s.shape[0]
    row = lambda i: (i, 0)
    return pl.pallas_call(
        functools.partial(_final_kernel, rc),
        grid=(m // bm,),
        in_specs=[
            pl.BlockSpec((bm, D_MODEL), row),
            pl.BlockSpec((bm, D_MODEL), row),
            pl.BlockSpec((bm, PLE_DIM), row),
            _resident(f_s), _resident(x1_s), _resident(p_s),
            _resident(g_post),
            pl.BlockSpec((pl.Squeezed(), D_MODEL, D_MODEL), lambda i: (0, 0, 0),
                         pipeline_mode=pl.Buffered(1)),
            pl.BlockSpec((pl.Squeezed(), PLE_DIM, D_MODEL), lambda i: (0, 0, 0),
                         pipeline_mode=pl.Buffered(1)),
        ],
        out_specs=[pl.BlockSpec((bm, D_MODEL), row), _resident(x1_s)],
        out_shape=[_sds((m, D_MODEL), F32), _sds((ms, D_MODEL), F32)],
        compiler_params=_params(1),
        name="ffn_norm_ple",
    )(f, x1, p, f_s, x1_s, p_s, g_post, w_ple_gate, w_ple)


def _rope_tables(pos):
    half = HEAD_DIM // 2
    inv = ROPE_THETA ** (-np.arange(half, dtype=np.float64) / half)
    ang = np.asarray(pos, dtype=np.float64)[:, None] * inv[None, :]
    cos, sin = np.cos(ang), np.sin(ang)
    reps = LANES // HEAD_DIM
    return (jnp.asarray(np.tile(np.concatenate([cos, cos], axis=1), (1, reps)), dtype=F32),
            jnp.asarray(np.tile(np.concatenate([-sin, sin], axis=1), (1, reps)), dtype=F32))


TILES = dict(qkv=dict(bm=1024, rc=256), attn=128, convmix=dict(bm=1024, bn=512, split=2, rc=512),
             merge=dict(bm=1024, bn=1024, split=4, rc=1024), oproj=dict(bm=512, rc=128),
             ffn_up=dict(bm=2048, bn=512, split=1, rc=1024), ffn_down=dict(bm=1024, bn=512, rc=1024),
             final=dict(bm=512, rc=256))


def kernel(x_prompt, x_sample, state_attn_k, state_attn_v, state_conv, state_ffn_conv, p_prompt, p_sample,
           norm_mix_pre, norm_mix_post, norm_ffn_pre, norm_ffn_post, w_in, attn_sinks, conv_w, w_attn_branch,
           w_conv_branch, b_gate, w_o, w_ffn_up, ffn_conv_w, ffn_conv_b, w_ffn_down, w_ple, w_ple_gate):
    assert w_in.shape[0] == 1, "weights are indexed with a squeezed depth axis of size 1"
    batch, seq, _ = x_prompt.shape
    ns, dec_seq, _ = x_sample.shape
    assert dec_seq == 1
    m = batch * seq
    x = x_prompt.reshape(m, D_MODEL)
    xs = x_sample.reshape(ns, D_MODEL)
    rope_p = _rope_tables(np.arange(seq))
    rope_s = _rope_tables(np.full((ns,), PAST_LEN))

    h, q, k, v, hs, qs, ks, vs = _qkv(x, xs, norm_mix_pre, w_in, rope_p, rope_s, **TILES["qkv"])

    merge_tiles = TILES["merge"]
    a, w_o_bf, w_ple_gate_bf, w_ple_bf, w_attn_branch_bf, w_conv_branch_bf = _attn_prompt(
        attn_sinks.reshape(N_HEADS), q, k, v, w_o, w_ple_gate, w_ple, w_attn_branch, w_conv_branch,
        n_seq=batch, seq_len=seq, bq=TILES["attn"])
    a3, new_ks, new_vs = _attn_sample(
        attn_sinks.reshape(N_HEADS, 1), qs.reshape(ns, N_HEADS, HEAD_DIM), ks.reshape(ns, 1, KV_WIDTH),
        vs.reshape(ns, 1, KV_WIDTH), state_attn_k.reshape(ns, WINDOW, KV_WIDTH),
        state_attn_v.reshape(ns, WINDOW, KV_WIDTH))
    a_s = a3.reshape(ns, ATTN_WIDTH)

    b, new_conv_p, b_s, pre_s, w_gates_bf = _convmix(
        h, hs, w_in, conv_w, state_conv[0, :, 0], state_conv[0, :, 1], seq_len=seq,
        gate_group=merge_tiles["bn"] // merge_tiles["split"], **TILES["convmix"])
    merged, merged_s = _merge(h, a, b, hs, a_s, b_s, w_gates_bf, w_attn_branch_bf, w_conv_branch_bf, b_gate,
                              **merge_tiles)
    x1, h2, x1_s, h2_s = _oproj(merged, x, merged_s, xs, w_o_bf, norm_mix_post, norm_ffn_pre,
                                **TILES["oproj"])
    act, new_ffn_p, act_s, up_s, w_ffn_down_bf = _ffn_up(
        h2, h2_s, w_ffn_up, ffn_conv_w, ffn_conv_b, state_ffn_conv[0, :, 0], state_ffn_conv[0, :, 1], w_ffn_down,
        seq_len=seq, **TILES["ffn_up"])
    f, f_s = _ffn_down(act, act_s, w_ffn_down_bf, **TILES["ffn_down"])
    y, y_s = _final(f, x1, p_prompt.reshape(m, PLE_DIM), f_s, x1_s, p_sample.reshape(ns, PLE_DIM),
                    norm_ffn_post, w_ple_gate_bf, w_ple_bf, **TILES["final"])

    kv_tail = lambda t: t.reshape(batch, seq, KV_WIDTH)[:, seq - WINDOW:].reshape(batch, WINDOW, N_KV_HEADS, HEAD_DIM)
    kv_state = lambda t: t.reshape(ns, WINDOW, N_KV_HEADS, HEAD_DIM)
    return (y.reshape(batch, seq, D_MODEL), y_s.reshape(ns, dec_seq, D_MODEL),
            kv_tail(k)[None], kv_tail(v)[None], new_conv_p[None], new_ffn_p[None],
            kv_state(new_ks)[None], kv_state(new_vs)[None],
            jnp.stack([state_conv[0, :, 1], pre_s], axis=1)[None],
            jnp.stack([state_ffn_conv[0, :, 1], up_s], axis=1)[None])
```

```python
import functools

import jax
import jax.numpy as jnp
import numpy as np
from jax import lax
from jax.experimental import pallas as pl
from jax.experimental.pallas import tpu as pltpu

D_MODEL = 2048
N_HEADS = 16
N_KV_HEADS = 2
HEAD_DIM = 64
GROUP = N_HEADS // N_KV_HEADS
ATTN_WIDTH = N_HEADS * HEAD_DIM
KV_WIDTH = N_KV_HEADS * HEAD_DIM
QKV_WIDTH = ATTN_WIDTH + 2 * KV_WIDTH
WINDOW = 128
PAST_LEN = 16384
ROPE_THETA = 10000.0
CONV_WIDTH = D_MODEL // 2
D_FF = 5632
PLE_DIM = 256
RMS_EPS = 1e-6
OFF_CB = QKV_WIDTH
OFF_CC = OFF_CB + CONV_WIDTH
OFF_CU = OFF_CC + CONV_WIDTH
OFF_GA = OFF_CU + CONV_WIDTH
OFF_GB = OFF_GA + D_MODEL

LANES = 128
UNIT_HEADS = GROUP
MASKED = -1e30
VMEM_LIMIT = 56 * 1024 * 1024

BF16 = jnp.bfloat16
F32 = jnp.float32


def _params(n_axes):
    return pltpu.CompilerParams(dimension_semantics=("arbitrary",) * n_axes,
                                vmem_limit_bytes=VMEM_LIMIT)


def _rms(x, g):
    return x * lax.rsqrt(jnp.mean(x * x, axis=-1, keepdims=True) + RMS_EPS) * g


def _dot(a, b):
    return jnp.dot(a, b, preferred_element_type=F32)


def _walk_chunks(i, bm, rc, chunk_fn):
    rc = min(bm, rc)

    def walk(first):
        for c in range(bm // rc):
            chunk_fn(pl.ds(c * rc, rc), first and c == 0)

    @pl.when(i == 0)
    def _():
        walk(True)

    @pl.when(i > 0)
    def _():
        walk(False)


def _ride(ref, rows, sample_ref, ride):
    x = ref[rows, :]
    return jnp.concatenate([x, sample_ref[...]], axis=0) if ride else x


def _cast_gate_slab(src_ref, dst_ref, group):
    for t in range(D_MODEL // group):
        src = slice(t * group, (t + 1) * group)
        dst_ref[:, 2 * t * group:(2 * t + 1) * group] = src_ref[:, src].astype(BF16)
        dst_ref[:, (2 * t + 1) * group:(2 * t + 2) * group] = (
            src_ref[:, D_MODEL + t * group:D_MODEL + (t + 1) * group].astype(BF16))


def _gate_slab_specs(slab, step_fn):
    gate_in = pl.BlockSpec((pl.Squeezed(), pl.Element(slab), pl.Element(2 * D_MODEL)),
                           lambda *g: (0, pl.multiple_of(step_fn(*g) * slab, slab), OFF_GA))
    gate_out = pl.BlockSpec((slab, 2 * D_MODEL), lambda *g: (step_fn(*g), 0))
    return gate_in, gate_out


def _wcol(k_rows, block_cols, col_fn):
    return pl.BlockSpec((pl.Squeezed(), pl.Element(k_rows), pl.Element(block_cols)),
                        lambda *g: (0, 0, pl.multiple_of(col_fn(*g), LANES)))


def _resident(arr):
    zeros = (0,) * arr.ndim
    return pl.BlockSpec(arr.shape, lambda *g: zeros)


def _sds(shape, dtype):
    return jax.ShapeDtypeStruct(shape, dtype)


def _qkv_kernel(rc, x_ref, xs_ref, g_ref, w_ref, cos_ref, sin_ref, coss_ref, sins_ref,
                h_ref, q_ref, k_ref, v_ref, hs_ref, qs_ref, ks_ref, vs_ref, wbf_ref):
    i = pl.program_id(0)
    n_q = ATTN_WIDTH // LANES

    @pl.when(i == 0)
    def _():
        wbf_ref[...] = w_ref[...].astype(BF16)

    def chunk(rows, ride):
        n = rows.size

        def put(ref, sample_ref, cols, val):
            ref[rows, cols] = val[:n]
            if ride:
                sample_ref[:, cols] = val[n:]

        h = _rms(_ride(x_ref, rows, xs_ref, ride), g_ref[...]).astype(BF16)
        put(h_ref, hs_ref, slice(None), h)
        proj = _dot(h, wbf_ref[...])
        cos = _ride(cos_ref, rows, coss_ref, ride)
        sin = _ride(sin_ref, rows, sins_ref, ride)
        lane = lax.broadcasted_iota(jnp.int32, cos.shape, 1)
        first_half = (lane % HEAD_DIM) < (HEAD_DIM // 2)
        for c in range(n_q + 1):
            cols = slice(c * LANES, (c + 1) * LANES)
            xg = proj[:, cols]
            partner = jnp.where(first_half, pltpu.roll(xg, LANES - HEAD_DIM // 2, 1),
                                pltpu.roll(xg, HEAD_DIM // 2, 1))
            yg = xg * cos + partner * sin
            if c < n_q:
                put(q_ref, qs_ref, cols, (yg * (HEAD_DIM ** -0.5)).astype(BF16))
            else:
                put(k_ref, ks_ref, slice(None), yg)
        put(v_ref, vs_ref, slice(None), proj[:, ATTN_WIDTH + KV_WIDTH:QKV_WIDTH])

    _walk_chunks(i, x_ref.shape[0], rc, chunk)


def _qkv(x, xs, g, w_in, rope_p, rope_s, *, bm, rc):
    m, ms = x.shape[0], xs.shape[0]
    t_tiles = rope_p[0].shape[0] // bm
    row = lambda i: (i, 0)
    pos = lambda i: (i % t_tiles, 0)
    outs = lambda rows: [_sds((rows, D_MODEL), BF16), _sds((rows, ATTN_WIDTH), BF16),
                         _sds((rows, KV_WIDTH), F32), _sds((rows, KV_WIDTH), F32)]
    out_p, out_s = outs(m), outs(ms)
    return pl.pallas_call(
        functools.partial(_qkv_kernel, rc),
        grid=(m // bm,),
        in_specs=[
            pl.BlockSpec((bm, D_MODEL), row),
            _resident(xs),
            _resident(g),
            pl.BlockSpec((pl.Squeezed(), D_MODEL, QKV_WIDTH), lambda i: (0, 0, 0),
                         pipeline_mode=pl.Buffered(1)),
            pl.BlockSpec((bm, LANES), pos),
            pl.BlockSpec((bm, LANES), pos),
            _resident(rope_s[0]),
            _resident(rope_s[1]),
        ],
        out_specs=[pl.BlockSpec((bm, s.shape[1]), row) for s in out_p] + [_resident(s) for s in out_s],
        out_shape=out_p + out_s,
        scratch_shapes=[pltpu.VMEM((D_MODEL, QKV_WIDTH), BF16)],
        compiler_params=_params(1),
        name="qkv_rope",
    )(x, xs, g, w_in, rope_p[0], rope_p[1], rope_s[0], rope_s[1])


def _attn_prompt_kernel(sinks_ref, q_ref, kc_ref, kp_ref, vc_ref, vp_ref, wo_ref, wg_ref, wp_ref, wa_ref, wb_ref,
                        o_ref, wo_bf_ref, wg_bf_ref, wp_bf_ref, wa_bf_ref, wb_bf_ref):
    n_sub = q_ref.shape[0] // WINDOW
    qi = lax.broadcasted_iota(jnp.int32, (WINDOW, WINDOW), 0)
    ci = lax.broadcasted_iota(jnp.int32, (WINDOW, WINDOW), 1)
    own = ci <= qi
    first_valid = own | (pl.program_id(1) > 0)
    k_all = jnp.concatenate([kp_ref[...], kc_ref[...]], axis=0).astype(BF16)
    v_all = jnp.concatenate([vp_ref[...], vc_ref[...]], axis=0).astype(BF16)
    units = [(sub, hg) for sub in range(n_sub) for hg in range(N_HEADS // UNIT_HEADS)]
    heads_of = lambda hg: range(hg * UNIT_HEADS, (hg + 1) * UNIT_HEADS)
    q_rows = lambda sub: slice(sub * WINDOW, (sub + 1) * WINDOW)
    kv_rows = lambda sub: slice(sub * WINDOW, (sub + 2) * WINDOW)
    kv_cols = lambda hg: slice((hg * UNIT_HEADS // GROUP) * HEAD_DIM, (hg * UNIT_HEADS // GROUP + 1) * HEAD_DIM)

    scores = []
    for sub, kh in units:
        qs = jnp.concatenate([q_ref[q_rows(sub), h * HEAD_DIM:(h + 1) * HEAD_DIM] for h in heads_of(kh)], axis=0)
        scores.append(lax.dot_general(qs, k_all[kv_rows(sub), kv_cols(kh)], (((1,), (1,)), ((), ())),
                                      preferred_element_type=F32))
    probs, dens = [], []
    for (sub, kh), s_all in zip(units, scores):
        ps, ds = [], []
        for g, h in enumerate(heads_of(kh)):
            s2 = s_all[g * WINDOW:(g + 1) * WINDOW, :]
            s = jnp.where(own, s2[:, WINDOW:], s2[:, :WINDOW])
            if sub == 0:
                s = jnp.where(first_valid, s, MASKED)
            sink = sinks_ref[h]
            m = jnp.maximum(jnp.max(s, axis=-1, keepdims=True), sink)
            p = jnp.exp(s - m)
            ds.append(jnp.sum(p, axis=-1, keepdims=True) + jnp.exp(sink - m))
            ps.append(jnp.concatenate([jnp.where(own, 0.0, p), jnp.where(own, p, 0.0)], axis=1).astype(BF16))
        probs.append(jnp.concatenate(ps, axis=0))
        dens.append(ds)
    wo_bf_ref[...] = wo_ref[...].astype(BF16)
    wg_bf_ref[...] = wg_ref[...].astype(BF16)
    wa_bf_ref[...] = wa_ref[...].astype(BF16)
    wb_bf_ref[...] = wb_ref[...].astype(BF16)
    outs = [_dot(p_all, v_all[kv_rows(sub), kv_cols(kh)]) for (sub, kh), p_all in zip(units, probs)]
    for (sub, kh), o_all, ds in zip(units, outs, dens):
        for g, h in enumerate(heads_of(kh)):
            o = o_all[g * WINDOW:(g + 1) * WINDOW, :] / ds[g]
            o_ref[q_rows(sub), h * HEAD_DIM:(h + 1) * HEAD_DIM] = o.astype(BF16)

    @pl.when((pl.program_id(0) == 0) & (pl.program_id(1) == 0))
    def _():
        wp_bf_ref[...] = wp_ref[...].astype(BF16)


def _attn_prompt(sinks, q, k, v, w_o, w_ple_gate, w_ple, w_attn_branch, w_conv_branch, *, n_seq, seq_len, bq):
    nb = seq_len // bq
    sub = bq // WINDOW
    steps = n_seq * nb
    slab, bslab = D_MODEL // steps, ATTN_WIDTH // steps
    cur = lambda n, b: (n * nb + b, 0)
    prev = lambda n, b: (jnp.maximum((n * nb + b) * sub - 1, 0), 0)
    wslab = pl.BlockSpec((pl.Squeezed(), slab, D_MODEL), lambda n, b: (0, n * nb + b, 0))
    bspec = pl.BlockSpec((pl.Squeezed(), bslab, D_MODEL), lambda n, b: (0, n * nb + b, 0))
    return pl.pallas_call(
        _attn_prompt_kernel,
        grid=(n_seq, nb),
        in_specs=[
            pl.BlockSpec(memory_space=pltpu.SMEM),
            pl.BlockSpec((bq, ATTN_WIDTH), cur),
            pl.BlockSpec((bq, KV_WIDTH), cur),
            pl.BlockSpec((WINDOW, KV_WIDTH), prev),
            pl.BlockSpec((bq, KV_WIDTH), cur),
            pl.BlockSpec((WINDOW, KV_WIDTH), prev),
            wslab, wslab, _resident(w_ple), bspec, bspec,
        ],
        out_specs=[pl.BlockSpec((bq, ATTN_WIDTH), cur), wslab, wslab, _resident(w_ple), bspec, bspec],
        out_shape=[_sds((n_seq * seq_len, ATTN_WIDTH), BF16), _sds(w_o.shape, BF16), _sds(w_ple_gate.shape, BF16),
                   _sds(w_ple.shape, BF16), _sds(w_attn_branch.shape, BF16), _sds(w_conv_branch.shape, BF16)],
        compiler_params=_params(2),
        name="attn_prompt",
    )(sinks, q, k, k, v, v, w_o, w_ple_gate, w_ple, w_attn_branch, w_conv_branch)


def _attn_sample_kernel(sinks_ref, q_ref, kn_ref, vn_ref, sk_ref, sv_ref, a_ref, nk_ref, nv_ref):
    nk_ref[:, 0:WINDOW - 1, :] = sk_ref[:, 1:WINDOW, :]
    nk_ref[:, WINDOW - 1:WINDOW, :] = kn_ref[...]
    nv_ref[:, 0:WINDOW - 1, :] = sv_ref[:, 1:WINDOW, :]
    nv_ref[:, WINDOW - 1:WINDOW, :] = vn_ref[...]
    for kh in range(N_KV_HEADS):
        cols = slice(kh * HEAD_DIM, (kh + 1) * HEAD_DIM)
        heads = slice(kh * GROUP, (kh + 1) * GROUP)
        k = nk_ref[:, :, cols].astype(BF16)
        v = nv_ref[:, :, cols].astype(BF16)
        s = jnp.einsum("ngd,ncd->ngc", q_ref[:, heads, :], k, preferred_element_type=F32)
        sink = sinks_ref[heads, :][None]
        m = jnp.maximum(jnp.max(s, axis=-1, keepdims=True), sink)
        p = jnp.exp(s - m)
        den = jnp.sum(p, axis=-1, keepdims=True) + jnp.exp(sink - m)
        o = jnp.einsum("ngc,ncd->ngd", p.astype(BF16), v, preferred_element_type=F32) / den
        a_ref[:, heads, :] = o.astype(BF16)


def _attn_sample(sinks_col, q3, k_new3, v_new3, state_k, state_v):
    n = q3.shape[0]
    return pl.pallas_call(
        _attn_sample_kernel,
        out_shape=[_sds((n, N_HEADS, HEAD_DIM), BF16), _sds((n, WINDOW, KV_WIDTH), F32),
                   _sds((n, WINDOW, KV_WIDTH), F32)],
        compiler_params=pltpu.CompilerParams(vmem_limit_bytes=VMEM_LIMIT),
        name="attn_sample",
    )(sinks_col, q3, k_new3, v_new3, state_k, state_v)


def _shift_rows(cur, halo):
    rid = lax.broadcasted_iota(jnp.int32, cur.shape, 0)
    h0 = halo[0:1, :]
    h1 = halo[1:2, :]
    s1 = jnp.where(rid == 0, h1, pltpu.roll(cur, 1, 0))
    s2 = jnp.where(rid == 0, h0, jnp.where(rid == 1, h1, pltpu.roll(cur, 2, 0)))
    return s1, s2


def _conv3(cw, cur, s1, s2):
    return cw[0:1, :] * s2 + cw[1:2, :] * s1 + cw[2:3, :] * cur


def _conv_tile(i, tiles_per_seq, split, parts, rc, bm, halo_ref, st_ref, project, group_fn):
    gw = halo_ref.shape[1] // split
    gcols = [slice(grp * gw, (grp + 1) * gw) for grp in range(split)]

    @pl.when(i % tiles_per_seq == 0)
    def _():
        halo_ref[...] = jnp.zeros_like(halo_ref)

    halos = {}

    def chunk(rows, ride):
        last = rows.start + rows.size == bm
        full = None if last else project(rows, ride, slice(None))
        for grp in range(split):
            pcols = slice(parts * grp * gw, parts * (grp + 1) * gw)
            proj = project(rows, ride, pcols) if last else full[:, pcols]
            halo = halos.get(grp)
            if halo is None:
                halo = halo_ref[:, gcols[grp]]
            cur = group_fn(rows, ride, grp, proj, lambda cur, halo=halo: _shift_rows(cur, halo))
            halos[grp] = cur[cur.shape[0] - 2:, :]
        if last:
            for grp in range(split):
                halo_ref[:, gcols[grp]] = halos.pop(grp)

    _walk_chunks(i, bm, rc, chunk)

    @pl.when(i % tiles_per_seq == tiles_per_seq - 1)
    def _():
        st_ref[...] = halo_ref[...]


def _convmix_kernel(tiles_per_seq, split, rc, gate_group, h_ref, hs_ref, wb_ref, wc_ref, wu_ref, cw_ref, p2_ref, p1_ref,
                    wgate_ref, o_ref, st_ref, os_ref, pres_ref, wgate_bf_ref, wbf_ref, halo_ref):
    i = pl.program_id(1)
    _cast_gate_slab(wgate_ref, wgate_bf_ref, gate_group)
    gw = wb_ref.shape[1] // split

    @pl.when(i == 0)
    def _():
        for grp in range(split):
            src = slice(grp * gw, (grp + 1) * gw)
            for part, w_ref in enumerate((wb_ref, wc_ref, wu_ref)):
                wbf_ref[:, (3 * grp + part) * gw:(3 * grp + part + 1) * gw] = w_ref[:, src].astype(BF16)

    def project(rows, ride, pcols):
        return _dot(_ride(h_ref, rows, hs_ref, ride), wbf_ref[:, pcols])

    def group(rows, ride, grp, proj, shifts):
        n = rows.size
        cols = slice(grp * gw, (grp + 1) * gw)
        cb, cur = proj[:, 0:gw], proj[:, gw:2 * gw] * proj[:, 2 * gw:3 * gw]
        cw = cw_ref[:, cols]
        s1, s2 = shifts(cur[:n])
        o_ref[rows, cols] = (cb[:n] * _conv3(cw, cur[:n], s1, s2)).astype(BF16)
        if ride:
            pres_ref[:, cols] = cur[n:]
            os_ref[:, cols] = (cb[n:] * _conv3(cw, cur[n:], p1_ref[:, cols], p2_ref[:, cols])).astype(BF16)
        return cur[:n]

    _conv_tile(i, tiles_per_seq, split, 3, rc, h_ref.shape[0], halo_ref, st_ref, project, group)


def _convmix(h, hs, w_in, conv_w, prev2, prev1, *, bm, bn, split, rc, seq_len, gate_group):
    m, ms = h.shape[0], hs.shape[0]
    tiles_per_seq = seq_len // bm
    n_m = m // bm
    gate_in, gate_out = _gate_slab_specs(D_MODEL // (CONV_WIDTH // bn * n_m), lambda j, i: j * n_m + i)
    tile = pl.BlockSpec((bm, bn), lambda j, i: (i, j))
    stile = pl.BlockSpec((ms, bn), lambda j, i: (0, j))
    wspec = lambda off: _wcol(D_MODEL, bn, lambda j, i: off + j * bn)
    return pl.pallas_call(
        functools.partial(_convmix_kernel, tiles_per_seq, split, rc, gate_group),
        grid=(CONV_WIDTH // bn, n_m),
        in_specs=[
            pl.BlockSpec((bm, D_MODEL), lambda j, i: (i, 0)),
            _resident(hs),
            wspec(OFF_CB), wspec(OFF_CC), wspec(OFF_CU),
            pl.BlockSpec((pl.Squeezed(), 3, bn), lambda j, i: (0, 0, j)),
            stile, stile, gate_in,
        ],
        out_specs=[
            tile,
            pl.BlockSpec((pl.Squeezed(), 2, bn), lambda j, i: (i // tiles_per_seq, 0, j)),
            stile, stile, gate_out,
        ],
        out_shape=[_sds((m, CONV_WIDTH), BF16), _sds((m // seq_len, 2, CONV_WIDTH), F32),
                   _sds((ms, CONV_WIDTH), BF16), _sds((ms, CONV_WIDTH), F32), _sds((D_MODEL, 2 * D_MODEL), BF16)],
        scratch_shapes=[pltpu.VMEM((D_MODEL, 3 * bn), BF16), pltpu.VMEM((2, bn), F32)],
        compiler_params=_params(2),
        name="convmix",
    )(h, hs, w_in, w_in, w_in, conv_w, prev2, prev1, w_in)


def _merge_kernel(split, rc, h_ref, a_ref, b_ref, hs_ref, as_ref, bs_ref, wg_ref, wa_ref, wb_ref, bg_ref,
                  o_ref, os_ref):
    gw = wa_ref.shape[1] // split

    def chunk(rows, ride):
        n = rows.size
        h = _ride(h_ref, rows, hs_ref, ride)
        a = _ride(a_ref, rows, as_ref, ride)
        b = _ride(b_ref, rows, bs_ref, ride)
        for grp in range(split):
            cols = slice(grp * gw, (grp + 1) * gw)
            gates = _dot(h, wg_ref[:, 2 * grp * gw:2 * (grp + 1) * gw])
            ga = jax.nn.sigmoid(gates[:, 0:gw] + bg_ref[0:1, cols])
            gb = jax.nn.sigmoid(gates[:, gw:2 * gw] + bg_ref[1:2, cols])
            merged = (ga * _dot(a, wa_ref[:, cols]) + gb * _dot(b, wb_ref[:, cols])).astype(BF16)
            o_ref[rows, cols] = merged[:n]
            if ride:
                os_ref[:, cols] = merged[n:]

    _walk_chunks(pl.program_id(1), h_ref.shape[0], rc, chunk)


def _merge(h, a, b, hs, a_s, b_s, w_gates, w_attn_branch, w_conv_branch, b_gate, *, bm, bn, split, rc):
    m, ms = h.shape[0], hs.shape[0]
    row = lambda j, i: (i, 0)
    return pl.pallas_call(
        functools.partial(_merge_kernel, split, rc),
        grid=(D_MODEL // bn, m // bm),
        in_specs=[
            pl.BlockSpec((bm, D_MODEL), row),
            pl.BlockSpec((bm, ATTN_WIDTH), row),
            pl.BlockSpec((bm, CONV_WIDTH), row),
            _resident(hs), _resident(a_s), _resident(b_s),
            pl.BlockSpec((D_MODEL, 2 * bn), lambda j, i: (0, j)),
            pl.BlockSpec((pl.Squeezed(), ATTN_WIDTH, bn), lambda j, i: (0, 0, j)),
            pl.BlockSpec((pl.Squeezed(), CONV_WIDTH, bn), lambda j, i: (0, 0, j)),
            pl.BlockSpec((pl.Squeezed(), 2, bn), lambda j, i: (0, 0, j)),
        ],
        out_specs=[pl.BlockSpec((bm, bn), lambda j, i: (i, j)),
                   pl.BlockSpec((ms, bn), lambda j, i: (0, j))],
        out_shape=[_sds((m, D_MODEL), BF16), _sds((ms, D_MODEL), BF16)],
        compiler_params=_params(2),
        name="merge",
    )(h, a, b, hs, a_s, b_s, w_gates, w_attn_branch, w_conv_branch, b_gate)


def _oproj_kernel(rc, mg_ref, x_ref, mgs_ref, xs_ref, wo_ref, gpost_ref, gpre_ref,
                  x1_ref, h2_ref, x1s_ref, h2s_ref):
    def chunk(rows, ride):
        n = rows.size
        y = _dot(_ride(mg_ref, rows, mgs_ref, ride), wo_ref[...])
        x1 = _ride(x_ref, rows, xs_ref, ride) + _rms(y, gpost_ref[...])
        h2 = _rms(x1, gpre_ref[...]).astype(BF16)
        x1_ref[rows, :] = x1[:n]
        h2_ref[rows, :] = h2[:n]
        if ride:
            x1s_ref[...] = x1[n:]
            h2s_ref[...] = h2[n:]

    _walk_chunks(pl.program_id(0), x_ref.shape[0], rc, chunk)


def _oproj(merged, x, merged_s, xs, w_o, g_post, g_pre, *, bm, rc):
    m, ms = x.shape[0], xs.shape[0]
    tile = pl.BlockSpec((bm, D_MODEL), lambda i: (i, 0))
    return pl.pallas_call(
        functools.partial(_oproj_kernel, rc),
        grid=(m // bm,),
        in_specs=[
            tile, tile, _resident(merged_s), _resident(xs),
            pl.BlockSpec((pl.Squeezed(), D_MODEL, D_MODEL), lambda i: (0, 0, 0),
                         pipeline_mode=pl.Buffered(1)),
            _resident(g_post), _resident(g_pre),
        ],
        out_specs=[tile, tile, _resident(xs), _resident(merged_s)],
        out_shape=[_sds((m, D_MODEL), F32), _sds((m, D_MODEL), BF16),
                   _sds((ms, D_MODEL), F32), _sds((ms, D_MODEL), BF16)],
        compiler_params=_params(1),
        name="oproj_norm",
    )(merged, x, merged_s, xs, w_o, g_post, g_pre)


def _ffn_up_kernel(tiles_per_seq, split, rc, h_ref, hs_ref, wa_ref, wg_ref, cw_ref, cb_ref, p2_ref, p1_ref, wd_ref,
                   o_ref, st_ref, os_ref, ups_ref, wd_bf_ref, wbf_ref, halo_ref):
    i = pl.program_id(1)
    wd_bf_ref[...] = wd_ref[...].astype(BF16)
    gw = wa_ref.shape[1] // split

    @pl.when(i == 0)
    def _():
        for grp in range(split):
            src = slice(grp * gw, (grp + 1) * gw)
            wbf_ref[:, 2 * grp * gw:(2 * grp + 1) * gw] = wa_ref[:, src].astype(BF16)
            wbf_ref[:, (2 * grp + 1) * gw:(2 * grp + 2) * gw] = wg_ref[:, src].astype(BF16)

    def project(rows, ride, pcols):
        return _dot(_ride(h_ref, rows, hs_ref, ride), wbf_ref[:, pcols])

    def group(rows, ride, grp, proj, shifts):
        n = rows.size
        cols = slice(grp * gw, (grp + 1) * gw)
        cur, gate = proj[:, 0:gw], proj[:, gw:2 * gw]
        cw = cw_ref[:, cols]

        def gated(cur, s1, s2, gate):
            act = jax.nn.gelu(_conv3(cw, cur, s1, s2) + cb_ref[:, cols], approximate=True)
            return (act * gate).astype(BF16)

        s1, s2 = shifts(cur[:n])
        o_ref[rows, cols] = gated(cur[:n], s1, s2, gate[:n])
        if ride:
            ups_ref[:, cols] = cur[n:]
            os_ref[:, cols] = gated(cur[n:], p1_ref[:, cols], p2_ref[:, cols], gate[n:])
        return cur[:n]

    _conv_tile(i, tiles_per_seq, split, 2, rc, h_ref.shape[0], halo_ref, st_ref, project, group)


def _ffn_up(h2, h2s, w_ffn_up, ffn_conv_w, ffn_conv_b, prev2, prev1, w_ffn_down, *, bm, bn, split, rc, seq_len):
    m, ms = h2.shape[0], h2s.shape[0]
    tiles_per_seq = seq_len // bm
    n_tiles = D_FF // bn
    n_m = m // bm
    slab = D_FF // (n_tiles * n_m)
    wslab = pl.BlockSpec((pl.Squeezed(), slab, D_MODEL), lambda j, i: (0, j * n_m + i, 0))
    tile = pl.BlockSpec((bm, bn), lambda j, i: (i, j))
    stile = pl.BlockSpec((ms, bn), lambda j, i: (0, j))
    return pl.pallas_call(
        functools.partial(_ffn_up_kernel, tiles_per_seq, split, rc),
        grid=(n_tiles, m // bm),
        in_specs=[
            pl.BlockSpec((bm, D_MODEL), lambda j, i: (i, 0)),
            _resident(h2s),
            pl.BlockSpec((pl.Squeezed(), D_MODEL, bn), lambda j, i: (0, 0, j)),
            pl.BlockSpec((pl.Squeezed(), D_MODEL, bn), lambda j, i: (0, 0, n_tiles + j)),
            pl.BlockSpec((pl.Squeezed(), 3, bn), lambda j, i: (0, 0, j)),
            pl.BlockSpec((1, bn), lambda j, i: (0, j)),
            stile, stile, wslab,
        ],
        out_specs=[
            tile,
            pl.BlockSpec((pl.Squeezed(), 2, bn), lambda j, i: (i // tiles_per_seq, 0, j)),
            stile, stile, wslab,
        ],
        out_shape=[_sds((m, D_FF), BF16), _sds((m // seq_len, 2, D_FF), F32),
                   _sds((ms, D_FF), BF16), _sds((ms, D_FF), F32), _sds(w_ffn_down.shape, BF16)],
        scratch_shapes=[pltpu.VMEM((D_MODEL, 2 * bn), BF16), pltpu.VMEM((2, bn), F32)],
        compiler_params=_params(2),
        name="ffn_up",
    )(h2, h2s, w_ffn_up, w_ffn_up, ffn_conv_w, ffn_conv_b, prev2, prev1, w_ffn_down)


def _ffn_down_kernel(rc, a_ref, as_ref, w_ref, o_ref, os_ref):
    def chunk(rows, ride):
        n = rows.size
        o = _dot(_ride(a_ref, rows, as_ref, ride), w_ref[...]).astype(BF16)
        o_ref[rows, :] = o[:n]
        if ride:
            os_ref[...] = o[n:]

    _walk_chunks(pl.program_id(1), a_ref.shape[0], rc, chunk)


def _ffn_down(act, act_s, w_ffn_down, *, bm, bn, rc):
    m, ms = act.shape[0], act_s.shape[0]
    return pl.pallas_call(
        functools.partial(_ffn_down_kernel, rc),
        grid=(D_MODEL // bn, m // bm),
        in_specs=[
            pl.BlockSpec((bm, D_FF), lambda j, i: (i, 0)),
            _resident(act_s),
            pl.BlockSpec((pl.Squeezed(), D_FF, bn), lambda j, i: (0, 0, j)),
        ],
        out_specs=[pl.BlockSpec((bm, bn), lambda j, i: (i, j)),
                   pl.BlockSpec((ms, bn), lambda j, i: (0, j))],
        out_shape=[_sds((m, D_MODEL), BF16), _sds((ms, D_MODEL), BF16)],
        compiler_params=_params(2),
        name="ffn_down",
    )(act, act_s, w_ffn_down)


def _final_kernel(rc, f_ref, x1_ref, p_ref, fs_ref, x1s_ref, ps_ref, g_ref, wg_ref, wp_ref, y_ref, ys_ref):
    def chunk(rows, ride):
        n = rows.size
        f = _ride(f_ref, rows, fs_ref, ride).astype(F32)
        x2 = _ride(x1_ref, rows, x1s_ref, ride) + _rms(f, g_ref[...])
        gate = jax.nn.sigmoid(_dot(x2.astype(BF16), wg_ref[...]))
        y = x2 + gate * _dot(_ride(p_ref, rows, ps_ref, ride).astype(BF16), wp_ref[...])
        y_ref[rows, :] = y[:n]
        if ride:
            ys_ref[...] = y[n:]

    _walk_chunks(pl.program_id(0), x1_ref.shape[0], rc, chunk)


def _final(f, x1, p, f_s, x1_s, p_s, g_post, w_ple_gate, w_ple, *, bm, rc):
    m, ms = x1.shape[0], x1_s.shape[0]
    row = lambda i: (i, 0)
    return pl.pallas_call(
        functools.partial(_final_kernel, rc),
        grid=(m // bm,),
        in_specs=[
            pl.BlockSpec((bm, D_MODEL), row),
            pl.BlockSpec((bm, D_MODEL), row),
            pl.BlockSpec((bm, PLE_DIM), row),
            _resident(f_s), _resident(x1_s), _resident(p_s),
            _resident(g_post),
            pl.BlockSpec((pl.Squeezed(), D_MODEL, D_MODEL), lambda i: (0, 0, 0),
                         pipeline_mode=pl.Buffered(1)),
            pl.BlockSpec((pl.Squeezed(), PLE_DIM, D_MODEL), lambda i: (0, 0, 0),
                         pipeline_mode=pl.Buffered(1)),
        ],
        out_specs=[pl.BlockSpec((bm, D_MODEL), row), _resident(x1_s)],
        out_shape=[_sds((m, D_MODEL), F32), _sds((ms, D_MODEL), F32)],
        compiler_params=_params(1),
        name="ffn_norm_ple",
    )(f, x1, p, f_s, x1_s, p_s, g_post, w_ple_gate, w_ple)


def _rope_tables(pos):
    half = HEAD_DIM // 2
    inv = ROPE_THETA ** (-np.arange(half, dtype=np.float64) / half)
    ang = np.asarray(pos, dtype=np.float64)[:, None] * inv[None, :]
    cos, sin = np.cos(ang), np.sin(ang)
    reps = LANES // HEAD_DIM
    return (jnp.asarray(np.tile(np.concatenate([cos, cos], axis=1), (1, reps)), dtype=F32),
            jnp.asarray(np.tile(np.concatenate([-sin, sin], axis=1), (1, reps)), dtype=F32))


TILES = dict(qkv=dict(bm=1024, rc=512), attn=128, convmix=dict(bm=1024, bn=512, split=2, rc=512),
             merge=dict(bm=1024, bn=1024, split=4, rc=1024), oproj=dict(bm=512, rc=256),
             ffn_up=dict(bm=2048, bn=512, split=1, rc=1024), ffn_down=dict(bm=1024, bn=512, rc=1024),
             final=dict(bm=512, rc=512))


def kernel(x_prompt, x_sample, state_attn_k, state_attn_v, state_conv, state_ffn_conv, p_prompt, p_sample,
           norm_mix_pre, norm_mix_post, norm_ffn_pre, norm_ffn_post, w_in, attn_sinks, conv_w, w_attn_branch,
           w_conv_branch, b_gate, w_o, w_ffn_up, ffn_conv_w, ffn_conv_b, w_ffn_down, w_ple, w_ple_gate):
    assert w_in.shape[0] == 1, "weights are indexed with a squeezed depth axis of size 1"
    batch, seq, _ = x_prompt.shape
    ns, dec_seq, _ = x_sample.shape
    assert dec_seq == 1
    m = batch * seq
    x = x_prompt.reshape(m, D_MODEL)
    xs = x_sample.reshape(ns, D_MODEL)
    rope_p = _rope_tables(np.arange(seq))
    rope_s = _rope_tables(np.full((ns,), PAST_LEN))

    h, q, k, v, hs, qs, ks, vs = _qkv(x, xs, norm_mix_pre, w_in, rope_p, rope_s, **TILES["qkv"])

    merge_tiles = TILES["merge"]
    a, w_o_bf, w_ple_gate_bf, w_ple_bf, w_attn_branch_bf, w_conv_branch_bf = _attn_prompt(
        attn_sinks.reshape(N_HEADS), q, k, v, w_o, w_ple_gate, w_ple, w_attn_branch, w_conv_branch,
        n_seq=batch, seq_len=seq, bq=TILES["attn"])
    a3, new_ks, new_vs = _attn_sample(
        attn_sinks.reshape(N_HEADS, 1), qs.reshape(ns, N_HEADS, HEAD_DIM), ks.reshape(ns, 1, KV_WIDTH),
        vs.reshape(ns, 1, KV_WIDTH), state_attn_k.reshape(ns, WINDOW, KV_WIDTH),
        state_attn_v.reshape(ns, WINDOW, KV_WIDTH))
    a_s = a3.reshape(ns, ATTN_WIDTH)

    b, new_conv_p, b_s, pre_s, w_gates_bf = _convmix(
        h, hs, w_in, conv_w, state_conv[0, :, 0], state_conv[0, :, 1], seq_len=seq,
        gate_group=merge_tiles["bn"] // merge_tiles["split"], **TILES["convmix"])
    merged, merged_s = _merge(h, a, b, hs, a_s, b_s, w_gates_bf, w_attn_branch_bf, w_conv_branch_bf, b_gate,
                              **merge_tiles)
    x1, h2, x1_s, h2_s = _oproj(merged, x, merged_s, xs, w_o_bf, norm_mix_post, norm_ffn_pre,
                                **TILES["oproj"])
    act, new_ffn_p, act_s, up_s, w_ffn_down_bf = _ffn_up(
        h2, h2_s, w_ffn_up, ffn_conv_w, ffn_conv_b, state_ffn_conv[0, :, 0], state_ffn_conv[0, :, 1], w_ffn_down,
        seq_len=seq, **TILES["ffn_up"])
    f, f_s = _ffn_down(act, act_s, w_ffn_down_bf, **TILES["ffn_down"])
    y, y_s = _final(f, x1, p_prompt.reshape(m, PLE_DIM), f_s, x1_s, p_sample.reshape(ns, PLE_DIM),
                    norm_ffn_post, w_ple_gate_bf, w_ple_bf, **TILES["final"])

    kv_tail = lambda t: t.reshape(batch, seq, KV_WIDTH)[:, seq - WINDOW:].reshape(batch, WINDOW, N_KV_HEADS, HEAD_DIM)
    kv_state = lambda t: t.reshape(ns, WINDOW, N_KV_HEADS, HEAD_DIM)
    return (y.reshape(batch, seq, D_MODEL), y_s.reshape(ns, dec_seq, D_MODEL),
            kv_tail(k)[None], kv_tail(v)[None], new_conv_p[None], new_ffn_p[None],
            kv_state(new_ks)[None], kv_state(new_vs)[None],
            jnp.stack([state_conv[0, :, 1], pre_s], axis=1)[None],
            jnp.stack([state_ffn_conv[0, :, 1], up_s], axis=1)[None])
```

```python
import functools

import jax
import jax.numpy as jnp
import numpy as np
from jax import lax
from jax.experimental import pallas as pl
from jax.experimental.pallas import tpu as pltpu

D_MODEL = 2048
N_HEADS = 16
N_KV_HEADS = 2
HEAD_DIM = 64
GROUP = N_HEADS // N_KV_HEADS
ATTN_WIDTH = N_HEADS * HEAD_DIM
KV_WIDTH = N_KV_HEADS * HEAD_DIM
QKV_WIDTH = ATTN_WIDTH + 2 * KV_WIDTH
WINDOW = 128
PAST_LEN = 16384
ROPE_THETA = 10000.0
CONV_WIDTH = D_MODEL // 2
D_FF = 5632
PLE_DIM = 256
RMS_EPS = 1e-6
OFF_CB = QKV_WIDTH
OFF_CC = OFF_CB + CONV_WIDTH
OFF_CU = OFF_CC + CONV_WIDTH
OFF_GA = OFF_CU + CONV_WIDTH

LANES = 128
UNIT_HEADS = GROUP
MASKED = -1e30
VMEM_LIMIT = 56 * 1024 * 1024

BF16 = jnp.bfloat16
F32 = jnp.float32


def _params(n_axes):
    return pltpu.CompilerParams(dimension_semantics=("arbitrary",) * n_axes,
                                vmem_limit_bytes=VMEM_LIMIT)


def _rms(x, g):
    return x * lax.rsqrt(jnp.mean(x * x, axis=-1, keepdims=True) + RMS_EPS) * g


def _dot(a, b):
    return jnp.dot(a, b, preferred_element_type=F32)


def _walk_chunks(i, bm, rc, chunk_fn):
    rc = min(bm, rc)

    def walk(first):
        for c in range(bm // rc):
            chunk_fn(pl.ds(c * rc, rc), first and c == 0)

    @pl.when(i == 0)
    def _():
        walk(True)

    @pl.when(i > 0)
    def _():
        walk(False)


def _ride(ref, rows, sample_ref, ride):
    x = ref[rows, :]
    return jnp.concatenate([x, sample_ref[...]], axis=0) if ride else x


def _cast_gate_slab(src_ref, dst_ref, group):
    for t in range(D_MODEL // group):
        src = slice(t * group, (t + 1) * group)
        dst_ref[:, 2 * t * group:(2 * t + 1) * group] = src_ref[:, src].astype(BF16)
        dst_ref[:, (2 * t + 1) * group:(2 * t + 2) * group] = (
            src_ref[:, D_MODEL + t * group:D_MODEL + (t + 1) * group].astype(BF16))


def _gate_slab_specs(slab, step_fn):
    gate_in = pl.BlockSpec((pl.Squeezed(), pl.Element(slab), pl.Element(2 * D_MODEL)),
                           lambda *g: (0, pl.multiple_of(step_fn(*g) * slab, slab), OFF_GA))
    gate_out = pl.BlockSpec((slab, 2 * D_MODEL), lambda *g: (step_fn(*g), 0))
    return gate_in, gate_out


def _wcol(k_rows, block_cols, col_fn):
    return pl.BlockSpec((pl.Squeezed(), pl.Element(k_rows), pl.Element(block_cols)),
                        lambda *g: (0, 0, pl.multiple_of(col_fn(*g), LANES)))


def _resident(arr):
    zeros = (0,) * arr.ndim
    return pl.BlockSpec(arr.shape, lambda *g: zeros)


def _sds(shape, dtype):
    return jax.ShapeDtypeStruct(shape, dtype)


def _qkv_kernel(rc, x_ref, xs_ref, g_ref, w_ref, cos_ref, sin_ref, coss_ref, sins_ref,
                h_ref, q_ref, k_ref, v_ref, hs_ref, qs_ref, ks_ref, vs_ref, wbf_ref):
    i = pl.program_id(0)
    n_q = ATTN_WIDTH // LANES

    @pl.when(i == 0)
    def _():
        wbf_ref[...] = w_ref[...].astype(BF16)

    def chunk(rows, ride):
        n = rows.size

        def put(ref, sample_ref, cols, val):
            ref[rows, cols] = val[:n]
            if ride:
                sample_ref[:, cols] = val[n:]

        h = _rms(_ride(x_ref, rows, xs_ref, ride), g_ref[...]).astype(BF16)
        put(h_ref, hs_ref, slice(None), h)
        proj = _dot(h, wbf_ref[...])
        cos = _ride(cos_ref, rows, coss_ref, ride)
        sin = _ride(sin_ref, rows, sins_ref, ride)
        lane = lax.broadcasted_iota(jnp.int32, cos.shape, 1)
        first_half = (lane % HEAD_DIM) < (HEAD_DIM // 2)
        for c in range(n_q + 1):
            cols = slice(c * LANES, (c + 1) * LANES)
            xg = proj[:, cols]
            partner = jnp.where(first_half, pltpu.roll(xg, LANES - HEAD_DIM // 2, 1),
                                pltpu.roll(xg, HEAD_DIM // 2, 1))
            yg = xg * cos + partner * sin
            if c < n_q:
                put(q_ref, qs_ref, cols, (yg * (HEAD_DIM ** -0.5)).astype(BF16))
            else:
                put(k_ref, ks_ref, slice(None), yg)
        put(v_ref, vs_ref, slice(None), proj[:, ATTN_WIDTH + KV_WIDTH:QKV_WIDTH])

    _walk_chunks(i, x_ref.shape[0], rc, chunk)


def _qkv(x, xs, g, w_in, rope_p, rope_s, *, bm, rc):
    m, ms = x.shape[0], xs.shape[0]
    t_tiles = rope_p[0].shape[0] // bm
    row = lambda i: (i, 0)
    pos = lambda i: (i % t_tiles, 0)
    outs = lambda rows: [_sds((rows, D_MODEL), BF16), _sds((rows, ATTN_WIDTH), BF16),
                         _sds((rows, KV_WIDTH), F32), _sds((rows, KV_WIDTH), F32)]
    out_p, out_s = outs(m), outs(ms)
    return pl.pallas_call(
        functools.partial(_qkv_kernel, rc),
        grid=(m // bm,),
        in_specs=[
            pl.BlockSpec((bm, D_MODEL), row),
            _resident(xs),
            _resident(g),
            pl.BlockSpec((pl.Squeezed(), D_MODEL, QKV_WIDTH), lambda i: (0, 0, 0),
                         pipeline_mode=pl.Buffered(1)),
            pl.BlockSpec((bm, LANES), pos),
            pl.BlockSpec((bm, LANES), pos),
            _resident(rope_s[0]),
            _resident(rope_s[1]),
        ],
        out_specs=[pl.BlockSpec((bm, s.shape[1]), row) for s in out_p] + [_resident(s) for s in out_s],
        out_shape=out_p + out_s,
        scratch_shapes=[pltpu.VMEM((D_MODEL, QKV_WIDTH), BF16)],
        compiler_params=_params(1),
        name="qkv_rope",
    )(x, xs, g, w_in, rope_p[0], rope_p[1], rope_s[0], rope_s[1])


def _attn_prompt_kernel(sinks_ref, q_ref, kc_ref, kp_ref, vc_ref, vp_ref, wo_ref, wg_ref, wp_ref, wa_ref, wb_ref,
                        o_ref, wo_bf_ref, wg_bf_ref, wp_bf_ref, wa_bf_ref, wb_bf_ref):
    wo_bf_ref[...] = wo_ref[...].astype(BF16)
    wg_bf_ref[...] = wg_ref[...].astype(BF16)
    wa_bf_ref[...] = wa_ref[...].astype(BF16)
    wb_bf_ref[...] = wb_ref[...].astype(BF16)

    @pl.when((pl.program_id(0) == 0) & (pl.program_id(1) == 0))
    def _():
        wp_bf_ref[...] = wp_ref[...].astype(BF16)

    n_sub = q_ref.shape[0] // WINDOW
    qi = lax.broadcasted_iota(jnp.int32, (WINDOW, WINDOW), 0)
    ci = lax.broadcasted_iota(jnp.int32, (WINDOW, WINDOW), 1)
    own = ci <= qi
    first_valid = own | (pl.program_id(1) > 0)
    k_all = jnp.concatenate([kp_ref[...], kc_ref[...]], axis=0).astype(BF16)
    v_all = jnp.concatenate([vp_ref[...], vc_ref[...]], axis=0).astype(BF16)
    units = [(sub, hg) for sub in range(n_sub) for hg in range(N_HEADS // UNIT_HEADS)]
    heads_of = lambda hg: range(hg * UNIT_HEADS, (hg + 1) * UNIT_HEADS)
    q_rows = lambda sub: slice(sub * WINDOW, (sub + 1) * WINDOW)
    kv_rows = lambda sub: slice(sub * WINDOW, (sub + 2) * WINDOW)
    kv_cols = lambda hg: slice((hg * UNIT_HEADS // GROUP) * HEAD_DIM, (hg * UNIT_HEADS // GROUP + 1) * HEAD_DIM)

    scores = []
    for sub, kh in units:
        qs = jnp.concatenate([q_ref[q_rows(sub), h * HEAD_DIM:(h + 1) * HEAD_DIM] for h in heads_of(kh)], axis=0)
        scores.append(lax.dot_general(qs, k_all[kv_rows(sub), kv_cols(kh)], (((1,), (1,)), ((), ())),
                                      preferred_element_type=F32))
    probs, dens = [], []
    for (sub, kh), s_all in zip(units, scores):
        ps, ds = [], []
        for g, h in enumerate(heads_of(kh)):
            s2 = s_all[g * WINDOW:(g + 1) * WINDOW, :]
            s = jnp.where(own, s2[:, WINDOW:], s2[:, :WINDOW])
            if sub == 0:
                s = jnp.where(first_valid, s, MASKED)
            sink = sinks_ref[h]
            m = jnp.maximum(jnp.max(s, axis=-1, keepdims=True), sink)
            p = jnp.exp(s - m)
            ds.append(jnp.sum(p, axis=-1, keepdims=True) + jnp.exp(sink - m))
            ps.append(jnp.concatenate([jnp.where(own, 0.0, p), jnp.where(own, p, 0.0)], axis=1).astype(BF16))
        probs.append(jnp.concatenate(ps, axis=0))
        dens.append(ds)
    outs = [_dot(p_all, v_all[kv_rows(sub), kv_cols(kh)]) for (sub, kh), p_all in zip(units, probs)]
    for (sub, kh), o_all, ds in zip(units, outs, dens):
        for g, h in enumerate(heads_of(kh)):
            o = o_all[g * WINDOW:(g + 1) * WINDOW, :] / ds[g]
            o_ref[q_rows(sub), h * HEAD_DIM:(h + 1) * HEAD_DIM] = o.astype(BF16)


def _attn_prompt(sinks, q, k, v, w_o, w_ple_gate, w_ple, w_attn_branch, w_conv_branch, *, n_seq, seq_len, bq):
    nb = seq_len // bq
    sub = bq // WINDOW
    steps = n_seq * nb
    slab, bslab = D_MODEL // steps, ATTN_WIDTH // steps
    cur = lambda n, b: (n * nb + b, 0)
    prev = lambda n, b: (jnp.maximum((n * nb + b) * sub - 1, 0), 0)
    wslab = pl.BlockSpec((pl.Squeezed(), slab, D_MODEL), lambda n, b: (0, n * nb + b, 0))
    bspec = pl.BlockSpec((pl.Squeezed(), bslab, D_MODEL), lambda n, b: (0, n * nb + b, 0))
    return pl.pallas_call(
        _attn_prompt_kernel,
        grid=(n_seq, nb),
        in_specs=[
            pl.BlockSpec(memory_space=pltpu.SMEM),
            pl.BlockSpec((bq, ATTN_WIDTH), cur),
            pl.BlockSpec((bq, KV_WIDTH), cur),
            pl.BlockSpec((WINDOW, KV_WIDTH), prev),
            pl.BlockSpec((bq, KV_WIDTH), cur),
            pl.BlockSpec((WINDOW, KV_WIDTH), prev),
            wslab, wslab, _resident(w_ple), bspec, bspec,
        ],
        out_specs=[pl.BlockSpec((bq, ATTN_WIDTH), cur), wslab, wslab, _resident(w_ple), bspec, bspec],
        out_shape=[_sds((n_seq * seq_len, ATTN_WIDTH), BF16), _sds(w_o.shape, BF16), _sds(w_ple_gate.shape, BF16),
                   _sds(w_ple.shape, BF16), _sds(w_attn_branch.shape, BF16), _sds(w_conv_branch.shape, BF16)],
        compiler_params=_params(2),
        name="attn_prompt",
    )(sinks, q, k, k, v, v, w_o, w_ple_gate, w_ple, w_attn_branch, w_conv_branch)


def _attn_sample_kernel(sinks_ref, q_ref, kn_ref, vn_ref, sk_ref, sv_ref, a_ref, nk_ref, nv_ref):
    nk_ref[:, 0:WINDOW - 1, :] = sk_ref[:, 1:WINDOW, :]
    nk_ref[:, WINDOW - 1:WINDOW, :] = kn_ref[...]
    nv_ref[:, 0:WINDOW - 1, :] = sv_ref[:, 1:WINDOW, :]
    nv_ref[:, WINDOW - 1:WINDOW, :] = vn_ref[...]
    for kh in range(N_KV_HEADS):
        cols = slice(kh * HEAD_DIM, (kh + 1) * HEAD_DIM)
        heads = slice(kh * GROUP, (kh + 1) * GROUP)
        k = nk_ref[:, :, cols].astype(BF16)
        v = nv_ref[:, :, cols].astype(BF16)
        s = jnp.einsum("ngd,ncd->ngc", q_ref[:, heads, :], k, preferred_element_type=F32)
        sink = sinks_ref[heads, :][None]
        m = jnp.maximum(jnp.max(s, axis=-1, keepdims=True), sink)
        p = jnp.exp(s - m)
        den = jnp.sum(p, axis=-1, keepdims=True) + jnp.exp(sink - m)
        o = jnp.einsum("ngc,ncd->ngd", p.astype(BF16), v, preferred_element_type=F32) / den
        a_ref[:, heads, :] = o.astype(BF16)


def _attn_sample(sinks_col, q3, k_new3, v_new3, state_k, state_v):
    n = q3.shape[0]
    return pl.pallas_call(
        _attn_sample_kernel,
        out_shape=[_sds((n, N_HEADS, HEAD_DIM), BF16), _sds((n, WINDOW, KV_WIDTH), F32),
                   _sds((n, WINDOW, KV_WIDTH), F32)],
        compiler_params=pltpu.CompilerParams(vmem_limit_bytes=VMEM_LIMIT),
        name="attn_sample",
    )(sinks_col, q3, k_new3, v_new3, state_k, state_v)


def _shift_rows(cur, halo):
    rid = lax.broadcasted_iota(jnp.int32, cur.shape, 0)
    h0 = halo[0:1, :]
    h1 = halo[1:2, :]
    s1 = jnp.where(rid == 0, h1, pltpu.roll(cur, 1, 0))
    s2 = jnp.where(rid == 0, h0, jnp.where(rid == 1, h1, pltpu.roll(cur, 2, 0)))
    return s1, s2


def _conv3(cw, cur, s1, s2):
    return cw[0:1, :] * s2 + cw[1:2, :] * s1 + cw[2:3, :] * cur


def _conv_tile(i, tiles_per_seq, split, parts, rc, bm, halo_ref, st_ref, project, group_fn):
    gw = halo_ref.shape[1] // split
    gcols = [slice(grp * gw, (grp + 1) * gw) for grp in range(split)]

    @pl.when(i % tiles_per_seq == 0)
    def _():
        halo_ref[...] = jnp.zeros_like(halo_ref)

    halos = {}

    def chunk(rows, ride):
        last = rows.start + rows.size == bm
        full = None if last else project(rows, ride, slice(None))
        for grp in range(split):
            pcols = slice(parts * grp * gw, parts * (grp + 1) * gw)
            proj = project(rows, ride, pcols) if last else full[:, pcols]
            halo = halos.get(grp)
            if halo is None:
                halo = halo_ref[:, gcols[grp]]
            cur = group_fn(rows, ride, grp, proj, lambda cur, halo=halo: _shift_rows(cur, halo))
            halos[grp] = cur[cur.shape[0] - 2:, :]
        if last:
            for grp in range(split):
                halo_ref[:, gcols[grp]] = halos.pop(grp)

    _walk_chunks(i, bm, rc, chunk)

    @pl.when(i % tiles_per_seq == tiles_per_seq - 1)
    def _():
        st_ref[...] = halo_ref[...]


def _convmix_kernel(tiles_per_seq, split, rc, gate_group, h_ref, hs_ref, wb_ref, wc_ref, wu_ref, cw_ref, p2_ref, p1_ref,
                    wgate_ref, o_ref, st_ref, os_ref, pres_ref, wgate_bf_ref, wbf_ref, halo_ref):
    i = pl.program_id(1)
    _cast_gate_slab(wgate_ref, wgate_bf_ref, gate_group)
    gw = wb_ref.shape[1] // split

    @pl.when(i == 0)
    def _():
        for grp in range(split):
            src = slice(grp * gw, (grp + 1) * gw)
            for part, w_ref in enumerate((wb_ref, wc_ref, wu_ref)):
                wbf_ref[:, (3 * grp + part) * gw:(3 * grp + part + 1) * gw] = w_ref[:, src].astype(BF16)

    def project(rows, ride, pcols):
        return _dot(_ride(h_ref, rows, hs_ref, ride), wbf_ref[:, pcols])

    def group(rows, ride, grp, proj, shifts):
        n = rows.size
        cols = slice(grp * gw, (grp + 1) * gw)
        cb, cur = proj[:, 0:gw], proj[:, gw:2 * gw] * proj[:, 2 * gw:3 * gw]
        cw = cw_ref[:, cols]
        s1, s2 = shifts(cur[:n])
        o_ref[rows, cols] = (cb[:n] * _conv3(cw, cur[:n], s1, s2)).astype(BF16)
        if ride:
            pres_ref[:, cols] = cur[n:]
            os_ref[:, cols] = (cb[n:] * _conv3(cw, cur[n:], p1_ref[:, cols], p2_ref[:, cols])).astype(BF16)
        return cur[:n]

    _conv_tile(i, tiles_per_seq, split, 3, rc, h_ref.shape[0], halo_ref, st_ref, project, group)


def _convmix(h, hs, w_in, conv_w, prev2, prev1, *, bm, bn, split, rc, seq_len, gate_group):
    m, ms = h.shape[0], hs.shape[0]
    tiles_per_seq = seq_len // bm
    n_m = m // bm
    gate_in, gate_out = _gate_slab_specs(D_MODEL // (CONV_WIDTH // bn * n_m), lambda j, i: j * n_m + i)
    tile = pl.BlockSpec((bm, bn), lambda j, i: (i, j))
    stile = pl.BlockSpec((ms, bn), lambda j, i: (0, j))
    wspec = lambda off: _wcol(D_MODEL, bn, lambda j, i: off + j * bn)
    return pl.pallas_call(
        functools.partial(_convmix_kernel, tiles_per_seq, split, rc, gate_group),
        grid=(CONV_WIDTH // bn, n_m),
        in_specs=[
            pl.BlockSpec((bm, D_MODEL), lambda j, i: (i, 0)),
            _resident(hs),
            wspec(OFF_CB), wspec(OFF_CC), wspec(OFF_CU),
            pl.BlockSpec((pl.Squeezed(), 3, bn), lambda j, i: (0, 0, j)),
            stile, stile, gate_in,
        ],
        out_specs=[
            tile,
            pl.BlockSpec((pl.Squeezed(), 2, bn), lambda j, i: (i // tiles_per_seq, 0, j)),
            stile, stile, gate_out,
        ],
        out_shape=[_sds((m, CONV_WIDTH), BF16), _sds((m // seq_len, 2, CONV_WIDTH), F32),
                   _sds((ms, CONV_WIDTH), BF16), _sds((ms, CONV_WIDTH), F32), _sds((D_MODEL, 2 * D_MODEL), BF16)],
        scratch_shapes=[pltpu.VMEM((D_MODEL, 3 * bn), BF16), pltpu.VMEM((2, bn), F32)],
        compiler_params=_params(2),
        name="convmix",
    )(h, hs, w_in, w_in, w_in, conv_w, prev2, prev1, w_in)


def _merge_kernel(split, rc, h_ref, a_ref, b_ref, hs_ref, as_ref, bs_ref, wg_ref, wa_ref, wb_ref, bg_ref,
                  o_ref, os_ref):
    gw = wa_ref.shape[1] // split

    def chunk(rows, ride):
        n = rows.size
        h = _ride(h_ref, rows, hs_ref, ride)
        a = _ride(a_ref, rows, as_ref, ride)
        b = _ride(b_ref, rows, bs_ref, ride)
        for grp in range(split):
            cols = slice(grp * gw, (grp + 1) * gw)
            gates = _dot(h, wg_ref[:, 2 * grp * gw:2 * (grp + 1) * gw])
            ga = jax.nn.sigmoid(gates[:, 0:gw] + bg_ref[0:1, cols])
            gb = jax.nn.sigmoid(gates[:, gw:2 * gw] + bg_ref[1:2, cols])
            merged = (ga * _dot(a, wa_ref[:, cols]) + gb * _dot(b, wb_ref[:, cols])).astype(BF16)
            o_ref[rows, cols] = merged[:n]
            if ride:
                os_ref[:, cols] = merged[n:]

    _walk_chunks(pl.program_id(1), h_ref.shape[0], rc, chunk)


def _merge(h, a, b, hs, a_s, b_s, w_gates, w_attn_branch, w_conv_branch, b_gate, *, bm, bn, split, rc):
    m, ms = h.shape[0], hs.shape[0]
    row = lambda j, i: (i, 0)
    return pl.pallas_call(
        functools.partial(_merge_kernel, split, rc),
        grid=(D_MODEL // bn, m // bm),
        in_specs=[
            pl.BlockSpec((bm, D_MODEL), row),
            pl.BlockSpec((bm, ATTN_WIDTH), row),
            pl.BlockSpec((bm, CONV_WIDTH), row),
            _resident(hs), _resident(a_s), _resident(b_s),
            pl.BlockSpec((D_MODEL, 2 * bn), lambda j, i: (0, j)),
            pl.BlockSpec((pl.Squeezed(), ATTN_WIDTH, bn), lambda j, i: (0, 0, j)),
            pl.BlockSpec((pl.Squeezed(), CONV_WIDTH, bn), lambda j, i: (0, 0, j)),
            pl.BlockSpec((pl.Squeezed(), 2, bn), lambda j, i: (0, 0, j)),
        ],
        out_specs=[pl.BlockSpec((bm, bn), lambda j, i: (i, j)),
                   pl.BlockSpec((ms, bn), lambda j, i: (0, j))],
        out_shape=[_sds((m, D_MODEL), BF16), _sds((ms, D_MODEL), BF16)],
        compiler_params=_params(2),
        name="merge",
    )(h, a, b, hs, a_s, b_s, w_gates, w_attn_branch, w_conv_branch, b_gate)


def _oproj_kernel(rc, mg_ref, x_ref, mgs_ref, xs_ref, wo_ref, gpost_ref, gpre_ref,
                  x1_ref, h2_ref, x1s_ref, h2s_ref):
    def chunk(rows, ride):
        n = rows.size
        y = _dot(_ride(mg_ref, rows, mgs_ref, ride), wo_ref[...])
        x1 = _ride(x_ref, rows, xs_ref, ride) + _rms(y, gpost_ref[...])
        h2 = _rms(x1, gpre_ref[...]).astype(BF16)
        x1_ref[rows, :] = x1[:n]
        h2_ref[rows, :] = h2[:n]
        if ride:
            x1s_ref[...] = x1[n:]
            h2s_ref[...] = h2[n:]

    _walk_chunks(pl.program_id(0), x_ref.shape[0], rc, chunk)


def _oproj(merged, x, merged_s, xs, w_o, g_post, g_pre, *, bm, rc):
    m, ms = x.shape[0], xs.shape[0]
    tile = pl.BlockSpec((bm, D_MODEL), lambda i: (i, 0))
    return pl.pallas_call(
        functools.partial(_oproj_kernel, rc),
        grid=(m // bm,),
        in_specs=[
            tile, tile, _resident(merged_s), _resident(xs),
            pl.BlockSpec((pl.Squeezed(), D_MODEL, D_MODEL), lambda i: (0, 0, 0),
                         pipeline_mode=pl.Buffered(1)),
            _resident(g_post), _resident(g_pre),
        ],
        out_specs=[tile, tile, _resident(xs), _resident(merged_s)],
        out_shape=[_sds((m, D_MODEL), F32), _sds((m, D_MODEL), BF16),
                   _sds((ms, D_MODEL), F32), _sds((ms, D_MODEL), BF16)],
        compiler_params=_params(1),
        name="oproj_norm",
    )(merged, x, merged_s, xs, w_o, g_post, g_pre)


def _ffn_up_kernel(tiles_per_seq, split, rc, h_ref, hs_ref, wa_ref, wg_ref, cw_ref, cb_ref, p2_ref, p1_ref, wd_ref,
                   o_ref, st_ref, os_ref, ups_ref, wd_bf_ref, wbf_ref, halo_ref):
    i = pl.program_id(1)
    wd_bf_ref[...] = wd_ref[...].astype(BF16)
    gw = wa_ref.shape[1] // split

    @pl.when(i == 0)
    def _():
        for grp in range(split):
            src = slice(grp * gw, (grp + 1) * gw)
            wbf_ref[:, 2 * grp * gw:(2 * grp + 1) * gw] = wa_ref[:, src].astype(BF16)
            wbf_ref[:, (2 * grp + 1) * gw:(2 * grp + 2) * gw] = wg_ref[:, src].astype(BF16)

    def project(rows, ride, pcols):
        return _dot(_ride(h_ref, rows, hs_ref, ride), wbf_ref[:, pcols])

    def group(rows, ride, grp, proj, shifts):
        n = rows.size
        cols = slice(grp * gw, (grp + 1) * gw)
        cur, gate = proj[:, 0:gw], proj[:, gw:2 * gw]
        cw = cw_ref[:, cols]

        def gated(cur, s1, s2, gate):
            act = jax.nn.gelu(_conv3(cw, cur, s1, s2) + cb_ref[:, cols], approximate=True)
            return (act * gate).astype(BF16)

        s1, s2 = shifts(cur[:n])
        o_ref[rows, cols] = gated(cur[:n], s1, s2, gate[:n])
        if ride:
            ups_ref[:, cols] = cur[n:]
            os_ref[:, cols] = gated(cur[n:], p1_ref[:, cols], p2_ref[:, cols], gate[n:])
        return cur[:n]

    _conv_tile(i, tiles_per_seq, split, 2, rc, h_ref.shape[0], halo_ref, st_ref, project, group)


def _ffn_up(h2, h2s, w_ffn_up, ffn_conv_w, ffn_conv_b, prev2, prev1, w_ffn_down, *, bm, bn, split, rc, seq_len):
    m, ms = h2.shape[0], h2s.shape[0]
    tiles_per_seq = seq_len // bm
    n_tiles = D_FF // bn
    n_m = m // bm
    slab = D_FF // (n_tiles * n_m)
    wslab = pl.BlockSpec((pl.Squeezed(), slab, D_MODEL), lambda j, i: (0, j * n_m + i, 0))
    tile = pl.BlockSpec((bm, bn), lambda j, i: (i, j))
    stile = pl.BlockSpec((ms, bn), lambda j, i: (0, j))
    return pl.pallas_call(
        functools.partial(_ffn_up_kernel, tiles_per_seq, split, rc),
        grid=(n_tiles, m // bm),
        in_specs=[
            pl.BlockSpec((bm, D_MODEL), lambda j, i: (i, 0)),
            _resident(h2s),
            pl.BlockSpec((pl.Squeezed(), D_MODEL, bn), lambda j, i: (0, 0, j)),
            pl.BlockSpec((pl.Squeezed(), D_MODEL, bn), lambda j, i: (0, 0, n_tiles + j)),
            pl.BlockSpec((pl.Squeezed(), 3, bn), lambda j, i: (0, 0, j)),
            pl.BlockSpec((1, bn), lambda j, i: (0, j)),
            stile, stile, wslab,
        ],
        out_specs=[
            tile,
            pl.BlockSpec((pl.Squeezed(), 2, bn), lambda j, i: (i // tiles_per_seq, 0, j)),
            stile, stile, wslab,
        ],
        out_shape=[_sds((m, D_FF), BF16), _sds((m // seq_len, 2, D_FF), F32),
                   _sds((ms, D_FF), BF16), _sds((ms, D_FF), F32), _sds(w_ffn_down.shape, BF16)],
        scratch_shapes=[pltpu.VMEM((D_MODEL, 2 * bn), BF16), pltpu.VMEM((2, bn), F32)],
        compiler_params=_params(2),
        name="ffn_up",
    )(h2, h2s, w_ffn_up, w_ffn_up, ffn_conv_w, ffn_conv_b, prev2, prev1, w_ffn_down)


def _ffn_down_kernel(rc, a_ref, as_ref, w_ref, o_ref, os_ref):
    def chunk(rows, ride):
        n = rows.size
        o = _dot(_ride(a_ref, rows, as_ref, ride), w_ref[...]).astype(BF16)
        o_ref[rows, :] = o[:n]
        if ride:
            os_ref[...] = o[n:]

    _walk_chunks(pl.program_id(1), a_ref.shape[0], rc, chunk)


def _ffn_down(act, act_s, w_ffn_down, *, bm, bn, rc):
    m, ms = act.shape[0], act_s.shape[0]
    return pl.pallas_call(
        functools.partial(_ffn_down_kernel, rc),
        grid=(D_MODEL // bn, m // bm),
        in_specs=[
            pl.BlockSpec((bm, D_FF), lambda j, i: (i, 0)),
            _resident(act_s),
            pl.BlockSpec((pl.Squeezed(), D_FF, bn), lambda j, i: (0, 0, j)),
        ],
        out_specs=[pl.BlockSpec((bm, bn), lambda j, i: (i, j)),
                   pl.BlockSpec((ms, bn), lambda j, i: (0, j))],
        out_shape=[_sds((m, D_MODEL), BF16), _sds((ms, D_MODEL), BF16)],
        compiler_params=_params(2),
        name="ffn_down",
    )(act, act_s, w_ffn_down)


def _final_kernel(rc, f_ref, x1_ref, p_ref, fs_ref, x1s_ref, ps_ref, g_ref, wg_ref, wp_ref, y_ref, ys_ref):
    def chunk(rows, ride):
        n = rows.size
        f = _ride(f_ref, rows, fs_ref, ride).astype(F32)
        x2 = _ride(x1_ref, rows, x1s_ref, ride) + _rms(f, g_ref[...])
        gate = jax.nn.sigmoid(_dot(x2.astype(BF16), wg_ref[...]))
        y = x2 + gate * _dot(_ride(p_ref, rows, ps_ref, ride).astype(BF16), wp_ref[...])
        y_ref[rows, :] = y[:n]
        if ride:
            ys_ref[...] = y[n:]

    _walk_chunks(pl.program_id(0), x1_ref.shape[0], rc, chunk)


def _final(f, x1, p, f_s, x1_s, p_s, g_post, w_ple_gate, w_ple, *, bm, rc):
    m, ms = x1.shape[0], x1_s.shape[0]
    row = lambda i: (i, 0)
    return pl.pallas_call(
        functools.partial(_final_kernel, rc),
        grid=(m // bm,),
        in_specs=[
            pl.BlockSpec((bm, D_MODEL), row),
            pl.BlockSpec((bm, D_MODEL), row),
            pl.BlockSpec((bm, PLE_DIM), row),
            _resident(f_s), _resident(x1_s), _resident(p_s),
            _resident(g_post),
            pl.BlockSpec((pl.Squeezed(), D_MODEL, D_MODEL), lambda i: (0, 0, 0),
                         pipeline_mode=pl.Buffered(1)),
            pl.BlockSpec((pl.Squeezed(), PLE_DIM, D_MODEL), lambda i: (0, 0, 0),
                         pipeline_mode=pl.Buffered(1)),
        ],
        out_specs=[pl.BlockSpec((bm, D_MODEL), row), _resident(x1_s)],
        out_shape=[_sds((m, D_MODEL), F32), _sds((ms, D_MODEL), F32)],
        compiler_params=_params(1),
        name="ffn_norm_ple",
    )(f, x1, p, f_s, x1_s, p_s, g_post, w_ple_gate, w_ple)


def _rope_tables(pos):
    half = HEAD_DIM // 2
    inv = ROPE_THETA ** (-np.arange(half, dtype=np.float64) / half)
    ang = np.asarray(pos, dtype=np.float64)[:, None] * inv[None, :]
    cos, sin = np.cos(ang), np.sin(ang)
    reps = LANES // HEAD_DIM
    return (jnp.asarray(np.tile(np.concatenate([cos, cos], axis=1), (1, reps)), dtype=F32),
            jnp.asarray(np.tile(np.concatenate([-sin, sin], axis=1), (1, reps)), dtype=F32))


TILES = dict(qkv=dict(bm=1024, rc=256), attn=128, convmix=dict(bm=1024, bn=512, split=2, rc=512),
             merge=dict(bm=1024, bn=1024, split=4, rc=1024), oproj=dict(bm=512, rc=128),
             ffn_up=dict(bm=2048, bn=512, split=1, rc=1024), ffn_down=dict(bm=1024, bn=512, rc=1024),
             final=dict(bm=512, rc=512))


def kernel(x_prompt, x_sample, state_attn_k, state_attn_v, state_conv, state_ffn_conv, p_prompt, p_sample,
           norm_mix_pre, norm_mix_post, norm_ffn_pre, norm_ffn_post, w_in, attn_sinks, conv_w, w_attn_branch,
           w_conv_branch, b_gate, w_o, w_ffn_up, ffn_conv_w, ffn_conv_b, w_ffn_down, w_ple, w_ple_gate):
    assert w_in.shape[0] == 1, "weights are indexed with a squeezed depth axis of size 1"
    batch, seq, _ = x_prompt.shape
    ns, dec_seq, _ = x_sample.shape
    assert dec_seq == 1
    m = batch * seq
    x = x_prompt.reshape(m, D_MODEL)
    xs = x_sample.reshape(ns, D_MODEL)
    rope_p = _rope_tables(np.arange(seq))
    rope_s = _rope_tables(np.full((ns,), PAST_LEN))

    h, q, k, v, hs, qs, ks, vs = _qkv(x, xs, norm_mix_pre, w_in, rope_p, rope_s, **TILES["qkv"])

    merge_tiles = TILES["merge"]
    a, w_o_bf, w_ple_gate_bf, w_ple_bf, w_attn_branch_bf, w_conv_branch_bf = _attn_prompt(
        attn_sinks.reshape(N_HEADS), q, k, v, w_o, w_ple_gate, w_ple, w_attn_branch, w_conv_branch,
        n_seq=batch, seq_len=seq, bq=TILES["attn"])
    a3, new_ks, new_vs = _attn_sample(
        attn_sinks.reshape(N_HEADS, 1), qs.reshape(ns, N_HEADS, HEAD_DIM), ks.reshape(ns, 1, KV_WIDTH),
        vs.reshape(ns, 1, KV_WIDTH), state_attn_k.reshape(ns, WINDOW, KV_WIDTH),
        state_attn_v.reshape(ns, WINDOW, KV_WIDTH))
    a_s = a3.reshape(ns, ATTN_WIDTH)

    b, new_conv_p, b_s, pre_s, w_gates_bf = _convmix(
        h, hs, w_in, conv_w, state_conv[0, :, 0], state_conv[0, :, 1], seq_len=seq,
        gate_group=merge_tiles["bn"] // merge_tiles["split"], **TILES["convmix"])
    merged, merged_s = _merge(h, a, b, hs, a_s, b_s, w_gates_bf, w_attn_branch_bf, w_conv_branch_bf, b_gate,
                              **merge_tiles)
    x1, h2, x1_s, h2_s = _oproj(merged, x, merged_s, xs, w_o_bf, norm_mix_post, norm_ffn_pre,
                                **TILES["oproj"])
    act, new_ffn_p, act_s, up_s, w_ffn_down_bf = _ffn_up(
        h2, h2_s, w_ffn_up, ffn_conv_w, ffn_conv_b, state_ffn_conv[0, :, 0], state_ffn_conv[0, :, 1], w_ffn_down,
        seq_len=seq, **TILES["ffn_up"])
    f, f_s = _ffn_down(act, act_s, w_ffn_down_bf, **TILES["ffn_down"])
    y, y_s = _final(f, x1, p_prompt.reshape(m, PLE_DIM), f_s, x1_s, p_sample.reshape(ns, PLE_DIM),
                    norm_ffn_post, w_ple_gate_bf, w_ple_bf, **TILES["final"])

    kv_tail = lambda t: t.reshape(batch, seq, KV_WIDTH)[:, seq - WINDOW:].reshape(batch, WINDOW, N_KV_HEADS, HEAD_DIM)
    kv_state = lambda t: t.reshape(ns, WINDOW, N_KV_HEADS, HEAD_DIM)
    return (y.reshape(batch, seq, D_MODEL), y_s.reshape(ns, dec_seq, D_MODEL),
            kv_tail(k)[None], kv_tail(v)[None], new_conv_p[None], new_ffn_p[None],
            kv_state(new_ks)[None], kv_state(new_vs)[None],
            jnp.stack([state_conv[0, :, 1], pre_s], axis=1)[None],
            jnp.stack([state_ffn_conv[0, :, 1], up_s], axis=1)[None])
```

```python
import functools

import jax
import jax.numpy as jnp
import numpy as np
from jax import lax
from jax.experimental import pallas as pl
from jax.experimental.pallas import tpu as pltpu

D_MODEL = 2048
N_HEADS = 16
N_KV_HEADS = 2
HEAD_DIM = 64
GROUP = N_HEADS // N_KV_HEADS
ATTN_WIDTH = N_HEADS * HEAD_DIM
KV_WIDTH = N_KV_HEADS * HEAD_DIM
QKV_WIDTH = ATTN_WIDTH + 2 * KV_WIDTH
WINDOW = 128
PAST_LEN = 16384
ROPE_THETA = 10000.0
CONV_WIDTH = D_MODEL // 2
D_FF = 5632
PLE_DIM = 256
RMS_EPS = 1e-6
OFF_CB = QKV_WIDTH
OFF_CC = OFF_CB + CONV_WIDTH
OFF_CU = OFF_CC + CONV_WIDTH
OFF_GA = OFF_CU + CONV_WIDTH

LANES = 128
UNIT_HEADS = GROUP
MASKED = -1e30
VMEM_LIMIT = 56 * 1024 * 1024

BF16 = jnp.bfloat16
F32 = jnp.float32


def _params(n_axes):
    return pltpu.CompilerParams(dimension_semantics=("arbitrary",) * n_axes,
                                vmem_limit_bytes=VMEM_LIMIT)


def _rms(x, g):
    return x * lax.rsqrt(jnp.mean(x * x, axis=-1, keepdims=True) + RMS_EPS) * g


def _dot(a, b):
    return jnp.dot(a, b, preferred_element_type=F32)


def _walk_chunks(i, bm, rc, chunk_fn):
    rc = min(bm, rc)

    def walk(first):
        for c in range(bm // rc):
            chunk_fn(pl.ds(c * rc, rc), first and c == 0)

    @pl.when(i == 0)
    def _():
        walk(True)

    @pl.when(i > 0)
    def _():
        walk(False)


def _ride(ref, rows, sample_ref, ride):
    x = ref[rows, :]
    return jnp.concatenate([x, sample_ref[...]], axis=0) if ride else x


def _cast_gate_slab(src_ref, dst_ref, group):
    for t in range(D_MODEL // group):
        src = slice(t * group, (t + 1) * group)
        dst_ref[:, 2 * t * group:(2 * t + 1) * group] = src_ref[:, src].astype(BF16)
        dst_ref[:, (2 * t + 1) * group:(2 * t + 2) * group] = (
            src_ref[:, D_MODEL + t * group:D_MODEL + (t + 1) * group].astype(BF16))


def _gate_slab_specs(slab, step_fn):
    gate_in = pl.BlockSpec((pl.Squeezed(), pl.Element(slab), pl.Element(2 * D_MODEL)),
                           lambda *g: (0, pl.multiple_of(step_fn(*g) * slab, slab), OFF_GA))
    gate_out = pl.BlockSpec((slab, 2 * D_MODEL), lambda *g: (step_fn(*g), 0))
    return gate_in, gate_out


def _wcol(k_rows, block_cols, col_fn):
    return pl.BlockSpec((pl.Squeezed(), pl.Element(k_rows), pl.Element(block_cols)),
                        lambda *g: (0, 0, pl.multiple_of(col_fn(*g), LANES)))


def _resident(arr):
    zeros = (0,) * arr.ndim
    return pl.BlockSpec(arr.shape, lambda *g: zeros)


def _sds(shape, dtype):
    return jax.ShapeDtypeStruct(shape, dtype)


def _qkv_kernel(rc, x_ref, xs_ref, g_ref, w_ref, cos_ref, sin_ref, coss_ref, sins_ref,
                h_ref, q_ref, k_ref, v_ref, hs_ref, qs_ref, ks_ref, vs_ref, wbf_ref):
    i = pl.program_id(0)
    n_q = ATTN_WIDTH // LANES

    @pl.when(i == 0)
    def _():
        wbf_ref[...] = w_ref[...].astype(BF16)

    def chunk(rows, ride):
        n = rows.size

        def put(ref, sample_ref, cols, val):
            ref[rows, cols] = val[:n]
            if ride:
                sample_ref[:, cols] = val[n:]

        h = _rms(_ride(x_ref, rows, xs_ref, ride), g_ref[...]).astype(BF16)
        put(h_ref, hs_ref, slice(None), h)
        proj = _dot(h, wbf_ref[...])
        cos = _ride(cos_ref, rows, coss_ref, ride)
        sin = _ride(sin_ref, rows, sins_ref, ride)
        lane = lax.broadcasted_iota(jnp.int32, cos.shape, 1)
        first_half = (lane % HEAD_DIM) < (HEAD_DIM // 2)
        for c in range(n_q + 1):
            cols = slice(c * LANES, (c + 1) * LANES)
            xg = proj[:, cols]
            partner = jnp.where(first_half, pltpu.roll(xg, LANES - HEAD_DIM // 2, 1),
                                pltpu.roll(xg, HEAD_DIM // 2, 1))
            yg = xg * cos + partner * sin
            if c < n_q:
                put(q_ref, qs_ref, cols, (yg * (HEAD_DIM ** -0.5)).astype(BF16))
            else:
                put(k_ref, ks_ref, slice(None), yg)
        put(v_ref, vs_ref, slice(None), proj[:, ATTN_WIDTH + KV_WIDTH:QKV_WIDTH])

    _walk_chunks(i, x_ref.shape[0], rc, chunk)


def _qkv(x, xs, g, w_in, rope_p, rope_s, *, bm, rc):
    m, ms = x.shape[0], xs.shape[0]
    t_tiles = rope_p[0].shape[0] // bm
    row = lambda i: (i, 0)
    pos = lambda i: (i % t_tiles, 0)
    outs = lambda rows: [_sds((rows, D_MODEL), BF16), _sds((rows, ATTN_WIDTH), BF16),
                         _sds((rows, KV_WIDTH), F32), _sds((rows, KV_WIDTH), F32)]
    out_p, out_s = outs(m), outs(ms)
    return pl.pallas_call(
        functools.partial(_qkv_kernel, rc),
        grid=(m // bm,),
        in_specs=[
            pl.BlockSpec((bm, D_MODEL), row),
            _resident(xs),
            _resident(g),
            pl.BlockSpec((pl.Squeezed(), D_MODEL, QKV_WIDTH), lambda i: (0, 0, 0),
                         pipeline_mode=pl.Buffered(1)),
            pl.BlockSpec((bm, LANES), pos),
            pl.BlockSpec((bm, LANES), pos),
            _resident(rope_s[0]),
            _resident(rope_s[1]),
        ],
        out_specs=[pl.BlockSpec((bm, s.shape[1]), row) for s in out_p] + [_resident(s) for s in out_s],
        out_shape=out_p + out_s,
        scratch_shapes=[pltpu.VMEM((D_MODEL, QKV_WIDTH), BF16)],
        compiler_params=_params(1),
        name="qkv_rope",
    )(x, xs, g, w_in, rope_p[0], rope_p[1], rope_s[0], rope_s[1])


def _attn_prompt_kernel(sinks_ref, q_ref, kc_ref, kp_ref, vc_ref, vp_ref, wo_ref, wg_ref, wp_ref, wa_ref, wb_ref,
                        o_ref, wo_bf_ref, wg_bf_ref, wp_bf_ref, wa_bf_ref, wb_bf_ref):
    wo_bf_ref[...] = wo_ref[...].astype(BF16)
    wg_bf_ref[...] = wg_ref[...].astype(BF16)
    wa_bf_ref[...] = wa_ref[...].astype(BF16)
    wb_bf_ref[...] = wb_ref[...].astype(BF16)

    @pl.when((pl.program_id(0) == 0) & (pl.program_id(1) == 0))
    def _():
        wp_bf_ref[...] = wp_ref[...].astype(BF16)

    n_sub = q_ref.shape[0] // WINDOW
    qi = lax.broadcasted_iota(jnp.int32, (WINDOW, WINDOW), 0)
    ci = lax.broadcasted_iota(jnp.int32, (WINDOW, WINDOW), 1)
    own = ci <= qi
    first_valid = own | (pl.program_id(1) > 0)
    k_all = jnp.concatenate([kp_ref[...], kc_ref[...]], axis=0).astype(BF16)
    v_all = jnp.concatenate([vp_ref[...], vc_ref[...]], axis=0).astype(BF16)
    units = [(sub, hg) for sub in range(n_sub) for hg in range(N_HEADS // UNIT_HEADS)]
    heads_of = lambda hg: range(hg * UNIT_HEADS, (hg + 1) * UNIT_HEADS)
    q_rows = lambda sub: slice(sub * WINDOW, (sub + 1) * WINDOW)
    kv_rows = lambda sub: slice(sub * WINDOW, (sub + 2) * WINDOW)
    kv_cols = lambda hg: slice((hg * UNIT_HEADS // GROUP) * HEAD_DIM, (hg * UNIT_HEADS // GROUP + 1) * HEAD_DIM)

    scores = []
    for sub, kh in units:
        qs = jnp.concatenate([q_ref[q_rows(sub), h * HEAD_DIM:(h + 1) * HEAD_DIM] for h in heads_of(kh)], axis=0)
        scores.append(lax.dot_general(qs, k_all[kv_rows(sub), kv_cols(kh)], (((1,), (1,)), ((), ())),
                                      preferred_element_type=F32))
    probs, dens = [], []
    for (sub, kh), s_all in zip(units, scores):
        ps, ds = [], []
        for g, h in enumerate(heads_of(kh)):
            s2 = s_all[g * WINDOW:(g + 1) * WINDOW, :]
            s = jnp.where(own, s2[:, WINDOW:], s2[:, :WINDOW])
            if sub == 0:
                s = jnp.where(first_valid, s, MASKED)
            sink = sinks_ref[h]
            m = jnp.maximum(jnp.max(s, axis=-1, keepdims=True), sink)
            p = jnp.exp(s - m)
            ds.append(jnp.sum(p, axis=-1, keepdims=True) + jnp.exp(sink - m))
            ps.append(jnp.concatenate([jnp.where(own, 0.0, p), jnp.where(own, p, 0.0)], axis=1).astype(BF16))
        probs.append(jnp.concatenate(ps, axis=0))
        dens.append(ds)
    outs = [_dot(p_all, v_all[kv_rows(sub), kv_cols(kh)]) for (sub, kh), p_all in zip(units, probs)]
    for (sub, kh), o_all, ds in zip(units, outs, dens):
        for g, h in enumerate(heads_of(kh)):
            o = o_all[g * WINDOW:(g + 1) * WINDOW, :] / ds[g]
            o_ref[q_rows(sub), h * HEAD_DIM:(h + 1) * HEAD_DIM] = o.astype(BF16)


def _attn_prompt(sinks, q, k, v, w_o, w_ple_gate, w_ple, w_attn_branch, w_conv_branch, *, n_seq, seq_len, bq):
    nb = seq_len // bq
    sub = bq // WINDOW
    steps = n_seq * nb
    slab, bslab = D_MODEL // steps, ATTN_WIDTH // steps
    cur = lambda n, b: (n * nb + b, 0)
    prev = lambda n, b: (jnp.maximum((n * nb + b) * sub - 1, 0), 0)
    wslab = pl.BlockSpec((pl.Squeezed(), slab, D_MODEL), lambda n, b: (0, n * nb + b, 0))
    bspec = pl.BlockSpec((pl.Squeezed(), bslab, D_MODEL), lambda n, b: (0, n * nb + b, 0))
    return pl.pallas_call(
        _attn_prompt_kernel,
        grid=(n_seq, nb),
        in_specs=[
            pl.BlockSpec(memory_space=pltpu.SMEM),
            pl.BlockSpec((bq, ATTN_WIDTH), cur),
            pl.BlockSpec((bq, KV_WIDTH), cur),
            pl.BlockSpec((WINDOW, KV_WIDTH), prev),
            pl.BlockSpec((bq, KV_WIDTH), cur),
            pl.BlockSpec((WINDOW, KV_WIDTH), prev),
            wslab, wslab, _resident(w_ple), bspec, bspec,
        ],
        out_specs=[pl.BlockSpec((bq, ATTN_WIDTH), cur), wslab, wslab, _resident(w_ple), bspec, bspec],
        out_shape=[_sds((n_seq * seq_len, ATTN_WIDTH), BF16), _sds(w_o.shape, BF16), _sds(w_ple_gate.shape, BF16),
                   _sds(w_ple.shape, BF16), _sds(w_attn_branch.shape, BF16), _sds(w_conv_branch.shape, BF16)],
        compiler_params=_params(2),
        name="attn_prompt",
    )(sinks, q, k, k, v, v, w_o, w_ple_gate, w_ple, w_attn_branch, w_conv_branch)


def _attn_sample_kernel(sinks_ref, q_ref, kn_ref, vn_ref, sk_ref, sv_ref, a_ref, nk_ref, nv_ref):
    nk_ref[:, 0:WINDOW - 1, :] = sk_ref[:, 1:WINDOW, :]
    nk_ref[:, WINDOW - 1:WINDOW, :] = kn_ref[...]
    nv_ref[:, 0:WINDOW - 1, :] = sv_ref[:, 1:WINDOW, :]
    nv_ref[:, WINDOW - 1:WINDOW, :] = vn_ref[...]
    for kh in range(N_KV_HEADS):
        cols = slice(kh * HEAD_DIM, (kh + 1) * HEAD_DIM)
        heads = slice(kh * GROUP, (kh + 1) * GROUP)
        k = nk_ref[:, :, cols].astype(BF16)
        v = nv_ref[:, :, cols].astype(BF16)
        s = jnp.einsum("ngd,ncd->ngc", q_ref[:, heads, :], k, preferred_element_type=F32)
        sink = sinks_ref[heads, :][None]
        m = jnp.maximum(jnp.max(s, axis=-1, keepdims=True), sink)
        p = jnp.exp(s - m)
        den = jnp.sum(p, axis=-1, keepdims=True) + jnp.exp(sink - m)
        o = jnp.einsum("ngc,ncd->ngd", p.astype(BF16), v, preferred_element_type=F32) / den
        a_ref[:, heads, :] = o.astype(BF16)


def _attn_sample(sinks_col, q3, k_new3, v_new3, state_k, state_v):
    n = q3.shape[0]
    return pl.pallas_call(
        _attn_sample_kernel,
        out_shape=[_sds((n, N_HEADS, HEAD_DIM), BF16), _sds((n, WINDOW, KV_WIDTH), F32),
                   _sds((n, WINDOW, KV_WIDTH), F32)],
        compiler_params=pltpu.CompilerParams(vmem_limit_bytes=VMEM_LIMIT),
        name="attn_sample",
    )(sinks_col, q3, k_new3, v_new3, state_k, state_v)


def _shift_rows(cur, halo):
    rid = lax.broadcasted_iota(jnp.int32, cur.shape, 0)
    h0 = halo[0:1, :]
    h1 = halo[1:2, :]
    s1 = jnp.where(rid == 0, h1, pltpu.roll(cur, 1, 0))
    s2 = jnp.where(rid == 0, h0, jnp.where(rid == 1, h1, pltpu.roll(cur, 2, 0)))
    return s1, s2


def _conv3(cw, cur, s1, s2):
    return cw[0:1, :] * s2 + cw[1:2, :] * s1 + cw[2:3, :] * cur


def _conv_tile(i, tiles_per_seq, split, parts, rc, bm, halo_ref, st_ref, project, group_fn):
    gw = halo_ref.shape[1] // split
    gcols = [slice(grp * gw, (grp + 1) * gw) for grp in range(split)]

    @pl.when(i % tiles_per_seq == 0)
    def _():
        halo_ref[...] = jnp.zeros_like(halo_ref)

    halos = {}

    def chunk(rows, ride):
        last = rows.start + rows.size == bm
        full = None if last else project(rows, ride, slice(None))
        for grp in range(split):
            pcols = slice(parts * grp * gw, parts * (grp + 1) * gw)
            proj = project(rows, ride, pcols) if last else full[:, pcols]
            halo = halos.get(grp)
            if halo is None:
                halo = halo_ref[:, gcols[grp]]
            cur = group_fn(rows, ride, grp, proj, lambda cur, halo=halo: _shift_rows(cur, halo))
            halos[grp] = cur[cur.shape[0] - 2:, :]
        if last:
            for grp in range(split):
                halo_ref[:, gcols[grp]] = halos.pop(grp)

    _walk_chunks(i, bm, rc, chunk)

    @pl.when(i % tiles_per_seq == tiles_per_seq - 1)
    def _():
        st_ref[...] = halo_ref[...]


def _convmix_kernel(tiles_per_seq, split, rc, gate_group, h_ref, hs_ref, wb_ref, wc_ref, wu_ref, cw_ref, p2_ref, p1_ref,
                    wgate_ref, o_ref, st_ref, os_ref, pres_ref, wgate_bf_ref, wbf_ref, halo_ref):
    i = pl.program_id(1)
    _cast_gate_slab(wgate_ref, wgate_bf_ref, gate_group)
    gw = wb_ref.shape[1] // split

    @pl.when(i == 0)
    def _():
        for grp in range(split):
            src = slice(grp * gw, (grp + 1) * gw)
            for part, w_ref in enumerate((wb_ref, wc_ref, wu_ref)):
                wbf_ref[:, (3 * grp + part) * gw:(3 * grp + part + 1) * gw] = w_ref[:, src].astype(BF16)

    def project(rows, ride, pcols):
        return _dot(_ride(h_ref, rows, hs_ref, ride), wbf_ref[:, pcols])

    def group(rows, ride, grp, proj, shifts):
        n = rows.size
        cols = slice(grp * gw, (grp + 1) * gw)
        cb, cur = proj[:, 0:gw], proj[:, gw:2 * gw] * proj[:, 2 * gw:3 * gw]
        cw = cw_ref[:, cols]
        s1, s2 = shifts(cur[:n])
        o_ref[rows, cols] = (cb[:n] * _conv3(cw, cur[:n], s1, s2)).astype(BF16)
        if ride:
            pres_ref[:, cols] = cur[n:]
            os_ref[:, cols] = (cb[n:] * _conv3(cw, cur[n:], p1_ref[:, cols], p2_ref[:, cols])).astype(BF16)
        return cur[:n]

    _conv_tile(i, tiles_per_seq, split, 3, rc, h_ref.shape[0], halo_ref, st_ref, project, group)


def _convmix(h, hs, w_in, conv_w, prev2, prev1, *, bm, bn, split, rc, seq_len, gate_group):
    m, ms = h.shape[0], hs.shape[0]
    tiles_per_seq = seq_len // bm
    n_m = m // bm
    gate_in, gate_out = _gate_slab_specs(D_MODEL // (CONV_WIDTH // bn * n_m), lambda j, i: j * n_m + i)
    tile = pl.BlockSpec((bm, bn), lambda j, i: (i, j))
    stile = pl.BlockSpec((ms, bn), lambda j, i: (0, j))
    wspec = lambda off: _wcol(D_MODEL, bn, lambda j, i: off + j * bn)
    return pl.pallas_call(
        functools.partial(_convmix_kernel, tiles_per_seq, split, rc, gate_group),
        grid=(CONV_WIDTH // bn, n_m),
        in_specs=[
            pl.BlockSpec((bm, D_MODEL), lambda j, i: (i, 0)),
            _resident(hs),
            wspec(OFF_CB), wspec(OFF_CC), wspec(OFF_CU),
            pl.BlockSpec((pl.Squeezed(), 3, bn), lambda j, i: (0, 0, j)),
            stile, stile, gate_in,
        ],
        out_specs=[
            tile,
            pl.BlockSpec((pl.Squeezed(), 2, bn), lambda j, i: (i // tiles_per_seq, 0, j)),
            stile, stile, gate_out,
        ],
        out_shape=[_sds((m, CONV_WIDTH), BF16), _sds((m // seq_len, 2, CONV_WIDTH), F32),
                   _sds((ms, CONV_WIDTH), BF16), _sds((ms, CONV_WIDTH), F32), _sds((D_MODEL, 2 * D_MODEL), BF16)],
        scratch_shapes=[pltpu.VMEM((D_MODEL, 3 * bn), BF16), pltpu.VMEM((2, bn), F32)],
        compiler_params=_params(2),
        name="convmix",
    )(h, hs, w_in, w_in, w_in, conv_w, prev2, prev1, w_in)


def _merge_kernel(split, rc, h_ref, a_ref, b_ref, hs_ref, as_ref, bs_ref, wg_ref, wa_ref, wb_ref, bg_ref,
                  o_ref, os_ref):
    gw = wa_ref.shape[1] // split

    def chunk(rows, ride):
        n = rows.size
        h = _ride(h_ref, rows, hs_ref, ride)
        a = _ride(a_ref, rows, as_ref, ride)
        b = _ride(b_ref, rows, bs_ref, ride)
        for grp in range(split):
            cols = slice(grp * gw, (grp + 1) * gw)
            gates = _dot(h, wg_ref[:, 2 * grp * gw:2 * (grp + 1) * gw])
            ga = jax.nn.sigmoid(gates[:, 0:gw] + bg_ref[0:1, cols])
            gb = jax.nn.sigmoid(gates[:, gw:2 * gw] + bg_ref[1:2, cols])
            merged = (ga * _dot(a, wa_ref[:, cols]) + gb * _dot(b, wb_ref[:, cols])).astype(BF16)
            o_ref[rows, cols] = merged[:n]
            if ride:
                os_ref[:, cols] = merged[n:]

    _walk_chunks(pl.program_id(1), h_ref.shape[0], rc, chunk)


def _merge(h, a, b, hs, a_s, b_s, w_gates, w_attn_branch, w_conv_branch, b_gate, *, bm, bn, split, rc):
    m, ms = h.shape[0], hs.shape[0]
    row = lambda j, i: (i, 0)
    return pl.pallas_call(
        functools.partial(_merge_kernel, split, rc),
        grid=(D_MODEL // bn, m // bm),
        in_specs=[
            pl.BlockSpec((bm, D_MODEL), row),
            pl.BlockSpec((bm, ATTN_WIDTH), row),
            pl.BlockSpec((bm, CONV_WIDTH), row),
            _resident(hs), _resident(a_s), _resident(b_s),
            pl.BlockSpec((D_MODEL, 2 * bn), lambda j, i: (0, j)),
            pl.BlockSpec((pl.Squeezed(), ATTN_WIDTH, bn), lambda j, i: (0, 0, j)),
            pl.BlockSpec((pl.Squeezed(), CONV_WIDTH, bn), lambda j, i: (0, 0, j)),
            pl.BlockSpec((pl.Squeezed(), 2, bn), lambda j, i: (0, 0, j)),
        ],
        out_specs=[pl.BlockSpec((bm, bn), lambda j, i: (i, j)),
                   pl.BlockSpec((ms, bn), lambda j, i: (0, j))],
        out_shape=[_sds((m, D_MODEL), BF16), _sds((ms, D_MODEL), BF16)],
        compiler_params=_params(2),
        name="merge",
    )(h, a, b, hs, a_s, b_s, w_gates, w_attn_branch, w_conv_branch, b_gate)


def _oproj_kernel(rc, mg_ref, x_ref, mgs_ref, xs_ref, wo_ref, gpost_ref, gpre_ref,
                  x1_ref, h2_ref, x1s_ref, h2s_ref):
    def chunk(rows, ride):
        n = rows.size
        y = _dot(_ride(mg_ref, rows, mgs_ref, ride), wo_ref[...])
        x1 = _ride(x_ref, rows, xs_ref, ride) + _rms(y, gpost_ref[...])
        h2 = _rms(x1, gpre_ref[...]).astype(BF16)
        x1_ref[rows, :] = x1[:n]
        h2_ref[rows, :] = h2[:n]
        if ride:
            x1s_ref[...] = x1[n:]
            h2s_ref[...] = h2[n:]

    _walk_chunks(pl.program_id(0), x_ref.shape[0], rc, chunk)


def _oproj(merged, x, merged_s, xs, w_o, g_post, g_pre, *, bm, rc):
    m, ms = x.shape[0], xs.shape[0]
    tile = pl.BlockSpec((bm, D_MODEL), lambda i: (i, 0))
    return pl.pallas_call(
        functools.partial(_oproj_kernel, rc),
        grid=(m // bm,),
        in_specs=[
            tile, tile, _resident(merged_s), _resident(xs),
            pl.BlockSpec((pl.Squeezed(), D_MODEL, D_MODEL), lambda i: (0, 0, 0),
                         pipeline_mode=pl.Buffered(1)),
            _resident(g_post), _resident(g_pre),
        ],
        out_specs=[tile, tile, _resident(xs), _resident(merged_s)],
        out_shape=[_sds((m, D_MODEL), F32), _sds((m, D_MODEL), BF16),
                   _sds((ms, D_MODEL), F32), _sds((ms, D_MODEL), BF16)],
        compiler_params=_params(1),
        name="oproj_norm",
    )(merged, x, merged_s, xs, w_o, g_post, g_pre)


def _ffn_up_kernel(tiles_per_seq, split, rc, h_ref, hs_ref, wa_ref, wg_ref, cw_ref, cb_ref, p2_ref, p1_ref, wd_ref,
                   o_ref, st_ref, os_ref, ups_ref, wd_bf_ref, wbf_ref, halo_ref):
    i = pl.program_id(1)
    wd_bf_ref[...] = wd_ref[...].astype(BF16)
    gw = wa_ref.shape[1] // split

    @pl.when(i == 0)
    def _():
        for grp in range(split):
            src = slice(grp * gw, (grp + 1) * gw)
            wbf_ref[:, 2 * grp * gw:(2 * grp + 1) * gw] = wa_ref[:, src].astype(BF16)
            wbf_ref[:, (2 * grp + 1) * gw:(2 * grp + 2) * gw] = wg_ref[:, src].astype(BF16)

    def project(rows, ride, pcols):
        return _dot(_ride(h_ref, rows, hs_ref, ride), wbf_ref[:, pcols])

    def group(rows, ride, grp, proj, shifts):
        n = rows.size
        cols = slice(grp * gw, (grp + 1) * gw)
        cur, gate = proj[:, 0:gw], proj[:, gw:2 * gw]
        cw = cw_ref[:, cols]

        def gated(cur, s1, s2, gate):
            act = jax.nn.gelu(_conv3(cw, cur, s1, s2) + cb_ref[:, cols], approximate=True)
            return (act * gate).astype(BF16)

        s1, s2 = shifts(cur[:n])
        o_ref[rows, cols] = gated(cur[:n], s1, s2, gate[:n])
        if ride:
            ups_ref[:, cols] = cur[n:]
            os_ref[:, cols] = gated(cur[n:], p1_ref[:, cols], p2_ref[:, cols], gate[n:])
        return cur[:n]

    _conv_tile(i, tiles_per_seq, split, 2, rc, h_ref.shape[0], halo_ref, st_ref, project, group)


def _ffn_up(h2, h2s, w_ffn_up, ffn_conv_w, ffn_conv_b, prev2, prev1, w_ffn_down, *, bm, bn, split, rc, seq_len):
    m, ms = h2.shape[0], h2s.shape[0]
    tiles_per_seq = seq_len // bm
    n_tiles = D_FF // bn
    n_m = m // bm
    slab = D_FF // (n_tiles * n_m)
    wslab = pl.BlockSpec((pl.Squeezed(), slab, D_MODEL), lambda j, i: (0, j * n_m + i, 0))
    tile = pl.BlockSpec((bm, bn), lambda j, i: (i, j))
    stile = pl.BlockSpec((ms, bn), lambda j, i: (0, j))
    return pl.pallas_call(
        functools.partial(_ffn_up_kernel, tiles_per_seq, split, rc),
        grid=(n_tiles, m // bm),
        in_specs=[
            pl.BlockSpec((bm, D_MODEL), lambda j, i: (i, 0)),
            _resident(h2s),
            pl.BlockSpec((pl.Squeezed(), D_MODEL, bn), lambda j, i: (0, 0, j)),
            pl.BlockSpec((pl.Squeezed(), D_MODEL, bn), lambda j, i: (0, 0, n_tiles + j)),
            pl.BlockSpec((pl.Squeezed(), 3, bn), lambda j, i: (0, 0, j)),
            pl.BlockSpec((1, bn), lambda j, i: (0, j)),
            stile, stile, wslab,
        ],
        out_specs=[
            tile,
            pl.BlockSpec((pl.Squeezed(), 2, bn), lambda j, i: (i // tiles_per_seq, 0, j)),
            stile, stile, wslab,
        ],
        out_shape=[_sds((m, D_FF), BF16), _sds((m // seq_len, 2, D_FF), F32),
                   _sds((ms, D_FF), BF16), _sds((ms, D_FF), F32), _sds(w_ffn_down.shape, BF16)],
        scratch_shapes=[pltpu.VMEM((D_MODEL, 2 * bn), BF16), pltpu.VMEM((2, bn), F32)],
        compiler_params=_params(2),
        name="ffn_up",
    )(h2, h2s, w_ffn_up, w_ffn_up, ffn_conv_w, ffn_conv_b, prev2, prev1, w_ffn_down)


def _ffn_down_kernel(rc, a_ref, as_ref, w_ref, o_ref, os_ref):
    def chunk(rows, ride):
        n = rows.size
        o = _dot(_ride(a_ref, rows, as_ref, ride), w_ref[...]).astype(BF16)
        o_ref[rows, :] = o[:n]
        if ride:
            os_ref[...] = o[n:]

    _walk_chunks(pl.program_id(1), a_ref.shape[0], rc, chunk)


def _ffn_down(act, act_s, w_ffn_down, *, bm, bn, rc):
    m, ms = act.shape[0], act_s.shape[0]
    return pl.pallas_call(
        functools.partial(_ffn_down_kernel, rc),
        grid=(D_MODEL // bn, m // bm),
        in_specs=[
            pl.BlockSpec((bm, D_FF), lambda j, i: (i, 0)),
            _resident(act_s),
            pl.BlockSpec((pl.Squeezed(), D_FF, bn), lambda j, i: (0, 0, j)),
        ],
        out_specs=[pl.BlockSpec((bm, bn), lambda j, i: (i, j)),
                   pl.BlockSpec((ms, bn), lambda j, i: (0, j))],
        out_shape=[_sds((m, D_MODEL), BF16), _sds((ms, D_MODEL), BF16)],
        compiler_params=_params(2),
        name="ffn_down",
    )(act, act_s, w_ffn_down)


def _final_kernel(rc, f_ref, x1_ref, p_ref, fs_ref, x1s_ref, ps_ref, g_ref, wg_ref, wp_ref, y_ref, ys_ref):
    def chunk(rows, ride):
        n = rows.size
        f = _ride(f_ref, rows, fs_ref, ride).astype(F32)
        x2 = _ride(x1_ref, rows, x1s_ref, ride) + _rms(f, g_ref[...])
        gate = jax.nn.sigmoid(_dot(x2.astype(BF16), wg_ref[...]))
        y = x2 + gate * _dot(_ride(p_ref, rows, ps_ref, ride).astype(BF16), wp_ref[...])
        y_ref[rows, :] = y[:n]
        if ride:
            ys_ref[...] = y[n:]

    _walk_chunks(pl.program_id(0), x1_ref.shape[0], rc, chunk)


def _final(f, x1, p, f_s, x1_s, p_s, g_post, w_ple_gate, w_ple, *, bm, rc):
    m, ms = x1.shape[0], x1_s.shape[0]
    row = lambda i: (i, 0)
    return pl.pallas_call(
        functools.partial(_final_kernel, rc),
        grid=(m // bm,),
        in_specs=[
            pl.BlockSpec((bm, D_MODEL), row),
            pl.BlockSpec((bm, D_MODEL), row),
            pl.BlockSpec((bm, PLE_DIM), row),
            _resident(f_s), _resident(x1_s), _resident(p_s),
            _resident(g_post),
            pl.BlockSpec((pl.Squeezed(), D_MODEL, D_MODEL), lambda i: (0, 0, 0),
                         pipeline_mode=pl.Buffered(1)),
            pl.BlockSpec((pl.Squeezed(), PLE_DIM, D_MODEL), lambda i: (0, 0, 0),
                         pipeline_mode=pl.Buffered(1)),
        ],
        out_specs=[pl.BlockSpec((bm, D_MODEL), row), _resident(x1_s)],
        out_shape=[_sds((m, D_MODEL), F32), _sds((ms, D_MODEL), F32)],
        compiler_params=_params(1),
        name="ffn_norm_ple",
    )(f, x1, p, f_s, x1_s, p_s, g_post, w_ple_gate, w_ple)


def _rope_tables(pos):
    half = HEAD_DIM // 2
    inv = ROPE_THETA ** (-np.arange(half, dtype=np.float64) / half)
    ang = np.asarray(pos, dtype=np.float64)[:, None] * inv[None, :]
    cos, sin = np.cos(ang), np.sin(ang)
    reps = LANES // HEAD_DIM
    return (jnp.asarray(np.tile(np.concatenate([cos, cos], axis=1), (1, reps)), dtype=F32),
            jnp.asarray(np.tile(np.concatenate([-sin, sin], axis=1), (1, reps)), dtype=F32))


TILES = dict(qkv=dict(bm=1024, rc=256), attn=128, convmix=dict(bm=1024, bn=512, split=2, rc=512),
             merge=dict(bm=1024, bn=1024, split=4, rc=1024), oproj=dict(bm=512, rc=128),
             ffn_up=dict(bm=2048, bn=512, split=1, rc=1024), ffn_down=dict(bm=1024, bn=1024, rc=1024),
             final=dict(bm=512, rc=512))


def kernel(x_prompt, x_sample, state_attn_k, state_attn_v, state_conv, state_ffn_conv, p_prompt, p_sample,
           norm_mix_pre, norm_mix_post, norm_ffn_pre, norm_ffn_post, w_in, attn_sinks, conv_w, w_attn_branch,
           w_conv_branch, b_gate, w_o, w_ffn_up, ffn_conv_w, ffn_conv_b, w_ffn_down, w_ple, w_ple_gate):
    assert w_in.shape[0] == 1, "weights are indexed with a squeezed depth axis of size 1"
    batch, seq, _ = x_prompt.shape
    ns, dec_seq, _ = x_sample.shape
    assert dec_seq == 1
    m = batch * seq
    x = x_prompt.reshape(m, D_MODEL)
    xs = x_sample.reshape(ns, D_MODEL)
    rope_p = _rope_tables(np.arange(seq))
    rope_s = _rope_tables(np.full((ns,), PAST_LEN))

    h, q, k, v, hs, qs, ks, vs = _qkv(x, xs, norm_mix_pre, w_in, rope_p, rope_s, **TILES["qkv"])

    merge_tiles = TILES["merge"]
    a, w_o_bf, w_ple_gate_bf, w_ple_bf, w_attn_branch_bf, w_conv_branch_bf = _attn_prompt(
        attn_sinks.reshape(N_HEADS), q, k, v, w_o, w_ple_gate, w_ple, w_attn_branch, w_conv_branch,
        n_seq=batch, seq_len=seq, bq=TILES["attn"])
    a3, new_ks, new_vs = _attn_sample(
        attn_sinks.reshape(N_HEADS, 1), qs.reshape(ns, N_HEADS, HEAD_DIM), ks.reshape(ns, 1, KV_WIDTH),
        vs.reshape(ns, 1, KV_WIDTH), state_attn_k.reshape(ns, WINDOW, KV_WIDTH),
        state_attn_v.reshape(ns, WINDOW, KV_WIDTH))
    a_s = a3.reshape(ns, ATTN_WIDTH)

    b, new_conv_p, b_s, pre_s, w_gates_bf = _convmix(
        h, hs, w_in, conv_w, state_conv[0, :, 0], state_conv[0, :, 1], seq_len=seq,
        gate_group=merge_tiles["bn"] // merge_tiles["split"], **TILES["convmix"])
    merged, merged_s = _merge(h, a, b, hs, a_s, b_s, w_gates_bf, w_attn_branch_bf, w_conv_branch_bf, b_gate,
                              **merge_tiles)
    x1, h2, x1_s, h2_s = _oproj(merged, x, merged_s, xs, w_o_bf, norm_mix_post, norm_ffn_pre,
                                **TILES["oproj"])
    act, new_ffn_p, act_s, up_s, w_ffn_down_bf = _ffn_up(
        h2, h2_s, w_ffn_up, ffn_conv_w, ffn_conv_b, state_ffn_conv[0, :, 0], state_ffn_conv[0, :, 1], w_ffn_down,
        seq_len=seq, **TILES["ffn_up"])
    f, f_s = _ffn_down(act, act_s, w_ffn_down_bf, **TILES["ffn_down"])
    y, y_s = _final(f, x1, p_prompt.reshape(m, PLE_DIM), f_s, x1_s, p_sample.reshape(ns, PLE_DIM),
                    norm_ffn_post, w_ple_gate_bf, w_ple_bf, **TILES["final"])

    kv_tail = lambda t: t.reshape(batch, seq, KV_WIDTH)[:, seq - WINDOW:].reshape(batch, WINDOW, N_KV_HEADS, HEAD_DIM)
    kv_state = lambda t: t.reshape(ns, WINDOW, N_KV_HEADS, HEAD_DIM)
    return (y.reshape(batch, seq, D_MODEL), y_s.reshape(ns, dec_seq, D_MODEL),
            kv_tail(k)[None], kv_tail(v)[None], new_conv_p[None], new_ffn_p[None],
            kv_state(new_ks)[None], kv_state(new_vs)[None],
            jnp.stack([state_conv[0, :, 1], pre_s], axis=1)[None],
            jnp.stack([state_ffn_conv[0, :, 1], up_s], axis=1)[None])
```

```python
import functools

import jax
import jax.numpy as jnp
import numpy as np
from jax import lax
from jax.experimental import pallas as pl
from jax.experimental.pallas import tpu as pltpu

D_MODEL = 2048
N_HEADS = 16
N_KV_HEADS = 2
HEAD_DIM = 64
GROUP = N_HEADS // N_KV_HEADS
ATTN_WIDTH = N_HEADS * HEAD_DIM
KV_WIDTH = N_KV_HEADS * HEAD_DIM
QKV_WIDTH = ATTN_WIDTH + 2 * KV_WIDTH
WINDOW = 128
PAST_LEN = 16384
ROPE_THETA = 10000.0
CONV_WIDTH = D_MODEL // 2
D_FF = 5632
PLE_DIM = 256
RMS_EPS = 1e-6
OFF_CB = QKV_WIDTH
OFF_CC = OFF_CB + CONV_WIDTH
OFF_CU = OFF_CC + CONV_WIDTH
OFF_GA = OFF_CU + CONV_WIDTH

LANES = 128
UNIT_HEADS = GROUP
MASKED = -1e30
VMEM_LIMIT = 56 * 1024 * 1024

BF16 = jnp.bfloat16
F32 = jnp.float32


def _params(n_axes):
    return pltpu.CompilerParams(dimension_semantics=("arbitrary",) * n_axes,
                                vmem_limit_bytes=VMEM_LIMIT)


def _rms(x, g):
    return x * lax.rsqrt(jnp.mean(x * x, axis=-1, keepdims=True) + RMS_EPS) * g


def _dot(a, b):
    return jnp.dot(a, b, preferred_element_type=F32)


def _walk_chunks(i, bm, rc, chunk_fn):
    rc = min(bm, rc)

    def walk(first):
        for c in range(bm // rc):
            chunk_fn(pl.ds(c * rc, rc), first and c == 0)

    @pl.when(i == 0)
    def _():
        walk(True)

    @pl.when(i > 0)
    def _():
        walk(False)


def _ride(ref, rows, sample_ref, ride):
    x = ref[rows, :]
    return jnp.concatenate([x, sample_ref[...]], axis=0) if ride else x


def _cast_gate_slab(src_ref, dst_ref, group):
    for t in range(D_MODEL // group):
        src = slice(t * group, (t + 1) * group)
        dst_ref[:, 2 * t * group:(2 * t + 1) * group] = src_ref[:, src].astype(BF16)
        dst_ref[:, (2 * t + 1) * group:(2 * t + 2) * group] = (
            src_ref[:, D_MODEL + t * group:D_MODEL + (t + 1) * group].astype(BF16))


def _gate_slab_specs(slab, step_fn):
    gate_in = pl.BlockSpec((pl.Squeezed(), pl.Element(slab), pl.Element(2 * D_MODEL)),
                           lambda *g: (0, pl.multiple_of(step_fn(*g) * slab, slab), OFF_GA))
    gate_out = pl.BlockSpec((slab, 2 * D_MODEL), lambda *g: (step_fn(*g), 0))
    return gate_in, gate_out


def _wcol(k_rows, block_cols, col_fn):
    return pl.BlockSpec((pl.Squeezed(), pl.Element(k_rows), pl.Element(block_cols)),
                        lambda *g: (0, 0, pl.multiple_of(col_fn(*g), LANES)))


def _resident(arr):
    zeros = (0,) * arr.ndim
    return pl.BlockSpec(arr.shape, lambda *g: zeros)


def _sds(shape, dtype):
    return jax.ShapeDtypeStruct(shape, dtype)


def _qkv_kernel(rc, x_ref, xs_ref, g_ref, w_ref, cos_ref, sin_ref, coss_ref, sins_ref,
                h_ref, q_ref, k_ref, v_ref, hs_ref, qs_ref, ks_ref, vs_ref, wbf_ref):
    i = pl.program_id(0)
    n_q = ATTN_WIDTH // LANES

    @pl.when(i == 0)
    def _():
        wbf_ref[...] = w_ref[...].astype(BF16)

    def chunk(rows, ride):
        n = rows.size

        def put(ref, sample_ref, cols, val):
            ref[rows, cols] = val[:n]
            if ride:
                sample_ref[:, cols] = val[n:]

        h = _rms(_ride(x_ref, rows, xs_ref, ride), g_ref[...]).astype(BF16)
        put(h_ref, hs_ref, slice(None), h)
        proj = _dot(h, wbf_ref[...])
        cos = _ride(cos_ref, rows, coss_ref, ride)
        sin = _ride(sin_ref, rows, sins_ref, ride)
        lane = lax.broadcasted_iota(jnp.int32, cos.shape, 1)
        first_half = (lane % HEAD_DIM) < (HEAD_DIM // 2)
        for c in range(n_q + 1):
            cols = slice(c * LANES, (c + 1) * LANES)
            xg = proj[:, cols]
            partner = jnp.where(first_half, pltpu.roll(xg, LANES - HEAD_DIM // 2, 1),
                                pltpu.roll(xg, HEAD_DIM // 2, 1))
            yg = xg * cos + partner * sin
            if c < n_q:
                put(q_ref, qs_ref, cols, (yg * (HEAD_DIM ** -0.5)).astype(BF16))
            else:
                put(k_ref, ks_ref, slice(None), yg)
        put(v_ref, vs_ref, slice(None), proj[:, ATTN_WIDTH + KV_WIDTH:QKV_WIDTH])

    _walk_chunks(i, x_ref.shape[0], rc, chunk)


def _qkv(x, xs, g, w_in, rope_p, rope_s, *, bm, rc):
    m, ms = x.shape[0], xs.shape[0]
    t_tiles = rope_p[0].shape[0] // bm
    row = lambda i: (i, 0)
    pos = lambda i: (i % t_tiles, 0)
    outs = lambda rows: [_sds((rows, D_MODEL), BF16), _sds((rows, ATTN_WIDTH), BF16),
                         _sds((rows, KV_WIDTH), F32), _sds((rows, KV_WIDTH), F32)]
    out_p, out_s = outs(m), outs(ms)
    return pl.pallas_call(
        functools.partial(_qkv_kernel, rc),
        grid=(m // bm,),
        in_specs=[
            pl.BlockSpec((bm, D_MODEL), row),
            _resident(xs),
            _resident(g),
            pl.BlockSpec((pl.Squeezed(), D_MODEL, QKV_WIDTH), lambda i: (0, 0, 0),
                         pipeline_mode=pl.Buffered(1)),
            pl.BlockSpec((bm, LANES), pos),
            pl.BlockSpec((bm, LANES), pos),
            _resident(rope_s[0]),
            _resident(rope_s[1]),
        ],
        out_specs=[pl.BlockSpec((bm, s.shape[1]), row) for s in out_p] + [_resident(s) for s in out_s],
        out_shape=out_p + out_s,
        scratch_shapes=[pltpu.VMEM((D_MODEL, QKV_WIDTH), BF16)],
        compiler_params=_params(1),
        name="qkv_rope",
    )(x, xs, g, w_in, rope_p[0], rope_p[1], rope_s[0], rope_s[1])


def _attn_prompt_kernel(sinks_ref, q_ref, kc_ref, kp_ref, vc_ref, vp_ref, wo_ref, wa_ref, wb_ref,
                        o_ref, wo_bf_ref, wa_bf_ref, wb_bf_ref):
    wo_bf_ref[...] = wo_ref[...].astype(BF16)
    wa_bf_ref[...] = wa_ref[...].astype(BF16)
    wb_bf_ref[...] = wb_ref[...].astype(BF16)

    n_sub = q_ref.shape[0] // WINDOW
    qi = lax.broadcasted_iota(jnp.int32, (WINDOW, WINDOW), 0)
    ci = lax.broadcasted_iota(jnp.int32, (WINDOW, WINDOW), 1)
    own = ci <= qi
    first_valid = own | (pl.program_id(1) > 0)
    k_all = jnp.concatenate([kp_ref[...], kc_ref[...]], axis=0).astype(BF16)
    v_all = jnp.concatenate([vp_ref[...], vc_ref[...]], axis=0).astype(BF16)
    units = [(sub, hg) for sub in range(n_sub) for hg in range(N_HEADS // UNIT_HEADS)]
    heads_of = lambda hg: range(hg * UNIT_HEADS, (hg + 1) * UNIT_HEADS)
    q_rows = lambda sub: slice(sub * WINDOW, (sub + 1) * WINDOW)
    kv_rows = lambda sub: slice(sub * WINDOW, (sub + 2) * WINDOW)
    kv_cols = lambda hg: slice((hg * UNIT_HEADS // GROUP) * HEAD_DIM, (hg * UNIT_HEADS // GROUP + 1) * HEAD_DIM)

    scores = []
    for sub, kh in units:
        qs = jnp.concatenate([q_ref[q_rows(sub), h * HEAD_DIM:(h + 1) * HEAD_DIM] for h in heads_of(kh)], axis=0)
        scores.append(lax.dot_general(qs, k_all[kv_rows(sub), kv_cols(kh)], (((1,), (1,)), ((), ())),
                                      preferred_element_type=F32))
    probs, dens = [], []
    for (sub, kh), s_all in zip(units, scores):
        ps, ds = [], []
        for g, h in enumerate(heads_of(kh)):
            s2 = s_all[g * WINDOW:(g + 1) * WINDOW, :]
            s = jnp.where(own, s2[:, WINDOW:], s2[:, :WINDOW])
            if sub == 0:
                s = jnp.where(first_valid, s, MASKED)
            sink = sinks_ref[h]
            m = jnp.maximum(jnp.max(s, axis=-1, keepdims=True), sink)
            p = jnp.exp(s - m)
            ds.append(jnp.sum(p, axis=-1, keepdims=True) + jnp.exp(sink - m))
            ps.append(jnp.concatenate([jnp.where(own, 0.0, p), jnp.where(own, p, 0.0)], axis=1).astype(BF16))
        probs.append(jnp.concatenate(ps, axis=0))
        dens.append(ds)
    outs = [_dot(p_all, v_all[kv_rows(sub), kv_cols(kh)]) for (sub, kh), p_all in zip(units, probs)]
    for (sub, kh), o_all, ds in zip(units, outs, dens):
        for g, h in enumerate(heads_of(kh)):
            o = o_all[g * WINDOW:(g + 1) * WINDOW, :] / ds[g]
            o_ref[q_rows(sub), h * HEAD_DIM:(h + 1) * HEAD_DIM] = o.astype(BF16)


def _attn_prompt(sinks, q, k, v, w_o, w_attn_branch, w_conv_branch, *, n_seq, seq_len, bq):
    nb = seq_len // bq
    sub = bq // WINDOW
    steps = n_seq * nb
    slab, bslab = D_MODEL // steps, ATTN_WIDTH // steps
    cur = lambda n, b: (n * nb + b, 0)
    prev = lambda n, b: (jnp.maximum((n * nb + b) * sub - 1, 0), 0)
    wslab = pl.BlockSpec((pl.Squeezed(), slab, D_MODEL), lambda n, b: (0, n * nb + b, 0))
    bspec = pl.BlockSpec((pl.Squeezed(), bslab, D_MODEL), lambda n, b: (0, n * nb + b, 0))
    return pl.pallas_call(
        _attn_prompt_kernel,
        grid=(n_seq, nb),
        in_specs=[
            pl.BlockSpec(memory_space=pltpu.SMEM),
            pl.BlockSpec((bq, ATTN_WIDTH), cur),
            pl.BlockSpec((bq, KV_WIDTH), cur),
            pl.BlockSpec((WINDOW, KV_WIDTH), prev),
            pl.BlockSpec((bq, KV_WIDTH), cur),
            pl.BlockSpec((WINDOW, KV_WIDTH), prev),
            wslab, bspec, bspec,
        ],
        out_specs=[pl.BlockSpec((bq, ATTN_WIDTH), cur), wslab, bspec, bspec],
        out_shape=[_sds((n_seq * seq_len, ATTN_WIDTH), BF16), _sds(w_o.shape, BF16),
                   _sds(w_attn_branch.shape, BF16), _sds(w_conv_branch.shape, BF16)],
        compiler_params=_params(2),
        name="attn_prompt",
    )(sinks, q, k, k, v, v, w_o, w_attn_branch, w_conv_branch)


def _attn_sample_kernel(sinks_ref, q_ref, kn_ref, vn_ref, sk_ref, sv_ref, a_ref, nk_ref, nv_ref):
    nk_ref[:, 0:WINDOW - 1, :] = sk_ref[:, 1:WINDOW, :]
    nk_ref[:, WINDOW - 1:WINDOW, :] = kn_ref[...]
    nv_ref[:, 0:WINDOW - 1, :] = sv_ref[:, 1:WINDOW, :]
    nv_ref[:, WINDOW - 1:WINDOW, :] = vn_ref[...]
    for kh in range(N_KV_HEADS):
        cols = slice(kh * HEAD_DIM, (kh + 1) * HEAD_DIM)
        heads = slice(kh * GROUP, (kh + 1) * GROUP)
        k = nk_ref[:, :, cols].astype(BF16)
        v = nv_ref[:, :, cols].astype(BF16)
        s = jnp.einsum("ngd,ncd->ngc", q_ref[:, heads, :], k, preferred_element_type=F32)
        sink = sinks_ref[heads, :][None]
        m = jnp.maximum(jnp.max(s, axis=-1, keepdims=True), sink)
        p = jnp.exp(s - m)
        den = jnp.sum(p, axis=-1, keepdims=True) + jnp.exp(sink - m)
        o = jnp.einsum("ngc,ncd->ngd", p.astype(BF16), v, preferred_element_type=F32) / den
        a_ref[:, heads, :] = o.astype(BF16)


def _attn_sample(sinks_col, q3, k_new3, v_new3, state_k, state_v):
    n = q3.shape[0]
    return pl.pallas_call(
        _attn_sample_kernel,
        out_shape=[_sds((n, N_HEADS, HEAD_DIM), BF16), _sds((n, WINDOW, KV_WIDTH), F32),
                   _sds((n, WINDOW, KV_WIDTH), F32)],
        compiler_params=pltpu.CompilerParams(vmem_limit_bytes=VMEM_LIMIT),
        name="attn_sample",
    )(sinks_col, q3, k_new3, v_new3, state_k, state_v)


def _shift_rows(cur, halo):
    rid = lax.broadcasted_iota(jnp.int32, cur.shape, 0)
    h0 = halo[0:1, :]
    h1 = halo[1:2, :]
    s1 = jnp.where(rid == 0, h1, pltpu.roll(cur, 1, 0))
    s2 = jnp.where(rid == 0, h0, jnp.where(rid == 1, h1, pltpu.roll(cur, 2, 0)))
    return s1, s2


def _conv3(cw, cur, s1, s2):
    return cw[0:1, :] * s2 + cw[1:2, :] * s1 + cw[2:3, :] * cur


def _conv_tile(i, tiles_per_seq, split, parts, rc, bm, halo_ref, st_ref, project, group_fn):
    gw = halo_ref.shape[1] // split
    gcols = [slice(grp * gw, (grp + 1) * gw) for grp in range(split)]

    @pl.when(i % tiles_per_seq == 0)
    def _():
        halo_ref[...] = jnp.zeros_like(halo_ref)

    halos = {}

    def chunk(rows, ride):
        last = rows.start + rows.size == bm
        full = None if last else project(rows, ride, slice(None))
        for grp in range(split):
            pcols = slice(parts * grp * gw, parts * (grp + 1) * gw)
            proj = project(rows, ride, pcols) if last else full[:, pcols]
            halo = halos.get(grp)
            if halo is None:
                halo = halo_ref[:, gcols[grp]]
            cur = group_fn(rows, ride, grp, proj, lambda cur, halo=halo: _shift_rows(cur, halo))
            halos[grp] = cur[cur.shape[0] - 2:, :]
        if last:
            for grp in range(split):
                halo_ref[:, gcols[grp]] = halos.pop(grp)

    _walk_chunks(i, bm, rc, chunk)

    @pl.when(i % tiles_per_seq == tiles_per_seq - 1)
    def _():
        st_ref[...] = halo_ref[...]


def _convmix_kernel(tiles_per_seq, split, rc, gate_group, h_ref, hs_ref, wb_ref, wc_ref, wu_ref, cw_ref, p2_ref, p1_ref,
                    wgate_ref, o_ref, st_ref, os_ref, pres_ref, wgate_bf_ref, wbf_ref, halo_ref):
    i = pl.program_id(1)
    _cast_gate_slab(wgate_ref, wgate_bf_ref, gate_group)
    gw = wb_ref.shape[1] // split

    @pl.when(i == 0)
    def _():
        for grp in range(split):
            src = slice(grp * gw, (grp + 1) * gw)
            for part, w_ref in enumerate((wb_ref, wc_ref, wu_ref)):
                wbf_ref[:, (3 * grp + part) * gw:(3 * grp + part + 1) * gw] = w_ref[:, src].astype(BF16)

    def project(rows, ride, pcols):
        return _dot(_ride(h_ref, rows, hs_ref, ride), wbf_ref[:, pcols])

    def group(rows, ride, grp, proj, shifts):
        n = rows.size
        cols = slice(grp * gw, (grp + 1) * gw)
        cb, cur = proj[:, 0:gw], proj[:, gw:2 * gw] * proj[:, 2 * gw:3 * gw]
        cw = cw_ref[:, cols]
        s1, s2 = shifts(cur[:n])
        o_ref[rows, cols] = (cb[:n] * _conv3(cw, cur[:n], s1, s2)).astype(BF16)
        if ride:
            pres_ref[:, cols] = cur[n:]
            os_ref[:, cols] = (cb[n:] * _conv3(cw, cur[n:], p1_ref[:, cols], p2_ref[:, cols])).astype(BF16)
        return cur[:n]

    _conv_tile(i, tiles_per_seq, split, 3, rc, h_ref.shape[0], halo_ref, st_ref, project, group)


def _convmix(h, hs, w_in, conv_w, prev2, prev1, *, bm, bn, split, rc, seq_len, gate_group):
    m, ms = h.shape[0], hs.shape[0]
    tiles_per_seq = seq_len // bm
    n_m = m // bm
    gate_in, gate_out = _gate_slab_specs(D_MODEL // (CONV_WIDTH // bn * n_m), lambda j, i: j * n_m + i)
    tile = pl.BlockSpec((bm, bn), lambda j, i: (i, j))
    stile = pl.BlockSpec((ms, bn), lambda j, i: (0, j))
    wspec = lambda off: _wcol(D_MODEL, bn, lambda j, i: off + j * bn)
    return pl.pallas_call(
        functools.partial(_convmix_kernel, tiles_per_seq, split, rc, gate_group),
        grid=(CONV_WIDTH // bn, n_m),
        in_specs=[
            pl.BlockSpec((bm, D_MODEL), lambda j, i: (i, 0)),
            _resident(hs),
            wspec(OFF_CB), wspec(OFF_CC), wspec(OFF_CU),
            pl.BlockSpec((pl.Squeezed(), 3, bn), lambda j, i: (0, 0, j)),
            stile, stile, gate_in,
        ],
        out_specs=[
            tile,
            pl.BlockSpec((pl.Squeezed(), 2, bn), lambda j, i: (i // tiles_per_seq, 0, j)),
            stile, stile, gate_out,
        ],
        out_shape=[_sds((m, CONV_WIDTH), BF16), _sds((m // seq_len, 2, CONV_WIDTH), F32),
                   _sds((ms, CONV_WIDTH), BF16), _sds((ms, CONV_WIDTH), F32), _sds((D_MODEL, 2 * D_MODEL), BF16)],
        scratch_shapes=[pltpu.VMEM((D_MODEL, 3 * bn), BF16), pltpu.VMEM((2, bn), F32)],
        compiler_params=_params(2),
        name="convmix",
    )(h, hs, w_in, w_in, w_in, conv_w, prev2, prev1, w_in)


def _merge_kernel(split, rc, h_ref, a_ref, b_ref, hs_ref, as_ref, bs_ref, wg_ref, wa_ref, wb_ref, bg_ref,
                  o_ref, os_ref):
    gw = wa_ref.shape[1] // split

    def chunk(rows, ride):
        n = rows.size
        h = _ride(h_ref, rows, hs_ref, ride)
        a = _ride(a_ref, rows, as_ref, ride)
        b = _ride(b_ref, rows, bs_ref, ride)
        for grp in range(split):
            cols = slice(grp * gw, (grp + 1) * gw)
            gates = _dot(h, wg_ref[:, 2 * grp * gw:2 * (grp + 1) * gw])
            ga = jax.nn.sigmoid(gates[:, 0:gw] + bg_ref[0:1, cols])
            gb = jax.nn.sigmoid(gates[:, gw:2 * gw] + bg_ref[1:2, cols])
            merged = (ga * _dot(a, wa_ref[:, cols]) + gb * _dot(b, wb_ref[:, cols])).astype(BF16)
            o_ref[rows, cols] = merged[:n]
            if ride:
                os_ref[:, cols] = merged[n:]

    _walk_chunks(pl.program_id(1), h_ref.shape[0], rc, chunk)


def _merge(h, a, b, hs, a_s, b_s, w_gates, w_attn_branch, w_conv_branch, b_gate, *, bm, bn, split, rc):
    m, ms = h.shape[0], hs.shape[0]
    row = lambda j, i: (i, 0)
    return pl.pallas_call(
        functools.partial(_merge_kernel, split, rc),
        grid=(D_MODEL // bn, m // bm),
        in_specs=[
            pl.BlockSpec((bm, D_MODEL), row),
            pl.BlockSpec((bm, ATTN_WIDTH), row),
            pl.BlockSpec((bm, CONV_WIDTH), row),
            _resident(hs), _resident(a_s), _resident(b_s),
            pl.BlockSpec((D_MODEL, 2 * bn), lambda j, i: (0, j)),
            pl.BlockSpec((pl.Squeezed(), ATTN_WIDTH, bn), lambda j, i: (0, 0, j)),
            pl.BlockSpec((pl.Squeezed(), CONV_WIDTH, bn), lambda j, i: (0, 0, j)),
            pl.BlockSpec((pl.Squeezed(), 2, bn), lambda j, i: (0, 0, j)),
        ],
        out_specs=[pl.BlockSpec((bm, bn), lambda j, i: (i, j)),
                   pl.BlockSpec((ms, bn), lambda j, i: (0, j))],
        out_shape=[_sds((m, D_MODEL), BF16), _sds((ms, D_MODEL), BF16)],
        compiler_params=_params(2),
        name="merge",
    )(h, a, b, hs, a_s, b_s, w_gates, w_attn_branch, w_conv_branch, b_gate)


def _oproj_kernel(rc, mg_ref, x_ref, mgs_ref, xs_ref, wo_ref, gpost_ref, gpre_ref,
                  x1_ref, h2_ref, x1s_ref, h2s_ref):
    def chunk(rows, ride):
        n = rows.size
        y = _dot(_ride(mg_ref, rows, mgs_ref, ride), wo_ref[...])
        x1 = _ride(x_ref, rows, xs_ref, ride) + _rms(y, gpost_ref[...])
        h2 = _rms(x1, gpre_ref[...]).astype(BF16)
        x1_ref[rows, :] = x1[:n]
        h2_ref[rows, :] = h2[:n]
        if ride:
            x1s_ref[...] = x1[n:]
            h2s_ref[...] = h2[n:]

    _walk_chunks(pl.program_id(0), x_ref.shape[0], rc, chunk)


def _oproj(merged, x, merged_s, xs, w_o, g_post, g_pre, *, bm, rc):
    m, ms = x.shape[0], xs.shape[0]
    tile = pl.BlockSpec((bm, D_MODEL), lambda i: (i, 0))
    return pl.pallas_call(
        functools.partial(_oproj_kernel, rc),
        grid=(m // bm,),
        in_specs=[
            tile, tile, _resident(merged_s), _resident(xs),
            pl.BlockSpec((pl.Squeezed(), D_MODEL, D_MODEL), lambda i: (0, 0, 0),
                         pipeline_mode=pl.Buffered(1)),
            _resident(g_post), _resident(g_pre),
        ],
        out_specs=[tile, tile, _resident(xs), _resident(merged_s)],
        out_shape=[_sds((m, D_MODEL), F32), _sds((m, D_MODEL), BF16),
                   _sds((ms, D_MODEL), F32), _sds((ms, D_MODEL), BF16)],
        compiler_params=_params(1),
        name="oproj_norm",
    )(merged, x, merged_s, xs, w_o, g_post, g_pre)


def _ffn_up_kernel(tiles_per_seq, split, rc, h_ref, hs_ref, wa_ref, wg_ref, cw_ref, cb_ref, p2_ref, p1_ref, wd_ref,
                   o_ref, st_ref, os_ref, ups_ref, wd_bf_ref, wbf_ref, halo_ref):
    i = pl.program_id(1)
    wd_bf_ref[...] = wd_ref[...].astype(BF16)
    gw = wa_ref.shape[1] // split

    @pl.when(i == 0)
    def _():
        for grp in range(split):
            src = slice(grp * gw, (grp + 1) * gw)
            wbf_ref[:, 2 * grp * gw:(2 * grp + 1) * gw] = wa_ref[:, src].astype(BF16)
            wbf_ref[:, (2 * grp + 1) * gw:(2 * grp + 2) * gw] = wg_ref[:, src].astype(BF16)

    def project(rows, ride, pcols):
        return _dot(_ride(h_ref, rows, hs_ref, ride), wbf_ref[:, pcols])

    def group(rows, ride, grp, proj, shifts):
        n = rows.size
        cols = slice(grp * gw, (grp + 1) * gw)
        cur, gate = proj[:, 0:gw], proj[:, gw:2 * gw]
        cw = cw_ref[:, cols]

        def gated(cur, s1, s2, gate):
            act = jax.nn.gelu(_conv3(cw, cur, s1, s2) + cb_ref[:, cols], approximate=True)
            return (act * gate).astype(BF16)

        s1, s2 = shifts(cur[:n])
        o_ref[rows, cols] = gated(cur[:n], s1, s2, gate[:n])
        if ride:
            ups_ref[:, cols] = cur[n:]
            os_ref[:, cols] = gated(cur[n:], p1_ref[:, cols], p2_ref[:, cols], gate[n:])
        return cur[:n]

    _conv_tile(i, tiles_per_seq, split, 2, rc, h_ref.shape[0], halo_ref, st_ref, project, group)


def _ffn_up(h2, h2s, w_ffn_up, ffn_conv_w, ffn_conv_b, prev2, prev1, w_ffn_down, *, bm, bn, split, rc, seq_len):
    m, ms = h2.shape[0], h2s.shape[0]
    tiles_per_seq = seq_len // bm
    n_tiles = D_FF // bn
    n_m = m // bm
    slab = D_FF // (n_tiles * n_m)
    wslab = pl.BlockSpec((pl.Squeezed(), slab, D_MODEL), lambda j, i: (0, j * n_m + i, 0))
    tile = pl.BlockSpec((bm, bn), lambda j, i: (i, j))
    stile = pl.BlockSpec((ms, bn), lambda j, i: (0, j))
    return pl.pallas_call(
        functools.partial(_ffn_up_kernel, tiles_per_seq, split, rc),
        grid=(n_tiles, m // bm),
        in_specs=[
            pl.BlockSpec((bm, D_MODEL), lambda j, i: (i, 0)),
            _resident(h2s),
            pl.BlockSpec((pl.Squeezed(), D_MODEL, bn), lambda j, i: (0, 0, j)),
            pl.BlockSpec((pl.Squeezed(), D_MODEL, bn), lambda j, i: (0, 0, n_tiles + j)),
            pl.BlockSpec((pl.Squeezed(), 3, bn), lambda j, i: (0, 0, j)),
            pl.BlockSpec((1, bn), lambda j, i: (0, j)),
            stile, stile, wslab,
        ],
        out_specs=[
            tile,
            pl.BlockSpec((pl.Squeezed(), 2, bn), lambda j, i: (i // tiles_per_seq, 0, j)),
            stile, stile, wslab,
        ],
        out_shape=[_sds((m, D_FF), BF16), _sds((m // seq_len, 2, D_FF), F32),
                   _sds((ms, D_FF), BF16), _sds((ms, D_FF), F32), _sds(w_ffn_down.shape, BF16)],
        scratch_shapes=[pltpu.VMEM((D_MODEL, 2 * bn), BF16), pltpu.VMEM((2, bn), F32)],
        compiler_params=_params(2),
        name="ffn_up",
    )(h2, h2s, w_ffn_up, w_ffn_up, ffn_conv_w, ffn_conv_b, prev2, prev1, w_ffn_down)


def _ffn_down_kernel(rc, a_ref, as_ref, w_ref, wg_ref, wp_ref, o_ref, os_ref, wg_bf_ref, wp_bf_ref):
    wg_bf_ref[...] = wg_ref[...].astype(BF16)
    wp_bf_ref[...] = wp_ref[...].astype(BF16)

    def chunk(rows, ride):
        n = rows.size
        o = _dot(_ride(a_ref, rows, as_ref, ride), w_ref[...]).astype(BF16)
        o_ref[rows, :] = o[:n]
        if ride:
            os_ref[...] = o[n:]

    _walk_chunks(pl.program_id(1), a_ref.shape[0], rc, chunk)


def _ffn_down(act, act_s, w_ffn_down, w_ple_gate, w_ple, *, bm, bn, rc):
    m, ms = act.shape[0], act_s.shape[0]
    n_m = m // bm
    steps = D_MODEL // bn * n_m
    slab = lambda w: pl.BlockSpec((pl.Squeezed(), w.shape[1] // steps, D_MODEL), lambda j, i: (0, j * n_m + i, 0))
    return pl.pallas_call(
        functools.partial(_ffn_down_kernel, rc),
        grid=(D_MODEL // bn, n_m),
        in_specs=[
            pl.BlockSpec((bm, D_FF), lambda j, i: (i, 0)),
            _resident(act_s),
            pl.BlockSpec((pl.Squeezed(), D_FF, bn), lambda j, i: (0, 0, j)),
            slab(w_ple_gate), slab(w_ple),
        ],
        out_specs=[pl.BlockSpec((bm, bn), lambda j, i: (i, j)),
                   pl.BlockSpec((ms, bn), lambda j, i: (0, j)),
                   slab(w_ple_gate), slab(w_ple)],
        out_shape=[_sds((m, D_MODEL), BF16), _sds((ms, D_MODEL), BF16), _sds(w_ple_gate.shape, BF16),
                   _sds(w_ple.shape, BF16)],
        compiler_params=_params(2),
        name="ffn_down",
    )(act, act_s, w_ffn_down, w_ple_gate, w_ple)


def _final_kernel(rc, f_ref, x1_ref, p_ref, fs_ref, x1s_ref, ps_ref, g_ref, wg_ref, wp_ref, y_ref, ys_ref):
    def chunk(rows, ride):
        n = rows.size
        f = _ride(f_ref, rows, fs_ref, ride).astype(F32)
        x2 = _ride(x1_ref, rows, x1s_ref, ride) + _rms(f, g_ref[...])
        gate = jax.nn.sigmoid(_dot(x2.astype(BF16), wg_ref[...]))
        y = x2 + gate * _dot(_ride(p_ref, rows, ps_ref, ride).astype(BF16), wp_ref[...])
        y_ref[rows, :] = y[:n]
        if ride:
            ys_ref[...] = y[n:]

    _walk_chunks(pl.program_id(0), x1_ref.shape[0], rc, chunk)


def _final(f, x1, p, f_s, x1_s, p_s, g_post, w_ple_gate, w_ple, *, bm, rc):
    m, ms = x1.shape[0], x1_s.shape[0]
    row = lambda i: (i, 0)
    return pl.pallas_call(
        functools.partial(_final_kernel, rc),
        grid=(m // bm,),
        in_specs=[
            pl.BlockSpec((bm, D_MODEL), row),
            pl.BlockSpec((bm, D_MODEL), row),
            pl.BlockSpec((bm, PLE_DIM), row),
            _resident(f_s), _resident(x1_s), _resident(p_s),
            _resident(g_post),
            pl.BlockSpec((pl.Squeezed(), D_MODEL, D_MODEL), lambda i: (0, 0, 0),
                         pipeline_mode=pl.Buffered(1)),
            pl.BlockSpec((pl.Squeezed(), PLE_DIM, D_MODEL), lambda i: (0, 0, 0),
                         pipeline_mode=pl.Buffered(1)),
        ],
        out_specs=[pl.BlockSpec((bm, D_MODEL), row), _resident(x1_s)],
        out_shape=[_sds((m, D_MODEL), F32), _sds((ms, D_MODEL), F32)],
        compiler_params=_params(1),
        name="ffn_norm_ple",
    )(f, x1, p, f_s, x1_s, p_s, g_post, w_ple_gate, w_ple)


def _rope_tables(pos):
    half = HEAD_DIM // 2
    inv = ROPE_THETA ** (-np.arange(half, dtype=np.float64) / half)
    ang = np.asarray(pos, dtype=np.float64)[:, None] * inv[None, :]
    cos, sin = np.cos(ang), np.sin(ang)
    reps = LANES // HEAD_DIM
    return (jnp.asarray(np.tile(np.concatenate([cos, cos], axis=1), (1, reps)), dtype=F32),
            jnp.asarray(np.tile(np.concatenate([-sin, sin], axis=1), (1, reps)), dtype=F32))


TILES = dict(qkv=dict(bm=1024, rc=256), attn=128, convmix=dict(bm=1024, bn=512, split=2, rc=512),
             merge=dict(bm=1024, bn=1024, split=4, rc=1024), oproj=dict(bm=512, rc=128),
             ffn_up=dict(bm=2048, bn=512, split=1, rc=1024), ffn_down=dict(bm=1024, bn=1024, rc=1024),
             final=dict(bm=512, rc=512))


def kernel(x_prompt, x_sample, state_attn_k, state_attn_v, state_conv, state_ffn_conv, p_prompt, p_sample,
           norm_mix_pre, norm_mix_post, norm_ffn_pre, norm_ffn_post, w_in, attn_sinks, conv_w, w_attn_branch,
           w_conv_branch, b_gate, w_o, w_ffn_up, ffn_conv_w, ffn_conv_b, w_ffn_down, w_ple, w_ple_gate):
    assert w_in.shape[0] == 1, "weights are indexed with a squeezed depth axis of size 1"
    batch, seq, _ = x_prompt.shape
    ns, dec_seq, _ = x_sample.shape
    assert dec_seq == 1
    m = batch * seq
    x = x_prompt.reshape(m, D_MODEL)
    xs = x_sample.reshape(ns, D_MODEL)
    rope_p = _rope_tables(np.arange(seq))
    rope_s = _rope_tables(np.full((ns,), PAST_LEN))

    h, q, k, v, hs, qs, ks, vs = _qkv(x, xs, norm_mix_pre, w_in, rope_p, rope_s, **TILES["qkv"])

    merge_tiles = TILES["merge"]
    a, w_o_bf, w_attn_branch_bf, w_conv_branch_bf = _attn_prompt(
        attn_sinks.reshape(N_HEADS), q, k, v, w_o, w_attn_branch, w_conv_branch,
        n_seq=batch, seq_len=seq, bq=TILES["attn"])
    a3, new_ks, new_vs = _attn_sample(
        attn_sinks.reshape(N_HEADS, 1), qs.reshape(ns, N_HEADS, HEAD_DIM), ks.reshape(ns, 1, KV_WIDTH),
        vs.reshape(ns, 1, KV_WIDTH), state_attn_k.reshape(ns, WINDOW, KV_WIDTH),
        state_attn_v.reshape(ns, WINDOW, KV_WIDTH))
    a_s = a3.reshape(ns, ATTN_WIDTH)

    b, new_conv_p, b_s, pre_s, w_gates_bf = _convmix(
        h, hs, w_in, conv_w, state_conv[0, :, 0], state_conv[0, :, 1], seq_len=seq,
        gate_group=merge_tiles["bn"] // merge_tiles["split"], **TILES["convmix"])
    merged, merged_s = _merge(h, a, b, hs, a_s, b_s, w_gates_bf, w_attn_branch_bf, w_conv_branch_bf, b_gate,
                              **merge_tiles)
    x1, h2, x1_s, h2_s = _oproj(merged, x, merged_s, xs, w_o_bf, norm_mix_post, norm_ffn_pre,
                                **TILES["oproj"])
    act, new_ffn_p, act_s, up_s, w_ffn_down_bf = _ffn_up(
        h2, h2_s, w_ffn_up, ffn_conv_w, ffn_conv_b, state_ffn_conv[0, :, 0], state_ffn_conv[0, :, 1], w_ffn_down,
        seq_len=seq, **TILES["ffn_up"])
    f, f_s, w_ple_gate_bf, w_ple_bf = _ffn_down(act, act_s, w_ffn_down_bf, w_ple_gate, w_ple, **TILES["ffn_down"])
    y, y_s = _final(f, x1, p_prompt.reshape(m, PLE_DIM), f_s, x1_s, p_sample.reshape(ns, PLE_DIM),
                    norm_ffn_post, w_ple_gate_bf, w_ple_bf, **TILES["final"])

    kv_tail = lambda t: t.reshape(batch, seq, KV_WIDTH)[:, seq - WINDOW:].reshape(batch, WINDOW, N_KV_HEADS, HEAD_DIM)
    kv_state = lambda t: t.reshape(ns, WINDOW, N_KV_HEADS, HEAD_DIM)
    return (y.reshape(batch, seq, D_MODEL), y_s.reshape(ns, dec_seq, D_MODEL),
            kv_tail(k)[None], kv_tail(v)[None], new_conv_p[None], new_ffn_p[None],
            kv_state(new_ks)[None], kv_state(new_vs)[None],
            jnp.stack([state_conv[0, :, 1], pre_s], axis=1)[None],
            jnp.stack([state_ffn_conv[0, :, 1], up_s], axis=1)[None])
```

```python
import functools

import jax
import jax.numpy as jnp
import numpy as np
from jax import lax
from jax.experimental import pallas as pl
from jax.experimental.pallas import tpu as pltpu

D_MODEL = 2048
N_HEADS = 16
N_KV_HEADS = 2
HEAD_DIM = 64
GROUP = N_HEADS // N_KV_HEADS
ATTN_WIDTH = N_HEADS * HEAD_DIM
KV_WIDTH = N_KV_HEADS * HEAD_DIM
QKV_WIDTH = ATTN_WIDTH + 2 * KV_WIDTH
WINDOW = 128
PAST_LEN = 16384
ROPE_THETA = 10000.0
CONV_WIDTH = D_MODEL // 2
D_FF = 5632
PLE_DIM = 256
RMS_EPS = 1e-6
OFF_CB = QKV_WIDTH
OFF_CC = OFF_CB + CONV_WIDTH
OFF_CU = OFF_CC + CONV_WIDTH
OFF_GA = OFF_CU + CONV_WIDTH

LANES = 128
UNIT_HEADS = GROUP
MASKED = -1e30
VMEM_LIMIT = 56 * 1024 * 1024

BF16 = jnp.bfloat16
F32 = jnp.float32


def _params(n_axes):
    return pltpu.CompilerParams(dimension_semantics=("arbitrary",) * n_axes,
                                vmem_limit_bytes=VMEM_LIMIT)


def _rms(x, g):
    return x * lax.rsqrt(jnp.mean(x * x, axis=-1, keepdims=True) + RMS_EPS) * g


def _dot(a, b):
    return jnp.dot(a, b, preferred_element_type=F32)


def _walk_chunks(i, bm, rc, chunk_fn, side_fn=None):
    rc = min(bm, rc)

    def walk(first):
        for c in range(bm // rc):
            chunk_fn(pl.ds(c * rc, rc), first and c == 0)
        if side_fn is not None:
            side_fn()

    @pl.when(i == 0)
    def _():
        walk(True)

    @pl.when(i > 0)
    def _():
        walk(False)


def _ride(ref, rows, sample_ref, ride):
    x = ref[rows, :]
    return jnp.concatenate([x, sample_ref[...]], axis=0) if ride else x


def _cast_gate_slab(src_ref, dst_ref, group):
    for t in range(D_MODEL // group):
        src = slice(t * group, (t + 1) * group)
        dst_ref[:, 2 * t * group:(2 * t + 1) * group] = src_ref[:, src].astype(BF16)
        dst_ref[:, (2 * t + 1) * group:(2 * t + 2) * group] = (
            src_ref[:, D_MODEL + t * group:D_MODEL + (t + 1) * group].astype(BF16))


def _gate_slab_specs(slab, step_fn):
    gate_in = pl.BlockSpec((pl.Squeezed(), pl.Element(slab), pl.Element(2 * D_MODEL)),
                           lambda *g: (0, pl.multiple_of(step_fn(*g) * slab, slab), OFF_GA))
    gate_out = pl.BlockSpec((slab, 2 * D_MODEL), lambda *g: (step_fn(*g), 0))
    return gate_in, gate_out


def _wcol(k_rows, block_cols, col_fn):
    return pl.BlockSpec((pl.Squeezed(), pl.Element(k_rows), pl.Element(block_cols)),
                        lambda *g: (0, 0, pl.multiple_of(col_fn(*g), LANES)))


def _resident(arr):
    zeros = (0,) * arr.ndim
    return pl.BlockSpec(arr.shape, lambda *g: zeros)


def _sds(shape, dtype):
    return jax.ShapeDtypeStruct(shape, dtype)


def _qkv_kernel(rc, x_ref, xs_ref, g_ref, w_ref, cos_ref, sin_ref, coss_ref, sins_ref,
                h_ref, q_ref, k_ref, v_ref, hs_ref, qs_ref, ks_ref, vs_ref, wbf_ref):
    i = pl.program_id(0)
    n_q = ATTN_WIDTH // LANES

    @pl.when(i == 0)
    def _():
        wbf_ref[...] = w_ref[...].astype(BF16)

    def chunk(rows, ride):
        n = rows.size

        def put(ref, sample_ref, cols, val):
            ref[rows, cols] = val[:n]
            if ride:
                sample_ref[:, cols] = val[n:]

        h = _rms(_ride(x_ref, rows, xs_ref, ride), g_ref[...]).astype(BF16)
        put(h_ref, hs_ref, slice(None), h)
        proj = _dot(h, wbf_ref[...])
        cos = _ride(cos_ref, rows, coss_ref, ride)
        sin = _ride(sin_ref, rows, sins_ref, ride)
        lane = lax.broadcasted_iota(jnp.int32, cos.shape, 1)
        first_half = (lane % HEAD_DIM) < (HEAD_DIM // 2)
        for c in range(n_q + 1):
            cols = slice(c * LANES, (c + 1) * LANES)
            xg = proj[:, cols]
            partner = jnp.where(first_half, pltpu.roll(xg, LANES - HEAD_DIM // 2, 1),
                                pltpu.roll(xg, HEAD_DIM // 2, 1))
            yg = xg * cos + partner * sin
            if c < n_q:
                put(q_ref, qs_ref, cols, (yg * (HEAD_DIM ** -0.5)).astype(BF16))
            else:
                put(k_ref, ks_ref, slice(None), yg)
        put(v_ref, vs_ref, slice(None), proj[:, ATTN_WIDTH + KV_WIDTH:QKV_WIDTH])

    _walk_chunks(i, x_ref.shape[0], rc, chunk)


def _qkv(x, xs, g, w_in, rope_p, rope_s, *, bm, rc):
    m, ms = x.shape[0], xs.shape[0]
    t_tiles = rope_p[0].shape[0] // bm
    row = lambda i: (i, 0)
    pos = lambda i: (i % t_tiles, 0)
    outs = lambda rows: [_sds((rows, D_MODEL), BF16), _sds((rows, ATTN_WIDTH), BF16),
                         _sds((rows, KV_WIDTH), F32), _sds((rows, KV_WIDTH), F32)]
    out_p, out_s = outs(m), outs(ms)
    return pl.pallas_call(
        functools.partial(_qkv_kernel, rc),
        grid=(m // bm,),
        in_specs=[
            pl.BlockSpec((bm, D_MODEL), row),
            _resident(xs),
            _resident(g),
            pl.BlockSpec((pl.Squeezed(), D_MODEL, QKV_WIDTH), lambda i: (0, 0, 0),
                         pipeline_mode=pl.Buffered(1)),
            pl.BlockSpec((bm, LANES), pos),
            pl.BlockSpec((bm, LANES), pos),
            _resident(rope_s[0]),
            _resident(rope_s[1]),
        ],
        out_specs=[pl.BlockSpec((bm, s.shape[1]), row) for s in out_p] + [_resident(s) for s in out_s],
        out_shape=out_p + out_s,
        scratch_shapes=[pltpu.VMEM((D_MODEL, QKV_WIDTH), BF16)],
        compiler_params=_params(1),
        name="qkv_rope",
    )(x, xs, g, w_in, rope_p[0], rope_p[1], rope_s[0], rope_s[1])


def _attn_prompt_kernel(sinks_ref, q_ref, kc_ref, kp_ref, vc_ref, vp_ref, wo_ref, wa_ref, wb_ref,
                        o_ref, wo_bf_ref, wa_bf_ref, wb_bf_ref):
    wo_bf_ref[...] = wo_ref[...].astype(BF16)
    wa_bf_ref[...] = wa_ref[...].astype(BF16)
    wb_bf_ref[...] = wb_ref[...].astype(BF16)

    n_sub = q_ref.shape[0] // WINDOW
    qi = lax.broadcasted_iota(jnp.int32, (WINDOW, WINDOW), 0)
    ci = lax.broadcasted_iota(jnp.int32, (WINDOW, WINDOW), 1)
    own = ci <= qi
    first_valid = own | (pl.program_id(1) > 0)
    k_all = jnp.concatenate([kp_ref[...], kc_ref[...]], axis=0).astype(BF16)
    v_all = jnp.concatenate([vp_ref[...], vc_ref[...]], axis=0).astype(BF16)
    units = [(sub, hg) for sub in range(n_sub) for hg in range(N_HEADS // UNIT_HEADS)]
    heads_of = lambda hg: range(hg * UNIT_HEADS, (hg + 1) * UNIT_HEADS)
    q_rows = lambda sub: slice(sub * WINDOW, (sub + 1) * WINDOW)
    kv_rows = lambda sub: slice(sub * WINDOW, (sub + 2) * WINDOW)
    kv_cols = lambda hg: slice((hg * UNIT_HEADS // GROUP) * HEAD_DIM, (hg * UNIT_HEADS // GROUP + 1) * HEAD_DIM)

    scores = []
    for sub, kh in units:
        qs = jnp.concatenate([q_ref[q_rows(sub), h * HEAD_DIM:(h + 1) * HEAD_DIM] for h in heads_of(kh)], axis=0)
        scores.append(lax.dot_general(qs, k_all[kv_rows(sub), kv_cols(kh)], (((1,), (1,)), ((), ())),
                                      preferred_element_type=F32))
    probs, dens = [], []
    for (sub, kh), s_all in zip(units, scores):
        ps, ds = [], []
        for g, h in enumerate(heads_of(kh)):
            s2 = s_all[g * WINDOW:(g + 1) * WINDOW, :]
            s = jnp.where(own, s2[:, WINDOW:], s2[:, :WINDOW])
            if sub == 0:
                s = jnp.where(first_valid, s, MASKED)
            sink = sinks_ref[h]
            m = jnp.maximum(jnp.max(s, axis=-1, keepdims=True), sink)
            p = jnp.exp(s - m)
            ds.append(jnp.sum(p, axis=-1, keepdims=True) + jnp.exp(sink - m))
            ps.append(jnp.concatenate([jnp.where(own, 0.0, p), jnp.where(own, p, 0.0)], axis=1).astype(BF16))
        probs.append(jnp.concatenate(ps, axis=0))
        dens.append(ds)
    outs = [_dot(p_all, v_all[kv_rows(sub), kv_cols(kh)]) for (sub, kh), p_all in zip(units, probs)]
    for (sub, kh), o_all, ds in zip(units, outs, dens):
        for g, h in enumerate(heads_of(kh)):
            o = o_all[g * WINDOW:(g + 1) * WINDOW, :] / ds[g]
            o_ref[q_rows(sub), h * HEAD_DIM:(h + 1) * HEAD_DIM] = o.astype(BF16)


def _attn_prompt(sinks, q, k, v, w_o, w_attn_branch, w_conv_branch, *, n_seq, seq_len, bq):
    nb = seq_len // bq
    sub = bq // WINDOW
    steps = n_seq * nb
    slab, bslab = D_MODEL // steps, ATTN_WIDTH // steps
    cur = lambda n, b: (n * nb + b, 0)
    prev = lambda n, b: (jnp.maximum((n * nb + b) * sub - 1, 0), 0)
    wslab = pl.BlockSpec((pl.Squeezed(), slab, D_MODEL), lambda n, b: (0, n * nb + b, 0))
    bspec = pl.BlockSpec((pl.Squeezed(), bslab, D_MODEL), lambda n, b: (0, n * nb + b, 0))
    return pl.pallas_call(
        _attn_prompt_kernel,
        grid=(n_seq, nb),
        in_specs=[
            pl.BlockSpec(memory_space=pltpu.SMEM),
            pl.BlockSpec((bq, ATTN_WIDTH), cur),
            pl.BlockSpec((bq, KV_WIDTH), cur),
            pl.BlockSpec((WINDOW, KV_WIDTH), prev),
            pl.BlockSpec((bq, KV_WIDTH), cur),
            pl.BlockSpec((WINDOW, KV_WIDTH), prev),
            wslab, bspec, bspec,
        ],
        out_specs=[pl.BlockSpec((bq, ATTN_WIDTH), cur), wslab, bspec, bspec],
        out_shape=[_sds((n_seq * seq_len, ATTN_WIDTH), BF16), _sds(w_o.shape, BF16),
                   _sds(w_attn_branch.shape, BF16), _sds(w_conv_branch.shape, BF16)],
        compiler_params=_params(2),
        name="attn_prompt",
    )(sinks, q, k, k, v, v, w_o, w_attn_branch, w_conv_branch)


def _attn_sample_kernel(sinks_ref, q_ref, kn_ref, vn_ref, sk_ref, sv_ref, a_ref, nk_ref, nv_ref):
    nk_ref[:, 0:WINDOW - 1, :] = sk_ref[:, 1:WINDOW, :]
    nk_ref[:, WINDOW - 1:WINDOW, :] = kn_ref[...]
    nv_ref[:, 0:WINDOW - 1, :] = sv_ref[:, 1:WINDOW, :]
    nv_ref[:, WINDOW - 1:WINDOW, :] = vn_ref[...]
    for kh in range(N_KV_HEADS):
        cols = slice(kh * HEAD_DIM, (kh + 1) * HEAD_DIM)
        heads = slice(kh * GROUP, (kh + 1) * GROUP)
        k = nk_ref[:, :, cols].astype(BF16)
        v = nv_ref[:, :, cols].astype(BF16)
        s = jnp.einsum("ngd,ncd->ngc", q_ref[:, heads, :], k, preferred_element_type=F32)
        sink = sinks_ref[heads, :][None]
        m = jnp.maximum(jnp.max(s, axis=-1, keepdims=True), sink)
        p = jnp.exp(s - m)
        den = jnp.sum(p, axis=-1, keepdims=True) + jnp.exp(sink - m)
        o = jnp.einsum("ngc,ncd->ngd", p.astype(BF16), v, preferred_element_type=F32) / den
        a_ref[:, heads, :] = o.astype(BF16)


def _attn_sample(sinks_col, q3, k_new3, v_new3, state_k, state_v):
    n = q3.shape[0]
    return pl.pallas_call(
        _attn_sample_kernel,
        out_shape=[_sds((n, N_HEADS, HEAD_DIM), BF16), _sds((n, WINDOW, KV_WIDTH), F32),
                   _sds((n, WINDOW, KV_WIDTH), F32)],
        compiler_params=pltpu.CompilerParams(vmem_limit_bytes=VMEM_LIMIT),
        name="attn_sample",
    )(sinks_col, q3, k_new3, v_new3, state_k, state_v)


def _shift_rows(cur, halo):
    rid = lax.broadcasted_iota(jnp.int32, cur.shape, 0)
    h0 = halo[0:1, :]
    h1 = halo[1:2, :]
    s1 = jnp.where(rid == 0, h1, pltpu.roll(cur, 1, 0))
    s2 = jnp.where(rid == 0, h0, jnp.where(rid == 1, h1, pltpu.roll(cur, 2, 0)))
    return s1, s2


def _conv3(cw, cur, s1, s2):
    return cw[0:1, :] * s2 + cw[1:2, :] * s1 + cw[2:3, :] * cur


def _conv_tile(i, tiles_per_seq, split, parts, rc, bm, halo_ref, st_ref, project, group_fn, side_fn):
    gw = halo_ref.shape[1] // split
    gcols = [slice(grp * gw, (grp + 1) * gw) for grp in range(split)]

    @pl.when(i % tiles_per_seq == 0)
    def _():
        halo_ref[...] = jnp.zeros_like(halo_ref)

    halos = {}

    def chunk(rows, ride):
        last = rows.start + rows.size == bm
        full = None if last else project(rows, ride, slice(None))
        for grp in range(split):
            pcols = slice(parts * grp * gw, parts * (grp + 1) * gw)
            proj = project(rows, ride, pcols) if last else full[:, pcols]
            halo = halos.get(grp)
            if halo is None:
                halo = halo_ref[:, gcols[grp]]
            cur = group_fn(rows, ride, grp, proj, lambda cur, halo=halo: _shift_rows(cur, halo))
            halos[grp] = cur[cur.shape[0] - 2:, :]
        if last:
            for grp in range(split):
                halo_ref[:, gcols[grp]] = halos.pop(grp)

    _walk_chunks(i, bm, rc, chunk, side_fn)

    @pl.when(i % tiles_per_seq == tiles_per_seq - 1)
    def _():
        st_ref[...] = halo_ref[...]


def _convmix_kernel(tiles_per_seq, split, rc, gate_group, h_ref, hs_ref, wb_ref, wc_ref, wu_ref, cw_ref, p2_ref, p1_ref,
                    wgate_ref, o_ref, st_ref, os_ref, pres_ref, wgate_bf_ref, wbf_ref, halo_ref):
    i = pl.program_id(1)
    gw = wb_ref.shape[1] // split

    @pl.when(i == 0)
    def _():
        for grp in range(split):
            src = slice(grp * gw, (grp + 1) * gw)
            for part, w_ref in enumerate((wb_ref, wc_ref, wu_ref)):
                wbf_ref[:, (3 * grp + part) * gw:(3 * grp + part + 1) * gw] = w_ref[:, src].astype(BF16)

    def project(rows, ride, pcols):
        return _dot(_ride(h_ref, rows, hs_ref, ride), wbf_ref[:, pcols])

    def group(rows, ride, grp, proj, shifts):
        n = rows.size
        cols = slice(grp * gw, (grp + 1) * gw)
        cb, cur = proj[:, 0:gw], proj[:, gw:2 * gw] * proj[:, 2 * gw:3 * gw]
        cw = cw_ref[:, cols]
        s1, s2 = shifts(cur[:n])
        o_ref[rows, cols] = (cb[:n] * _conv3(cw, cur[:n], s1, s2)).astype(BF16)
        if ride:
            pres_ref[:, cols] = cur[n:]
            os_ref[:, cols] = (cb[n:] * _conv3(cw, cur[n:], p1_ref[:, cols], p2_ref[:, cols])).astype(BF16)
        return cur[:n]

    _conv_tile(i, tiles_per_seq, split, 3, rc, h_ref.shape[0], halo_ref, st_ref, project, group,
               lambda: _cast_gate_slab(wgate_ref, wgate_bf_ref, gate_group))


def _convmix(h, hs, w_in, conv_w, prev2, prev1, *, bm, bn, split, rc, seq_len, gate_group):
    m, ms = h.shape[0], hs.shape[0]
    tiles_per_seq = seq_len // bm
    n_m = m // bm
    gate_in, gate_out = _gate_slab_specs(D_MODEL // (CONV_WIDTH // bn * n_m), lambda j, i: j * n_m + i)
    tile = pl.BlockSpec((bm, bn), lambda j, i: (i, j))
    stile = pl.BlockSpec((ms, bn), lambda j, i: (0, j))
    wspec = lambda off: _wcol(D_MODEL, bn, lambda j, i: off + j * bn)
    return pl.pallas_call(
        functools.partial(_convmix_kernel, tiles_per_seq, split, rc, gate_group),
        grid=(CONV_WIDTH // bn, n_m),
        in_specs=[
            pl.BlockSpec((bm, D_MODEL), lambda j, i: (i, 0)),
            _resident(hs),
            wspec(OFF_CB), wspec(OFF_CC), wspec(OFF_CU),
            pl.BlockSpec((pl.Squeezed(), 3, bn), lambda j, i: (0, 0, j)),
            stile, stile, gate_in,
        ],
        out_specs=[
            tile,
            pl.BlockSpec((pl.Squeezed(), 2, bn), lambda j, i: (i // tiles_per_seq, 0, j)),
            stile, stile, gate_out,
        ],
        out_shape=[_sds((m, CONV_WIDTH), BF16), _sds((m // seq_len, 2, CONV_WIDTH), F32),
                   _sds((ms, CONV_WIDTH), BF16), _sds((ms, CONV_WIDTH), F32), _sds((D_MODEL, 2 * D_MODEL), BF16)],
        scratch_shapes=[pltpu.VMEM((D_MODEL, 3 * bn), BF16), pltpu.VMEM((2, bn), F32)],
        compiler_params=_params(2),
        name="convmix",
    )(h, hs, w_in, w_in, w_in, conv_w, prev2, prev1, w_in)


def _merge_kernel(split, rc, h_ref, a_ref, b_ref, hs_ref, as_ref, bs_ref, wg_ref, wa_ref, wb_ref, bg_ref,
                  o_ref, os_ref):
    gw = wa_ref.shape[1] // split

    def chunk(rows, ride):
        n = rows.size
        h = _ride(h_ref, rows, hs_ref, ride)
        a = _ride(a_ref, rows, as_ref, ride)
        b = _ride(b_ref, rows, bs_ref, ride)
        for grp in range(split):
            cols = slice(grp * gw, (grp + 1) * gw)
            gates = _dot(h, wg_ref[:, 2 * grp * gw:2 * (grp + 1) * gw])
            ga = jax.nn.sigmoid(gates[:, 0:gw] + bg_ref[0:1, cols])
            gb = jax.nn.sigmoid(gates[:, gw:2 * gw] + bg_ref[1:2, cols])
            merged = (ga * _dot(a, wa_ref[:, cols]) + gb * _dot(b, wb_ref[:, cols])).astype(BF16)
            o_ref[rows, cols] = merged[:n]
            if ride:
                os_ref[:, cols] = merged[n:]

    _walk_chunks(pl.program_id(1), h_ref.shape[0], rc, chunk)


def _merge(h, a, b, hs, a_s, b_s, w_gates, w_attn_branch, w_conv_branch, b_gate, *, bm, bn, split, rc):
    m, ms = h.shape[0], hs.shape[0]
    row = lambda j, i: (i, 0)
    return pl.pallas_call(
        functools.partial(_merge_kernel, split, rc),
        grid=(D_MODEL // bn, m // bm),
        in_specs=[
            pl.BlockSpec((bm, D_MODEL), row),
            pl.BlockSpec((bm, ATTN_WIDTH), row),
            pl.BlockSpec((bm, CONV_WIDTH), row),
            _resident(hs), _resident(a_s), _resident(b_s),
            pl.BlockSpec((D_MODEL, 2 * bn), lambda j, i: (0, j)),
            pl.BlockSpec((pl.Squeezed(), ATTN_WIDTH, bn), lambda j, i: (0, 0, j)),
            pl.BlockSpec((pl.Squeezed(), CONV_WIDTH, bn), lambda j, i: (0, 0, j)),
            pl.BlockSpec((pl.Squeezed(), 2, bn), lambda j, i: (0, 0, j)),
        ],
        out_specs=[pl.BlockSpec((bm, bn), lambda j, i: (i, j)),
                   pl.BlockSpec((ms, bn), lambda j, i: (0, j))],
        out_shape=[_sds((m, D_MODEL), BF16), _sds((ms, D_MODEL), BF16)],
        compiler_params=_params(2),
        name="merge",
    )(h, a, b, hs, a_s, b_s, w_gates, w_attn_branch, w_conv_branch, b_gate)


def _oproj_kernel(rc, mg_ref, x_ref, mgs_ref, xs_ref, wo_ref, gpost_ref, gpre_ref,
                  x1_ref, h2_ref, x1s_ref, h2s_ref):
    def chunk(rows, ride):
        n = rows.size
        y = _dot(_ride(mg_ref, rows, mgs_ref, ride), wo_ref[...])
        x1 = _ride(x_ref, rows, xs_ref, ride) + _rms(y, gpost_ref[...])
        h2 = _rms(x1, gpre_ref[...]).astype(BF16)
        x1_ref[rows, :] = x1[:n]
        h2_ref[rows, :] = h2[:n]
        if ride:
            x1s_ref[...] = x1[n:]
            h2s_ref[...] = h2[n:]

    _walk_chunks(pl.program_id(0), x_ref.shape[0], rc, chunk)


def _oproj(merged, x, merged_s, xs, w_o, g_post, g_pre, *, bm, rc):
    m, ms = x.shape[0], xs.shape[0]
    tile = pl.BlockSpec((bm, D_MODEL), lambda i: (i, 0))
    return pl.pallas_call(
        functools.partial(_oproj_kernel, rc),
        grid=(m // bm,),
        in_specs=[
            tile, tile, _resident(merged_s), _resident(xs),
            pl.BlockSpec((pl.Squeezed(), D_MODEL, D_MODEL), lambda i: (0, 0, 0),
                         pipeline_mode=pl.Buffered(1)),
            _resident(g_post), _resident(g_pre),
        ],
        out_specs=[tile, tile, _resident(xs), _resident(merged_s)],
        out_shape=[_sds((m, D_MODEL), F32), _sds((m, D_MODEL), BF16),
                   _sds((ms, D_MODEL), F32), _sds((ms, D_MODEL), BF16)],
        compiler_params=_params(1),
        name="oproj_norm",
    )(merged, x, merged_s, xs, w_o, g_post, g_pre)


def _ffn_up_kernel(tiles_per_seq, split, rc, h_ref, hs_ref, wa_ref, wg_ref, cw_ref, cb_ref, p2_ref, p1_ref, wd_ref,
                   o_ref, st_ref, os_ref, ups_ref, wd_bf_ref, wbf_ref, halo_ref):
    i = pl.program_id(1)
    gw = wa_ref.shape[1] // split

    @pl.when(i == 0)
    def _():
        for grp in range(split):
            src = slice(grp * gw, (grp + 1) * gw)
            wbf_ref[:, 2 * grp * gw:(2 * grp + 1) * gw] = wa_ref[:, src].astype(BF16)
            wbf_ref[:, (2 * grp + 1) * gw:(2 * grp + 2) * gw] = wg_ref[:, src].astype(BF16)

    def project(rows, ride, pcols):
        return _dot(_ride(h_ref, rows, hs_ref, ride), wbf_ref[:, pcols])

    def group(rows, ride, grp, proj, shifts):
        n = rows.size
        cols = slice(grp * gw, (grp + 1) * gw)
        cur, gate = proj[:, 0:gw], proj[:, gw:2 * gw]
        cw = cw_ref[:, cols]

        def gated(cur, s1, s2, gate):
            act = jax.nn.gelu(_conv3(cw, cur, s1, s2) + cb_ref[:, cols], approximate=True)
            return (act * gate).astype(BF16)

        s1, s2 = shifts(cur[:n])
        o_ref[rows, cols] = gated(cur[:n], s1, s2, gate[:n])
        if ride:
            ups_ref[:, cols] = cur[n:]
            os_ref[:, cols] = gated(cur[n:], p1_ref[:, cols], p2_ref[:, cols], gate[n:])
        return cur[:n]

    def cast_down_slab():
        wd_bf_ref[...] = wd_ref[...].astype(BF16)

    _conv_tile(i, tiles_per_seq, split, 2, rc, h_ref.shape[0], halo_ref, st_ref, project, group, cast_down_slab)


def _ffn_up(h2, h2s, w_ffn_up, ffn_conv_w, ffn_conv_b, prev2, prev1, w_ffn_down, *, bm, bn, split, rc, seq_len):
    m, ms = h2.shape[0], h2s.shape[0]
    tiles_per_seq = seq_len // bm
    n_tiles = D_FF // bn
    n_m = m // bm
    slab = D_FF // (n_tiles * n_m)
    wslab = pl.BlockSpec((pl.Squeezed(), slab, D_MODEL), lambda j, i: (0, j * n_m + i, 0))
    tile = pl.BlockSpec((bm, bn), lambda j, i: (i, j))
    stile = pl.BlockSpec((ms, bn), lambda j, i: (0, j))
    return pl.pallas_call(
        functools.partial(_ffn_up_kernel, tiles_per_seq, split, rc),
        grid=(n_tiles, m // bm),
        in_specs=[
            pl.BlockSpec((bm, D_MODEL), lambda j, i: (i, 0)),
            _resident(h2s),
            pl.BlockSpec((pl.Squeezed(), D_MODEL, bn), lambda j, i: (0, 0, j)),
            pl.BlockSpec((pl.Squeezed(), D_MODEL, bn), lambda j, i: (0, 0, n_tiles + j)),
            pl.BlockSpec((pl.Squeezed(), 3, bn), lambda j, i: (0, 0, j)),
            pl.BlockSpec((1, bn), lambda j, i: (0, j)),
            stile, stile, wslab,
        ],
        out_specs=[
            tile,
            pl.BlockSpec((pl.Squeezed(), 2, bn), lambda j, i: (i // tiles_per_seq, 0, j)),
            stile, stile, wslab,
        ],
        out_shape=[_sds((m, D_FF), BF16), _sds((m // seq_len, 2, D_FF), F32),
                   _sds((ms, D_FF), BF16), _sds((ms, D_FF), F32), _sds(w_ffn_down.shape, BF16)],
        scratch_shapes=[pltpu.VMEM((D_MODEL, 2 * bn), BF16), pltpu.VMEM((2, bn), F32)],
        compiler_params=_params(2),
        name="ffn_up",
    )(h2, h2s, w_ffn_up, w_ffn_up, ffn_conv_w, ffn_conv_b, prev2, prev1, w_ffn_down)


def _ffn_down_kernel(rc, a_ref, as_ref, w_ref, wg_ref, wp_ref, o_ref, os_ref, wg_bf_ref, wp_bf_ref):
    def cast_ple_slabs():
        wg_bf_ref[...] = wg_ref[...].astype(BF16)
        wp_bf_ref[...] = wp_ref[...].astype(BF16)

    def chunk(rows, ride):
        n = rows.size
        o = _dot(_ride(a_ref, rows, as_ref, ride), w_ref[...]).astype(BF16)
        o_ref[rows, :] = o[:n]
        if ride:
            os_ref[...] = o[n:]

    _walk_chunks(pl.program_id(1), a_ref.shape[0], rc, chunk, cast_ple_slabs)


def _ffn_down(act, act_s, w_ffn_down, w_ple_gate, w_ple, *, bm, bn, rc):
    m, ms = act.shape[0], act_s.shape[0]
    n_m = m // bm
    steps = D_MODEL // bn * n_m
    slab = lambda w: pl.BlockSpec((pl.Squeezed(), w.shape[1] // steps, D_MODEL), lambda j, i: (0, j * n_m + i, 0))
    return pl.pallas_call(
        functools.partial(_ffn_down_kernel, rc),
        grid=(D_MODEL // bn, n_m),
        in_specs=[
            pl.BlockSpec((bm, D_FF), lambda j, i: (i, 0)),
            _resident(act_s),
            pl.BlockSpec((pl.Squeezed(), D_FF, bn), lambda j, i: (0, 0, j)),
            slab(w_ple_gate), slab(w_ple),
        ],
        out_specs=[pl.BlockSpec((bm, bn), lambda j, i: (i, j)),
                   pl.BlockSpec((ms, bn), lambda j, i: (0, j)),
                   slab(w_ple_gate), slab(w_ple)],
        out_shape=[_sds((m, D_MODEL), BF16), _sds((ms, D_MODEL), BF16), _sds(w_ple_gate.shape, BF16),
                   _sds(w_ple.shape, BF16)],
        compiler_params=_params(2),
        name="ffn_down",
    )(act, act_s, w_ffn_down, w_ple_gate, w_ple)


def _final_kernel(rc, f_ref, x1_ref, p_ref, fs_ref, x1s_ref, ps_ref, g_ref, wg_ref, wp_ref, y_ref, ys_ref):
    def chunk(rows, ride):
        n = rows.size
        f = _ride(f_ref, rows, fs_ref, ride).astype(F32)
        x2 = _ride(x1_ref, rows, x1s_ref, ride) + _rms(f, g_ref[...])
        gate = jax.nn.sigmoid(_dot(x2.astype(BF16), wg_ref[...]))
        y = x2 + gate * _dot(_ride(p_ref, rows, ps_ref, ride).astype(BF16), wp_ref[...])
        y_ref[rows, :] = y[:n]
        if ride:
            ys_ref[...] = y[n:]

    _walk_chunks(pl.program_id(0), x1_ref.shape[0], rc, chunk)


def _final(f, x1, p, f_s, x1_s, p_s, g_post, w_ple_gate, w_ple, *, bm, rc):
    m, ms = x1.shape[0], x1_s.shape[0]
    row = lambda i: (i, 0)
    return pl.pallas_call(
        functools.partial(_final_kernel, rc),
        grid=(m // bm,),
        in_specs=[
            pl.BlockSpec((bm, D_MODEL), row),
            pl.BlockSpec((bm, D_MODEL), row),
            pl.BlockSpec((bm, PLE_DIM), row),
            _resident(f_s), _resident(x1_s), _resident(p_s),
            _resident(g_post),
            pl.BlockSpec((pl.Squeezed(), D_MODEL, D_MODEL), lambda i: (0, 0, 0),
                         pipeline_mode=pl.Buffered(1)),
            pl.BlockSpec((pl.Squeezed(), PLE_DIM, D_MODEL), lambda i: (0, 0, 0),
                         pipeline_mode=pl.Buffered(1)),
        ],
        out_specs=[pl.BlockSpec((bm, D_MODEL), row), _resident(x1_s)],
        out_shape=[_sds((m, D_MODEL), F32), _sds((ms, D_MODEL), F32)],
        compiler_params=_params(1),
        name="ffn_norm_ple",
    )(f, x1, p, f_s, x1_s, p_s, g_post, w_ple_gate, w_ple)


def _rope_tables(pos):
    half = HEAD_DIM // 2
    inv = ROPE_THETA ** (-np.arange(half, dtype=np.float64) / half)
    ang = np.asarray(pos, dtype=np.float64)[:, None] * inv[None, :]
    cos, sin = np.cos(ang), np.sin(ang)
    reps = LANES // HEAD_DIM
    return (jnp.asarray(np.tile(np.concatenate([cos, cos], axis=1), (1, reps)), dtype=F32),
            jnp.asarray(np.tile(np.concatenate([-sin, sin], axis=1), (1, reps)), dtype=F32))


TILES = dict(qkv=dict(bm=1024, rc=256), attn=128, convmix=dict(bm=1024, bn=512, split=2, rc=512),
             merge=dict(bm=1024, bn=1024, split=4, rc=1024), oproj=dict(bm=512, rc=128),
             ffn_up=dict(bm=2048, bn=512, split=1, rc=1024), ffn_down=dict(bm=1024, bn=1024, rc=1024),
             final=dict(bm=512, rc=512))


def kernel(x_prompt, x_sample, state_attn_k, state_attn_v, state_conv, state_ffn_conv, p_prompt, p_sample,
           norm_mix_pre, norm_mix_post, norm_ffn_pre, norm_ffn_post, w_in, attn_sinks, conv_w, w_attn_branch,
           w_conv_branch, b_gate, w_o, w_ffn_up, ffn_conv_w, ffn_conv_b, w_ffn_down, w_ple, w_ple_gate):
    assert w_in.shape[0] == 1, "weights are indexed with a squeezed depth axis of size 1"
    batch, seq, _ = x_prompt.shape
    ns, dec_seq, _ = x_sample.shape
    assert dec_seq == 1
    m = batch * seq
    x = x_prompt.reshape(m, D_MODEL)
    xs = x_sample.reshape(ns, D_MODEL)
    rope_p = _rope_tables(np.arange(seq))
    rope_s = _rope_tables(np.full((ns,), PAST_LEN))

    h, q, k, v, hs, qs, ks, vs = _qkv(x, xs, norm_mix_pre, w_in, rope_p, rope_s, **TILES["qkv"])

    merge_tiles = TILES["merge"]
    a, w_o_bf, w_attn_branch_bf, w_conv_branch_bf = _attn_prompt(
        attn_sinks.reshape(N_HEADS), q, k, v, w_o, w_attn_branch, w_conv_branch,
        n_seq=batch, seq_len=seq, bq=TILES["attn"])
    a3, new_ks, new_vs = _attn_sample(
        attn_sinks.reshape(N_HEADS, 1), qs.reshape(ns, N_HEADS, HEAD_DIM), ks.reshape(ns, 1, KV_WIDTH),
        vs.reshape(ns, 1, KV_WIDTH), state_attn_k.reshape(ns, WINDOW, KV_WIDTH),
        state_attn_v.reshape(ns, WINDOW, KV_WIDTH))
    a_s = a3.reshape(ns, ATTN_WIDTH)

    b, new_conv_p, b_s, pre_s, w_gates_bf = _convmix(
        h, hs, w_in, conv_w, state_conv[0, :, 0], state_conv[0, :, 1], seq_len=seq,
        gate_group=merge_tiles["bn"] // merge_tiles["split"], **TILES["convmix"])
    merged, merged_s = _merge(h, a, b, hs, a_s, b_s, w_gates_bf, w_attn_branch_bf, w_conv_branch_bf, b_gate,
                              **merge_tiles)
    x1, h2, x1_s, h2_s = _oproj(merged, x, merged_s, xs, w_o_bf, norm_mix_post, norm_ffn_pre,
                                **TILES["oproj"])
    act, new_ffn_p, act_s, up_s, w_ffn_down_bf = _ffn_up(
        h2, h2_s, w_ffn_up, ffn_conv_w, ffn_conv_b, state_ffn_conv[0, :, 0], state_ffn_conv[0, :, 1], w_ffn_down,
        seq_len=seq, **TILES["ffn_up"])
    f, f_s, w_ple_gate_bf, w_ple_bf = _ffn_down(act, act_s, w_ffn_down_bf, w_ple_gate, w_ple, **TILES["ffn_down"])
    y, y_s = _final(f, x1, p_prompt.reshape(m, PLE_DIM), f_s, x1_s, p_sample.reshape(ns, PLE_DIM),
                    norm_ffn_post, w_ple_gate_bf, w_ple_bf, **TILES["final"])

    kv_tail = lambda t: t.reshape(batch, seq, KV_WIDTH)[:, seq - WINDOW:].reshape(batch, WINDOW, N_KV_HEADS, HEAD_DIM)
    kv_state = lambda t: t.reshape(ns, WINDOW, N_KV_HEADS, HEAD_DIM)
    return (y.reshape(batch, seq, D_MODEL), y_s.reshape(ns, dec_seq, D_MODEL),
            kv_tail(k)[None], kv_tail(v)[None], new_conv_p[None], new_ffn_p[None],
            kv_state(new_ks)[None], kv_state(new_vs)[None],
            jnp.stack([state_conv[0, :, 1], pre_s], axis=1)[None],
            jnp.stack([state_ffn_conv[0, :, 1], up_s], axis=1)[None])
```

```python
import functools

import jax
import jax.numpy as jnp
import numpy as np
from jax import lax
from jax.experimental import pallas as pl
from jax.experimental.pallas import tpu as pltpu

D_MODEL = 2048
N_HEADS = 16
N_KV_HEADS = 2
HEAD_DIM = 64
GROUP = N_HEADS // N_KV_HEADS
ATTN_WIDTH = N_HEADS * HEAD_DIM
KV_WIDTH = N_KV_HEADS * HEAD_DIM
QKV_WIDTH = ATTN_WIDTH + 2 * KV_WIDTH
WINDOW = 128
PAST_LEN = 16384
ROPE_THETA = 10000.0
CONV_WIDTH = D_MODEL // 2
D_FF = 5632
PLE_DIM = 256
RMS_EPS = 1e-6
OFF_CB = QKV_WIDTH
OFF_CC = OFF_CB + CONV_WIDTH
OFF_CU = OFF_CC + CONV_WIDTH
OFF_GA = OFF_CU + CONV_WIDTH

LANES = 128
UNIT_HEADS = GROUP
MASKED = -1e30
VMEM_LIMIT = 56 * 1024 * 1024

BF16 = jnp.bfloat16
F32 = jnp.float32


def _params(n_axes):
    return pltpu.CompilerParams(dimension_semantics=("arbitrary",) * n_axes,
                                vmem_limit_bytes=VMEM_LIMIT)


def _rms(x, g):
    return x * lax.rsqrt(jnp.mean(x * x, axis=-1, keepdims=True) + RMS_EPS) * g


def _dot(a, b):
    return jnp.dot(a, b, preferred_element_type=F32)


def _walk_chunks(i, bm, rc, chunk_fn):
    rc = min(bm, rc)

    def walk(first):
        for c in range(bm // rc):
            chunk_fn(pl.ds(c * rc, rc), first and c == 0)

    @pl.when(i == 0)
    def _():
        walk(True)

    @pl.when(i > 0)
    def _():
        walk(False)


def _ride(ref, rows, sample_ref, ride):
    x = ref[rows, :]
    return jnp.concatenate([x, sample_ref[...]], axis=0) if ride else x


def _cast_gate_slab(src_ref, dst_ref, group):
    for t in range(D_MODEL // group):
        src = slice(t * group, (t + 1) * group)
        dst_ref[:, 2 * t * group:(2 * t + 1) * group] = src_ref[:, src].astype(BF16)
        dst_ref[:, (2 * t + 1) * group:(2 * t + 2) * group] = (
            src_ref[:, D_MODEL + t * group:D_MODEL + (t + 1) * group].astype(BF16))


def _gate_slab_specs(slab, step_fn):
    gate_in = pl.BlockSpec((pl.Squeezed(), pl.Element(slab), pl.Element(2 * D_MODEL)),
                           lambda *g: (0, pl.multiple_of(step_fn(*g) * slab, slab), OFF_GA))
    gate_out = pl.BlockSpec((slab, 2 * D_MODEL), lambda *g: (step_fn(*g), 0))
    return gate_in, gate_out


def _wcol(k_rows, block_cols, col_fn):
    return pl.BlockSpec((pl.Squeezed(), pl.Element(k_rows), pl.Element(block_cols)),
                        lambda *g: (0, 0, pl.multiple_of(col_fn(*g), LANES)))


def _resident(arr):
    zeros = (0,) * arr.ndim
    return pl.BlockSpec(arr.shape, lambda *g: zeros)


def _sds(shape, dtype):
    return jax.ShapeDtypeStruct(shape, dtype)


def _qkv_kernel(rc, x_ref, xs_ref, g_ref, w_ref, cos_ref, sin_ref, coss_ref, sins_ref,
                h_ref, q_ref, k_ref, v_ref, hs_ref, qs_ref, ks_ref, vs_ref, wbf_ref):
    i = pl.program_id(0)
    n_q = ATTN_WIDTH // LANES

    @pl.when(i == 0)
    def _():
        wbf_ref[...] = w_ref[...].astype(BF16)

    def chunk(rows, ride):
        n = rows.size

        def put(ref, sample_ref, cols, val):
            ref[rows, cols] = val[:n]
            if ride:
                sample_ref[:, cols] = val[n:]

        h = _rms(_ride(x_ref, rows, xs_ref, ride), g_ref[...]).astype(BF16)
        put(h_ref, hs_ref, slice(None), h)
        proj = _dot(h, wbf_ref[...])
        cos = _ride(cos_ref, rows, coss_ref, ride)
        sin = _ride(sin_ref, rows, sins_ref, ride)
        lane = lax.broadcasted_iota(jnp.int32, cos.shape, 1)
        first_half = (lane % HEAD_DIM) < (HEAD_DIM // 2)
        for c in range(n_q + 1):
            cols = slice(c * LANES, (c + 1) * LANES)
            xg = proj[:, cols]
            partner = jnp.where(first_half, pltpu.roll(xg, LANES - HEAD_DIM // 2, 1),
                                pltpu.roll(xg, HEAD_DIM // 2, 1))
            yg = xg * cos + partner * sin
            if c < n_q:
                put(q_ref, qs_ref, cols, (yg * (HEAD_DIM ** -0.5)).astype(BF16))
            else:
                put(k_ref, ks_ref, slice(None), yg)
        put(v_ref, vs_ref, slice(None), proj[:, ATTN_WIDTH + KV_WIDTH:QKV_WIDTH])

    _walk_chunks(i, x_ref.shape[0], rc, chunk)


def _qkv(x, xs, g, w_in, rope_p, rope_s, *, bm, rc):
    m, ms = x.shape[0], xs.shape[0]
    t_tiles = rope_p[0].shape[0] // bm
    row = lambda i: (i, 0)
    pos = lambda i: (i % t_tiles, 0)
    outs = lambda rows: [_sds((rows, D_MODEL), BF16), _sds((rows, ATTN_WIDTH), BF16),
                         _sds((rows, KV_WIDTH), F32), _sds((rows, KV_WIDTH), F32)]
    out_p, out_s = outs(m), outs(ms)
    return pl.pallas_call(
        functools.partial(_qkv_kernel, rc),
        grid=(m // bm,),
        in_specs=[
            pl.BlockSpec((bm, D_MODEL), row),
            _resident(xs),
            _resident(g),
            pl.BlockSpec((pl.Squeezed(), D_MODEL, QKV_WIDTH), lambda i: (0, 0, 0),
                         pipeline_mode=pl.Buffered(1)),
            pl.BlockSpec((bm, LANES), pos),
            pl.BlockSpec((bm, LANES), pos),
            _resident(rope_s[0]),
            _resident(rope_s[1]),
        ],
        out_specs=[pl.BlockSpec((bm, s.shape[1]), row) for s in out_p] + [_resident(s) for s in out_s],
        out_shape=out_p + out_s,
        scratch_shapes=[pltpu.VMEM((D_MODEL, QKV_WIDTH), BF16)],
        compiler_params=_params(1),
        name="qkv_rope",
    )(x, xs, g, w_in, rope_p[0], rope_p[1], rope_s[0], rope_s[1])


def _attn_prompt_kernel(sinks_ref, q_ref, kc_ref, kp_ref, vc_ref, vp_ref, wo_ref, wa_ref, wb_ref,
                        o_ref, wo_bf_ref, wa_bf_ref, wb_bf_ref):
    wo_bf_ref[...] = wo_ref[...].astype(BF16)
    wa_bf_ref[...] = wa_ref[...].astype(BF16)
    wb_bf_ref[...] = wb_ref[...].astype(BF16)

    n_sub = q_ref.shape[0] // WINDOW
    qi = lax.broadcasted_iota(jnp.int32, (WINDOW, WINDOW), 0)
    ci = lax.broadcasted_iota(jnp.int32, (WINDOW, WINDOW), 1)
    own = ci <= qi
    first_valid = own | (pl.program_id(1) > 0)
    k_all = jnp.concatenate([kp_ref[...], kc_ref[...]], axis=0).astype(BF16)
    v_all = jnp.concatenate([vp_ref[...], vc_ref[...]], axis=0).astype(BF16)
    units = [(sub, hg) for sub in range(n_sub) for hg in range(N_HEADS // UNIT_HEADS)]
    heads_of = lambda hg: range(hg * UNIT_HEADS, (hg + 1) * UNIT_HEADS)
    q_rows = lambda sub: slice(sub * WINDOW, (sub + 1) * WINDOW)
    kv_rows = lambda sub: slice(sub * WINDOW, (sub + 2) * WINDOW)
    kv_cols = lambda hg: slice((hg * UNIT_HEADS // GROUP) * HEAD_DIM, (hg * UNIT_HEADS // GROUP + 1) * HEAD_DIM)

    scores = []
    for sub, kh in units:
        qs = jnp.concatenate([q_ref[q_rows(sub), h * HEAD_DIM:(h + 1) * HEAD_DIM] for h in heads_of(kh)], axis=0)
        scores.append(lax.dot_general(qs, k_all[kv_rows(sub), kv_cols(kh)], (((1,), (1,)), ((), ())),
                                      preferred_element_type=F32))
    probs, dens = [], []
    for (sub, kh), s_all in zip(units, scores):
        ps, ds = [], []
        for g, h in enumerate(heads_of(kh)):
            s2 = s_all[g * WINDOW:(g + 1) * WINDOW, :]
            s = jnp.where(own, s2[:, WINDOW:], s2[:, :WINDOW])
            if sub == 0:
                s = jnp.where(first_valid, s, MASKED)
            sink = sinks_ref[h]
            m = jnp.maximum(jnp.max(s, axis=-1, keepdims=True), sink)
            p = jnp.exp(s - m)
            ds.append(jnp.sum(p, axis=-1, keepdims=True) + jnp.exp(sink - m))
            ps.append(jnp.concatenate([jnp.where(own, 0.0, p), jnp.where(own, p, 0.0)], axis=1).astype(BF16))
        probs.append(jnp.concatenate(ps, axis=0))
        dens.append(ds)
    outs = [_dot(p_all, v_all[kv_rows(sub), kv_cols(kh)]) for (sub, kh), p_all in zip(units, probs)]
    for (sub, kh), o_all, ds in zip(units, outs, dens):
        for g, h in enumerate(heads_of(kh)):
            o = o_all[g * WINDOW:(g + 1) * WINDOW, :] / ds[g]
            o_ref[q_rows(sub), h * HEAD_DIM:(h + 1) * HEAD_DIM] = o.astype(BF16)


def _attn_prompt(sinks, q, k, v, w_o, w_attn_branch, w_conv_branch, *, n_seq, seq_len, bq):
    nb = seq_len // bq
    sub = bq // WINDOW
    steps = n_seq * nb
    slab, bslab = D_MODEL // steps, ATTN_WIDTH // steps
    cur = lambda n, b: (n * nb + b, 0)
    prev = lambda n, b: (jnp.maximum((n * nb + b) * sub - 1, 0), 0)
    wslab = pl.BlockSpec((pl.Squeezed(), slab, D_MODEL), lambda n, b: (0, n * nb + b, 0))
    bspec = pl.BlockSpec((pl.Squeezed(), bslab, D_MODEL), lambda n, b: (0, n * nb + b, 0))
    return pl.pallas_call(
        _attn_prompt_kernel,
        grid=(n_seq, nb),
        in_specs=[
            pl.BlockSpec(memory_space=pltpu.SMEM),
            pl.BlockSpec((bq, ATTN_WIDTH), cur),
            pl.BlockSpec((bq, KV_WIDTH), cur),
            pl.BlockSpec((WINDOW, KV_WIDTH), prev),
            pl.BlockSpec((bq, KV_WIDTH), cur),
            pl.BlockSpec((WINDOW, KV_WIDTH), prev),
            wslab, bspec, bspec,
        ],
        out_specs=[pl.BlockSpec((bq, ATTN_WIDTH), cur), wslab, bspec, bspec],
        out_shape=[_sds((n_seq * seq_len, ATTN_WIDTH), BF16), _sds(w_o.shape, BF16),
                   _sds(w_attn_branch.shape, BF16), _sds(w_conv_branch.shape, BF16)],
        compiler_params=_params(2),
        name="attn_prompt",
    )(sinks, q, k, k, v, v, w_o, w_attn_branch, w_conv_branch)


def _attn_sample_kernel(sinks_ref, q_ref, kn_ref, vn_ref, sk_ref, sv_ref, a_ref, nk_ref, nv_ref):
    nk_ref[:, 0:WINDOW - 1, :] = sk_ref[:, 1:WINDOW, :]
    nk_ref[:, WINDOW - 1:WINDOW, :] = kn_ref[...]
    nv_ref[:, 0:WINDOW - 1, :] = sv_ref[:, 1:WINDOW, :]
    nv_ref[:, WINDOW - 1:WINDOW, :] = vn_ref[...]
    for kh in range(N_KV_HEADS):
        cols = slice(kh * HEAD_DIM, (kh + 1) * HEAD_DIM)
        heads = slice(kh * GROUP, (kh + 1) * GROUP)
        k = nk_ref[:, :, cols].astype(BF16)
        v = nv_ref[:, :, cols].astype(BF16)
        s = jnp.einsum("ngd,ncd->ngc", q_ref[:, heads, :], k, preferred_element_type=F32)
        sink = sinks_ref[heads, :][None]
        m = jnp.maximum(jnp.max(s, axis=-1, keepdims=True), sink)
        p = jnp.exp(s - m)
        den = jnp.sum(p, axis=-1, keepdims=True) + jnp.exp(sink - m)
        o = jnp.einsum("ngc,ncd->ngd", p.astype(BF16), v, preferred_element_type=F32) / den
        a_ref[:, heads, :] = o.astype(BF16)


def _attn_sample(sinks_col, q3, k_new3, v_new3, state_k, state_v):
    n = q3.shape[0]
    return pl.pallas_call(
        _attn_sample_kernel,
        out_shape=[_sds((n, N_HEADS, HEAD_DIM), BF16), _sds((n, WINDOW, KV_WIDTH), F32),
                   _sds((n, WINDOW, KV_WIDTH), F32)],
        compiler_params=pltpu.CompilerParams(vmem_limit_bytes=VMEM_LIMIT),
        name="attn_sample",
    )(sinks_col, q3, k_new3, v_new3, state_k, state_v)


def _shift_rows(cur, halo):
    rid = lax.broadcasted_iota(jnp.int32, cur.shape, 0)
    h0 = halo[0:1, :]
    h1 = halo[1:2, :]
    s1 = jnp.where(rid == 0, h1, pltpu.roll(cur, 1, 0))
    s2 = jnp.where(rid == 0, h0, jnp.where(rid == 1, h1, pltpu.roll(cur, 2, 0)))
    return s1, s2


def _conv3(cw, cur, s1, s2):
    return cw[0:1, :] * s2 + cw[1:2, :] * s1 + cw[2:3, :] * cur


def _conv_tile(i, tiles_per_seq, split, parts, rc, bm, halo_ref, st_ref, project, group_fn):
    gw = halo_ref.shape[1] // split
    gcols = [slice(grp * gw, (grp + 1) * gw) for grp in range(split)]

    @pl.when(i % tiles_per_seq == 0)
    def _():
        halo_ref[...] = jnp.zeros_like(halo_ref)

    halos = {}

    def chunk(rows, ride):
        last = rows.start + rows.size == bm
        full = None if last else project(rows, ride, slice(None))
        for grp in range(split):
            pcols = slice(parts * grp * gw, parts * (grp + 1) * gw)
            proj = project(rows, ride, pcols) if last else full[:, pcols]
            halo = halos.get(grp)
            if halo is None:
                halo = halo_ref[:, gcols[grp]]
            cur = group_fn(rows, ride, grp, proj, lambda cur, halo=halo: _shift_rows(cur, halo))
            halos[grp] = cur[cur.shape[0] - 2:, :]
        if last:
            for grp in range(split):
                halo_ref[:, gcols[grp]] = halos.pop(grp)

    _walk_chunks(i, bm, rc, chunk)

    @pl.when(i % tiles_per_seq == tiles_per_seq - 1)
    def _():
        st_ref[...] = halo_ref[...]


def _convmix_kernel(tiles_per_seq, split, rc, gate_group, h_ref, hs_ref, wb_ref, wc_ref, wu_ref, cw_ref, p2_ref, p1_ref,
                    wgate_ref, o_ref, st_ref, os_ref, pres_ref, wgate_bf_ref, wbf_ref, halo_ref):
    i = pl.program_id(1)
    _cast_gate_slab(wgate_ref, wgate_bf_ref, gate_group)
    gw = wb_ref.shape[1] // split

    @pl.when(i == 0)
    def _():
        for grp in range(split):
            src = slice(grp * gw, (grp + 1) * gw)
            for part, w_ref in enumerate((wb_ref, wc_ref, wu_ref)):
                wbf_ref[:, (3 * grp + part) * gw:(3 * grp + part + 1) * gw] = w_ref[:, src].astype(BF16)

    def project(rows, ride, pcols):
        return _dot(_ride(h_ref, rows, hs_ref, ride), wbf_ref[:, pcols])

    def group(rows, ride, grp, proj, shifts):
        n = rows.size
        cols = slice(grp * gw, (grp + 1) * gw)
        cb, cur = proj[:, 0:gw], proj[:, gw:2 * gw] * proj[:, 2 * gw:3 * gw]
        cw = cw_ref[:, cols]
        s1, s2 = shifts(cur[:n])
        o_ref[rows, cols] = (cb[:n] * _conv3(cw, cur[:n], s1, s2)).astype(BF16)
        if ride:
            pres_ref[:, cols] = cur[n:]
            os_ref[:, cols] = (cb[n:] * _conv3(cw, cur[n:], p1_ref[:, cols], p2_ref[:, cols])).astype(BF16)
        return cur[:n]

    _conv_tile(i, tiles_per_seq, split, 3, rc, h_ref.shape[0], halo_ref, st_ref, project, group)


def _convmix(h, hs, w_in, conv_w, prev2, prev1, *, bm, bn, split, rc, seq_len, gate_group):
    m, ms = h.shape[0], hs.shape[0]
    tiles_per_seq = seq_len // bm
    n_m = m // bm
    gate_in, gate_out = _gate_slab_specs(D_MODEL // (CONV_WIDTH // bn * n_m), lambda j, i: j * n_m + i)
    tile = pl.BlockSpec((bm, bn), lambda j, i: (i, j))
    stile = pl.BlockSpec((ms, bn), lambda j, i: (0, j))
    wspec = lambda off: _wcol(D_MODEL, bn, lambda j, i: off + j * bn)
    return pl.pallas_call(
        functools.partial(_convmix_kernel, tiles_per_seq, split, rc, gate_group),
        grid=(CONV_WIDTH // bn, n_m),
        in_specs=[
            pl.BlockSpec((bm, D_MODEL), lambda j, i: (i, 0)),
            _resident(hs),
            wspec(OFF_CB), wspec(OFF_CC), wspec(OFF_CU),
            pl.BlockSpec((pl.Squeezed(), 3, bn), lambda j, i: (0, 0, j)),
            stile, stile, gate_in,
        ],
        out_specs=[
            tile,
            pl.BlockSpec((pl.Squeezed(), 2, bn), lambda j, i: (i // tiles_per_seq, 0, j)),
            stile, stile, gate_out,
        ],
        out_shape=[_sds((m, CONV_WIDTH), BF16), _sds((m // seq_len, 2, CONV_WIDTH), F32),
                   _sds((ms, CONV_WIDTH), BF16), _sds((ms, CONV_WIDTH), F32), _sds((D_MODEL, 2 * D_MODEL), BF16)],
        scratch_shapes=[pltpu.VMEM((D_MODEL, 3 * bn), BF16), pltpu.VMEM((2, bn), F32)],
        compiler_params=_params(2),
        name="convmix",
    )(h, hs, w_in, w_in, w_in, conv_w, prev2, prev1, w_in)


def _merge_kernel(split, rc, h_ref, a_ref, b_ref, hs_ref, as_ref, bs_ref, wg_ref, wa_ref, wb_ref, bg_ref,
                  o_ref, os_ref):
    gw = wa_ref.shape[1] // split

    def chunk(rows, ride):
        n = rows.size
        h = _ride(h_ref, rows, hs_ref, ride)
        a = _ride(a_ref, rows, as_ref, ride)
        b = _ride(b_ref, rows, bs_ref, ride)
        for grp in range(split):
            cols = slice(grp * gw, (grp + 1) * gw)
            gates = _dot(h, wg_ref[:, 2 * grp * gw:2 * (grp + 1) * gw])
            ga = jax.nn.sigmoid(gates[:, 0:gw] + bg_ref[0:1, cols])
            gb = jax.nn.sigmoid(gates[:, gw:2 * gw] + bg_ref[1:2, cols])
            merged = (ga * _dot(a, wa_ref[:, cols]) + gb * _dot(b, wb_ref[:, cols])).astype(BF16)
            o_ref[rows, cols] = merged[:n]
            if ride:
                os_ref[:, cols] = merged[n:]

    _walk_chunks(pl.program_id(1), h_ref.shape[0], rc, chunk)


def _merge(h, a, b, hs, a_s, b_s, w_gates, w_attn_branch, w_conv_branch, b_gate, *, bm, bn, split, rc):
    m, ms = h.shape[0], hs.shape[0]
    row = lambda j, i: (i, 0)
    return pl.pallas_call(
        functools.partial(_merge_kernel, split, rc),
        grid=(D_MODEL // bn, m // bm),
        in_specs=[
            pl.BlockSpec((bm, D_MODEL), row),
            pl.BlockSpec((bm, ATTN_WIDTH), row),
            pl.BlockSpec((bm, CONV_WIDTH), row),
            _resident(hs), _resident(a_s), _resident(b_s),
            pl.BlockSpec((D_MODEL, 2 * bn), lambda j, i: (0, j)),
            pl.BlockSpec((pl.Squeezed(), ATTN_WIDTH, bn), lambda j, i: (0, 0, j)),
            pl.BlockSpec((pl.Squeezed(), CONV_WIDTH, bn), lambda j, i: (0, 0, j)),
            pl.BlockSpec((pl.Squeezed(), 2, bn), lambda j, i: (0, 0, j)),
        ],
        out_specs=[pl.BlockSpec((bm, bn), lambda j, i: (i, j)),
                   pl.BlockSpec((ms, bn), lambda j, i: (0, j))],
        out_shape=[_sds((m, D_MODEL), BF16), _sds((ms, D_MODEL), BF16)],
        compiler_params=_params(2),
        name="merge",
    )(h, a, b, hs, a_s, b_s, w_gates, w_attn_branch, w_conv_branch, b_gate)


def _oproj_kernel(rc, mg_ref, x_ref, mgs_ref, xs_ref, wo_ref, gpost_ref, gpre_ref,
                  x1_ref, h2_ref, x1s_ref, h2s_ref):
    def chunk(rows, ride):
        n = rows.size
        y = _dot(_ride(mg_ref, rows, mgs_ref, ride), wo_ref[...])
        x1 = _ride(x_ref, rows, xs_ref, ride) + _rms(y, gpost_ref[...])
        h2 = _rms(x1, gpre_ref[...]).astype(BF16)
        x1_ref[rows, :] = x1[:n]
        h2_ref[rows, :] = h2[:n]
        if ride:
            x1s_ref[...] = x1[n:]
            h2s_ref[...] = h2[n:]

    _walk_chunks(pl.program_id(0), x_ref.shape[0], rc, chunk)


def _oproj(merged, x, merged_s, xs, w_o, g_post, g_pre, *, bm, rc):
    m, ms = x.shape[0], xs.shape[0]
    tile = pl.BlockSpec((bm, D_MODEL), lambda i: (i, 0))
    return pl.pallas_call(
        functools.partial(_oproj_kernel, rc),
        grid=(m // bm,),
        in_specs=[
            tile, tile, _resident(merged_s), _resident(xs),
            pl.BlockSpec((pl.Squeezed(), D_MODEL, D_MODEL), lambda i: (0, 0, 0),
                         pipeline_mode=pl.Buffered(1)),
            _resident(g_post), _resident(g_pre),
        ],
        out_specs=[tile, tile, _resident(xs), _resident(merged_s)],
        out_shape=[_sds((m, D_MODEL), F32), _sds((m, D_MODEL), BF16),
                   _sds((ms, D_MODEL), F32), _sds((ms, D_MODEL), BF16)],
        compiler_params=_params(1),
        name="oproj_norm",
    )(merged, x, merged_s, xs, w_o, g_post, g_pre)


def _ffn_up_kernel(tiles_per_seq, split, rc, h_ref, hs_ref, wa_ref, wg_ref, cw_ref, cb_ref, p2_ref, p1_ref, wd_ref,
                   o_ref, st_ref, os_ref, ups_ref, wd_bf_ref, wbf_ref, halo_ref):
    i = pl.program_id(1)
    wd_bf_ref[...] = wd_ref[...].astype(BF16)
    gw = wa_ref.shape[1] // split

    @pl.when(i == 0)
    def _():
        for grp in range(split):
            src = slice(grp * gw, (grp + 1) * gw)
            wbf_ref[:, 2 * grp * gw:(2 * grp + 1) * gw] = wa_ref[:, src].astype(BF16)
            wbf_ref[:, (2 * grp + 1) * gw:(2 * grp + 2) * gw] = wg_ref[:, src].astype(BF16)

    def project(rows, ride, pcols):
        return _dot(_ride(h_ref, rows, hs_ref, ride), wbf_ref[:, pcols])

    def group(rows, ride, grp, proj, shifts):
        n = rows.size
        cols = slice(grp * gw, (grp + 1) * gw)
        cur, gate = proj[:, 0:gw], proj[:, gw:2 * gw]
        cw = cw_ref[:, cols]

        def gated(cur, s1, s2, gate):
            act = jax.nn.gelu(_conv3(cw, cur, s1, s2) + cb_ref[:, cols], approximate=True)
            return (act * gate).astype(BF16)

        s1, s2 = shifts(cur[:n])
        o_ref[rows, cols] = gated(cur[:n], s1, s2, gate[:n])
        if ride:
            ups_ref[:, cols] = cur[n:]
            os_ref[:, cols] = gated(cur[n:], p1_ref[:, cols], p2_ref[:, cols], gate[n:])
        return cur[:n]

    _conv_tile(i, tiles_per_seq, split, 2, rc, h_ref.shape[0], halo_ref, st_ref, project, group)


def _ffn_up(h2, h2s, w_ffn_up, ffn_conv_w, ffn_conv_b, prev2, prev1, w_ffn_down, *, bm, bn, split, rc, seq_len):
    m, ms = h2.shape[0], h2s.shape[0]
    tiles_per_seq = seq_len // bm
    n_tiles = D_FF // bn
    n_m = m // bm
    slab = D_FF // (n_tiles * n_m)
    wslab = pl.BlockSpec((pl.Squeezed(), slab, D_MODEL), lambda j, i: (0, j * n_m + i, 0))
    tile = pl.BlockSpec((bm, bn), lambda j, i: (i, j))
    stile = pl.BlockSpec((ms, bn), lambda j, i: (0, j))
    return pl.pallas_call(
        functools.partial(_ffn_up_kernel, tiles_per_seq, split, rc),
        grid=(n_tiles, m // bm),
        in_specs=[
            pl.BlockSpec((bm, D_MODEL), lambda j, i: (i, 0)),
            _resident(h2s),
            pl.BlockSpec((pl.Squeezed(), D_MODEL, bn), lambda j, i: (0, 0, j)),
            pl.BlockSpec((pl.Squeezed(), D_MODEL, bn), lambda j, i: (0, 0, n_tiles + j)),
            pl.BlockSpec((pl.Squeezed(), 3, bn), lambda j, i: (0, 0, j)),
            pl.BlockSpec((1, bn), lambda j, i: (0, j)),
            stile, stile, wslab,
        ],
        out_specs=[
            tile,
            pl.BlockSpec((pl.Squeezed(), 2, bn), lambda j, i: (i // tiles_per_seq, 0, j)),
            stile, stile, wslab,
        ],
        out_shape=[_sds((m, D_FF), BF16), _sds((m // seq_len, 2, D_FF), F32),
                   _sds((ms, D_FF), BF16), _sds((ms, D_FF), F32), _sds(w_ffn_down.shape, BF16)],
        scratch_shapes=[pltpu.VMEM((D_MODEL, 2 * bn), BF16), pltpu.VMEM((2, bn), F32)],
        compiler_params=_params(2),
        name="ffn_up",
    )(h2, h2s, w_ffn_up, w_ffn_up, ffn_conv_w, ffn_conv_b, prev2, prev1, w_ffn_down)


def _ffn_down_kernel(rc, a_ref, as_ref, w_ref, wg_ref, wp_ref, o_ref, os_ref, wg_bf_ref, wp_bf_ref):
    wg_bf_ref[...] = wg_ref[...].astype(BF16)
    wp_bf_ref[...] = wp_ref[...].astype(BF16)

    def chunk(rows, ride):
        n = rows.size
        o = _dot(_ride(a_ref, rows, as_ref, ride), w_ref[...]).astype(BF16)
        o_ref[rows, :] = o[:n]
        if ride:
            os_ref[...] = o[n:]

    _walk_chunks(pl.program_id(1), a_ref.shape[0], rc, chunk)


def _ffn_down(act, act_s, w_ffn_down, w_ple_gate, w_ple, *, bm, bn, rc):
    m, ms = act.shape[0], act_s.shape[0]
    n_m = m // bm
    steps = D_MODEL // bn * n_m
    slab = lambda w: pl.BlockSpec((pl.Squeezed(), w.shape[1] // steps, D_MODEL), lambda j, i: (0, j * n_m + i, 0))
    return pl.pallas_call(
        functools.partial(_ffn_down_kernel, rc),
        grid=(D_MODEL // bn, n_m),
        in_specs=[
            pl.BlockSpec((bm, D_FF), lambda j, i: (i, 0)),
            _resident(act_s),
            pl.BlockSpec((pl.Squeezed(), D_FF, bn), lambda j, i: (0, 0, j)),
            slab(w_ple_gate), slab(w_ple),
        ],
        out_specs=[pl.BlockSpec((bm, bn), lambda j, i: (i, j)),
                   pl.BlockSpec((ms, bn), lambda j, i: (0, j)),
                   slab(w_ple_gate), slab(w_ple)],
        out_shape=[_sds((m, D_MODEL), BF16), _sds((ms, D_MODEL), BF16), _sds(w_ple_gate.shape, BF16),
                   _sds(w_ple.shape, BF16)],
        compiler_params=_params(2),
        name="ffn_down",
    )(act, act_s, w_ffn_down, w_ple_gate, w_ple)


def _final_kernel(rc, f_ref, x1_ref, p_ref, fs_ref, x1s_ref, ps_ref, g_ref, wg_ref, wp_ref, y_ref, ys_ref):
    def chunk(rows, ride):
        n = rows.size
        f = _ride(f_ref, rows, fs_ref, ride).astype(F32)
        x2 = _ride(x1_ref, rows, x1s_ref, ride) + _rms(f, g_ref[...])
        gate = jax.nn.sigmoid(_dot(x2.astype(BF16), wg_ref[...]))
        y = x2 + gate * _dot(_ride(p_ref, rows, ps_ref, ride).astype(BF16), wp_ref[...])
        y_ref[rows, :] = y[:n]
        if ride:
            ys_ref[...] = y[n:]

    _walk_chunks(pl.program_id(0), x1_ref.shape[0], rc, chunk)


def _final(f, x1, p, f_s, x1_s, p_s, g_post, w_ple_gate, w_ple, *, bm, rc):
    m, ms = x1.shape[0], x1_s.shape[0]
    row = lambda i: (i, 0)
    return pl.pallas_call(
        functools.partial(_final_kernel, rc),
        grid=(m // bm,),
        in_specs=[
            pl.BlockSpec((bm, D_MODEL), row),
            pl.BlockSpec((bm, D_MODEL), row),
            pl.BlockSpec((bm, PLE_DIM), row),
            _resident(f_s), _resident(x1_s), _resident(p_s),
            _resident(g_post),
            pl.BlockSpec((pl.Squeezed(), D_MODEL, D_MODEL), lambda i: (0, 0, 0),
                         pipeline_mode=pl.Buffered(1)),
            pl.BlockSpec((pl.Squeezed(), PLE_DIM, D_MODEL), lambda i: (0, 0, 0),
                         pipeline_mode=pl.Buffered(1)),
        ],
        out_specs=[pl.BlockSpec((bm, D_MODEL), row), _resident(x1_s)],
        out_shape=[_sds((m, D_MODEL), F32), _sds((ms, D_MODEL), F32)],
        compiler_params=_params(1),
        name="ffn_norm_ple",
    )(f, x1, p, f_s, x1_s, p_s, g_post, w_ple_gate, w_ple)


def _rope_tables(pos):
    half = HEAD_DIM // 2
    inv = ROPE_THETA ** (-np.arange(half, dtype=np.float64) / half)
    ang = np.asarray(pos, dtype=np.float64)[:, None] * inv[None, :]
    cos, sin = np.cos(ang), np.sin(ang)
    reps = LANES // HEAD_DIM
    return (jnp.asarray(np.tile(np.concatenate([cos, cos], axis=1), (1, reps)), dtype=F32),
            jnp.asarray(np.tile(np.concatenate([-sin, sin], axis=1), (1, reps)), dtype=F32))


TILES = dict(qkv=dict(bm=1024, rc=256), attn=128, convmix=dict(bm=1024, bn=512, split=2, rc=512),
             merge=dict(bm=1024, bn=1024, split=4, rc=1024), oproj=dict(bm=512, rc=128),
             ffn_up=dict(bm=2048, bn=512, split=1, rc=1024), ffn_down=dict(bm=1024, bn=1024, rc=1024),
             final=dict(bm=512, rc=512))


def kernel(x_prompt, x_sample, state_attn_k, state_attn_v, state_conv, state_ffn_conv, p_prompt, p_sample,
           norm_mix_pre, norm_mix_post, norm_ffn_pre, norm_ffn_post, w_in, attn_sinks, conv_w, w_attn_branch,
           w_conv_branch, b_gate, w_o, w_ffn_up, ffn_conv_w, ffn_conv_b, w_ffn_down, w_ple, w_ple_gate):
    assert w_in.shape[0] == 1, "weights are indexed with a squeezed depth axis of size 1"
    batch, seq, _ = x_prompt.shape
    ns, dec_seq, _ = x_sample.shape
    assert dec_seq == 1
    m = batch * seq
    x = x_prompt.reshape(m, D_MODEL)
    xs = x_sample.reshape(ns, D_MODEL)
    rope_p = _rope_tables(np.arange(seq))
    rope_s = _rope_tables(np.full((ns,), PAST_LEN))

    h, q, k, v, hs, qs, ks, vs = _qkv(x, xs, norm_mix_pre, w_in, rope_p, rope_s, **TILES["qkv"])

    merge_tiles = TILES["merge"]
    a, w_o_bf, w_attn_branch_bf, w_conv_branch_bf = _attn_prompt(
        attn_sinks.reshape(N_HEADS), q, k, v, w_o, w_attn_branch, w_conv_branch,
        n_seq=batch, seq_len=seq, bq=TILES["attn"])
    a3, new_ks, new_vs = _attn_sample(
        attn_sinks.reshape(N_HEADS, 1), qs.reshape(ns, N_HEADS, HEAD_DIM), ks.reshape(ns, 1, KV_WIDTH),
        vs.reshape(ns, 1, KV_WIDTH), state_attn_k.reshape(ns, WINDOW, KV_WIDTH),
        state_attn_v.reshape(ns, WINDOW, KV_WIDTH))
    a_s = a3.reshape(ns, ATTN_WIDTH)

    b, new_conv_p, b_s, pre_s, w_gates_bf = _convmix(
        h, hs, w_in, conv_w, state_conv[0, :, 0], state_conv[0, :, 1], seq_len=seq,
        gate_group=merge_tiles["bn"] // merge_tiles["split"], **TILES["convmix"])
    merged, merged_s = _merge(h, a, b, hs, a_s, b_s, w_gates_bf, w_attn_branch_bf, w_conv_branch_bf, b_gate,
                              **merge_tiles)
    x1, h2, x1_s, h2_s = _oproj(merged, x, merged_s, xs, w_o_bf, norm_mix_post, norm_ffn_pre,
                                **TILES["oproj"])
    act, new_ffn_p, act_s, up_s, w_ffn_down_bf = _ffn_up(
        h2, h2_s, w_ffn_up, ffn_conv_w, ffn_conv_b, state_ffn_conv[0, :, 0], state_ffn_conv[0, :, 1], w_ffn_down,
        seq_len=seq, **TILES["ffn_up"])
    f, f_s, w_ple_gate_bf, w_ple_bf = _ffn_down(act, act_s, w_ffn_down_bf, w_ple_gate, w_ple, **TILES["ffn_down"])
    y, y_s = _final(f, x1, p_prompt.reshape(m, PLE_DIM), f_s, x1_s, p_sample.reshape(ns, PLE_DIM),
                    norm_ffn_post, w_ple_gate_bf, w_ple_bf, **TILES["final"])

    kv_tail = lambda t: t.reshape(batch, seq, KV_WIDTH)[:, seq - WINDOW:].reshape(batch, WINDOW, N_KV_HEADS, HEAD_DIM)
    kv_state = lambda t: t.reshape(ns, WINDOW, N_KV_HEADS, HEAD_DIM)
    return (y.reshape(batch, seq, D_MODEL), y_s.reshape(ns, dec_seq, D_MODEL),
            kv_tail(k)[None], kv_tail(v)[None], new_conv_p[None], new_ffn_p[None],
            kv_state(new_ks)[None], kv_state(new_vs)[None],
            jnp.stack([state_conv[0, :, 1], pre_s], axis=1)[None],
            jnp.stack([state_ffn_conv[0, :, 1], up_s], axis=1)[None])
```

```python
import functools

import jax
import jax.numpy as jnp
import numpy as np
from jax import lax
from jax.experimental import pallas as pl
from jax.experimental.pallas import tpu as pltpu

D_MODEL = 2048
N_HEADS = 16
N_KV_HEADS = 2
HEAD_DIM = 64
GROUP = N_HEADS // N_KV_HEADS
ATTN_WIDTH = N_HEADS * HEAD_DIM
KV_WIDTH = N_KV_HEADS * HEAD_DIM
QKV_WIDTH = ATTN_WIDTH + 2 * KV_WIDTH
WINDOW = 128
PAST_LEN = 16384
ROPE_THETA = 10000.0
CONV_WIDTH = D_MODEL // 2
D_FF = 5632
PLE_DIM = 256
RMS_EPS = 1e-6
OFF_CB = QKV_WIDTH
OFF_CC = OFF_CB + CONV_WIDTH
OFF_CU = OFF_CC + CONV_WIDTH
OFF_GA = OFF_CU + CONV_WIDTH

LANES = 128
UNIT_HEADS = GROUP
MASKED = -1e30
VMEM_LIMIT = 56 * 1024 * 1024

BF16 = jnp.bfloat16
F32 = jnp.float32


def _params(n_axes):
    return pltpu.CompilerParams(dimension_semantics=("arbitrary",) * n_axes,
                                vmem_limit_bytes=VMEM_LIMIT)


def _rms(x, g):
    return x * lax.rsqrt(jnp.mean(x * x, axis=-1, keepdims=True) + RMS_EPS) * g


def _dot(a, b):
    return jnp.dot(a, b, preferred_element_type=F32)


def _walk_chunks(i, bm, rc, chunk_fn):
    rc = min(bm, rc)

    def walk(first):
        for c in range(bm // rc):
            chunk_fn(pl.ds(c * rc, rc), first and c == 0)

    @pl.when(i == 0)
    def _():
        walk(True)

    @pl.when(i > 0)
    def _():
        walk(False)


def _ride(ref, rows, sample_ref, ride):
    x = ref[rows, :]
    return jnp.concatenate([x, sample_ref[...]], axis=0) if ride else x


def _cast_gate_slab(src_ref, dst_ref, group):
    for t in range(D_MODEL // group):
        src = slice(t * group, (t + 1) * group)
        dst_ref[:, 2 * t * group:(2 * t + 1) * group] = src_ref[:, src].astype(BF16)
        dst_ref[:, (2 * t + 1) * group:(2 * t + 2) * group] = (
            src_ref[:, D_MODEL + t * group:D_MODEL + (t + 1) * group].astype(BF16))


def _gate_slab_specs(slab, step_fn):
    gate_in = pl.BlockSpec((pl.Squeezed(), pl.Element(slab), pl.Element(2 * D_MODEL)),
                           lambda *g: (0, pl.multiple_of(step_fn(*g) * slab, slab), OFF_GA))
    gate_out = pl.BlockSpec((slab, 2 * D_MODEL), lambda *g: (step_fn(*g), 0))
    return gate_in, gate_out


def _wcol(k_rows, block_cols, col_fn):
    return pl.BlockSpec((pl.Squeezed(), pl.Element(k_rows), pl.Element(block_cols)),
                        lambda *g: (0, 0, pl.multiple_of(col_fn(*g), LANES)))


def _resident(arr):
    zeros = (0,) * arr.ndim
    return pl.BlockSpec(arr.shape, lambda *g: zeros)


def _sds(shape, dtype):
    return jax.ShapeDtypeStruct(shape, dtype)


def _qkv_kernel(rc, x_ref, xs_ref, g_ref, w_ref, cos_ref, sin_ref, coss_ref, sins_ref,
                h_ref, q_ref, k_ref, v_ref, hs_ref, qs_ref, ks_ref, vs_ref, wbf_ref):
    i = pl.program_id(0)
    n_q = ATTN_WIDTH // LANES

    @pl.when(i == 0)
    def _():
        wbf_ref[...] = w_ref[...].astype(BF16)

    def chunk(rows, ride):
        n = rows.size

        def put(ref, sample_ref, cols, val):
            ref[rows, cols] = val[:n]
            if ride:
                sample_ref[:, cols] = val[n:]

        h = _rms(_ride(x_ref, rows, xs_ref, ride), g_ref[...]).astype(BF16)
        put(h_ref, hs_ref, slice(None), h)
        proj = _dot(h, wbf_ref[...])
        cos = _ride(cos_ref, rows, coss_ref, ride)
        sin = _ride(sin_ref, rows, sins_ref, ride)
        lane = lax.broadcasted_iota(jnp.int32, cos.shape, 1)
        first_half = (lane % HEAD_DIM) < (HEAD_DIM // 2)
        for c in range(n_q + 1):
            cols = slice(c * LANES, (c + 1) * LANES)
            xg = proj[:, cols]
            partner = jnp.where(first_half, pltpu.roll(xg, LANES - HEAD_DIM // 2, 1),
                                pltpu.roll(xg, HEAD_DIM // 2, 1))
            yg = xg * cos + partner * sin
            if c < n_q:
                put(q_ref, qs_ref, cols, (yg * (HEAD_DIM ** -0.5)).astype(BF16))
            else:
                put(k_ref, ks_ref, slice(None), yg)
        put(v_ref, vs_ref, slice(None), proj[:, ATTN_WIDTH + KV_WIDTH:QKV_WIDTH])

    _walk_chunks(i, x_ref.shape[0], rc, chunk)


def _qkv(x, xs, g, w_in, rope_p, rope_s, *, bm, rc):
    m, ms = x.shape[0], xs.shape[0]
    t_tiles = rope_p[0].shape[0] // bm
    row = lambda i: (i, 0)
    pos = lambda i: (i % t_tiles, 0)
    outs = lambda rows: [_sds((rows, D_MODEL), BF16), _sds((rows, ATTN_WIDTH), BF16),
                         _sds((rows, KV_WIDTH), F32), _sds((rows, KV_WIDTH), F32)]
    out_p, out_s = outs(m), outs(ms)
    return pl.pallas_call(
        functools.partial(_qkv_kernel, rc),
        grid=(m // bm,),
        in_specs=[
            pl.BlockSpec((bm, D_MODEL), row),
            _resident(xs),
            _resident(g),
            pl.BlockSpec((pl.Squeezed(), D_MODEL, QKV_WIDTH), lambda i: (0, 0, 0),
                         pipeline_mode=pl.Buffered(1)),
            pl.BlockSpec((bm, LANES), pos),
            pl.BlockSpec((bm, LANES), pos),
            _resident(rope_s[0]),
            _resident(rope_s[1]),
        ],
        out_specs=[pl.BlockSpec((bm, s.shape[1]), row) for s in out_p] + [_resident(s) for s in out_s],
        out_shape=out_p + out_s,
        scratch_shapes=[pltpu.VMEM((D_MODEL, QKV_WIDTH), BF16)],
        compiler_params=_params(1),
        name="qkv_rope",
    )(x, xs, g, w_in, rope_p[0], rope_p[1], rope_s[0], rope_s[1])


def _attn_prompt_kernel(sinks_ref, q_ref, kc_ref, kp_ref, vc_ref, vp_ref, wo_ref, wa_ref, wb_ref,
                        o_ref, wo_bf_ref, wa_bf_ref, wb_bf_ref):
    wo_bf_ref[...] = wo_ref[...].astype(BF16)
    wa_bf_ref[...] = wa_ref[...].astype(BF16)
    wb_bf_ref[...] = wb_ref[...].astype(BF16)

    n_sub = q_ref.shape[0] // WINDOW
    qi = lax.broadcasted_iota(jnp.int32, (WINDOW, WINDOW), 0)
    ci = lax.broadcasted_iota(jnp.int32, (WINDOW, WINDOW), 1)
    own = ci <= qi
    first_valid = own | (pl.program_id(1) > 0)
    k_all = jnp.concatenate([kp_ref[...], kc_ref[...]], axis=0).astype(BF16)
    v_all = jnp.concatenate([vp_ref[...], vc_ref[...]], axis=0).astype(BF16)
    units = [(sub, hg) for sub in range(n_sub) for hg in range(N_HEADS // UNIT_HEADS)]
    heads_of = lambda hg: range(hg * UNIT_HEADS, (hg + 1) * UNIT_HEADS)
    q_rows = lambda sub: slice(sub * WINDOW, (sub + 1) * WINDOW)
    kv_rows = lambda sub: slice(sub * WINDOW, (sub + 2) * WINDOW)
    kv_cols = lambda hg: slice((hg * UNIT_HEADS // GROUP) * HEAD_DIM, (hg * UNIT_HEADS // GROUP + 1) * HEAD_DIM)

    scores = []
    for sub, kh in units:
        qs = jnp.concatenate([q_ref[q_rows(sub), h * HEAD_DIM:(h + 1) * HEAD_DIM] for h in heads_of(kh)], axis=0)
        scores.append(lax.dot_general(qs, k_all[kv_rows(sub), kv_cols(kh)], (((1,), (1,)), ((), ())),
                                      preferred_element_type=F32))
    probs, dens = [], []
    for (sub, kh), s_all in zip(units, scores):
        ps, ds = [], []
        for g, h in enumerate(heads_of(kh)):
            s2 = s_all[g * WINDOW:(g + 1) * WINDOW, :]
            s = jnp.where(own, s2[:, WINDOW:], s2[:, :WINDOW])
            if sub == 0:
                s = jnp.where(first_valid, s, MASKED)
            sink = sinks_ref[h]
            m = jnp.maximum(jnp.max(s, axis=-1, keepdims=True), sink)
            p = jnp.exp(s - m)
            ds.append(jnp.sum(p, axis=-1, keepdims=True) + jnp.exp(sink - m))
            ps.append(jnp.concatenate([jnp.where(own, 0.0, p), jnp.where(own, p, 0.0)], axis=1).astype(BF16))
        probs.append(jnp.concatenate(ps, axis=0))
        dens.append(ds)
    outs = [_dot(p_all, v_all[kv_rows(sub), kv_cols(kh)]) for (sub, kh), p_all in zip(units, probs)]
    for (sub, kh), o_all, ds in zip(units, outs, dens):
        for g, h in enumerate(heads_of(kh)):
            o = o_all[g * WINDOW:(g + 1) * WINDOW, :] / ds[g]
            o_ref[q_rows(sub), h * HEAD_DIM:(h + 1) * HEAD_DIM] = o.astype(BF16)


def _attn_prompt(sinks, q, k, v, w_o, w_attn_branch, w_conv_branch, *, n_seq, seq_len, bq):
    nb = seq_len // bq
    sub = bq // WINDOW
    steps = n_seq * nb
    slab, bslab = D_MODEL // steps, ATTN_WIDTH // steps
    cur = lambda n, b: (n * nb + b, 0)
    prev = lambda n, b: (jnp.maximum((n * nb + b) * sub - 1, 0), 0)
    wslab = pl.BlockSpec((pl.Squeezed(), slab, D_MODEL), lambda n, b: (0, n * nb + b, 0))
    bspec = pl.BlockSpec((pl.Squeezed(), bslab, D_MODEL), lambda n, b: (0, n * nb + b, 0))
    return pl.pallas_call(
        _attn_prompt_kernel,
        grid=(n_seq, nb),
        in_specs=[
            pl.BlockSpec(memory_space=pltpu.SMEM),
            pl.BlockSpec((bq, ATTN_WIDTH), cur),
            pl.BlockSpec((bq, KV_WIDTH), cur),
            pl.BlockSpec((WINDOW, KV_WIDTH), prev),
            pl.BlockSpec((bq, KV_WIDTH), cur),
            pl.BlockSpec((WINDOW, KV_WIDTH), prev),
            wslab, bspec, bspec,
        ],
        out_specs=[pl.BlockSpec((bq, ATTN_WIDTH), cur), wslab, bspec, bspec],
        out_shape=[_sds((n_seq * seq_len, ATTN_WIDTH), BF16), _sds(w_o.shape, BF16),
                   _sds(w_attn_branch.shape, BF16), _sds(w_conv_branch.shape, BF16)],
        compiler_params=_params(2),
        name="attn_prompt",
    )(sinks, q, k, k, v, v, w_o, w_attn_branch, w_conv_branch)


def _attn_sample_kernel(sinks_ref, q_ref, kn_ref, vn_ref, sk_ref, sv_ref, a_ref, nk_ref, nv_ref):
    nk_ref[:, 0:WINDOW - 1, :] = sk_ref[:, 1:WINDOW, :]
    nk_ref[:, WINDOW - 1:WINDOW, :] = kn_ref[...]
    nv_ref[:, 0:WINDOW - 1, :] = sv_ref[:, 1:WINDOW, :]
    nv_ref[:, WINDOW - 1:WINDOW, :] = vn_ref[...]
    for kh in range(N_KV_HEADS):
        cols = slice(kh * HEAD_DIM, (kh + 1) * HEAD_DIM)
        heads = slice(kh * GROUP, (kh + 1) * GROUP)
        k = nk_ref[:, :, cols].astype(BF16)
        v = nv_ref[:, :, cols].astype(BF16)
        s = jnp.einsum("ngd,ncd->ngc", q_ref[:, heads, :], k, preferred_element_type=F32)
        sink = sinks_ref[heads, :][None]
        m = jnp.maximum(jnp.max(s, axis=-1, keepdims=True), sink)
        p = jnp.exp(s - m)
        den = jnp.sum(p, axis=-1, keepdims=True) + jnp.exp(sink - m)
        o = jnp.einsum("ngc,ncd->ngd", p.astype(BF16), v, preferred_element_type=F32) / den
        a_ref[:, heads, :] = o.astype(BF16)


def _attn_sample(sinks_col, q3, k_new3, v_new3, state_k, state_v):
    n = q3.shape[0]
    return pl.pallas_call(
        _attn_sample_kernel,
        out_shape=[_sds((n, N_HEADS, HEAD_DIM), BF16), _sds((n, WINDOW, KV_WIDTH), F32),
                   _sds((n, WINDOW, KV_WIDTH), F32)],
        compiler_params=pltpu.CompilerParams(vmem_limit_bytes=VMEM_LIMIT),
        name="attn_sample",
    )(sinks_col, q3, k_new3, v_new3, state_k, state_v)


def _shift_rows(cur, halo):
    rid = lax.broadcasted_iota(jnp.int32, cur.shape, 0)
    h0 = halo[0:1, :]
    h1 = halo[1:2, :]
    s1 = jnp.where(rid == 0, h1, pltpu.roll(cur, 1, 0))
    s2 = jnp.where(rid == 0, h0, jnp.where(rid == 1, h1, pltpu.roll(cur, 2, 0)))
    return s1, s2


def _conv3(cw, cur, s1, s2):
    return cw[0:1, :] * s2 + cw[1:2, :] * s1 + cw[2:3, :] * cur


def _conv_tile(i, tiles_per_seq, split, parts, rc, bm, halo_ref, st_ref, project, group_fn):
    gw = halo_ref.shape[1] // split
    gcols = [slice(grp * gw, (grp + 1) * gw) for grp in range(split)]

    @pl.when(i % tiles_per_seq == 0)
    def _():
        halo_ref[...] = jnp.zeros_like(halo_ref)

    halos = {}

    def chunk(rows, ride):
        last = rows.start + rows.size == bm
        full = None if last else project(rows, ride, slice(None))
        for grp in range(split):
            pcols = slice(parts * grp * gw, parts * (grp + 1) * gw)
            proj = project(rows, ride, pcols) if last else full[:, pcols]
            halo = halos.get(grp)
            if halo is None:
                halo = halo_ref[:, gcols[grp]]
            cur = group_fn(rows, ride, grp, proj, lambda cur, halo=halo: _shift_rows(cur, halo))
            halos[grp] = cur[cur.shape[0] - 2:, :]
        if last:
            for grp in range(split):
                halo_ref[:, gcols[grp]] = halos.pop(grp)

    _walk_chunks(i, bm, rc, chunk)

    @pl.when(i % tiles_per_seq == tiles_per_seq - 1)
    def _():
        st_ref[...] = halo_ref[...]


def _convmix_kernel(tiles_per_seq, split, rc, gate_group, h_ref, hs_ref, wb_ref, wc_ref, wu_ref, cw_ref, p2_ref, p1_ref,
                    wgate_ref, o_ref, st_ref, os_ref, pres_ref, wgate_bf_ref, wbf_ref, halo_ref):
    i = pl.program_id(1)
    _cast_gate_slab(wgate_ref, wgate_bf_ref, gate_group)
    gw = wb_ref.shape[1] // split

    @pl.when(i == 0)
    def _():
        for grp in range(split):
            src = slice(grp * gw, (grp + 1) * gw)
            for part, w_ref in enumerate((wb_ref, wc_ref, wu_ref)):
                wbf_ref[:, (3 * grp + part) * gw:(3 * grp + part + 1) * gw] = w_ref[:, src].astype(BF16)

    def project(rows, ride, pcols):
        return _dot(_ride(h_ref, rows, hs_ref, ride), wbf_ref[:, pcols])

    def group(rows, ride, grp, proj, shifts):
        n = rows.size
        cols = slice(grp * gw, (grp + 1) * gw)
        cb, cur = proj[:, 0:gw], proj[:, gw:2 * gw] * proj[:, 2 * gw:3 * gw]
        cw = cw_ref[:, cols]
        s1, s2 = shifts(cur[:n])
        o_ref[rows, cols] = (cb[:n] * _conv3(cw, cur[:n], s1, s2)).astype(BF16)
        if ride:
            pres_ref[:, cols] = cur[n:]
            os_ref[:, cols] = (cb[n:] * _conv3(cw, cur[n:], p1_ref[:, cols], p2_ref[:, cols])).astype(BF16)
        return cur[:n]

    _conv_tile(i, tiles_per_seq, split, 3, rc, h_ref.shape[0], halo_ref, st_ref, project, group)


def _convmix(h, hs, w_in, conv_w, prev2, prev1, *, bm, bn, split, rc, seq_len, gate_group):
    m, ms = h.shape[0], hs.shape[0]
    tiles_per_seq = seq_len // bm
    n_m = m // bm
    gate_in, gate_out = _gate_slab_specs(D_MODEL // (CONV_WIDTH // bn * n_m), lambda j, i: j * n_m + i)
    tile = pl.BlockSpec((bm, bn), lambda j, i: (i, j))
    stile = pl.BlockSpec((ms, bn), lambda j, i: (0, j))
    wspec = lambda off: _wcol(D_MODEL, bn, lambda j, i: off + j * bn)
    return pl.pallas_call(
        functools.partial(_convmix_kernel, tiles_per_seq, split, rc, gate_group),
        grid=(CONV_WIDTH // bn, n_m),
        in_specs=[
            pl.BlockSpec((bm, D_MODEL), lambda j, i: (i, 0)),
            _resident(hs),
            wspec(OFF_CB), wspec(OFF_CC), wspec(OFF_CU),
            pl.BlockSpec((pl.Squeezed(), 3, bn), lambda j, i: (0, 0, j)),
            stile, stile, gate_in,
        ],
        out_specs=[
            tile,
            pl.BlockSpec((pl.Squeezed(), 2, bn), lambda j, i: (i // tiles_per_seq, 0, j)),
            stile, stile, gate_out,
        ],
        out_shape=[_sds((m, CONV_WIDTH), BF16), _sds((m // seq_len, 2, CONV_WIDTH), F32),
                   _sds((ms, CONV_WIDTH), BF16), _sds((ms, CONV_WIDTH), F32), _sds((D_MODEL, 2 * D_MODEL), BF16)],
        scratch_shapes=[pltpu.VMEM((D_MODEL, 3 * bn), BF16), pltpu.VMEM((2, bn), F32)],
        compiler_params=_params(2),
        name="convmix",
    )(h, hs, w_in, w_in, w_in, conv_w, prev2, prev1, w_in)


def _merge_kernel(split, rc, h_ref, a_ref, b_ref, hs_ref, as_ref, bs_ref, wg_ref, wa_ref, wb_ref, bg_ref,
                  o_ref, os_ref):
    gw = wa_ref.shape[1] // split

    def chunk(rows, ride):
        n = rows.size
        h = _ride(h_ref, rows, hs_ref, ride)
        a = _ride(a_ref, rows, as_ref, ride)
        b = _ride(b_ref, rows, bs_ref, ride)
        for grp in range(split):
            cols = slice(grp * gw, (grp + 1) * gw)
            gates = _dot(h, wg_ref[:, 2 * grp * gw:2 * (grp + 1) * gw])
            ga = jax.nn.sigmoid(gates[:, 0:gw] + bg_ref[0:1, cols])
            gb = jax.nn.sigmoid(gates[:, gw:2 * gw] + bg_ref[1:2, cols])
            merged = (ga * _dot(a, wa_ref[:, cols]) + gb * _dot(b, wb_ref[:, cols])).astype(BF16)
            o_ref[rows, cols] = merged[:n]
            if ride:
                os_ref[:, cols] = merged[n:]

    _walk_chunks(pl.program_id(1), h_ref.shape[0], rc, chunk)


def _merge(h, a, b, hs, a_s, b_s, w_gates, w_attn_branch, w_conv_branch, b_gate, *, bm, bn, split, rc):
    m, ms = h.shape[0], hs.shape[0]
    row = lambda j, i: (i, 0)
    return pl.pallas_call(
        functools.partial(_merge_kernel, split, rc),
        grid=(D_MODEL // bn, m // bm),
        in_specs=[
            pl.BlockSpec((bm, D_MODEL), row),
            pl.BlockSpec((bm, ATTN_WIDTH), row),
            pl.BlockSpec((bm, CONV_WIDTH), row),
            _resident(hs), _resident(a_s), _resident(b_s),
            pl.BlockSpec((D_MODEL, 2 * bn), lambda j, i: (0, j)),
            pl.BlockSpec((pl.Squeezed(), ATTN_WIDTH, bn), lambda j, i: (0, 0, j)),
            pl.BlockSpec((pl.Squeezed(), CONV_WIDTH, bn), lambda j, i: (0, 0, j)),
            pl.BlockSpec((pl.Squeezed(), 2, bn), lambda j, i: (0, 0, j)),
        ],
        out_specs=[pl.BlockSpec((bm, bn), lambda j, i: (i, j)),
                   pl.BlockSpec((ms, bn), lambda j, i: (0, j))],
        out_shape=[_sds((m, D_MODEL), BF16), _sds((ms, D_MODEL), BF16)],
        compiler_params=_params(2),
        name="merge",
    )(h, a, b, hs, a_s, b_s, w_gates, w_attn_branch, w_conv_branch, b_gate)


def _oproj_kernel(rc, mg_ref, x_ref, mgs_ref, xs_ref, wo_ref, gpost_ref, gpre_ref,
                  x1_ref, h2_ref, x1s_ref, h2s_ref):
    def chunk(rows, ride):
        n = rows.size
        y = _dot(_ride(mg_ref, rows, mgs_ref, ride), wo_ref[...])
        x1 = _ride(x_ref, rows, xs_ref, ride) + _rms(y, gpost_ref[...])
        h2 = _rms(x1, gpre_ref[...]).astype(BF16)
        x1_ref[rows, :] = x1[:n]
        h2_ref[rows, :] = h2[:n]
        if ride:
            x1s_ref[...] = x1[n:]
            h2s_ref[...] = h2[n:]

    _walk_chunks(pl.program_id(0), x_ref.shape[0], rc, chunk)


def _oproj(merged, x, merged_s, xs, w_o, g_post, g_pre, *, bm, rc):
    m, ms = x.shape[0], xs.shape[0]
    tile = pl.BlockSpec((bm, D_MODEL), lambda i: (i, 0))
    return pl.pallas_call(
        functools.partial(_oproj_kernel, rc),
        grid=(m // bm,),
        in_specs=[
            tile, tile, _resident(merged_s), _resident(xs),
            pl.BlockSpec((pl.Squeezed(), D_MODEL, D_MODEL), lambda i: (0, 0, 0),
                         pipeline_mode=pl.Buffered(1)),
            _resident(g_post), _resident(g_pre),
        ],
        out_specs=[tile, tile, _resident(xs), _resident(merged_s)],
        out_shape=[_sds((m, D_MODEL), F32), _sds((m, D_MODEL), BF16),
                   _sds((ms, D_MODEL), F32), _sds((ms, D_MODEL), BF16)],
        compiler_params=_params(1),
        name="oproj_norm",
    )(merged, x, merged_s, xs, w_o, g_post, g_pre)


def _ffn_up_kernel(tiles_per_seq, split, rc, h_ref, hs_ref, wa_ref, wg_ref, cw_ref, cb_ref, p2_ref, p1_ref, wd_ref,
                   o_ref, st_ref, os_ref, ups_ref, wd_bf_ref, wbf_ref, halo_ref):
    i = pl.program_id(1)
    wd_bf_ref[...] = wd_ref[...].astype(BF16)
    gw = wa_ref.shape[1] // split

    @pl.when(i == 0)
    def _():
        for grp in range(split):
            src = slice(grp * gw, (grp + 1) * gw)
            wbf_ref[:, 2 * grp * gw:(2 * grp + 1) * gw] = wa_ref[:, src].astype(BF16)
            wbf_ref[:, (2 * grp + 1) * gw:(2 * grp + 2) * gw] = wg_ref[:, src].astype(BF16)

    def project(rows, ride, pcols):
        return _dot(_ride(h_ref, rows, hs_ref, ride), wbf_ref[:, pcols])

    def group(rows, ride, grp, proj, shifts):
        n = rows.size
        cols = slice(grp * gw, (grp + 1) * gw)
        cur, gate = proj[:, 0:gw], proj[:, gw:2 * gw]
        cw = cw_ref[:, cols]

        def gated(cur, s1, s2, gate):
            act = jax.nn.gelu(_conv3(cw, cur, s1, s2) + cb_ref[:, cols], approximate=True)
            return (act * gate).astype(BF16)

        s1, s2 = shifts(cur[:n])
        o_ref[rows, cols] = gated(cur[:n], s1, s2, gate[:n])
        if ride:
            ups_ref[:, cols] = cur[n:]
            os_ref[:, cols] = gated(cur[n:], p1_ref[:, cols], p2_ref[:, cols], gate[n:])
        return cur[:n]

    _conv_tile(i, tiles_per_seq, split, 2, rc, h_ref.shape[0], halo_ref, st_ref, project, group)


def _ffn_up(h2, h2s, w_ffn_up, ffn_conv_w, ffn_conv_b, prev2, prev1, w_ffn_down, *, bm, bn, split, rc, seq_len):
    m, ms = h2.shape[0], h2s.shape[0]
    tiles_per_seq = seq_len // bm
    n_tiles = D_FF // bn
    n_m = m // bm
    slab = D_FF // (n_tiles * n_m)
    wslab = pl.BlockSpec((pl.Squeezed(), slab, D_MODEL), lambda j, i: (0, j * n_m + i, 0))
    tile = pl.BlockSpec((bm, bn), lambda j, i: (i, j))
    stile = pl.BlockSpec((ms, bn), lambda j, i: (0, j))
    return pl.pallas_call(
        functools.partial(_ffn_up_kernel, tiles_per_seq, split, rc),
        grid=(n_tiles, m // bm),
        in_specs=[
            pl.BlockSpec((bm, D_MODEL), lambda j, i: (i, 0)),
            _resident(h2s),
            pl.BlockSpec((pl.Squeezed(), D_MODEL, bn), lambda j, i: (0, 0, j)),
            pl.BlockSpec((pl.Squeezed(), D_MODEL, bn), lambda j, i: (0, 0, n_tiles + j)),
            pl.BlockSpec((pl.Squeezed(), 3, bn), lambda j, i: (0, 0, j)),
            pl.BlockSpec((1, bn), lambda j, i: (0, j)),
            stile, stile, wslab,
        ],
        out_specs=[
            tile,
            pl.BlockSpec((pl.Squeezed(), 2, bn), lambda j, i: (i // tiles_per_seq, 0, j)),
            stile, stile, wslab,
        ],
        out_shape=[_sds((m, D_FF), BF16), _sds((m // seq_len, 2, D_FF), F32),
                   _sds((ms, D_FF), BF16), _sds((ms, D_FF), F32), _sds(w_ffn_down.shape, BF16)],
        scratch_shapes=[pltpu.VMEM((D_MODEL, 2 * bn), BF16), pltpu.VMEM((2, bn), F32)],
        compiler_params=_params(2),
        name="ffn_up",
    )(h2, h2s, w_ffn_up, w_ffn_up, ffn_conv_w, ffn_conv_b, prev2, prev1, w_ffn_down)


def _ffn_down_kernel(rc, a_ref, as_ref, w_ref, wg_ref, wp_ref, o_ref, os_ref, wg_bf_ref, wp_bf_ref):
    wg_bf_ref[...] = wg_ref[...].astype(BF16)
    wp_bf_ref[...] = wp_ref[...].astype(BF16)

    def chunk(rows, ride):
        n = rows.size
        o = _dot(_ride(a_ref, rows, as_ref, ride), w_ref[...]).astype(BF16)
        o_ref[rows, :] = o[:n]
        if ride:
            os_ref[...] = o[n:]

    _walk_chunks(pl.program_id(1), a_ref.shape[0], rc, chunk)


def _ffn_down(act, act_s, w_ffn_down, w_ple_gate, w_ple, *, bm, bn, rc):
    m, ms = act.shape[0], act_s.shape[0]
    n_m = m // bm
    steps = D_MODEL // bn * n_m
    slab = lambda w: pl.BlockSpec((pl.Squeezed(), w.shape[1] // steps, D_MODEL), lambda j, i: (0, j * n_m + i, 0))
    return pl.pallas_call(
        functools.partial(_ffn_down_kernel, rc),
        grid=(D_MODEL // bn, n_m),
        in_specs=[
            pl.BlockSpec((bm, D_FF), lambda j, i: (i, 0)),
            _resident(act_s),
            pl.BlockSpec((pl.Squeezed(), D_FF, bn), lambda j, i: (0, 0, j)),
            slab(w_ple_gate), slab(w_ple),
        ],
        out_specs=[pl.BlockSpec((bm, bn), lambda j, i: (i, j)),
                   pl.BlockSpec((ms, bn), lambda j, i: (0, j)),
                   slab(w_ple_gate), slab(w_ple)],
        out_shape=[_sds((m, D_MODEL), BF16), _sds((ms, D_MODEL), BF16), _sds(w_ple_gate.shape, BF16),
                   _sds(w_ple.shape, BF16)],
        compiler_params=_params(2),
        name="ffn_down",
    )(act, act_s, w_ffn_down, w_ple_gate, w_ple)


def _final_kernel(rc, f_ref, x1_ref, p_ref, fs_ref, x1s_ref, ps_ref, g_ref, wg_ref, wp_ref, y_ref, ys_ref):
    def chunk(rows, ride):
        n = rows.size
        f = _ride(f_ref, rows, fs_ref, ride).astype(F32)
        x2 = _ride(x1_ref, rows, x1s_ref, ride) + _rms(f, g_ref[...])
        gate = jax.nn.sigmoid(_dot(x2.astype(BF16), wg_ref[...]))
        y = x2 + gate * _dot(_ride(p_ref, rows, ps_ref, ride).astype(BF16), wp_ref[...])
        y_ref[rows, :] = y[:n]
        if ride:
            ys_ref[...] = y[n:]

    _walk_chunks(pl.program_id(0), x1_ref.shape[0], rc, chunk)


def _final(f, x1, p, f_s, x1_s, p_s, g_post, w_ple_gate, w_ple, *, bm, rc):
    m, ms = x1.shape[0], x1_s.shape[0]
    row = lambda i: (i, 0)
    return pl.pallas_call(
        functools.partial(_final_kernel, rc),
        grid=(m // bm,),
        in_specs=[
            pl.BlockSpec((bm, D_MODEL), row),
            pl.BlockSpec((bm, D_MODEL), row),
            pl.BlockSpec((bm, PLE_DIM), row),
            _resident(f_s), _resident(x1_s), _resident(p_s),
            _resident(g_post),
            pl.BlockSpec((pl.Squeezed(), D_MODEL, D_MODEL), lambda i: (0, 0, 0),
                         pipeline_mode=pl.Buffered(1)),
            pl.BlockSpec((pl.Squeezed(), PLE_DIM, D_MODEL), lambda i: (0, 0, 0),
                         pipeline_mode=pl.Buffered(1)),
        ],
        out_specs=[pl.BlockSpec((bm, D_MODEL), row), _resident(x1_s)],
        out_shape=[_sds((m, D_MODEL), F32), _sds((ms, D_MODEL), F32)],
        compiler_params=_params(1),
        name="ffn_norm_ple",
    )(f, x1, p, f_s, x1_s, p_s, g_post, w_ple_gate, w_ple)


def _rope_tables(pos):
    half = HEAD_DIM // 2
    inv = ROPE_THETA ** (-np.arange(half, dtype=np.float64) / half)
    ang = np.asarray(pos, dtype=np.float64)[:, None] * inv[None, :]
    cos, sin = np.cos(ang), np.sin(ang)
    reps = LANES // HEAD_DIM
    return (jnp.asarray(np.tile(np.concatenate([cos, cos], axis=1), (1, reps)), dtype=F32),
            jnp.asarray(np.tile(np.concatenate([-sin, sin], axis=1), (1, reps)), dtype=F32))


TILES = dict(qkv=dict(bm=512, rc=256), attn=128, convmix=dict(bm=1024, bn=512, split=2, rc=512),
             merge=dict(bm=1024, bn=1024, split=4, rc=1024), oproj=dict(bm=512, rc=128),
             ffn_up=dict(bm=2048, bn=512, split=1, rc=1024), ffn_down=dict(bm=1024, bn=1024, rc=1024),
             final=dict(bm=512, rc=512))


def kernel(x_prompt, x_sample, state_attn_k, state_attn_v, state_conv, state_ffn_conv, p_prompt, p_sample,
           norm_mix_pre, norm_mix_post, norm_ffn_pre, norm_ffn_post, w_in, attn_sinks, conv_w, w_attn_branch,
           w_conv_branch, b_gate, w_o, w_ffn_up, ffn_conv_w, ffn_conv_b, w_ffn_down, w_ple, w_ple_gate):
    assert w_in.shape[0] == 1, "weights are indexed with a squeezed depth axis of size 1"
    batch, seq, _ = x_prompt.shape
    ns, dec_seq, _ = x_sample.shape
    assert dec_seq == 1
    m = batch * seq
    x = x_prompt.reshape(m, D_MODEL)
    xs = x_sample.reshape(ns, D_MODEL)
    rope_p = _rope_tables(np.arange(seq))
    rope_s = _rope_tables(np.full((ns,), PAST_LEN))

    h, q, k, v, hs, qs, ks, vs = _qkv(x, xs, norm_mix_pre, w_in, rope_p, rope_s, **TILES["qkv"])

    merge_tiles = TILES["merge"]
    a, w_o_bf, w_attn_branch_bf, w_conv_branch_bf = _attn_prompt(
        attn_sinks.reshape(N_HEADS), q, k, v, w_o, w_attn_branch, w_conv_branch,
        n_seq=batch, seq_len=seq, bq=TILES["attn"])
    a3, new_ks, new_vs = _attn_sample(
        attn_sinks.reshape(N_HEADS, 1), qs.reshape(ns, N_HEADS, HEAD_DIM), ks.reshape(ns, 1, KV_WIDTH),
        vs.reshape(ns, 1, KV_WIDTH), state_attn_k.reshape(ns, WINDOW, KV_WIDTH),
        state_attn_v.reshape(ns, WINDOW, KV_WIDTH))
    a_s = a3.reshape(ns, ATTN_WIDTH)

    b, new_conv_p, b_s, pre_s, w_gates_bf = _convmix(
        h, hs, w_in, conv_w, state_conv[0, :, 0], state_conv[0, :, 1], seq_len=seq,
        gate_group=merge_tiles["bn"] // merge_tiles["split"], **TILES["convmix"])
    merged, merged_s = _merge(h, a, b, hs, a_s, b_s, w_gates_bf, w_attn_branch_bf, w_conv_branch_bf, b_gate,
                              **merge_tiles)
    x1, h2, x1_s, h2_s = _oproj(merged, x, merged_s, xs, w_o_bf, norm_mix_post, norm_ffn_pre,
                                **TILES["oproj"])
    act, new_ffn_p, act_s, up_s, w_ffn_down_bf = _ffn_up(
        h2, h2_s, w_ffn_up, ffn_conv_w, ffn_conv_b, state_ffn_conv[0, :, 0], state_ffn_conv[0, :, 1], w_ffn_down,
        seq_len=seq, **TILES["ffn_up"])
    f, f_s, w_ple_gate_bf, w_ple_bf = _ffn_down(act, act_s, w_ffn_down_bf, w_ple_gate, w_ple, **TILES["ffn_down"])
    y, y_s = _final(f, x1, p_prompt.reshape(m, PLE_DIM), f_s, x1_s, p_sample.reshape(ns, PLE_DIM),
                    norm_ffn_post, w_ple_gate_bf, w_ple_bf, **TILES["final"])

    kv_tail = lambda t: t.reshape(batch, seq, KV_WIDTH)[:, seq - WINDOW:].reshape(batch, WINDOW, N_KV_HEADS, HEAD_DIM)
    kv_state = lambda t: t.reshape(ns, WINDOW, N_KV_HEADS, HEAD_DIM)
    return (y.reshape(batch, seq, D_MODEL), y_s.reshape(ns, dec_seq, D_MODEL),
            kv_tail(k)[None], kv_tail(v)[None], new_conv_p[None], new_ffn_p[None],
            kv_state(new_ks)[None], kv_state(new_vs)[None],
            jnp.stack([state_conv[0, :, 1], pre_s], axis=1)[None],
            jnp.stack([state_ffn_conv[0, :, 1], up_s], axis=1)[None])
```
